```python
import math
import jax
import jax.numpy as jnp
from jax import lax
import numpy as np

D_MODEL = 1024
BATCH = 8
SEQ = 4096
DEPTH = 4

N_MEM = 256
HEAD_DIM = 64
EPS = 1e-6
MASK_VALUE = -1e9
LB_FLOOR = 1e-30
FOX_W = D_MODEL // 4
FOX_H = FOX_W // HEAD_DIM
FOX_BLOCK = 128
HG_W = D_MODEL // 4
HG_H = HG_W // HEAD_DIM
HG_K = HEAD_DIM
HG_V = HEAD_DIM
HG_CHUNK = 64
M2_W = D_MODEL // 2
M2_P = HEAD_DIM
M2_H = M2_W // M2_P
M2_G = 2
M2_N = 128
M2_CONV = 4
M2_CHUNK = 128
M2_CONV_CH = M2_W + 2 * M2_G * M2_N
D_MIX = FOX_W + HG_W + M2_W
IN_SPLITS = (FOX_W, FOX_W, FOX_W, FOX_H, HG_W, HG_W, HG_W, HG_W, M2_W, M2_CONV_CH, M2_H)
D_IN = 3 * FOX_W + FOX_H + 4 * HG_W + M2_W + M2_CONV_CH + M2_H
XA_H = 4
XA_D = D_MODEL // XA_H
D_FF = 128 * ((8 * D_MODEL // 3 + 127) // 128)
N_EXPERTS = 8
TOP_K = 2
MOE_BLOCK = 128
N_DENSE = (DEPTH + 1) // 2
N_MOE = DEPTH // 2

kernel_name = 'hybrid_fox_hgrn2_ssd_moe_trunk'


def rms_norm(x, w):
    xf = x.astype(jnp.float32)
    y = xf * lax.rsqrt(jnp.mean(xf * xf, axis=-1, keepdims=True) + EPS)
    return (y * w.astype(jnp.float32)).astype(x.dtype)


def _causal_mask(n):
    return jnp.tril(jnp.ones((n, n), dtype=bool))


def masked_exp(x, mask):
    return jnp.where(mask, jnp.exp(jnp.where(mask, x, 0.0)), 0.0)


def fox_attention(q, k, v, log_f):
    b, s, h, d = q.shape
    nblk = s // FOX_BLOCK
    scale = d ** -0.5
    c = jnp.cumsum(log_f, axis=1).transpose(0, 2, 1)
    kt = k.transpose(0, 2, 1, 3)
    vt = v.transpose(0, 2, 1, 3)
    qb = q.reshape(b, nblk, FOX_BLOCK, h, d).transpose(1, 0, 3, 2, 4)
    cb = c.reshape(b, h, nblk, FOX_BLOCK).transpose(2, 0, 1, 3)
    kpos = jnp.arange(s)

    def query_block(args):
        q_i, c_i, i = args
        logits = jnp.einsum('bhqd,bhkd->bhqk', q_i, kt, preferred_element_type=jnp.float32) * scale
        logits = logits + c_i[..., :, None] - c[:, :, None, :]
        qpos = i * FOX_BLOCK + jnp.arange(FOX_BLOCK)
        logits = jnp.where(kpos[None, :] <= qpos[:, None], logits, MASK_VALUE)
        p = jax.nn.softmax(logits, axis=-1)
        return jnp.einsum('bhqk,bhkd->bhqd', p.astype(vt.dtype), vt)

    o = lax.map(query_block, (qb, cb, jnp.arange(nblk)))
    return o.transpose(1, 0, 3, 2, 4).reshape(b, s, h, d)


def hgrn2_recurrence(q, k, v, log_f):
    b, s, h, dk = q.shape
    dv = v.shape[-1]
    n = s // HG_CHUNK
    causal = _causal_mask(HG_CHUNK)[:, :, None]

    def to_chunks(t):
        return t.astype(jnp.float32).reshape(b, n, HG_CHUNK, h, t.shape[-1]).transpose(1, 0, 3, 2, 4)

    def chunk_step(state, inp):
        q_c, k_c, v_c, g_c = inp
        cum = jnp.cumsum(g_c, axis=2)
        diff = cum[:, :, :, None, :] - cum[:, :, None, :, :]
        decay = masked_exp(diff, causal)
        scores = jnp.einsum('bhtsk,bhsk->bhts', q_c[:, :, :, None, :] * decay, k_c)
        o = jnp.einsum('bhts,bhsv->bhtv', scores, v_c) + jnp.einsum('bhtk,bhkv->bhtv', q_c * jnp.exp(cum), state)
        last = cum[:, :, -1:, :]
        state = jnp.exp(last[:, :, 0, :, None]) * state + jnp.einsum('bhsk,bhsv->bhkv', k_c * jnp.exp(last - cum), v_c)
        return state, o

    state0 = jnp.zeros((b, h, dk, dv), jnp.float32)
    _, o = lax.scan(chunk_step, state0, (to_chunks(q), to_chunks(k), to_chunks(v), to_chunks(log_f)))
    return o.transpose(1, 0, 3, 2, 4).reshape(b, s, h, dv)


def causal_depthwise_conv(x, w, bias):
    width, ch = w.shape
    y = lax.conv_general_dilated(x, w[:, None, :].astype(x.dtype), window_strides=(1,), padding=[(width - 1, 0)], dimension_numbers=('NWC', 'WIO', 'NWC'), feature_group_count=ch)
    return y + bias.astype(y.dtype)


def ssd_chunked(x, dt, a, bm, cm):
    b, s, h, p = x.shape
    g, nst = bm.shape[-2:]
    e = h // g
    n = s // M2_CHUNK
    ln = M2_CHUNK
    xc = (x.astype(jnp.float32) * dt[..., None]).reshape(b, n, ln, g, e, p)
    bc = bm.astype(jnp.float32).reshape(b, n, ln, g, nst)
    cc = cm.astype(jnp.float32).reshape(b, n, ln, g, nst)
    a_dt = (dt * a).reshape(b, n, ln, g, e).transpose(0, 3, 4, 1, 2)
    a_cum = jnp.cumsum(a_dt, axis=-1)
    seg = a_cum[..., :, None] - a_cum[..., None, :]
    lmat = masked_exp(seg, _causal_mask(ln))
    cb = jnp.einsum('bclgn,bcsgn->bgcls', cc, bc)
    y_diag = jnp.einsum('bgecls,bcsgep->bclgep', cb[:, :, None] * lmat, xc)
    decay_states = jnp.exp(a_cum[..., -1:] - a_cum).transpose(0, 3, 4, 1, 2)
    states = jnp.einsum('bclgn,bclgep->bcgepn', bc, xc * decay_states[..., None])
    cx = jnp.cumsum(jnp.pad(a_cum[..., -1], ((0, 0), (0, 0), (0, 0), (1, 0))), axis=-1)
    decay_chunk = masked_exp(cx[..., :, None] - cx[..., None, :], _causal_mask(n + 1))
    states = jnp.concatenate([jnp.zeros_like(states[:, :1]), states], axis=1)
    new_states = jnp.einsum('bgezc,bcgepn->bzgepn', decay_chunk, states)[:, :-1]
    state_decay_out = jnp.exp(a_cum).transpose(0, 3, 4, 1, 2)
    y_off = jnp.einsum('bclgn,bcgepn->bclgep', cc, new_states) * state_decay_out[..., None]
    return (y_diag + y_off).reshape(b, s, h, p)


def mamba2_group(z, xbc, dt_raw, conv_w, conv_b, dt_bias, a_log, d_skip, norm_w):
    b, s, _ = z.shape
    xbc = jax.nn.silu(causal_depthwise_conv(xbc, conv_w, conv_b))
    xs, bm, cm = jnp.split(xbc, [M2_W, M2_W + M2_G * M2_N], axis=-1)
    xs = xs.reshape(b, s, M2_H, M2_P)
    dt = jax.nn.softplus(dt_raw.astype(jnp.float32) + dt_bias.astype(jnp.float32))
    a = -jnp.exp(a_log.astype(jnp.float32))
    y = ssd_chunked(xs, dt, a, bm.reshape(b, s, M2_G, M2_N), cm.reshape(b, s, M2_G, M2_N))
    y = y + d_skip.astype(jnp.float32)[:, None] * xs.astype(jnp.float32)
    y = y.reshape(b, s, M2_W) * jax.nn.silu(z.astype(jnp.float32))
    y = rms_norm(y.reshape(b, s, M2_G, M2_W // M2_G), norm_w.reshape(M2_G, M2_W // M2_G))
    return y.reshape(b, s, M2_W)


def parallel_mixer(h, norm_w, w_in, fox_f_bias, fox_norm_w, hg_lb, hg_norm_w, conv_w, conv_b, dt_bias, a_log, d_skip, m2_norm_w, w_out):
    b, s, _ = h.shape
    split_at = [int(i) for i in np.cumsum(IN_SPLITS)[:-1]]
    fq, fk, fv, ff, hq, hf, hi, hgate, mz, mxbc, mdt = jnp.split(rms_norm(h, norm_w) @ w_in, split_at, axis=-1)
    fox_log_f = jax.nn.log_sigmoid(ff.astype(jnp.float32) + fox_f_bias.astype(jnp.float32))
    o_fox = fox_attention(fq.reshape(b, s, FOX_H, HEAD_DIM), fk.reshape(b, s, FOX_H, HEAD_DIM), fv.reshape(b, s, FOX_H, HEAD_DIM), fox_log_f)
    o_fox = rms_norm(o_fox, fox_norm_w.reshape(FOX_H, HEAD_DIM)).reshape(b, s, FOX_W)
    lb = hg_lb.reshape(HG_H, HG_K)
    f_raw = hf.astype(jnp.float32).reshape(b, s, HG_H, HG_K)
    hg_log_f = jnp.logaddexp(jnp.log(jnp.maximum(lb, LB_FLOOR)), jnp.log1p(-lb) + jax.nn.log_sigmoid(f_raw))
    hg_k = (1.0 - lb) * jax.nn.sigmoid(-f_raw)
    hg_q = jax.nn.silu(hq.astype(jnp.float32)).reshape(b, s, HG_H, HG_K) * (HG_K ** -0.5)
    o_hg = hgrn2_recurrence(hg_q, hg_k, hi.reshape(b, s, HG_H, HG_V), hg_log_f)
    o_hg = rms_norm(o_hg, hg_norm_w.reshape(HG_H, HG_V)) * jax.nn.silu(hgate.astype(jnp.float32).reshape(b, s, HG_H, HG_V))
    o_hg = o_hg.reshape(b, s, HG_W)
    o_m2 = mamba2_group(mz, mxbc, mdt, conv_w, conv_b, dt_bias, a_log, d_skip, m2_norm_w)
    o = jnp.concatenate([o_fox.astype(h.dtype), o_hg.astype(h.dtype), o_m2.astype(h.dtype)], axis=-1)
    return o @ w_out


def memory_cross_attention(h, mem, norm_w, mem_norm_w, w_q, w_kv, w_o):
    b, s, _ = h.shape
    m = mem.shape[1]
    q = (rms_norm(h, norm_w) @ w_q).reshape(b, s, XA_H, XA_D)
    k, v = jnp.split(rms_norm(mem, mem_norm_w) @ w_kv, 2, axis=-1)
    k = k.reshape(b, m, XA_H, XA_D)
    v = v.reshape(b, m, XA_H, XA_D)
    logits = jnp.einsum('bshd,bmhd->bhsm', q, k, preferred_element_type=jnp.float32) * (XA_D ** -0.5)
    p = jax.nn.softmax(logits, axis=-1)
    o = jnp.einsum('bhsm,bmhd->bshd', p.astype(v.dtype), v).reshape(b, s, XA_H * XA_D)
    return o @ w_o


def swiglu(x, w1, w2):
    gate, up = jnp.split(x @ w1, 2, axis=-1)
    return (jax.nn.silu(gate) * up) @ w2


def moe_ffn(x, router_w, w1, w2):
    b, s, d = x.shape
    xt = x.reshape(-1, d)
    n_tok = xt.shape[0]
    logits = (xt @ router_w).astype(jnp.float32)
    top_logit, top_idx = lax.top_k(logits, TOP_K)
    gates = jax.nn.softmax(top_logit, axis=-1)
    n_assign = n_tok * TOP_K
    e_flat = top_idx.reshape(-1)
    g_flat = gates.reshape(-1)
    t_flat = jnp.arange(n_assign) // TOP_K
    order = jnp.argsort(e_flat)
    e_sorted = e_flat[order]
    counts = jnp.bincount(e_flat, length=N_EXPERTS)
    starts = jnp.cumsum(counts) - counts
    padded = (counts + MOE_BLOCK - 1) // MOE_BLOCK * MOE_BLOCK
    pends = jnp.cumsum(padded)
    pstarts = pends - padded
    dest = pstarts[e_sorted] + (jnp.arange(n_assign) - starts[e_sorted])
    cap = ((n_assign + MOE_BLOCK - 1) // MOE_BLOCK + N_EXPERTS) * MOE_BLOCK
    slot_tok = jnp.zeros((cap,), jnp.int32).at[dest].set(t_flat[order].astype(jnp.int32))
    slot_gate = jnp.zeros((cap,), jnp.float32).at[dest].set(g_flat[order])
    n_blk = cap // MOE_BLOCK
    blk_expert = jnp.minimum(jnp.searchsorted(pends, jnp.arange(n_blk) * MOE_BLOCK, side='right'), N_EXPERTS - 1)
    xb = xt[slot_tok].reshape(n_blk, MOE_BLOCK, d)

    def expert_group(args):
        x_blk, e = args
        return swiglu(x_blk, w1[e], w2[e])

    yb = lax.map(expert_group, (xb, blk_expert)).reshape(cap, d)
    yb = yb * slot_gate[:, None].astype(yb.dtype)
    y = jnp.zeros_like(xt).at[slot_tok].add(yb)
    return y.reshape(b, s, d)


def setup_inputs(seed: int = 0) -> dict:
    key = jax.random.key(seed)
    ks = jax.random.split(key, 27)
    f32 = jnp.float32

    def normal(k, shape, scale):
        return jax.random.normal(k, shape, f32) * scale

    def gain(k, shape):
        return 1.0 + 0.05 * jax.random.normal(k, shape, f32)

    dt0 = jnp.exp(jax.random.uniform(ks[10], (DEPTH, M2_H), f32) * (math.log(0.1) - math.log(0.001)) + math.log(0.001))
    return {
        'x': normal(ks[0], (BATCH, SEQ, D_MODEL), 1.0),
        'mem': normal(ks[1], (BATCH, N_MEM, D_MODEL), 1.0),
        'mix_norm_w': gain(ks[2], (DEPTH, D_MODEL)),
        'w_in': normal(ks[3], (DEPTH, D_MODEL, D_IN), D_MODEL ** -0.5),
        'fox_f_bias': 2.0 + 0.5 * jax.random.normal(ks[4], (DEPTH, FOX_H), f32),
        'fox_norm_w': gain(ks[5], (DEPTH, FOX_W)),
        'hg_lb_raw': normal(ks[6], (DEPTH, HG_W), 0.1),
        'hg_norm_w': gain(ks[7], (DEPTH, HG_W)),
        'm2_conv_w': normal(ks[8], (DEPTH, M2_CONV, M2_CONV_CH), M2_CONV ** -0.5),
        'm2_conv_b': normal(ks[9], (DEPTH, M2_CONV_CH), 0.02),
        'm2_dt_bias': dt0 + jnp.log(-jnp.expm1(-dt0)),
        'm2_a_log': jnp.log(jax.random.uniform(ks[11], (DEPTH, M2_H), f32, minval=1.0, maxval=16.0)),
        'm2_d': 1.0 + 0.1 * jax.random.normal(ks[12], (DEPTH, M2_H), f32),
        'm2_norm_w': gain(ks[13], (DEPTH, M2_W)),
        'w_out': normal(ks[14], (DEPTH, D_MIX, D_MODEL), D_MIX ** -0.5),
        'xa_norm_w': gain(ks[15], (DEPTH, D_MODEL)),
        'xa_mem_norm_w': gain(ks[16], (DEPTH, D_MODEL)),
        'xa_w_q': normal(ks[17], (DEPTH, D_MODEL, XA_H * XA_D), D_MODEL ** -0.5),
        'xa_w_kv': normal(ks[18], (DEPTH, D_MODEL, 2 * XA_H * XA_D), D_MODEL ** -0.5),
        'xa_w_o': normal(ks[19], (DEPTH, XA_H * XA_D, D_MODEL), (XA_H * XA_D) ** -0.5),
        'ffn_norm_w': gain(ks[20], (DEPTH, D_MODEL)),
        'ffn_w1': normal(ks[21], (N_DENSE, D_MODEL, 2 * D_FF), D_MODEL ** -0.5),
        'ffn_w2': normal(ks[22], (N_DENSE, D_FF, D_MODEL), D_FF ** -0.5),
        'moe_router': normal(ks[23], (N_MOE, D_MODEL, N_EXPERTS), D_MODEL ** -0.5),
        'moe_w1': normal(ks[24], (N_MOE, N_EXPERTS, D_MODEL, 2 * D_FF), D_MODEL ** -0.5),
        'moe_w2': normal(ks[25], (N_MOE, N_EXPERTS, D_FF, D_MODEL), D_FF ** -0.5),
        'final_norm_w': gain(ks[26], (D_MODEL,)),
    }


def reference(x, mem, mix_norm_w, w_in, fox_f_bias, fox_norm_w, hg_lb_raw, hg_norm_w, m2_conv_w, m2_conv_b, m2_dt_bias, m2_a_log, m2_d, m2_norm_w, w_out, xa_norm_w, xa_mem_norm_w, xa_w_q, xa_w_kv, xa_w_o, ffn_norm_w, ffn_w1, ffn_w2, moe_router, moe_w1, moe_w2, final_norm_w):
    lb_p = jax.nn.softmax(hg_lb_raw.astype(jnp.float32), axis=0)
    hg_lb = jnp.cumsum(lb_p, axis=0) - lb_p[0]
    h = x
    for layer in range(DEPTH):
        h = h + parallel_mixer(h, mix_norm_w[layer], w_in[layer], fox_f_bias[layer], fox_norm_w[layer], hg_lb[layer], hg_norm_w[layer], m2_conv_w[layer], m2_conv_b[layer], m2_dt_bias[layer], m2_a_log[layer], m2_d[layer], m2_norm_w[layer], w_out[layer])
        h = h + memory_cross_attention(h, mem, xa_norm_w[layer], xa_mem_norm_w[layer], xa_w_q[layer], xa_w_kv[layer], xa_w_o[layer])
        hn = rms_norm(h, ffn_norm_w[layer])
        if layer % 2 == 0:
            h = h + swiglu(hn, ffn_w1[layer // 2], ffn_w2[layer // 2])
        else:
            h = h + moe_ffn(hn, moe_router[layer // 2], moe_w1[layer // 2], moe_w2[layer // 2])
    return rms_norm(h, final_norm_w)
```

```python
import functools
import math

import jax
import jax.numpy as jnp
from jax import lax
from jax.experimental import pallas as pl
from jax.experimental.pallas import tpu as pltpu

F32 = jnp.float32
BF16 = jnp.bfloat16

EPS = 1e-6
MASK_VALUE = -1e9
LB_FLOOR = 1e-30
HEAD_DIM = 64
N_FOX_HEADS = 4
N_HG_HEADS = 4
M2_HEADS = 8
M2_GROUPS = 2
M2_STATE = 128
M2_CONV = 4
XA_HEADS = 4
N_EXPERTS = 8

LANES = 128
SUBLANES = 8
VMEM_LIMIT_BYTES = 56 * 1024 * 1024

ROW_BLOCK = 512
SEQ_BLOCK = 512
SCAN_CHUNK = 128
SUB_BLOCK = 16
MOE_ROWS = 512


def _cparams(*sem):
    return pltpu.CompilerParams(dimension_semantics=sem, vmem_limit_bytes=VMEM_LIMIT_BYTES)


def _split3(x):
    x1 = x.astype(BF16)
    r1 = x - x1.astype(F32)
    x2 = r1.astype(BF16)
    x3 = (r1 - x2.astype(F32)).astype(BF16)
    return x1, x2, x3


def _dot(a, b):
    return jnp.dot(a, b, preferred_element_type=F32)


def _dot_nt(a, b):
    return lax.dot_general(a, b, (((1,), (1,)), ((), ())), preferred_element_type=F32)


def _exact_left(mat01, x):
    x1, x2, x3 = _split3(x)
    return _dot(mat01, x1) + _dot(mat01, x2) + _dot(mat01, x3)


def _exact_right(x, mat01):
    x1, x2, x3 = _split3(x)
    return _dot(x1, mat01) + _dot(x2, mat01) + _dot(x3, mat01)


def _rms(x, w):
    return x * lax.rsqrt(jnp.mean(x * x, axis=-1, keepdims=True) + EPS) * w


def _silu(x):
    return x * jax.nn.sigmoid(x)


def _log_sigmoid(x):
    return jnp.minimum(x, 0.0) - jnp.log1p(jnp.exp(-jnp.abs(x)))


def _softplus(x):
    return jnp.maximum(x, 0.0) + jnp.log1p(jnp.exp(-jnp.abs(x)))


def _masked_exp(x, mask):
    return jnp.where(mask, jnp.exp(jnp.where(mask, x, 0.0)), 0.0)


def _in_proj_kernel(x_ref, nw_ref, w_ref, fox_ref, small_ref, hg_ref, z_ref, xbc_ref, *, splits):
    xn = _rms(x_ref[...], nw_ref[...]).astype(BF16)
    outs = (fox_ref, small_ref, hg_ref, z_ref, xbc_ref)
    off = 0
    for o_ref, width in zip(outs, splits):
        o_ref[...] = _dot(xn, w_ref[:, off:off + width]).astype(o_ref.dtype)
        off += width


def _in_proj(h, nw, w_cat, splits):
    t, d = h.shape
    out_dtypes = (BF16, F32, F32, F32, F32)
    return pl.pallas_call(
        functools.partial(_in_proj_kernel, splits=splits),
        grid=(t // ROW_BLOCK,),
        in_specs=[
            pl.BlockSpec((ROW_BLOCK, d), lambda i: (i, 0)),
            pl.BlockSpec((1, d), lambda i: (0, 0)),
            pl.BlockSpec(w_cat.shape, lambda i: (0, 0)),
        ],
        out_specs=[pl.BlockSpec((ROW_BLOCK, w), lambda i: (i, 0)) for w in splits],
        out_shape=[jax.ShapeDtypeStruct((t, w), dt) for w, dt in zip(splits, out_dtypes)],
        compiler_params=_cparams("parallel"),
        name="in_proj",
    )(h, nw, w_cat)


def _norm_proj_kernel(x_ref, nw_ref, w_ref, o_ref):
    xn = _rms(x_ref[...], nw_ref[...]).astype(BF16)
    o_ref[...] = _dot(xn, w_ref[...]).astype(o_ref.dtype)


def _norm_proj(x, nw, w, out_dtype, rows):
    t, d = x.shape
    n = w.shape[1]
    return pl.pallas_call(
        _norm_proj_kernel,
        grid=(t // rows,),
        in_specs=[
            pl.BlockSpec((rows, d), lambda i: (i, 0)),
            pl.BlockSpec((1, d), lambda i: (0, 0)),
            pl.BlockSpec(w.shape, lambda i: (0, 0)),
        ],
        out_specs=pl.BlockSpec((rows, n), lambda i: (i, 0)),
        out_shape=jax.ShapeDtypeStruct((t, n), out_dtype),
        compiler_params=_cparams("parallel"),
        name="norm_proj",
    )(x, nw, w)


def _fox_prep_kernel(qkv_ref, small_ref, bias_ref, tri_ref, q_ref, k_ref, v_ref, carry_ref):
    @pl.when(pl.program_id(1) == 0)
    def _():
        carry_ref[...] = jnp.zeros_like(carry_ref)

    rows = qkv_ref.shape[0]
    log_f = _log_sigmoid(small_ref[...] + bias_ref[...])
    c = _exact_left(tri_ref[...], log_f) + carry_ref[...]
    carry_ref[...] = c[rows - 1:rows, :]

    lane = lax.broadcasted_iota(jnp.int32, (rows, LANES), 1)
    width = N_FOX_HEADS * HEAD_DIM
    scale = HEAD_DIM ** -0.5
    for hd in range(N_FOX_HEADS):
        tile = (hd * HEAD_DIM) // LANES
        ch = c[:, hd:hd + 1]
        c1 = ch.astype(BF16).astype(F32)
        r1 = ch - c1
        c2 = r1.astype(BF16).astype(F32)
        c3 = r1 - c2

        def head_tile(base):
            x = qkv_ref[:, base + tile * LANES: base + (tile + 1) * LANES].astype(F32)
            if (hd * HEAD_DIM) % LANES:
                x = pltpu.roll(x, LANES - (hd * HEAD_DIM) % LANES, axis=1)
            return x

        def augment(x, first, second):
            out = jnp.where(lane < HEAD_DIM, x, 0.0)
            for j, val in enumerate(first + second):
                out = jnp.where(lane == HEAD_DIM + j, val, out)
            return out.astype(BF16)

        ones = (1.0, 1.0, 1.0)
        q_ref[hd] = augment(head_tile(0) * scale, ones, (c1, c2, c3))
        k_ref[hd] = augment(head_tile(width), (-c1, -c2, -c3), ones)
        v_ref[hd] = jnp.where(lane < HEAD_DIM, head_tile(2 * width), 0.0).astype(BF16)


def _fox_prep(qkv, small, bias_row, tri, batch, seq):
    t = qkv.shape[0]
    nb = seq // SEQ_BLOCK
    row = lambda b, c: (b * nb + c, 0)
    head_spec = pl.BlockSpec((N_FOX_HEADS, SEQ_BLOCK, LANES), lambda b, c: (0, b * nb + c, 0))
    head_shape = jax.ShapeDtypeStruct((N_FOX_HEADS, t, LANES), BF16)
    return pl.pallas_call(
        _fox_prep_kernel,
        grid=(batch, nb),
        in_specs=[
            pl.BlockSpec((SEQ_BLOCK, qkv.shape[1]), row),
            pl.BlockSpec((SEQ_BLOCK, LANES), row),
            pl.BlockSpec((1, LANES), lambda b, c: (0, 0)),
            pl.BlockSpec((SEQ_BLOCK, SEQ_BLOCK), lambda b, c: (0, 0)),
        ],
        out_specs=[head_spec, head_spec, head_spec],
        out_shape=[head_shape, head_shape, head_shape],
        scratch_shapes=[pltpu.VMEM((1, LANES), F32)],
        compiler_params=_cparams("parallel", "arbitrary"),
        name="fox_prep",
    )(qkv, small, bias_row, tri)


def _fox_attn_kernel(q_ref, k_ref, v_ref, nw_ref, o_ref):
    i = pl.program_id(1)
    tq = q_ref.shape[1]
    row = lax.broadcasted_iota(jnp.int32, (tq, tq), 0)
    col = lax.broadcasted_iota(jnp.int32, (tq, tq), 1)
    causal = col <= row

    def step(hd, j, carry, masked):
        m, l, acc = carry
        start = pl.multiple_of(j * tq, tq)
        s = _dot_nt(q_ref[hd], k_ref[hd, pl.ds(start, tq), :])
        if masked:
            s = jnp.where(causal, s, MASK_VALUE)
        m_new = jnp.maximum(m, jnp.max(s, axis=-1, keepdims=True))
        alpha = jnp.exp(m - m_new)
        p = jnp.exp(s - m_new)
        l = alpha * l + jnp.sum(p, axis=-1, keepdims=True)
        acc = alpha * acc + _dot(p.astype(BF16), v_ref[hd, pl.ds(start, tq), :])
        return m_new, l, acc

    tiles = []
    for hd in range(N_FOX_HEADS):
        init = (jnp.full((tq, 1), -jnp.inf, F32), jnp.zeros((tq, 1), F32), jnp.zeros((tq, LANES), F32))
        carry = lax.fori_loop(0, i, lambda j, c, hd=hd: step(hd, j, c, False), init)
        _, l, acc = step(hd, i, carry, True)
        o = acc / l
        ms = jnp.sum(o * o, axis=-1, keepdims=True) * (1.0 / HEAD_DIM)
        tiles.append(o * lax.rsqrt(ms + EPS) * nw_ref[hd:hd + 1, :])
    pairs = [tiles[a] + pltpu.roll(tiles[a + 1], HEAD_DIM, axis=1) for a in range(0, N_FOX_HEADS, 2)]
    o_ref[...] = jnp.concatenate(pairs, axis=1).astype(o_ref.dtype)


def _fox_attn(qa, ka, va, nw_heads, batch, seq):
    t = qa.shape[1]
    nq = seq // SEQ_BLOCK
    seq_spec = pl.BlockSpec((N_FOX_HEADS, seq, LANES), lambda b, i: (0, b, 0))
    return pl.pallas_call(
        _fox_attn_kernel,
        grid=(batch, nq),
        in_specs=[
            pl.BlockSpec((N_FOX_HEADS, SEQ_BLOCK, LANES), lambda b, i: (0, b * nq + i, 0)),
            seq_spec,
            seq_spec,
            pl.BlockSpec((SUBLANES, LANES), lambda b, i: (0, 0)),
        ],
        out_specs=pl.BlockSpec((SEQ_BLOCK, N_FOX_HEADS * HEAD_DIM), lambda b, i: (b * nq + i, 0)),
        out_shape=jax.ShapeDtypeStruct((t, N_FOX_HEADS * HEAD_DIM), BF16),
        compiler_params=_cparams("parallel", "arbitrary"),
        name="fox_attn",
    )(qa, ka, va, nw_heads)


def _hgrn2_kernel(hq_ref, hf_ref, hi_ref, hg_ref, par_ref, tri_ref, seg_ref, o_ref, st_ref):
    @pl.when(pl.program_id(1) == 0)
    def _():
        st_ref[...] = jnp.zeros_like(st_ref)

    rows, width = hq_ref.shape
    log_lb, log1m_lb, one_m_lb, nw = (par_ref[r:r + 1, :] for r in range(4))
    f_raw = hf_ref[...]
    q = _silu(hq_ref[...]) * (HEAD_DIM ** -0.5)
    b = log1m_lb + _log_sigmoid(f_raw)
    g = jnp.maximum(log_lb, b) + jnp.log1p(jnp.exp(-jnp.abs(log_lb - b)))
    k = one_m_lb * jax.nn.sigmoid(-f_raw)
    v = hi_ref[...]
    cum = _exact_left(tri_ref[...], g)

    lane = lax.broadcasted_iota(jnp.int32, (1, width), 1)
    head_masks = [(lane >= hd * HEAD_DIM) & (lane < (hd + 1) * HEAD_DIM) for hd in range(N_HG_HEADS)]
    seg = seg_ref[...]
    k_b = k.astype(BF16)
    v_b = v.astype(BF16)

    st = st_ref[...]
    o_state = _dot_nt((q * jnp.exp(cum)).astype(BF16), st.astype(BF16))

    t_idx = lax.broadcasted_iota(jnp.int32, (SUB_BLOCK, 1), 0)
    blocks = []
    for i in range(rows // SUB_BLOCK):
        r0 = i * SUB_BLOCK
        q_i = q[r0:r0 + SUB_BLOCK]
        cum_i = cum[r0:r0 + SUB_BLOCK]
        o_i = o_state[r0:r0 + SUB_BLOCK]
        if i > 0:
            ref = cum[r0 - 1:r0]
            qs = q_i * jnp.exp(cum_i - ref)
            ks = (k[:r0] * jnp.exp(ref - cum[:r0])).astype(BF16)
            q4 = jnp.concatenate([jnp.where(mk, qs, 0.0) for mk in head_masks], axis=0).astype(BF16)
            p4 = _dot(_dot_nt(q4, ks).astype(BF16), v_b[:r0])
            for hd, mk in enumerate(head_masks):
                o_i = o_i + jnp.where(mk, p4[hd * SUB_BLOCK:(hd + 1) * SUB_BLOCK], 0.0)
        terms = []
        for s in range(SUB_BLOCK):
            keep = t_idx >= s
            e = _masked_exp(cum_i - cum[r0 + s:r0 + s + 1], keep)
            terms.append((q_i * e * k[r0 + s:r0 + s + 1]).astype(BF16))
        sums = _dot(jnp.concatenate(terms, axis=0), seg)
        for s in range(SUB_BLOCK):
            o_i = o_i + sums[s * SUB_BLOCK:(s + 1) * SUB_BLOCK] * v[r0 + s:r0 + s + 1]
        blocks.append(o_i)
    o = jnp.concatenate(blocks, axis=0)

    last = cum[rows - 1:rows]
    kd = (k * jnp.exp(last - cum)).astype(BF16)
    upd = _dot(v.T.astype(BF16), kd)
    st_ref[...] = st * jnp.exp(last) + jnp.where(seg > 0, upd, 0.0)

    sq = o * o
    s1 = sq.astype(BF16)
    s2 = (sq - s1.astype(F32)).astype(BF16)
    ms = (_dot(s1, seg) + _dot(s2, seg)) * (1.0 / HEAD_DIM)
    o_ref[...] = (o * lax.rsqrt(ms + EPS) * nw * _silu(hg_ref[...])).astype(o_ref.dtype)


def _hgrn2(hg, params, tri, seg, batch, seq):
    t = hg.shape[0]
    width = N_HG_HEADS * HEAD_DIM
    nc = seq // SCAN_CHUNK
    col = lambda c: pl.BlockSpec((SCAN_CHUNK, width), lambda b, j, c=c: (b * nc + j, c))
    return pl.pallas_call(
        _hgrn2_kernel,
        grid=(batch, nc),
        in_specs=[
            col(0), col(1), col(2), col(3),
            pl.BlockSpec(params.shape, lambda b, j: (0, 0)),
            pl.BlockSpec(tri.shape, lambda b, j: (0, 0)),
            pl.BlockSpec(seg.shape, lambda b, j: (0, 0)),
        ],
        out_specs=pl.BlockSpec((SCAN_CHUNK, width), lambda b, j: (b * nc + j, 0)),
        out_shape=jax.ShapeDtypeStruct((t, width), BF16),
        scratch_shapes=[pltpu.VMEM((width, width), F32)],
        compiler_params=_cparams("parallel", "arbitrary"),
        name="hgrn2",
    )(hg, hg, hg, hg, params, tri, seg)


def _ssd_kernel(z_ref, xbc_ref, small_ref, cw_ref, cb_ref, hp_ref, sp_ref, nw_ref, tri_ref, exp_ref,
                o_ref, buf_ref, st_ref):
    rows = z_ref.shape[0]
    width = z_ref.shape[1]
    halo = SUBLANES

    @pl.when(pl.program_id(1) == 0)
    def _():
        buf_ref[0:halo, :] = jnp.zeros((halo, buf_ref.shape[1]), F32)
        st_ref[...] = jnp.zeros_like(st_ref)

    buf_ref[halo:halo + rows, :] = xbc_ref[...]
    conv = cb_ref[...]
    for i in range(M2_CONV):
        off = halo - (M2_CONV - 1) + i
        conv = conv + cw_ref[i:i + 1, :] * buf_ref[off:off + rows, :]
    tail = buf_ref[rows:rows + halo, :]
    buf_ref[0:halo, :] = tail
    xc = _silu(conv)
    xs = xc[:, :width]
    gn = M2_GROUPS * M2_STATE
    bm = xc[:, width:width + gn]
    cm = xc[:, width + gn:width + 2 * gn]

    dt_bias_x, a_log_x, d_x = (hp_ref[r:r + 1, :] for r in range(3))
    dt_bias_c, a_log_c = (sp_ref[r:r + 1, :] for r in range(2))
    small = small_ref[...]
    tri = tri_ref[...]
    dt_x = _softplus(_exact_right(small, exp_ref[...]) + dt_bias_x)
    a_cum_x = _exact_left(tri, dt_x * -jnp.exp(a_log_x))
    a_cum_c = _exact_left(tri, _softplus(small + dt_bias_c) * -jnp.exp(a_log_c))
    a_cum_t = a_cum_c.T

    row = lax.broadcasted_iota(jnp.int32, (rows, rows), 0)
    col = lax.broadcasted_iota(jnp.int32, (rows, rows), 1)
    causal = col <= row
    lane = lax.broadcasted_iota(jnp.int32, (1, LANES), 1)
    low = lane < HEAD_DIM

    xdt = xs * dt_x
    a_last = a_cum_x[rows - 1:rows]
    xd = (xdt * jnp.exp(a_last - a_cum_x)).astype(BF16)
    decay_out = jnp.exp(a_cum_x)
    st = st_ref[...]
    heads_per_group = M2_HEADS // M2_GROUPS
    gw = heads_per_group * HEAD_DIM
    y_tiles = []
    new_states = []
    for g in range(M2_GROUPS):
        b_g = bm[:, g * M2_STATE:(g + 1) * M2_STATE]
        c_g = cm[:, g * M2_STATE:(g + 1) * M2_STATE].astype(BF16)
        cb = _dot_nt(c_g, b_g.astype(BF16))
        y_off = _dot(c_g, st[:, g * gw:(g + 1) * gw].astype(BF16)) * decay_out[:, g * gw:(g + 1) * gw]
        new_states.append(_dot(b_g.T.astype(BF16), xd[:, g * gw:(g + 1) * gw]))
        for pair in range(heads_per_group // 2):
            tile = g * (heads_per_group // 2) + pair
            x_tile = xdt[:, tile * LANES:(tile + 1) * LANES]
            y_pair = y_off[:, pair * LANES:(pair + 1) * LANES]
            for half in range(2):
                hd = 2 * tile + half
                seg = a_cum_c[:, SUBLANES + hd:SUBLANES + hd + 1] - a_cum_t[SUBLANES + hd:SUBLANES + hd + 1, :]
                m_h = (cb * _masked_exp(seg, causal)).astype(BF16)
                x_h = jnp.where(low if half == 0 else ~low, x_tile, 0.0).astype(BF16)
                y_pair = y_pair + _dot(m_h, x_h)
            y_tiles.append(y_pair)
    st_ref[...] = st * jnp.exp(a_last) + jnp.concatenate(new_states, axis=1)

    y = jnp.concatenate(y_tiles, axis=1) + d_x * xs
    y = y * _silu(z_ref[...])
    nw = nw_ref[...]
    outs = []
    for g in range(M2_GROUPS):
        outs.append(_rms(y[:, g * gw:(g + 1) * gw], nw[:, g * gw:(g + 1) * gw]))
    o_ref[...] = jnp.concatenate(outs, axis=1).astype(o_ref.dtype)


def _ssd(z, xbc, small, conv_w, conv_b, head_rows, small_rows, norm_w, tri, expand, batch, seq):
    t, width = z.shape
    ch = xbc.shape[1]
    nc = seq // SCAN_CHUNK
    row = lambda b, j: (b * nc + j, 0)
    const = lambda a: pl.BlockSpec(a.shape, lambda b, j: (0, 0))
    return pl.pallas_call(
        _ssd_kernel,
        grid=(batch, nc),
        in_specs=[
            pl.BlockSpec((SCAN_CHUNK, width), row),
            pl.BlockSpec((SCAN_CHUNK, ch), row),
            pl.BlockSpec((SCAN_CHUNK, LANES), row),
            const(conv_w), const(conv_b), const(head_rows), const(small_rows), const(norm_w),
            const(tri), const(expand),
        ],
        out_specs=pl.BlockSpec((SCAN_CHUNK, width), row),
        out_shape=jax.ShapeDtypeStruct((t, width), BF16),
        scratch_shapes=[
            pltpu.VMEM((SCAN_CHUNK + SUBLANES, ch), F32),
            pltpu.VMEM((M2_STATE, width), F32),
        ],
        compiler_params=_cparams("parallel", "arbitrary"),
        name="ssd",
    )(z, xbc, small, conv_w, conv_b, head_rows, small_rows, norm_w, tri, expand)


def _out_proj_kernel(h_ref, a_ref, b_ref, c_ref, w_ref, o_ref):
    wa = a_ref.shape[1]
    wb = b_ref.shape[1]
    acc = h_ref[...] + _dot(a_ref[...], w_ref[0:wa, :])
    acc = acc + _dot(b_ref[...], w_ref[wa:wa + wb, :])
    o_ref[...] = acc + _dot(c_ref[...], w_ref[wa + wb:, :])


def _out_proj(h, a, b, c, w):
    t, d = h.shape
    row = lambda x: pl.BlockSpec((ROW_BLOCK, x.shape[1]), lambda i: (i, 0))
    return pl.pallas_call(
        _out_proj_kernel,
        grid=(t // ROW_BLOCK,),
        in_specs=[row(h), row(a), row(b), row(c), pl.BlockSpec(w.shape, lambda i: (0, 0))],
        out_specs=row(h),
        out_shape=jax.ShapeDtypeStruct((t, d), F32),
        compiler_params=_cparams("parallel"),
        name="out_proj",
    )(h, a, b, c, w)


def _xattn_kernel(h_ref, nw_ref, wq_ref, kv_ref, wo_ref, o_ref):
    h = h_ref[...]
    d = h.shape[1]
    hd = d // XA_HEADS
    q = (_dot(_rms(h, nw_ref[...]).astype(BF16), wq_ref[...]) * (hd ** -0.5)).astype(BF16)
    outs = []
    for a in range(XA_HEADS):
        k_a = kv_ref[:, a * hd:(a + 1) * hd]
        v_a = kv_ref[:, d + a * hd:d + (a + 1) * hd]
        s = _dot_nt(q[:, a * hd:(a + 1) * hd], k_a)
        p = jnp.exp(s - jnp.max(s, axis=-1, keepdims=True))
        p = p / jnp.sum(p, axis=-1, keepdims=True)
        outs.append(_dot(p.astype(BF16), v_a).astype(BF16))
    o_ref[...] = h + _dot(jnp.concatenate(outs, axis=1), wo_ref[...])


def _xattn(h, nw, wq, kv, wo, batch, seq):
    t, d = h.shape
    n_mem = kv.shape[0] // batch
    nb = seq // ROW_BLOCK
    row = lambda b, i: (b * nb + i, 0)
    const = lambda a: pl.BlockSpec(a.shape, lambda b, i: (0, 0))
    return pl.pallas_call(
        _xattn_kernel,
        grid=(batch, nb),
        in_specs=[
            pl.BlockSpec((ROW_BLOCK, d), row),
            const(nw), const(wq),
            pl.BlockSpec((n_mem, kv.shape[1]), lambda b, i: (b, 0)),
            const(wo),
        ],
        out_specs=pl.BlockSpec((ROW_BLOCK, d), row),
        out_shape=jax.ShapeDtypeStruct((t, d), F32),
        compiler_params=_cparams("parallel", "parallel"),
        name="xattn",
    )(h, nw, wq, kv, wo)


def _ffn_tile(d_ff):
    for cand in (1408, 1024, 768, 512, 256, 128):
        if d_ff % cand == 0:
            return cand
    raise ValueError(d_ff)


def _ffn_kernel(h_ref, nw_ref, wg_ref, wu_ref, w2_ref, o_ref, xn_ref, acc_ref):
    j = pl.program_id(1)

    @pl.when(j == 0)
    def _():
        xn_ref[...] = _rms(h_ref[...], nw_ref[...]).astype(BF16)
        acc_ref[...] = h_ref[...]

    xn = xn_ref[...]
    a = (_silu(_dot(xn, wg_ref[...])) * _dot(xn, wu_ref[...])).astype(BF16)
    acc_ref[...] += _dot(a, w2_ref[...])

    @pl.when(j == pl.num_programs(1) - 1)
    def _():
        o_ref[...] = acc_ref[...]


def _ffn(h, nw, w1, w2):
    t, d = h.shape
    d_ff = w2.shape[0]
    tf = _ffn_tile(d_ff)
    nf = d_ff // tf
    return pl.pallas_call(
        _ffn_kernel,
        grid=(t // ROW_BLOCK, nf),
        in_specs=[
            pl.BlockSpec((ROW_BLOCK, d), lambda i, j: (i, 0)),
            pl.BlockSpec((1, d), lambda i, j: (0, 0)),
            pl.BlockSpec((d, tf), lambda i, j: (0, j)),
            pl.BlockSpec((d, tf), lambda i, j: (0, j + nf)),
            pl.BlockSpec((tf, d), lambda i, j: (j, 0)),
        ],
        out_specs=pl.BlockSpec((ROW_BLOCK, d), lambda i, j: (i, 0)),
        out_shape=jax.ShapeDtypeStruct((t, d), F32),
        scratch_shapes=[pltpu.VMEM((ROW_BLOCK, d), BF16), pltpu.VMEM((ROW_BLOCK, d), F32)],
        compiler_params=_cparams("parallel", "arbitrary"),
        name="ffn",
    )(h, nw, w1, w1, w2)


def _router_kernel(h_ref, nw_ref, wr_ref, xn_ref, idx_ref, gate_ref):
    xn = _rms(h_ref[...], nw_ref[...])
    xn_ref[...] = xn
    x1 = xn.astype(BF16)
    x2 = (xn - x1.astype(F32)).astype(BF16)
    w1, w2, w3 = wr_ref[0], wr_ref[1], wr_ref[2]
    logits = _dot(x1, w1) + _dot(x1, w2) + _dot(x2, w1) + _dot(x1, w3) + _dot(x2, w2)
    lane = lax.broadcasted_iota(jnp.int32, logits.shape, 1)
    logits = jnp.where(lane < N_EXPERTS, logits, -jnp.inf)
    m1 = jnp.max(logits, axis=-1, keepdims=True)
    i1 = jnp.min(jnp.where(logits == m1, lane, LANES), axis=-1, keepdims=True)
    rest = jnp.where(lane == i1, -jnp.inf, logits)
    m2 = jnp.max(rest, axis=-1, keepdims=True)
    i2 = jnp.min(jnp.where(rest == m2, lane, LANES), axis=-1, keepdims=True)
    e2 = jnp.exp(m2 - m1)
    denom = 1.0 + e2
    idx_ref[...] = jnp.where(lane == 0, i1, jnp.where(lane == 1, i2, 0))
    gate_ref[...] = jnp.where(lane == 0, 1.0 / denom, jnp.where(lane == 1, e2 / denom, 0.0))


def _router(h, nw, wr3):
    t, d = h.shape
    row = lambda w: pl.BlockSpec((ROW_BLOCK, w), lambda i: (i, 0))
    return pl.pallas_call(
        _router_kernel,
        grid=(t // ROW_BLOCK,),
        in_specs=[row(d), pl.BlockSpec((1, d), lambda i: (0, 0)), pl.BlockSpec(wr3.shape, lambda i: (0, 0, 0))],
        out_specs=[row(d), row(LANES), row(LANES)],
        out_shape=[
            jax.ShapeDtypeStruct((t, d), F32),
            jax.ShapeDtypeStruct((t, LANES), jnp.int32),
            jax.ShapeDtypeStruct((t, LANES), F32),
        ],
        compiler_params=_cparams("parallel"),
        name="router",
    )(h, nw, wr3)


def _expert_kernel(be_ref, nused_ref, x_ref, g_ref, wg_ref, wu_ref, w2_ref, o_ref, acc_ref):
    i = pl.program_id(0)
    j = pl.program_id(1)
    live = i < nused_ref[0]

    @pl.when(j == 0)
    def _():
        acc_ref[...] = jnp.zeros_like(acc_ref)

    @pl.when(live)
    def _():
        x = x_ref[...]
        a = (_silu(_dot(x, wg_ref[0])) * _dot(x, wu_ref[0])).astype(BF16)
        acc_ref[...] += _dot(a, w2_ref[0])

    @pl.when(j == pl.num_programs(1) - 1)
    def _():
        o_ref[...] = acc_ref[...] * g_ref[...]


def _experts(blk_expert, n_used, xs, gates, w1, w2):
    cap, d = xs.shape
    d_ff = w2.shape[1]
    tf = _ffn_tile(d_ff)
    nf = d_ff // tf
    grid_spec = pltpu.PrefetchScalarGridSpec(
        num_scalar_prefetch=2,
        grid=(cap // MOE_ROWS, nf),
        in_specs=[
            pl.BlockSpec((MOE_ROWS, d), lambda i, j, be, nu: (i, 0)),
            pl.BlockSpec((MOE_ROWS, 1), lambda i, j, be, nu: (i, 0)),
            pl.BlockSpec((1, d, tf), lambda i, j, be, nu: (be[i], 0, j)),
            pl.BlockSpec((1, d, tf), lambda i, j, be, nu: (be[i], 0, j + nf)),
            pl.BlockSpec((1, tf, d), lambda i, j, be, nu: (be[i], j, 0)),
        ],
        out_specs=pl.BlockSpec((MOE_ROWS, d), lambda i, j, be, nu: (i, 0)),
        scratch_shapes=[pltpu.VMEM((MOE_ROWS, d), F32)],
    )
    return pl.pallas_call(
        _expert_kernel,
        grid_spec=grid_spec,
        out_shape=jax.ShapeDtypeStruct((cap, d), F32),
        compiler_params=_cparams("parallel", "arbitrary"),
        name="experts",
    )(blk_expert, n_used, xs, gates, w1, w1, w2)


def _final_norm_kernel(h_ref, nw_ref, o_ref):
    o_ref[...] = _rms(h_ref[...], nw_ref[...])


def _final_norm(h, nw):
    t, d = h.shape
    return pl.pallas_call(
        _final_norm_kernel,
        grid=(t // ROW_BLOCK,),
        in_specs=[pl.BlockSpec((ROW_BLOCK, d), lambda i: (i, 0)), pl.BlockSpec((1, d), lambda i: (0, 0))],
        out_specs=pl.BlockSpec((ROW_BLOCK, d), lambda i: (i, 0)),
        out_shape=jax.ShapeDtypeStruct((t, d), F32),
        compiler_params=_cparams("parallel"),
        name="final_norm",
    )(h, nw)


def _moe(h, nw, router_w, w1, w2):
    t, d = h.shape
    r1, r2, r3 = _split3(jnp.pad(router_w, ((0, 0), (0, LANES - N_EXPERTS))))
    xn, idx, gate = _router(h, nw, jnp.stack([r1, r2, r3]))
    top_idx = idx[:, :2]
    gates = gate[:, :2]
    n_assign = 2 * t
    e_flat = top_idx.reshape(-1)
    g_flat = gates.reshape(-1)
    t_flat = jnp.arange(n_assign, dtype=jnp.int32) // 2
    order = jnp.argsort(e_flat)
    e_sorted = e_flat[order]
    counts = jnp.bincount(e_flat, length=N_EXPERTS).astype(jnp.int32)
    starts = jnp.cumsum(counts) - counts
    padded = (counts + MOE_ROWS - 1) // MOE_ROWS * MOE_ROWS
    pends = jnp.cumsum(padded)
    pstarts = pends - padded
    dest = pstarts[e_sorted] + (jnp.arange(n_assign, dtype=jnp.int32) - starts[e_sorted])
    cap = (n_assign // MOE_ROWS + N_EXPERTS) * MOE_ROWS
    slot_tok = jnp.zeros((cap,), jnp.int32).at[dest].set(t_flat[order])
    slot_gate = jnp.zeros((cap,), F32).at[dest].set(g_flat[order])
    n_blk = cap // MOE_ROWS
    blk_expert = jnp.minimum(
        jnp.searchsorted(pends, jnp.arange(n_blk, dtype=jnp.int32) * MOE_ROWS, side="right"), N_EXPERTS - 1
    ).astype(jnp.int32)
    n_used = (pends[-1] // MOE_ROWS).astype(jnp.int32).reshape(1)
    xs = xn[slot_tok].astype(BF16)
    yb = _experts(blk_expert, n_used, xs, slot_gate[:, None], w1, w2)
    return h.at[slot_tok].add(yb)


def _tri(n):
    return jnp.tril(jnp.ones((n, n), F32)).astype(BF16)


def kernel(x, mem, mix_norm_w, w_in, fox_f_bias, fox_norm_w, hg_lb_raw, hg_norm_w, m2_conv_w, m2_conv_b, m2_dt_bias, m2_a_log, m2_d, m2_norm_w, w_out, xa_norm_w, xa_mem_norm_w, xa_w_q, xa_w_kv, xa_w_o, ffn_norm_w, ffn_w1, ffn_w2, moe_router, moe_w1, moe_w2, final_norm_w):
    batch, seq, d = x.shape
    depth = w_in.shape[0]
    t = batch * seq
    fox_w = N_FOX_HEADS * HEAD_DIM
    hg_w = N_HG_HEADS * HEAD_DIM
    m2_w = M2_HEADS * HEAD_DIM
    conv_ch = m2_w + 2 * M2_GROUPS * M2_STATE
    in_splits = (fox_w, fox_w, fox_w, N_FOX_HEADS, hg_w, hg_w, hg_w, hg_w, m2_w, conv_ch, M2_HEADS)
    offs = [0]
    for s in in_splits:
        offs.append(offs[-1] + s)

    lb_p = jax.nn.softmax(hg_lb_raw.astype(F32), axis=0)
    hg_lb = jnp.cumsum(lb_p, axis=0) - lb_p[0]

    tri_seq = _tri(SEQ_BLOCK)
    tri_chunk = _tri(SCAN_CHUNK)
    head_of_lane = jnp.arange(hg_w) // HEAD_DIM
    seg_ones = (head_of_lane[:, None] == head_of_lane[None, :]).astype(BF16)
    expand = (jnp.arange(LANES)[:, None] - SUBLANES == jnp.arange(m2_w)[None, :] // HEAD_DIM).astype(BF16)

    def pad_lanes(v, offset=0, width=LANES):
        return jnp.zeros((1, width), F32).at[0, offset:offset + v.shape[0]].set(v)

    h = x.reshape(t, d)
    mem2 = mem.reshape(batch * mem.shape[1], d)
    for layer in range(depth):
        w = w_in[layer]
        small_w = jnp.zeros((d, LANES), F32)
        small_w = small_w.at[:, 0:N_FOX_HEADS].set(w[:, offs[3]:offs[4]])
        small_w = small_w.at[:, SUBLANES:SUBLANES + M2_HEADS].set(w[:, offs[10]:offs[11]])
        w_cat = jnp.concatenate(
            [w[:, offs[0]:offs[3]], small_w, w[:, offs[4]:offs[8]], w[:, offs[8]:offs[9]], w[:, offs[9]:offs[10]]],
            axis=1).astype(BF16)
        splits = (3 * fox_w, LANES, 4 * hg_w, m2_w, conv_ch)
        qkv, small, hg, z, xbc = _in_proj(h, mix_norm_w[layer][None, :], w_cat, splits)

        qa, ka, va = _fox_prep(qkv, small, pad_lanes(fox_f_bias[layer]), tri_seq, batch, seq)
        fox_nw = jnp.zeros((SUBLANES, LANES), F32).at[:N_FOX_HEADS, :HEAD_DIM].set(
            fox_norm_w[layer].reshape(N_FOX_HEADS, HEAD_DIM))
        o_fox = _fox_attn(qa, ka, va, fox_nw, batch, seq)

        lb = hg_lb[layer]
        hg_params = jnp.zeros((SUBLANES, hg_w), F32)
        hg_params = hg_params.at[0].set(jnp.log(jnp.maximum(lb, LB_FLOOR)))
        hg_params = hg_params.at[1].set(jnp.log1p(-lb))
        hg_params = hg_params.at[2].set(1.0 - lb)
        hg_params = hg_params.at[3].set(hg_norm_w[layer])
        o_hg = _hgrn2(hg, hg_params, tri_chunk, seg_ones, batch, seq)

        conv_w = jnp.zeros((SUBLANES, conv_ch), F32).at[:M2_CONV].set(m2_conv_w[layer])
        head_rows = jnp.zeros((SUBLANES, m2_w), F32)
        head_rows = head_rows.at[0].set(jnp.repeat(m2_dt_bias[layer], HEAD_DIM))
        head_rows = head_rows.at[1].set(jnp.repeat(m2_a_log[layer], HEAD_DIM))
        head_rows = head_rows.at[2].set(jnp.repeat(m2_d[layer], HEAD_DIM))
        small_rows = jnp.concatenate(
            [pad_lanes(m2_dt_bias[layer], SUBLANES), pad_lanes(m2_a_log[layer], SUBLANES),
             jnp.zeros((SUBLANES - 2, LANES), F32)], axis=0)
        o_m2 = _ssd(z, xbc, small, conv_w, m2_conv_b[layer][None, :], head_rows, small_rows,
                    m2_norm_w[layer][None, :], tri_chunk, expand, batch, seq)

        h = _out_proj(h, o_fox, o_hg, o_m2, w_out[layer].astype(BF16))

        kv = _norm_proj(mem2, xa_mem_norm_w[layer][None, :], xa_w_kv[layer].astype(BF16), BF16, mem.shape[1])
        h = _xattn(h, xa_norm_w[layer][None, :], xa_w_q[layer].astype(BF16), kv, xa_w_o[layer].astype(BF16),
                   batch, seq)

        nw = ffn_norm_w[layer][None, :]
        if layer % 2 == 0:
            h = _ffn(h, nw, ffn_w1[layer // 2].astype(BF16), ffn_w2[layer // 2].astype(BF16))
        else:
            h = _moe(h, nw, moe_router[layer // 2], moe_w1[layer // 2].astype(BF16),
                     moe_w2[layer // 2].astype(BF16))
    return _final_norm(h, final_norm_w[None, :]).reshape(batch, seq, d)
```

```python
import functools
import math

import jax
import jax.numpy as jnp
from jax import lax
from jax.experimental import pallas as pl
from jax.experimental.pallas import tpu as pltpu

F32 = jnp.float32
BF16 = jnp.bfloat16

EPS = 1e-6
MASK_VALUE = -1e9
LB_FLOOR = 1e-30
HEAD_DIM = 64
N_FOX_HEADS = 4
N_HG_HEADS = 4
M2_HEADS = 8
M2_GROUPS = 2
M2_STATE = 128
M2_CONV = 4
XA_HEADS = 4
N_EXPERTS = 8

LANES = 128
SUBLANES = 8
VMEM_LIMIT_BYTES = 56 * 1024 * 1024

ROW_BLOCK = 512
SEQ_BLOCK = 512
SCAN_CHUNK = 128
SUB_BLOCK = 16
MOE_ROWS = 512


def _cparams(*sem):
    return pltpu.CompilerParams(dimension_semantics=sem, vmem_limit_bytes=VMEM_LIMIT_BYTES)


def _split3(x):
    x1 = x.astype(BF16)
    r1 = x - x1.astype(F32)
    x2 = r1.astype(BF16)
    x3 = (r1 - x2.astype(F32)).astype(BF16)
    return x1, x2, x3


def _dot(a, b):
    return jnp.dot(a, b, preferred_element_type=F32)


def _dot_nt(a, b):
    return lax.dot_general(a, b, (((1,), (1,)), ((), ())), preferred_element_type=F32)


def _exact_left(mat01, x):
    x1, x2, x3 = _split3(x)
    return _dot(mat01, x1) + _dot(mat01, x2) + _dot(mat01, x3)


def _exact_right(x, mat01):
    x1, x2, x3 = _split3(x)
    return _dot(x1, mat01) + _dot(x2, mat01) + _dot(x3, mat01)


def _rms(x, w):
    return x * lax.rsqrt(jnp.mean(x * x, axis=-1, keepdims=True) + EPS) * w


def _silu(x):
    return x * jax.nn.sigmoid(x)


def _log_sigmoid(x):
    return jnp.minimum(x, 0.0) - jnp.log1p(jnp.exp(-jnp.abs(x)))


def _softplus(x):
    return jnp.maximum(x, 0.0) + jnp.log1p(jnp.exp(-jnp.abs(x)))


def _masked_exp(x, mask):
    return jnp.where(mask, jnp.exp(jnp.where(mask, x, 0.0)), 0.0)


def _in_proj_kernel(x_ref, nw_ref, w_ref, fox_ref, small_ref, hg_ref, z_ref, xbc_ref, *, splits):
    xn = _rms(x_ref[...], nw_ref[...]).astype(BF16)
    outs = (fox_ref, small_ref, hg_ref, z_ref, xbc_ref)
    off = 0
    for o_ref, width in zip(outs, splits):
        o_ref[...] = _dot(xn, w_ref[:, off:off + width]).astype(o_ref.dtype)
        off += width


def _in_proj(h, nw, w_cat, splits):
    t, d = h.shape
    out_dtypes = (BF16, F32, F32, F32, F32)
    return pl.pallas_call(
        functools.partial(_in_proj_kernel, splits=splits),
        grid=(t // ROW_BLOCK,),
        in_specs=[
            pl.BlockSpec((ROW_BLOCK, d), lambda i: (i, 0)),
            pl.BlockSpec((1, d), lambda i: (0, 0)),
            pl.BlockSpec(w_cat.shape, lambda i: (0, 0)),
        ],
        out_specs=[pl.BlockSpec((ROW_BLOCK, w), lambda i: (i, 0)) for w in splits],
        out_shape=[jax.ShapeDtypeStruct((t, w), dt) for w, dt in zip(splits, out_dtypes)],
        compiler_params=_cparams("parallel"),
        name="in_proj",
    )(h, nw, w_cat)


def _norm_proj_kernel(x_ref, nw_ref, w_ref, o_ref):
    xn = _rms(x_ref[...], nw_ref[...]).astype(BF16)
    o_ref[...] = _dot(xn, w_ref[...]).astype(o_ref.dtype)


def _norm_proj(x, nw, w, out_dtype, rows):
    t, d = x.shape
    n = w.shape[1]
    return pl.pallas_call(
        _norm_proj_kernel,
        grid=(t // rows,),
        in_specs=[
            pl.BlockSpec((rows, d), lambda i: (i, 0)),
            pl.BlockSpec((1, d), lambda i: (0, 0)),
            pl.BlockSpec(w.shape, lambda i: (0, 0)),
        ],
        out_specs=pl.BlockSpec((rows, n), lambda i: (i, 0)),
        out_shape=jax.ShapeDtypeStruct((t, n), out_dtype),
        compiler_params=_cparams("parallel"),
        name="norm_proj",
    )(x, nw, w)


def _fox_prep_kernel(qkv_ref, small_ref, bias_ref, tri_ref, q_ref, k_ref, v_ref, carry_ref):
    @pl.when(pl.program_id(1) == 0)
    def _():
        carry_ref[...] = jnp.zeros_like(carry_ref)

    rows = qkv_ref.shape[0]
    log_f = _log_sigmoid(small_ref[...] + bias_ref[...])
    c = _exact_left(tri_ref[...], log_f) + carry_ref[...]
    carry_ref[...] = c[rows - 1:rows, :]

    lane = lax.broadcasted_iota(jnp.int32, (rows, LANES), 1)
    width = N_FOX_HEADS * HEAD_DIM
    scale = HEAD_DIM ** -0.5
    for hd in range(N_FOX_HEADS):
        tile = (hd * HEAD_DIM) // LANES
        ch = c[:, hd:hd + 1]
        c1 = ch.astype(BF16).astype(F32)
        r1 = ch - c1
        c2 = r1.astype(BF16).astype(F32)
        c3 = r1 - c2

        def head_tile(base):
            x = qkv_ref[:, base + tile * LANES: base + (tile + 1) * LANES].astype(F32)
            if (hd * HEAD_DIM) % LANES:
                x = pltpu.roll(x, LANES - (hd * HEAD_DIM) % LANES, axis=1)
            return x

        def augment(x, first, second):
            out = jnp.where(lane < HEAD_DIM, x, 0.0)
            for j, val in enumerate(first + second):
                out = jnp.where(lane == HEAD_DIM + j, val, out)
            return out.astype(BF16)

        ones = (1.0, 1.0, 1.0)
        q_ref[hd] = augment(head_tile(0) * scale, ones, (c1, c2, c3))
        k_ref[hd] = augment(head_tile(width), (-c1, -c2, -c3), ones)
        v_ref[hd] = jnp.where(lane < HEAD_DIM, head_tile(2 * width), 0.0).astype(BF16)


def _fox_prep(qkv, small, bias_row, tri, batch, seq):
    t = qkv.shape[0]
    nb = seq // SEQ_BLOCK
    row = lambda b, c: (b * nb + c, 0)
    head_spec = pl.BlockSpec((N_FOX_HEADS, SEQ_BLOCK, LANES), lambda b, c: (0, b * nb + c, 0))
    head_shape = jax.ShapeDtypeStruct((N_FOX_HEADS, t, LANES), BF16)
    return pl.pallas_call(
        _fox_prep_kernel,
        grid=(batch, nb),
        in_specs=[
            pl.BlockSpec((SEQ_BLOCK, qkv.shape[1]), row),
            pl.BlockSpec((SEQ_BLOCK, LANES), row),
            pl.BlockSpec((1, LANES), lambda b, c: (0, 0)),
            pl.BlockSpec((SEQ_BLOCK, SEQ_BLOCK), lambda b, c: (0, 0)),
        ],
        out_specs=[head_spec, head_spec, head_spec],
        out_shape=[head_shape, head_shape, head_shape],
        scratch_shapes=[pltpu.VMEM((1, LANES), F32)],
        compiler_params=_cparams("parallel", "arbitrary"),
        name="fox_prep",
    )(qkv, small, bias_row, tri)


def _fox_attn_kernel(q_ref, k_ref, v_ref, nw_ref, o_ref):
    i = pl.program_id(1)
    tq = q_ref.shape[1]
    row = lax.broadcasted_iota(jnp.int32, (tq, tq), 0)
    col = lax.broadcasted_iota(jnp.int32, (tq, tq), 1)
    causal = col <= row

    def step(hd, j, carry, masked):
        m, l, acc = carry
        start = pl.multiple_of(j * tq, tq)
        s = _dot_nt(q_ref[hd], k_ref[hd, pl.ds(start, tq), :])
        if masked:
            s = jnp.where(causal, s, MASK_VALUE)
        m_new = jnp.maximum(m, jnp.max(s, axis=-1, keepdims=True))
        alpha = jnp.exp(m - m_new)
        p = jnp.exp(s - m_new)
        l = alpha * l + jnp.sum(p, axis=-1, keepdims=True)
        acc = alpha * acc + _dot(p.astype(BF16), v_ref[hd, pl.ds(start, tq), :])
        return m_new, l, acc

    tiles = []
    for hd in range(N_FOX_HEADS):
        init = (jnp.full((tq, 1), -jnp.inf, F32), jnp.zeros((tq, 1), F32), jnp.zeros((tq, LANES), F32))
        carry = lax.fori_loop(0, i, lambda j, c, hd=hd: step(hd, j, c, False), init)
        _, l, acc = step(hd, i, carry, True)
        o = acc / l
        ms = jnp.sum(o * o, axis=-1, keepdims=True) * (1.0 / HEAD_DIM)
        tiles.append(o * lax.rsqrt(ms + EPS) * nw_ref[hd:hd + 1, :])
    pairs = [tiles[a] + pltpu.roll(tiles[a + 1], HEAD_DIM, axis=1) for a in range(0, N_FOX_HEADS, 2)]
    o_ref[...] = jnp.concatenate(pairs, axis=1).astype(o_ref.dtype)


def _fox_attn(qa, ka, va, nw_heads, batch, seq):
    t = qa.shape[1]
    nq = seq // SEQ_BLOCK
    seq_spec = pl.BlockSpec((N_FOX_HEADS, seq, LANES), lambda b, i: (0, b, 0))
    return pl.pallas_call(
        _fox_attn_kernel,
        grid=(batch, nq),
        in_specs=[
            pl.BlockSpec((N_FOX_HEADS, SEQ_BLOCK, LANES), lambda b, i: (0, b * nq + i, 0)),
            seq_spec,
            seq_spec,
            pl.BlockSpec((SUBLANES, LANES), lambda b, i: (0, 0)),
        ],
        out_specs=pl.BlockSpec((SEQ_BLOCK, N_FOX_HEADS * HEAD_DIM), lambda b, i: (b * nq + i, 0)),
        out_shape=jax.ShapeDtypeStruct((t, N_FOX_HEADS * HEAD_DIM), BF16),
        compiler_params=_cparams("parallel", "arbitrary"),
        name="fox_attn",
    )(qa, ka, va, nw_heads)


def _hgrn2_kernel(hq_ref, hf_ref, hi_ref, hg_ref, par_ref, tri_ref, seg_ref, o_ref, st_ref):
    @pl.when(pl.program_id(1) == 0)
    def _():
        st_ref[...] = jnp.zeros_like(st_ref)

    rows, width = hq_ref.shape
    log_lb, log1m_lb, one_m_lb, nw = (par_ref[r:r + 1, :] for r in range(4))
    f_raw = hf_ref[...]
    q = _silu(hq_ref[...]) * (HEAD_DIM ** -0.5)
    b = log1m_lb + _log_sigmoid(f_raw)
    g = jnp.maximum(log_lb, b) + jnp.log1p(jnp.exp(-jnp.abs(log_lb - b)))
    k = one_m_lb * jax.nn.sigmoid(-f_raw)
    v = hi_ref[...]
    cum = _exact_left(tri_ref[...], g)

    lane = lax.broadcasted_iota(jnp.int32, (1, width), 1)
    head_masks = [(lane >= hd * HEAD_DIM) & (lane < (hd + 1) * HEAD_DIM) for hd in range(N_HG_HEADS)]
    seg = seg_ref[...]
    k_b = k.astype(BF16)
    v_b = v.astype(BF16)

    st = st_ref[...]
    o_state = _dot_nt((q * jnp.exp(cum)).astype(BF16), st.astype(BF16))

    t_idx = lax.broadcasted_iota(jnp.int32, (SUB_BLOCK, 1), 0)
    blocks = []
    for i in range(rows // SUB_BLOCK):
        r0 = i * SUB_BLOCK
        q_i = q[r0:r0 + SUB_BLOCK]
        cum_i = cum[r0:r0 + SUB_BLOCK]
        o_i = o_state[r0:r0 + SUB_BLOCK]
        if i > 0:
            ref = cum[r0 - 1:r0]
            qs = q_i * jnp.exp(cum_i - ref)
            ks = (k[:r0] * jnp.exp(ref - cum[:r0])).astype(BF16)
            q4 = jnp.concatenate([jnp.where(mk, qs, 0.0) for mk in head_masks], axis=0).astype(BF16)
            p4 = _dot(_dot_nt(q4, ks).astype(BF16), v_b[:r0])
            for hd, mk in enumerate(head_masks):
                o_i = o_i + jnp.where(mk, p4[hd * SUB_BLOCK:(hd + 1) * SUB_BLOCK], 0.0)
        terms = []
        for s in range(SUB_BLOCK):
            keep = t_idx >= s
            e = _masked_exp(cum_i - cum[r0 + s:r0 + s + 1], keep)
            terms.append((q_i * e * k[r0 + s:r0 + s + 1]).astype(BF16))
        sums = _dot(jnp.concatenate(terms, axis=0), seg)
        for s in range(SUB_BLOCK):
            o_i = o_i + sums[s * SUB_BLOCK:(s + 1) * SUB_BLOCK] * v[r0 + s:r0 + s + 1]
        blocks.append(o_i)
    o = jnp.concatenate(blocks, axis=0)

    last = cum[rows - 1:rows]
    kd = (k * jnp.exp(last - cum)).astype(BF16)
    upd = _dot(v.T.astype(BF16), kd)
    st_ref[...] = st * jnp.exp(last) + jnp.where(seg > 0, upd, 0.0)

    sq = o * o
    s1 = sq.astype(BF16)
    s2 = (sq - s1.astype(F32)).astype(BF16)
    ms = (_dot(s1, seg) + _dot(s2, seg)) * (1.0 / HEAD_DIM)
    o_ref[...] = (o * lax.rsqrt(ms + EPS) * nw * _silu(hg_ref[...])).astype(o_ref.dtype)


def _hgrn2(hg, params, tri, seg, batch, seq):
    t = hg.shape[0]
    width = N_HG_HEADS * HEAD_DIM
    nc = seq // SCAN_CHUNK
    col = lambda c: pl.BlockSpec((SCAN_CHUNK, width), lambda b, j, c=c: (b * nc + j, c))
    return pl.pallas_call(
        _hgrn2_kernel,
        grid=(batch, nc),
        in_specs=[
            col(0), col(1), col(2), col(3),
            pl.BlockSpec(params.shape, lambda b, j: (0, 0)),
            pl.BlockSpec(tri.shape, lambda b, j: (0, 0)),
            pl.BlockSpec(seg.shape, lambda b, j: (0, 0)),
        ],
        out_specs=pl.BlockSpec((SCAN_CHUNK, width), lambda b, j: (b * nc + j, 0)),
        out_shape=jax.ShapeDtypeStruct((t, width), BF16),
        scratch_shapes=[pltpu.VMEM((width, width), F32)],
        compiler_params=_cparams("parallel", "arbitrary"),
        name="hgrn2",
    )(hg, hg, hg, hg, params, tri, seg)


def _ssd_kernel(z_ref, xbc_ref, small_ref, cw_ref, cb_ref, hp_ref, sp_ref, nw_ref, tri_ref, exp_ref,
                o_ref, buf_ref, st_ref):
    rows = z_ref.shape[0]
    width = z_ref.shape[1]
    halo = SUBLANES

    @pl.when(pl.program_id(1) == 0)
    def _():
        buf_ref[0:halo, :] = jnp.zeros((halo, buf_ref.shape[1]), F32)
        st_ref[...] = jnp.zeros_like(st_ref)

    buf_ref[halo:halo + rows, :] = xbc_ref[...]
    conv = cb_ref[...]
    for i in range(M2_CONV):
        off = halo - (M2_CONV - 1) + i
        conv = conv + cw_ref[i:i + 1, :] * buf_ref[off:off + rows, :]
    tail = buf_ref[rows:rows + halo, :]
    buf_ref[0:halo, :] = tail
    xc = _silu(conv)
    xs = xc[:, :width]
    gn = M2_GROUPS * M2_STATE
    bm = xc[:, width:width + gn]
    cm = xc[:, width + gn:width + 2 * gn]

    dt_bias_x, a_log_x, d_x = (hp_ref[r:r + 1, :] for r in range(3))
    dt_bias_c, a_log_c = (sp_ref[r:r + 1, :] for r in range(2))
    small = small_ref[...]
    tri = tri_ref[...]
    dt_x = _softplus(_exact_right(small, exp_ref[...]) + dt_bias_x)
    a_cum_x = _exact_left(tri, dt_x * -jnp.exp(a_log_x))
    a_cum_c = _exact_left(tri, _softplus(small + dt_bias_c) * -jnp.exp(a_log_c))
    a_cum_t = a_cum_c.T

    row = lax.broadcasted_iota(jnp.int32, (rows, rows), 0)
    col = lax.broadcasted_iota(jnp.int32, (rows, rows), 1)
    causal = col <= row
    lane = lax.broadcasted_iota(jnp.int32, (1, LANES), 1)
    low = lane < HEAD_DIM

    xdt = xs * dt_x
    a_last = a_cum_x[rows - 1:rows]
    xd = (xdt * jnp.exp(a_last - a_cum_x)).astype(BF16)
    decay_out = jnp.exp(a_cum_x)
    st = st_ref[...]
    heads_per_group = M2_HEADS // M2_GROUPS
    gw = heads_per_group * HEAD_DIM
    y_tiles = []
    new_states = []
    for g in range(M2_GROUPS):
        b_g = bm[:, g * M2_STATE:(g + 1) * M2_STATE]
        c_g = cm[:, g * M2_STATE:(g + 1) * M2_STATE].astype(BF16)
        cb = _dot_nt(c_g, b_g.astype(BF16))
        y_off = _dot(c_g, st[:, g * gw:(g + 1) * gw].astype(BF16)) * decay_out[:, g * gw:(g + 1) * gw]
        new_states.append(_dot(b_g.T.astype(BF16), xd[:, g * gw:(g + 1) * gw]))
        for pair in range(heads_per_group // 2):
            tile = g * (heads_per_group // 2) + pair
            x_tile = xdt[:, tile * LANES:(tile + 1) * LANES]
            y_pair = y_off[:, pair * LANES:(pair + 1) * LANES]
            for half in range(2):
                hd = 2 * tile + half
                seg = a_cum_c[:, SUBLANES + hd:SUBLANES + hd + 1] - a_cum_t[SUBLANES + hd:SUBLANES + hd + 1, :]
                m_h = (cb * _masked_exp(seg, causal)).astype(BF16)
                x_h = jnp.where(low if half == 0 else ~low, x_tile, 0.0).astype(BF16)
                y_pair = y_pair + _dot(m_h, x_h)
            y_tiles.append(y_pair)
    st_ref[...] = st * jnp.exp(a_last) + jnp.concatenate(new_states, axis=1)

    y = jnp.concatenate(y_tiles, axis=1) + d_x * xs
    y = y * _silu(z_ref[...])
    nw = nw_ref[...]
    outs = []
    for g in range(M2_GROUPS):
        outs.append(_rms(y[:, g * gw:(g + 1) * gw], nw[:, g * gw:(g + 1) * gw]))
    o_ref[...] = jnp.concatenate(outs, axis=1).astype(o_ref.dtype)


def _ssd(z, xbc, small, conv_w, conv_b, head_rows, small_rows, norm_w, tri, expand, batch, seq):
    t, width = z.shape
    ch = xbc.shape[1]
    nc = seq // SCAN_CHUNK
    row = lambda b, j: (b * nc + j, 0)
    const = lambda a: pl.BlockSpec(a.shape, lambda b, j: (0, 0))
    return pl.pallas_call(
        _ssd_kernel,
        grid=(batch, nc),
        in_specs=[
            pl.BlockSpec((SCAN_CHUNK, width), row),
            pl.BlockSpec((SCAN_CHUNK, ch), row),
            pl.BlockSpec((SCAN_CHUNK, LANES), row),
            const(conv_w), const(conv_b), const(head_rows), const(small_rows), const(norm_w),
            const(tri), const(expand),
        ],
        out_specs=pl.BlockSpec((SCAN_CHUNK, width), row),
        out_shape=jax.ShapeDtypeStruct((t, width), BF16),
        scratch_shapes=[
            pltpu.VMEM((SCAN_CHUNK + SUBLANES, ch), F32),
            pltpu.VMEM((M2_STATE, width), F32),
        ],
        compiler_params=_cparams("parallel", "arbitrary"),
        name="ssd",
    )(z, xbc, small, conv_w, conv_b, head_rows, small_rows, norm_w, tri, expand)


def _out_proj_kernel(h_ref, a_ref, b_ref, c_ref, w_ref, o_ref):
    wa = a_ref.shape[1]
    wb = b_ref.shape[1]
    acc = h_ref[...] + _dot(a_ref[...], w_ref[0:wa, :])
    acc = acc + _dot(b_ref[...], w_ref[wa:wa + wb, :])
    o_ref[...] = acc + _dot(c_ref[...], w_ref[wa + wb:, :])


def _out_proj(h, a, b, c, w):
    t, d = h.shape
    row = lambda x: pl.BlockSpec((ROW_BLOCK, x.shape[1]), lambda i: (i, 0))
    return pl.pallas_call(
        _out_proj_kernel,
        grid=(t // ROW_BLOCK,),
        in_specs=[row(h), row(a), row(b), row(c), pl.BlockSpec(w.shape, lambda i: (0, 0))],
        out_specs=row(h),
        out_shape=jax.ShapeDtypeStruct((t, d), F32),
        compiler_params=_cparams("parallel"),
        name="out_proj",
    )(h, a, b, c, w)


def _xattn_kernel(h_ref, nw_ref, wq_ref, kv_ref, wo_ref, o_ref):
    h = h_ref[...]
    d = h.shape[1]
    hd = d // XA_HEADS
    q = (_dot(_rms(h, nw_ref[...]).astype(BF16), wq_ref[...]) * (hd ** -0.5)).astype(BF16)
    outs = []
    for a in range(XA_HEADS):
        k_a = kv_ref[:, a * hd:(a + 1) * hd]
        v_a = kv_ref[:, d + a * hd:d + (a + 1) * hd]
        s = _dot_nt(q[:, a * hd:(a + 1) * hd], k_a)
        p = jnp.exp(s - jnp.max(s, axis=-1, keepdims=True))
        p = p / jnp.sum(p, axis=-1, keepdims=True)
        outs.append(_dot(p.astype(BF16), v_a).astype(BF16))
    o_ref[...] = h + _dot(jnp.concatenate(outs, axis=1), wo_ref[...])


def _xattn(h, nw, wq, kv, wo, batch, seq):
    t, d = h.shape
    n_mem = kv.shape[0] // batch
    nb = seq // ROW_BLOCK
    row = lambda b, i: (b * nb + i, 0)
    const = lambda a: pl.BlockSpec(a.shape, lambda b, i: (0, 0))
    return pl.pallas_call(
        _xattn_kernel,
        grid=(batch, nb),
        in_specs=[
            pl.BlockSpec((ROW_BLOCK, d), row),
            const(nw), const(wq),
            pl.BlockSpec((n_mem, kv.shape[1]), lambda b, i: (b, 0)),
            const(wo),
        ],
        out_specs=pl.BlockSpec((ROW_BLOCK, d), row),
        out_shape=jax.ShapeDtypeStruct((t, d), F32),
        compiler_params=_cparams("parallel", "parallel"),
        name="xattn",
    )(h, nw, wq, kv, wo)


def _ffn_tile(d_ff):
    for cand in (1408, 1024, 768, 512, 256, 128):
        if d_ff % cand == 0:
            return cand
    raise ValueError(d_ff)


def _ffn_kernel(h_ref, nw_ref, wg_ref, wu_ref, w2_ref, o_ref, xn_ref):
    @pl.when(pl.program_id(1) == 0)
    def _():
        xn_ref[...] = _rms(h_ref[...], nw_ref[...]).astype(BF16)
        o_ref[...] = h_ref[...]

    xn = xn_ref[...]
    a = (_silu(_dot(xn, wg_ref[...])) * _dot(xn, wu_ref[...])).astype(BF16)
    o_ref[...] += _dot(a, w2_ref[...])


def _ffn(h, nw, w1, w2):
    t, d = h.shape
    d_ff = w2.shape[0]
    tf = _ffn_tile(d_ff)
    nf = d_ff // tf
    return pl.pallas_call(
        _ffn_kernel,
        grid=(t // ROW_BLOCK, nf),
        in_specs=[
            pl.BlockSpec((ROW_BLOCK, d), lambda i, j: (i, 0)),
            pl.BlockSpec((1, d), lambda i, j: (0, 0)),
            pl.BlockSpec((d, tf), lambda i, j: (0, j)),
            pl.BlockSpec((d, tf), lambda i, j: (0, j + nf)),
            pl.BlockSpec((tf, d), lambda i, j: (j, 0)),
        ],
        out_specs=pl.BlockSpec((ROW_BLOCK, d), lambda i, j: (i, 0)),
        out_shape=jax.ShapeDtypeStruct((t, d), F32),
        scratch_shapes=[pltpu.VMEM((ROW_BLOCK, d), BF16)],
        compiler_params=_cparams("parallel", "arbitrary"),
        name="ffn",
    )(h, nw, w1, w1, w2)


def _router_kernel(h_ref, nw_ref, wr_ref, tri_ref, xn_ref, idx_ref, gate_ref, rank_ref, cnt_ref, carry_ref):
    @pl.when(pl.program_id(0) == 0)
    def _():
        carry_ref[...] = jnp.zeros_like(carry_ref)

    xn = _rms(h_ref[...], nw_ref[...])
    xn_ref[...] = xn
    x1 = xn.astype(BF16)
    x2 = (xn - x1.astype(F32)).astype(BF16)
    w1, w2, w3 = wr_ref[0], wr_ref[1], wr_ref[2]
    logits = _dot(x1, w1) + _dot(x1, w2) + _dot(x2, w1) + _dot(x1, w3) + _dot(x2, w2)
    lane = lax.broadcasted_iota(jnp.int32, logits.shape, 1)
    logits = jnp.where(lane < N_EXPERTS, logits, -jnp.inf)
    m1 = jnp.max(logits, axis=-1, keepdims=True)
    i1 = jnp.min(jnp.where(logits == m1, lane, LANES), axis=-1, keepdims=True)
    rest = jnp.where(lane == i1, -jnp.inf, logits)
    m2 = jnp.max(rest, axis=-1, keepdims=True)
    i2 = jnp.min(jnp.where(rest == m2, lane, LANES), axis=-1, keepdims=True)
    e2 = jnp.exp(m2 - m1)
    denom = 1.0 + e2
    idx_ref[...] = jnp.where(lane == 0, i1, jnp.where(lane == 1, i2, 0))
    gate_ref[...] = jnp.where(lane == 0, 1.0 / denom, jnp.where(lane == 1, e2 / denom, 0.0))

    hit1 = lane == i1
    hit2 = lane == i2
    member = jnp.where(hit1, 1.0, jnp.where(hit2, 1.0, 0.0))
    incl = _dot(tri_ref[...], member.astype(BF16)) + carry_ref[...]
    excl = incl - member
    r1 = jnp.sum(jnp.where(hit1, excl, 0.0), axis=-1, keepdims=True)
    r2 = jnp.sum(jnp.where(hit2, excl, 0.0), axis=-1, keepdims=True)
    rows = member.shape[0]
    carry_ref[...] = incl[rows - 1:rows, :]
    rank_ref[...] = jnp.where(lane == 0, r1, jnp.where(lane == 1, r2, 0.0)).astype(jnp.int32)
    cnt_ref[...] = jnp.broadcast_to(incl[rows - 1:rows, :], cnt_ref.shape)


def _router(h, nw, wr3, tri):
    t, d = h.shape
    row = lambda w: pl.BlockSpec((ROW_BLOCK, w), lambda i: (i, 0))
    return pl.pallas_call(
        _router_kernel,
        grid=(t // ROW_BLOCK,),
        in_specs=[
            row(d),
            pl.BlockSpec((1, d), lambda i: (0, 0)),
            pl.BlockSpec(wr3.shape, lambda i: (0, 0, 0)),
            pl.BlockSpec(tri.shape, lambda i: (0, 0)),
        ],
        out_specs=[row(d), row(LANES), row(LANES), row(LANES), pl.BlockSpec((SUBLANES, LANES), lambda i: (0, 0))],
        out_shape=[
            jax.ShapeDtypeStruct((t, d), F32),
            jax.ShapeDtypeStruct((t, LANES), jnp.int32),
            jax.ShapeDtypeStruct((t, LANES), F32),
            jax.ShapeDtypeStruct((t, LANES), jnp.int32),
            jax.ShapeDtypeStruct((SUBLANES, LANES), F32),
        ],
        scratch_shapes=[pltpu.VMEM((1, LANES), F32)],
        compiler_params=_cparams("arbitrary"),
        name="router",
    )(h, nw, wr3, tri)


def _slot_kernel(idx_ref, rank_ref, start_ref, o_ref):
    idx = idx_ref[...]
    lane = lax.broadcasted_iota(jnp.int32, idx.shape, 1)
    starts = start_ref[...]
    s1 = jnp.sum(jnp.where(lane == idx[:, 0:1], starts, 0.0), axis=-1, keepdims=True)
    s2 = jnp.sum(jnp.where(lane == idx[:, 1:2], starts, 0.0), axis=-1, keepdims=True)
    o_ref[...] = jnp.where(lane == 0, s1, jnp.where(lane == 1, s2, 0.0)).astype(jnp.int32) + rank_ref[...]


def _slots(idx, rank, start_row):
    t = idx.shape[0]
    row = pl.BlockSpec((ROW_BLOCK, LANES), lambda i: (i, 0))
    return pl.pallas_call(
        _slot_kernel,
        grid=(t // ROW_BLOCK,),
        in_specs=[row, row, pl.BlockSpec((1, LANES), lambda i: (0, 0))],
        out_specs=row,
        out_shape=jax.ShapeDtypeStruct((t, LANES), jnp.int32),
        compiler_params=_cparams("parallel"),
        name="moe_slots",
    )(idx, rank, start_row)


def _dispatch_kernel(pend_ref, padded_ref, nused_ref, dest_ref, x_ref, xs_ref, zero_ref, sem):
    rows = x_ref.shape[0]
    n_blk = xs_ref.shape[0] // MOE_ROWS

    @pl.when(pl.program_id(0) == 0)
    def _():
        zero_ref[...] = jnp.zeros_like(zero_ref)

        def fill_block(start):
            fill = pltpu.make_async_copy(zero_ref, xs_ref.at[pl.ds(start, MOE_ROWS), :], sem)
            fill.start()
            fill.wait()

        for e in range(N_EXPERTS):
            @pl.when(padded_ref[e] > 0)
            def _():
                fill_block(pl.multiple_of(pend_ref[e] - MOE_ROWS, MOE_ROWS))

            @pl.when(nused_ref[0] + e < n_blk)
            def _():
                fill_block(pl.multiple_of((nused_ref[0] + e) * MOE_ROWS, MOE_ROWS))

    def issue(t, carry):
        for k in range(2):
            slot = dest_ref[0, k, t]
            pltpu.make_async_copy(x_ref.at[pl.ds(t, 1), :], xs_ref.at[pl.ds(slot, 1), :], sem).start()
        return carry

    lax.fori_loop(0, rows, issue, 0, unroll=8)
    for k in range(2):
        pltpu.make_async_copy(x_ref, xs_ref.at[pl.ds(0, rows), :], sem).wait()


def _dispatch(pends, padded, n_used, dest_t, xn, cap):
    t, d = xn.shape
    grid_spec = pltpu.PrefetchScalarGridSpec(
        num_scalar_prefetch=3,
        grid=(t // ROW_BLOCK,),
        in_specs=[
            pl.BlockSpec((1, 2, ROW_BLOCK), lambda i, pe, pa, nu: (i, 0, 0), memory_space=pltpu.SMEM),
            pl.BlockSpec((ROW_BLOCK, d), lambda i, pe, pa, nu: (i, 0)),
        ],
        out_specs=pl.BlockSpec(memory_space=pl.ANY),
        scratch_shapes=[pltpu.VMEM((MOE_ROWS, d), F32), pltpu.SemaphoreType.DMA(())],
    )
    return pl.pallas_call(
        _dispatch_kernel,
        grid_spec=grid_spec,
        out_shape=jax.ShapeDtypeStruct((cap, d), F32),
        compiler_params=_cparams("arbitrary"),
        name="moe_dispatch",
    )(pends, padded, n_used, dest_t, xn)


def _expert_kernel(be_ref, nused_ref, x_ref, wg_ref, wu_ref, w2_ref, o_ref, xb_ref):
    j = pl.program_id(1)
    live = pl.program_id(0) < nused_ref[0]

    @pl.when(live)
    def _():
        @pl.when(j == 0)
        def _():
            xb_ref[...] = x_ref[...].astype(BF16)

        x = xb_ref[...]
        a = (_silu(_dot(x, wg_ref[0])) * _dot(x, wu_ref[0])).astype(BF16)
        y = _dot(a, w2_ref[0])

        @pl.when(j == 0)
        def _():
            o_ref[...] = y

        @pl.when(j > 0)
        def _():
            o_ref[...] += y

    @pl.when(jnp.logical_not(live) & (j == 0))
    def _():
        o_ref[...] = jnp.zeros_like(o_ref)


def _experts(blk_expert, n_used, xs, w1, w2):
    cap, d = xs.shape
    d_ff = w2.shape[1]
    tf = _ffn_tile(d_ff)
    nf = d_ff // tf
    blk = lambda i, j, be, nu: (i, 0)
    grid_spec = pltpu.PrefetchScalarGridSpec(
        num_scalar_prefetch=2,
        grid=(cap // MOE_ROWS, nf),
        in_specs=[
            pl.BlockSpec((MOE_ROWS, d), blk),
            pl.BlockSpec((1, d, tf), lambda i, j, be, nu: (be[i], 0, j)),
            pl.BlockSpec((1, d, tf), lambda i, j, be, nu: (be[i], 0, j + nf)),
            pl.BlockSpec((1, tf, d), lambda i, j, be, nu: (be[i], j, 0)),
        ],
        out_specs=pl.BlockSpec((MOE_ROWS, d), blk),
        scratch_shapes=[pltpu.VMEM((MOE_ROWS, d), BF16)],
    )
    return pl.pallas_call(
        _expert_kernel,
        grid_spec=grid_spec,
        out_shape=jax.ShapeDtypeStruct((cap, d), F32),
        compiler_params=_cparams("arbitrary", "arbitrary"),
        name="experts",
    )(blk_expert, n_used, xs, w1, w1, w2)


def _combine_kernel(dest_ref, h_ref, gate_ref, nw_ref, yb_ref, o_ref, buf_ref, sem, *, final_norm):
    rows = h_ref.shape[0]

    def issue(t, carry):
        for k in range(2):
            slot = dest_ref[0, k, t]
            pltpu.make_async_copy(yb_ref.at[pl.ds(slot, 1), :], buf_ref.at[k, pl.ds(t, 1), :], sem).start()
        return carry

    lax.fori_loop(0, rows, issue, 0, unroll=8)
    for k in range(2):
        pltpu.make_async_copy(yb_ref.at[pl.ds(0, rows), :], buf_ref.at[k], sem).wait()
    gate = gate_ref[...]
    out = h_ref[...] + gate[:, 0:1] * buf_ref[0] + gate[:, 1:2] * buf_ref[1]
    if final_norm:
        out = _rms(out, nw_ref[...])
    o_ref[...] = out


def _combine(dest_t, h, gate, nw, yb, final_norm):
    t, d = h.shape
    return pl.pallas_call(
        functools.partial(_combine_kernel, final_norm=final_norm),
        grid=(t // ROW_BLOCK,),
        in_specs=[
            pl.BlockSpec((1, 2, ROW_BLOCK), lambda i: (i, 0, 0), memory_space=pltpu.SMEM),
            pl.BlockSpec((ROW_BLOCK, d), lambda i: (i, 0)),
            pl.BlockSpec((ROW_BLOCK, LANES), lambda i: (i, 0)),
            pl.BlockSpec((1, d), lambda i: (0, 0)),
            pl.BlockSpec(memory_space=pl.ANY),
        ],
        out_specs=pl.BlockSpec((ROW_BLOCK, d), lambda i: (i, 0)),
        out_shape=jax.ShapeDtypeStruct((t, d), F32),
        scratch_shapes=[pltpu.VMEM((2, ROW_BLOCK, d), F32), pltpu.SemaphoreType.DMA(())],
        compiler_params=_cparams("arbitrary"),
        name="moe_combine",
    )(dest_t, h, gate, nw, yb)


def _final_norm_kernel(h_ref, nw_ref, o_ref):
    o_ref[...] = _rms(h_ref[...], nw_ref[...])


def _final_norm(h, nw):
    t, d = h.shape
    return pl.pallas_call(
        _final_norm_kernel,
        grid=(t // ROW_BLOCK,),
        in_specs=[pl.BlockSpec((ROW_BLOCK, d), lambda i: (i, 0)), pl.BlockSpec((1, d), lambda i: (0, 0))],
        out_specs=pl.BlockSpec((ROW_BLOCK, d), lambda i: (i, 0)),
        out_shape=jax.ShapeDtypeStruct((t, d), F32),
        compiler_params=_cparams("parallel"),
        name="final_norm",
    )(h, nw)


def _moe(h, nw, router_w, w1, w2, tri, out_nw, final_norm):
    t, d = h.shape
    assert MOE_ROWS == ROW_BLOCK
    r1, r2, r3 = _split3(jnp.pad(router_w, ((0, 0), (0, LANES - N_EXPERTS))))
    xn, idx, gate, rank, cnt = _router(h, nw, jnp.stack([r1, r2, r3]), tri)
    counts = cnt[0, :N_EXPERTS].astype(jnp.int32)
    padded = (counts + MOE_ROWS - 1) // MOE_ROWS * MOE_ROWS
    pends = jnp.cumsum(padded)
    starts = pends - padded
    cap = (2 * t // MOE_ROWS + N_EXPERTS) * MOE_ROWS
    n_blk = cap // MOE_ROWS
    blk_start = jnp.arange(n_blk, dtype=jnp.int32) * MOE_ROWS
    blk_expert = jnp.minimum(jnp.sum(blk_start[:, None] >= pends[None, :], axis=1), N_EXPERTS - 1).astype(jnp.int32)
    n_used = (pends[-1:] // MOE_ROWS).astype(jnp.int32)
    start_row = jnp.zeros((1, LANES), F32).at[0, :N_EXPERTS].set(starts.astype(F32))
    dest = _slots(idx, rank, start_row)
    dest_t = dest[:, :2].reshape(t // ROW_BLOCK, ROW_BLOCK, 2).transpose(0, 2, 1)
    xs = _dispatch(pends.astype(jnp.int32), padded.astype(jnp.int32), n_used, dest_t, xn, cap)
    yb = _experts(blk_expert, n_used, xs, w1, w2)
    return _combine(dest_t, h, gate, out_nw, yb, final_norm)


def _tri(n):
    return jnp.tril(jnp.ones((n, n), F32)).astype(BF16)


def kernel(x, mem, mix_norm_w, w_in, fox_f_bias, fox_norm_w, hg_lb_raw, hg_norm_w, m2_conv_w, m2_conv_b, m2_dt_bias, m2_a_log, m2_d, m2_norm_w, w_out, xa_norm_w, xa_mem_norm_w, xa_w_q, xa_w_kv, xa_w_o, ffn_norm_w, ffn_w1, ffn_w2, moe_router, moe_w1, moe_w2, final_norm_w):
    batch, seq, d = x.shape
    depth = w_in.shape[0]
    t = batch * seq
    fox_w = N_FOX_HEADS * HEAD_DIM
    hg_w = N_HG_HEADS * HEAD_DIM
    m2_w = M2_HEADS * HEAD_DIM
    conv_ch = m2_w + 2 * M2_GROUPS * M2_STATE
    in_splits = (fox_w, fox_w, fox_w, N_FOX_HEADS, hg_w, hg_w, hg_w, hg_w, m2_w, conv_ch, M2_HEADS)
    offs = [0]
    for s in in_splits:
        offs.append(offs[-1] + s)

    lb_p = jax.nn.softmax(hg_lb_raw.astype(F32), axis=0)
    hg_lb = jnp.cumsum(lb_p, axis=0) - lb_p[0]

    tri_seq = _tri(SEQ_BLOCK)
    tri_chunk = _tri(SCAN_CHUNK)
    head_of_lane = jnp.arange(hg_w) // HEAD_DIM
    seg_ones = (head_of_lane[:, None] == head_of_lane[None, :]).astype(BF16)
    expand = (jnp.arange(LANES)[:, None] - SUBLANES == jnp.arange(m2_w)[None, :] // HEAD_DIM).astype(BF16)

    def pad_lanes(v, offset=0, width=LANES):
        return jnp.zeros((1, width), F32).at[0, offset:offset + v.shape[0]].set(v)

    h = x.reshape(t, d)
    mem2 = mem.reshape(batch * mem.shape[1], d)
    for layer in range(depth):
        w = w_in[layer]
        small_w = jnp.zeros((d, LANES), F32)
        small_w = small_w.at[:, 0:N_FOX_HEADS].set(w[:, offs[3]:offs[4]])
        small_w = small_w.at[:, SUBLANES:SUBLANES + M2_HEADS].set(w[:, offs[10]:offs[11]])
        w_cat = jnp.concatenate(
            [w[:, offs[0]:offs[3]], small_w, w[:, offs[4]:offs[8]], w[:, offs[8]:offs[9]], w[:, offs[9]:offs[10]]],
            axis=1).astype(BF16)
        splits = (3 * fox_w, LANES, 4 * hg_w, m2_w, conv_ch)
        qkv, small, hg, z, xbc = _in_proj(h, mix_norm_w[layer][None, :], w_cat, splits)

        qa, ka, va = _fox_prep(qkv, small, pad_lanes(fox_f_bias[layer]), tri_seq, batch, seq)
        fox_nw = jnp.zeros((SUBLANES, LANES), F32).at[:N_FOX_HEADS, :HEAD_DIM].set(
            fox_norm_w[layer].reshape(N_FOX_HEADS, HEAD_DIM))
        o_fox = _fox_attn(qa, ka, va, fox_nw, batch, seq)

        lb = hg_lb[layer]
        hg_params = jnp.zeros((SUBLANES, hg_w), F32)
        hg_params = hg_params.at[0].set(jnp.log(jnp.maximum(lb, LB_FLOOR)))
        hg_params = hg_params.at[1].set(jnp.log1p(-lb))
        hg_params = hg_params.at[2].set(1.0 - lb)
        hg_params = hg_params.at[3].set(hg_norm_w[layer])
        o_hg = _hgrn2(hg, hg_params, tri_chunk, seg_ones, batch, seq)

        conv_w = jnp.zeros((SUBLANES, conv_ch), F32).at[:M2_CONV].set(m2_conv_w[layer])
        head_rows = jnp.zeros((SUBLANES, m2_w), F32)
        head_rows = head_rows.at[0].set(jnp.repeat(m2_dt_bias[layer], HEAD_DIM))
        head_rows = head_rows.at[1].set(jnp.repeat(m2_a_log[layer], HEAD_DIM))
        head_rows = head_rows.at[2].set(jnp.repeat(m2_d[layer], HEAD_DIM))
        small_rows = jnp.concatenate(
            [pad_lanes(m2_dt_bias[layer], SUBLANES), pad_lanes(m2_a_log[layer], SUBLANES),
             jnp.zeros((SUBLANES - 2, LANES), F32)], axis=0)
        o_m2 = _ssd(z, xbc, small, conv_w, m2_conv_b[layer][None, :], head_rows, small_rows,
                    m2_norm_w[layer][None, :], tri_chunk, expand, batch, seq)

        h = _out_proj(h, o_fox, o_hg, o_m2, w_out[layer].astype(BF16))

        kv = _norm_proj(mem2, xa_mem_norm_w[layer][None, :], xa_w_kv[layer].astype(BF16), BF16, mem.shape[1])
        h = _xattn(h, xa_norm_w[layer][None, :], xa_w_q[layer].astype(BF16), kv, xa_w_o[layer].astype(BF16),
                   batch, seq)

        nw = ffn_norm_w[layer][None, :]
        if layer % 2 == 0:
            h = _ffn(h, nw, ffn_w1[layer // 2].astype(BF16), ffn_w2[layer // 2].astype(BF16))
        else:
            last = layer == depth - 1
            h = _moe(h, nw, moe_router[layer // 2], moe_w1[layer // 2].astype(BF16),
                     moe_w2[layer // 2].astype(BF16), tri_seq, final_norm_w[None, :], last)
    if depth % 2:
        h = _final_norm(h, final_norm_w[None, :])
    return h.reshape(batch, seq, d)
```

```python
import functools
import math

import jax
import jax.numpy as jnp
from jax import lax
from jax.experimental import pallas as pl
from jax.experimental.pallas import tpu as pltpu

F32 = jnp.float32
BF16 = jnp.bfloat16

EPS = 1e-6
MASK_VALUE = -1e9
LB_FLOOR = 1e-30
HEAD_DIM = 64
N_FOX_HEADS = 4
N_HG_HEADS = 4
M2_HEADS = 8
M2_GROUPS = 2
M2_STATE = 128
M2_CONV = 4
XA_HEADS = 4
N_EXPERTS = 8

LANES = 128
SUBLANES = 8
VMEM_LIMIT_BYTES = 56 * 1024 * 1024

ROW_BLOCK = 512
SEQ_BLOCK = 512
KV_BLOCK = 512
SCAN_CHUNK = 128
SUB_BLOCK = 16
HG_SAFE_DECAY = 60.0
MOE_ROWS = 512


def _cparams(*sem):
    return pltpu.CompilerParams(dimension_semantics=sem, vmem_limit_bytes=VMEM_LIMIT_BYTES)


def _split3(x):
    x1 = x.astype(BF16)
    r1 = x - x1.astype(F32)
    x2 = r1.astype(BF16)
    x3 = (r1 - x2.astype(F32)).astype(BF16)
    return x1, x2, x3


def _dot(a, b):
    return jnp.dot(a, b, preferred_element_type=F32)


def _dot_nt(a, b):
    return lax.dot_general(a, b, (((1,), (1,)), ((), ())), preferred_element_type=F32)


def _exact_left(mat01, x):
    x1, x2, x3 = _split3(x)
    return _dot(mat01, x1) + _dot(mat01, x2) + _dot(mat01, x3)


def _exact_right(x, mat01):
    x1, x2, x3 = _split3(x)
    return _dot(x1, mat01) + _dot(x2, mat01) + _dot(x3, mat01)


def _rms(x, w):
    return x * lax.rsqrt(jnp.mean(x * x, axis=-1, keepdims=True) + EPS) * w


def _silu(x):
    return x * jax.nn.sigmoid(x)


def _log_sigmoid(x):
    return jnp.minimum(x, 0.0) - jnp.log1p(jnp.exp(-jnp.abs(x)))


def _softplus(x):
    return jnp.maximum(x, 0.0) + jnp.log1p(jnp.exp(-jnp.abs(x)))


def _masked_exp(x, mask):
    return jnp.where(mask, jnp.exp(jnp.where(mask, x, 0.0)), 0.0)


def _in_proj_kernel(x_ref, nw_ref, w_ref, fox_ref, small_ref, hg_ref, z_ref, xbc_ref, *, splits):
    xn = _rms(x_ref[...], nw_ref[...]).astype(BF16)
    outs = (fox_ref, small_ref, hg_ref, z_ref, xbc_ref)
    off = 0
    for o_ref, width in zip(outs, splits):
        o_ref[...] = _dot(xn, w_ref[:, off:off + width]).astype(o_ref.dtype)
        off += width


def _in_proj(h, nw, w_cat, splits):
    t, d = h.shape
    out_dtypes = (BF16, F32, F32, F32, F32)
    return pl.pallas_call(
        functools.partial(_in_proj_kernel, splits=splits),
        grid=(t // ROW_BLOCK,),
        in_specs=[
            pl.BlockSpec((ROW_BLOCK, d), lambda i: (i, 0)),
            pl.BlockSpec((1, d), lambda i: (0, 0)),
            pl.BlockSpec(w_cat.shape, lambda i: (0, 0)),
        ],
        out_specs=[pl.BlockSpec((ROW_BLOCK, w), lambda i: (i, 0)) for w in splits],
        out_shape=[jax.ShapeDtypeStruct((t, w), dt) for w, dt in zip(splits, out_dtypes)],
        compiler_params=_cparams("parallel"),
        name="in_proj",
    )(h, nw, w_cat)


def _norm_proj_kernel(x_ref, nw_ref, w_ref, o_ref):
    xn = _rms(x_ref[...], nw_ref[...]).astype(BF16)
    o_ref[...] = _dot(xn, w_ref[...]).astype(o_ref.dtype)


def _norm_proj(x, nw, w, out_dtype, rows):
    t, d = x.shape
    n = w.shape[1]
    return pl.pallas_call(
        _norm_proj_kernel,
        grid=(t // rows,),
        in_specs=[
            pl.BlockSpec((rows, d), lambda i: (i, 0)),
            pl.BlockSpec((1, d), lambda i: (0, 0)),
            pl.BlockSpec(w.shape, lambda i: (0, 0)),
        ],
        out_specs=pl.BlockSpec((rows, n), lambda i: (i, 0)),
        out_shape=jax.ShapeDtypeStruct((t, n), out_dtype),
        compiler_params=_cparams("parallel"),
        name="norm_proj",
    )(x, nw, w)


def _fox_prep_kernel(qkv_ref, small_ref, bias_ref, tri_ref, q_ref, k_ref, v_ref, carry_ref):
    @pl.when(pl.program_id(1) == 0)
    def _():
        carry_ref[...] = jnp.zeros_like(carry_ref)

    rows = qkv_ref.shape[0]
    log_f = _log_sigmoid(small_ref[...] + bias_ref[...])
    c = _exact_left(tri_ref[...], log_f) + carry_ref[...]
    carry_ref[...] = c[rows - 1:rows, :]

    lane = lax.broadcasted_iota(jnp.int32, (rows, LANES), 1)
    width = N_FOX_HEADS * HEAD_DIM
    scale = HEAD_DIM ** -0.5
    for hd in range(N_FOX_HEADS):
        tile = (hd * HEAD_DIM) // LANES
        ch = c[:, hd:hd + 1]
        c1 = ch.astype(BF16).astype(F32)
        r1 = ch - c1
        c2 = r1.astype(BF16).astype(F32)
        c3 = r1 - c2

        def head_tile(base):
            x = qkv_ref[:, base + tile * LANES: base + (tile + 1) * LANES].astype(F32)
            if (hd * HEAD_DIM) % LANES:
                x = pltpu.roll(x, LANES - (hd * HEAD_DIM) % LANES, axis=1)
            return x

        def augment(x, first, second):
            out = jnp.where(lane < HEAD_DIM, x, 0.0)
            for j, val in enumerate(first + second):
                out = jnp.where(lane == HEAD_DIM + j, val, out)
            return out.astype(BF16)

        ones = (1.0, 1.0, 1.0)
        q_ref[hd] = augment(head_tile(0) * scale, ones, (c1, c2, c3))
        k_ref[hd] = augment(head_tile(width), (-c1, -c2, -c3), ones)
        v_aug = jnp.where(lane < HEAD_DIM, head_tile(2 * width), jnp.where(lane == HEAD_DIM, 1.0, 0.0))
        v_ref[hd] = v_aug.T.astype(BF16)


def _fox_prep(qkv, small, bias_row, tri, batch, seq):
    t = qkv.shape[0]
    nb = seq // SEQ_BLOCK
    row = lambda b, c: (b * nb + c, 0)
    head_spec = pl.BlockSpec((N_FOX_HEADS, SEQ_BLOCK, LANES), lambda b, c: (0, b * nb + c, 0))
    head_shape = jax.ShapeDtypeStruct((N_FOX_HEADS, t, LANES), BF16)
    vt_spec = pl.BlockSpec((N_FOX_HEADS, LANES, SEQ_BLOCK), lambda b, c: (0, 0, b * nb + c))
    vt_shape = jax.ShapeDtypeStruct((N_FOX_HEADS, LANES, t), BF16)
    return pl.pallas_call(
        _fox_prep_kernel,
        grid=(batch, nb),
        in_specs=[
            pl.BlockSpec((SEQ_BLOCK, qkv.shape[1]), row),
            pl.BlockSpec((SEQ_BLOCK, LANES), row),
            pl.BlockSpec((1, LANES), lambda b, c: (0, 0)),
            pl.BlockSpec((SEQ_BLOCK, SEQ_BLOCK), lambda b, c: (0, 0)),
        ],
        out_specs=[head_spec, head_spec, vt_spec],
        out_shape=[head_shape, head_shape, vt_shape],
        scratch_shapes=[pltpu.VMEM((1, LANES), F32)],
        compiler_params=_cparams("parallel", "arbitrary"),
        name="fox_prep",
    )(qkv, small, bias_row, tri)


def _fox_attn_kernel(q_ref, k_ref, vt_ref, nw_ref, o_ref):
    i = pl.program_id(1)
    tq = q_ref.shape[1]
    tk = KV_BLOCK
    per_q = tq // tk
    kv_idx = lax.broadcasted_iota(jnp.int32, (tk, tq), 0)
    q_idx = lax.broadcasted_iota(jnp.int32, (tk, tq), 1)

    def step(j, carry, diag_offset):
        start = pl.multiple_of(j * tk, tk)
        logits = [_dot_nt(k_ref[hd, pl.ds(start, tk), :], q_ref[hd]) for hd in range(N_FOX_HEADS)]
        new = []
        for hd in range(N_FOX_HEADS):
            m_old, acc = carry[hd]
            s = logits[hd]
            if diag_offset is not None:
                s = jnp.where(kv_idx + diag_offset <= q_idx, s, MASK_VALUE)
            m_new = jnp.maximum(m_old, jnp.max(s, axis=0, keepdims=True))
            p = jnp.exp(s - m_new).astype(BF16)
            acc = jnp.exp(m_old - m_new) * acc + _dot(vt_ref[hd, :, pl.ds(start, tk)], p)
            new.append((m_new, acc))
        return tuple(new)

    init = tuple((jnp.full((1, tq), -jnp.inf, F32), jnp.zeros((LANES, tq), F32)) for _ in range(N_FOX_HEADS))
    carry = lax.fori_loop(0, i * per_q, lambda j, c: step(j, c, None), init)
    for a in range(per_q):
        carry = step(i * per_q + a, carry, a * tk)

    normed = []
    for hd in range(N_FOX_HEADS):
        acc = carry[hd][1]
        o = acc[:HEAD_DIM] / acc[HEAD_DIM:HEAD_DIM + 1]
        ms = jnp.mean(o * o, axis=0, keepdims=True)
        normed.append(o * lax.rsqrt(ms + EPS))
    tiles = []
    for pair in range(N_FOX_HEADS // 2):
        both = jnp.concatenate(normed[2 * pair:2 * pair + 2], axis=0)
        tiles.append(both.T * nw_ref[pair:pair + 1, :])
    o_ref[...] = jnp.concatenate(tiles, axis=1).astype(o_ref.dtype)


def _fox_attn(qa, ka, vt, nw_pairs, batch, seq):
    t = qa.shape[1]
    nq = seq // SEQ_BLOCK
    return pl.pallas_call(
        _fox_attn_kernel,
        grid=(batch, nq),
        in_specs=[
            pl.BlockSpec((N_FOX_HEADS, SEQ_BLOCK, LANES), lambda b, i: (0, b * nq + i, 0)),
            pl.BlockSpec((N_FOX_HEADS, seq, LANES), lambda b, i: (0, b, 0)),
            pl.BlockSpec((N_FOX_HEADS, LANES, seq), lambda b, i: (0, 0, b)),
            pl.BlockSpec((SUBLANES, LANES), lambda b, i: (0, 0)),
        ],
        out_specs=pl.BlockSpec((SEQ_BLOCK, N_FOX_HEADS * HEAD_DIM), lambda b, i: (b * nq + i, 0)),
        out_shape=jax.ShapeDtypeStruct((t, N_FOX_HEADS * HEAD_DIM), BF16),
        compiler_params=_cparams("parallel", "arbitrary"),
        name="fox_attn",
    )(qa, ka, vt, nw_pairs)


def _hgrn2_kernel(hq_ref, hf_ref, hi_ref, hg_ref, par_ref, tri_ref, seg_ref, o_ref, st_ref):
    @pl.when(pl.program_id(1) == 0)
    def _():
        st_ref[...] = jnp.zeros_like(st_ref)

    rows, width = hq_ref.shape
    log_lb, log1m_lb, one_m_lb, nw = (par_ref[r:r + 1, :] for r in range(4))
    f_raw = hf_ref[...]
    q = _silu(hq_ref[...]) * (HEAD_DIM ** -0.5)
    b = log1m_lb + _log_sigmoid(f_raw)
    g = jnp.maximum(log_lb, b) + jnp.log1p(jnp.exp(-jnp.abs(log_lb - b)))
    k = one_m_lb * jax.nn.sigmoid(-f_raw)
    v = hi_ref[...]
    cum = _exact_left(tri_ref[...], g)

    lane = lax.broadcasted_iota(jnp.int32, (1, width), 1)
    head_masks = [(lane >= hd * HEAD_DIM) & (lane < (hd + 1) * HEAD_DIM) for hd in range(N_HG_HEADS)]
    seg = seg_ref[...]
    k_b = k.astype(BF16)
    v_b = v.astype(BF16)

    st = st_ref[...]
    o_state = _dot_nt((q * jnp.exp(cum)).astype(BF16), st.astype(BF16))

    n_sub = rows // SUB_BLOCK
    refs = [cum[i * SUB_BLOCK - 1:i * SUB_BLOCK] if i else jnp.zeros((1, width), F32) for i in range(n_sub)]
    local = [cum[i * SUB_BLOCK:(i + 1) * SUB_BLOCK] - refs[i] for i in range(n_sub)]

    def stack_heads(x):
        return jnp.concatenate([jnp.where(mk, x, 0.0) for mk in head_masks], axis=0).astype(BF16)

    def unstack_heads(base, p4):
        for hd, mk in enumerate(head_masks):
            base = base + jnp.where(mk, p4[hd * SUB_BLOCK:(hd + 1) * SUB_BLOCK], 0.0)
        return base

    def factored():
        scores = []
        for i in range(n_sub):
            hi = (i + 1) * SUB_BLOCK
            qs = q[hi - SUB_BLOCK:hi] * jnp.exp(local[i])
            ks = (k[:hi] * jnp.exp(refs[i] - cum[:hi])).astype(BF16)
            scores.append(_dot_nt(stack_heads(qs), ks))
        blocks = []
        for i in range(n_sub):
            hi = (i + 1) * SUB_BLOCK
            t_in = lax.broadcasted_iota(jnp.int32, (N_HG_HEADS * SUB_BLOCK, hi), 0) & (SUB_BLOCK - 1)
            s_in = lax.broadcasted_iota(jnp.int32, (N_HG_HEADS * SUB_BLOCK, hi), 1)
            sc = jnp.where(s_in <= t_in + (hi - SUB_BLOCK), scores[i], 0.0).astype(BF16)
            blocks.append(unstack_heads(o_state[hi - SUB_BLOCK:hi], _dot(sc, v_b[:hi])))
        return jnp.concatenate(blocks, axis=0)

    def pairwise():
        t_idx = lax.broadcasted_iota(jnp.int32, (SUB_BLOCK, 1), 0)
        blocks = []
        for i in range(n_sub):
            r0 = i * SUB_BLOCK
            q_i = q[r0:r0 + SUB_BLOCK]
            cum_i = cum[r0:r0 + SUB_BLOCK]
            o_i = o_state[r0:r0 + SUB_BLOCK]
            if i > 0:
                ks = (k[:r0] * jnp.exp(refs[i] - cum[:r0])).astype(BF16)
                sc = _dot_nt(stack_heads(q_i * jnp.exp(local[i])), ks)
                o_i = unstack_heads(o_i, _dot(sc.astype(BF16), v_b[:r0]))
            terms = []
            for s in range(SUB_BLOCK):
                keep = t_idx >= s
                e = _masked_exp(cum_i - cum[r0 + s:r0 + s + 1], keep)
                terms.append((q_i * e * k[r0 + s:r0 + s + 1]).astype(BF16))
            sums = _dot(jnp.concatenate(terms, axis=0), seg)
            for s in range(SUB_BLOCK):
                o_i = o_i + sums[s * SUB_BLOCK:(s + 1) * SUB_BLOCK] * v[r0 + s:r0 + s + 1]
            blocks.append(o_i)
        return jnp.concatenate(blocks, axis=0)

    worst = functools.reduce(jnp.minimum, local)
    o = lax.cond(jnp.min(worst) >= -HG_SAFE_DECAY, factored, pairwise)

    last = cum[rows - 1:rows]
    kd = (k * jnp.exp(last - cum)).astype(BF16)
    upd = _dot(v.T.astype(BF16), kd)
    st_ref[...] = st * jnp.exp(last) + jnp.where(seg > 0, upd, 0.0)

    sq = o * o
    s1 = sq.astype(BF16)
    s2 = (sq - s1.astype(F32)).astype(BF16)
    ms = (_dot(s1, seg) + _dot(s2, seg)) * (1.0 / HEAD_DIM)
    o_ref[...] = (o * lax.rsqrt(ms + EPS) * nw * _silu(hg_ref[...])).astype(o_ref.dtype)


def _hgrn2(hg, params, tri, seg, batch, seq):
    t = hg.shape[0]
    width = N_HG_HEADS * HEAD_DIM
    nc = seq // SCAN_CHUNK
    col = lambda c: pl.BlockSpec((SCAN_CHUNK, width), lambda b, j, c=c: (b * nc + j, c))
    return pl.pallas_call(
        _hgrn2_kernel,
        grid=(batch, nc),
        in_specs=[
            col(0), col(1), col(2), col(3),
            pl.BlockSpec(params.shape, lambda b, j: (0, 0)),
            pl.BlockSpec(tri.shape, lambda b, j: (0, 0)),
            pl.BlockSpec(seg.shape, lambda b, j: (0, 0)),
        ],
        out_specs=pl.BlockSpec((SCAN_CHUNK, width), lambda b, j: (b * nc + j, 0)),
        out_shape=jax.ShapeDtypeStruct((t, width), BF16),
        scratch_shapes=[pltpu.VMEM((width, width), F32)],
        compiler_params=_cparams("parallel", "arbitrary"),
        name="hgrn2",
    )(hg, hg, hg, hg, params, tri, seg)


def _ssd_kernel(z_ref, xbc_ref, small_ref, cw_ref, cb_ref, hp_ref, sp_ref, nw_ref, tri_ref, exp_ref,
                o_ref, buf_ref, st_ref):
    rows = z_ref.shape[0]
    width = z_ref.shape[1]
    halo = SUBLANES

    @pl.when(pl.program_id(1) == 0)
    def _():
        buf_ref[0:halo, :] = jnp.zeros((halo, buf_ref.shape[1]), F32)
        st_ref[...] = jnp.zeros_like(st_ref)

    buf_ref[halo:halo + rows, :] = xbc_ref[...]
    conv = cb_ref[...]
    for i in range(M2_CONV):
        off = halo - (M2_CONV - 1) + i
        conv = conv + cw_ref[i:i + 1, :] * buf_ref[off:off + rows, :]
    tail = buf_ref[rows:rows + halo, :]
    buf_ref[0:halo, :] = tail
    xc = _silu(conv)
    xs = xc[:, :width]
    gn = M2_GROUPS * M2_STATE
    bm = xc[:, width:width + gn]
    cm = xc[:, width + gn:width + 2 * gn]

    dt_bias_x, a_log_x, d_x = (hp_ref[r:r + 1, :] for r in range(3))
    dt_bias_c, a_log_c = (sp_ref[r:r + 1, :] for r in range(2))
    small = small_ref[...]
    tri = tri_ref[...]
    dt_x = _softplus(_exact_right(small, exp_ref[...]) + dt_bias_x)
    a_cum_x = _exact_left(tri, dt_x * -jnp.exp(a_log_x))
    a_cum_c = _exact_left(tri, _softplus(small + dt_bias_c) * -jnp.exp(a_log_c))
    a_cum_t = a_cum_c.T

    row = lax.broadcasted_iota(jnp.int32, (rows, rows), 0)
    col = lax.broadcasted_iota(jnp.int32, (rows, rows), 1)
    causal = col <= row
    lane = lax.broadcasted_iota(jnp.int32, (1, LANES), 1)
    low = lane < HEAD_DIM

    xdt = xs * dt_x
    a_last = a_cum_x[rows - 1:rows]
    xd = (xdt * jnp.exp(a_last - a_cum_x)).astype(BF16)
    decay_out = jnp.exp(a_cum_x)
    st = st_ref[...]
    heads_per_group = M2_HEADS // M2_GROUPS
    gw = heads_per_group * HEAD_DIM
    y_tiles = []
    new_states = []
    for g in range(M2_GROUPS):
        b_g = bm[:, g * M2_STATE:(g + 1) * M2_STATE]
        c_g = cm[:, g * M2_STATE:(g + 1) * M2_STATE].astype(BF16)
        cb = _dot_nt(c_g, b_g.astype(BF16))
        y_off = _dot(c_g, st[:, g * gw:(g + 1) * gw].astype(BF16)) * decay_out[:, g * gw:(g + 1) * gw]
        new_states.append(_dot(b_g.T.astype(BF16), xd[:, g * gw:(g + 1) * gw]))
        for pair in range(heads_per_group // 2):
            tile = g * (heads_per_group // 2) + pair
            x_tile = xdt[:, tile * LANES:(tile + 1) * LANES]
            y_pair = y_off[:, pair * LANES:(pair + 1) * LANES]
            for half in range(2):
                hd = 2 * tile + half
                seg = a_cum_c[:, SUBLANES + hd:SUBLANES + hd + 1] - a_cum_t[SUBLANES + hd:SUBLANES + hd + 1, :]
                m_h = (cb * _masked_exp(seg, causal)).astype(BF16)
                x_h = jnp.where(low if half == 0 else ~low, x_tile, 0.0).astype(BF16)
                y_pair = y_pair + _dot(m_h, x_h)
            y_tiles.append(y_pair)
    st_ref[...] = st * jnp.exp(a_last) + jnp.concatenate(new_states, axis=1)

    y = jnp.concatenate(y_tiles, axis=1) + d_x * xs
    y = y * _silu(z_ref[...])
    nw = nw_ref[...]
    outs = []
    for g in range(M2_GROUPS):
        outs.append(_rms(y[:, g * gw:(g + 1) * gw], nw[:, g * gw:(g + 1) * gw]))
    o_ref[...] = jnp.concatenate(outs, axis=1).astype(o_ref.dtype)


def _ssd(z, xbc, small, conv_w, conv_b, head_rows, small_rows, norm_w, tri, expand, batch, seq):
    t, width = z.shape
    ch = xbc.shape[1]
    nc = seq // SCAN_CHUNK
    row = lambda b, j: (b * nc + j, 0)
    const = lambda a: pl.BlockSpec(a.shape, lambda b, j: (0, 0))
    return pl.pallas_call(
        _ssd_kernel,
        grid=(batch, nc),
        in_specs=[
            pl.BlockSpec((SCAN_CHUNK, width), row),
            pl.BlockSpec((SCAN_CHUNK, ch), row),
            pl.BlockSpec((SCAN_CHUNK, LANES), row),
            const(conv_w), const(conv_b), const(head_rows), const(small_rows), const(norm_w),
            const(tri), const(expand),
        ],
        out_specs=pl.BlockSpec((SCAN_CHUNK, width), row),
        out_shape=jax.ShapeDtypeStruct((t, width), BF16),
        scratch_shapes=[
            pltpu.VMEM((SCAN_CHUNK + SUBLANES, ch), F32),
            pltpu.VMEM((M2_STATE, width), F32),
        ],
        compiler_params=_cparams("parallel", "arbitrary"),
        name="ssd",
    )(z, xbc, small, conv_w, conv_b, head_rows, small_rows, norm_w, tri, expand)


def _out_proj_kernel(h_ref, a_ref, b_ref, c_ref, w_ref, o_ref):
    wa = a_ref.shape[1]
    wb = b_ref.shape[1]
    acc = h_ref[...] + _dot(a_ref[...], w_ref[0:wa, :])
    acc = acc + _dot(b_ref[...], w_ref[wa:wa + wb, :])
    o_ref[...] = acc + _dot(c_ref[...], w_ref[wa + wb:, :])


def _out_proj(h, a, b, c, w):
    t, d = h.shape
    row = lambda x: pl.BlockSpec((ROW_BLOCK, x.shape[1]), lambda i: (i, 0))
    return pl.pallas_call(
        _out_proj_kernel,
        grid=(t // ROW_BLOCK,),
        in_specs=[row(h), row(a), row(b), row(c), pl.BlockSpec(w.shape, lambda i: (0, 0))],
        out_specs=row(h),
        out_shape=jax.ShapeDtypeStruct((t, d), F32),
        compiler_params=_cparams("parallel"),
        name="out_proj",
    )(h, a, b, c, w)


def _xattn_kernel(h_ref, nw_ref, wq_ref, kv_ref, wo_ref, o_ref):
    h = h_ref[...]
    d = h.shape[1]
    hd = d // XA_HEADS
    q = (_dot(_rms(h, nw_ref[...]).astype(BF16), wq_ref[...]) * (hd ** -0.5)).astype(BF16)
    outs = []
    for a in range(XA_HEADS):
        k_a = kv_ref[:, a * hd:(a + 1) * hd]
        v_a = kv_ref[:, d + a * hd:d + (a + 1) * hd]
        s = _dot_nt(q[:, a * hd:(a + 1) * hd], k_a)
        p = jnp.exp(s - jnp.max(s, axis=-1, keepdims=True))
        p = p / jnp.sum(p, axis=-1, keepdims=True)
        outs.append(_dot(p.astype(BF16), v_a).astype(BF16))
    o_ref[...] = h + _dot(jnp.concatenate(outs, axis=1), wo_ref[...])


def _xattn(h, nw, wq, kv, wo, batch, seq):
    t, d = h.shape
    n_mem = kv.shape[0] // batch
    nb = seq // ROW_BLOCK
    row = lambda b, i: (b * nb + i, 0)
    const = lambda a: pl.BlockSpec(a.shape, lambda b, i: (0, 0))
    return pl.pallas_call(
        _xattn_kernel,
        grid=(batch, nb),
        in_specs=[
            pl.BlockSpec((ROW_BLOCK, d), row),
            const(nw), const(wq),
            pl.BlockSpec((n_mem, kv.shape[1]), lambda b, i: (b, 0)),
            const(wo),
        ],
        out_specs=pl.BlockSpec((ROW_BLOCK, d), row),
        out_shape=jax.ShapeDtypeStruct((t, d), F32),
        compiler_params=_cparams("parallel", "parallel"),
        name="xattn",
    )(h, nw, wq, kv, wo)


def _ffn_tile(d_ff):
    for cand in (1408, 1024, 768, 512, 256, 128):
        if d_ff % cand == 0:
            return cand
    raise ValueError(d_ff)


def _ffn_kernel(h_ref, nw_ref, wg_ref, wu_ref, w2_ref, o_ref, xn_ref):
    @pl.when(pl.program_id(1) == 0)
    def _():
        xn_ref[...] = _rms(h_ref[...], nw_ref[...]).astype(BF16)
        o_ref[...] = h_ref[...]

    xn = xn_ref[...]
    a = (_silu(_dot(xn, wg_ref[...])) * _dot(xn, wu_ref[...])).astype(BF16)
    o_ref[...] += _dot(a, w2_ref[...])


def _ffn(h, nw, w1, w2):
    t, d = h.shape
    d_ff = w2.shape[0]
    tf = _ffn_tile(d_ff)
    nf = d_ff // tf
    return pl.pallas_call(
        _ffn_kernel,
        grid=(t // ROW_BLOCK, nf),
        in_specs=[
            pl.BlockSpec((ROW_BLOCK, d), lambda i, j: (i, 0)),
            pl.BlockSpec((1, d), lambda i, j: (0, 0)),
            pl.BlockSpec((d, tf), lambda i, j: (0, j)),
            pl.BlockSpec((d, tf), lambda i, j: (0, j + nf)),
            pl.BlockSpec((tf, d), lambda i, j: (j, 0)),
        ],
        out_specs=pl.BlockSpec((ROW_BLOCK, d), lambda i, j: (i, 0)),
        out_shape=jax.ShapeDtypeStruct((t, d), F32),
        scratch_shapes=[pltpu.VMEM((ROW_BLOCK, d), BF16)],
        compiler_params=_cparams("parallel", "arbitrary"),
        name="ffn",
    )(h, nw, w1, w1, w2)


def _router_kernel(h_ref, nw_ref, wr_ref, tri_ref, xn_ref, idx_ref, gate_ref, rank_ref, cnt_ref, carry_ref):
    @pl.when(pl.program_id(0) == 0)
    def _():
        carry_ref[...] = jnp.zeros_like(carry_ref)

    xn = _rms(h_ref[...], nw_ref[...])
    xn_ref[...] = xn
    x1 = xn.astype(BF16)
    x2 = (xn - x1.astype(F32)).astype(BF16)
    w1, w2, w3 = wr_ref[0], wr_ref[1], wr_ref[2]
    logits = _dot(x1, w1) + _dot(x1, w2) + _dot(x2, w1) + _dot(x1, w3) + _dot(x2, w2)
    lane = lax.broadcasted_iota(jnp.int32, logits.shape, 1)
    logits = jnp.where(lane < N_EXPERTS, logits, -jnp.inf)
    m1 = jnp.max(logits, axis=-1, keepdims=True)
    i1 = jnp.min(jnp.where(logits == m1, lane, LANES), axis=-1, keepdims=True)
    rest = jnp.where(lane == i1, -jnp.inf, logits)
    m2 = jnp.max(rest, axis=-1, keepdims=True)
    i2 = jnp.min(jnp.where(rest == m2, lane, LANES), axis=-1, keepdims=True)
    e2 = jnp.exp(m2 - m1)
    denom = 1.0 + e2
    idx_ref[...] = jnp.where(lane == 0, i1, jnp.where(lane == 1, i2, 0))
    gate_ref[...] = jnp.where(lane == 0, 1.0 / denom, jnp.where(lane == 1, e2 / denom, 0.0))

    hit1 = lane == i1
    hit2 = lane == i2
    member = jnp.where(hit1, 1.0, jnp.where(hit2, 1.0, 0.0))
    incl = _dot(tri_ref[...], member.astype(BF16)) + carry_ref[...]
    excl = incl - member
    r1 = jnp.sum(jnp.where(hit1, excl, 0.0), axis=-1, keepdims=True)
    r2 = jnp.sum(jnp.where(hit2, excl, 0.0), axis=-1, keepdims=True)
    rows = member.shape[0]
    carry_ref[...] = incl[rows - 1:rows, :]
    rank_ref[...] = jnp.where(lane == 0, r1, jnp.where(lane == 1, r2, 0.0)).astype(jnp.int32)
    cnt_ref[...] = jnp.broadcast_to(incl[rows - 1:rows, :], cnt_ref.shape)


def _router(h, nw, wr3, tri):
    t, d = h.shape
    row = lambda w: pl.BlockSpec((ROW_BLOCK, w), lambda i: (i, 0))
    return pl.pallas_call(
        _router_kernel,
        grid=(t // ROW_BLOCK,),
        in_specs=[
            row(d),
            pl.BlockSpec((1, d), lambda i: (0, 0)),
            pl.BlockSpec(wr3.shape, lambda i: (0, 0, 0)),
            pl.BlockSpec(tri.shape, lambda i: (0, 0)),
        ],
        out_specs=[row(d), row(LANES), row(LANES), row(LANES), pl.BlockSpec((SUBLANES, LANES), lambda i: (0, 0))],
        out_shape=[
            jax.ShapeDtypeStruct((t, d), F32),
            jax.ShapeDtypeStruct((t, LANES), jnp.int32),
            jax.ShapeDtypeStruct((t, LANES), F32),
            jax.ShapeDtypeStruct((t, LANES), jnp.int32),
            jax.ShapeDtypeStruct((SUBLANES, LANES), F32),
        ],
        scratch_shapes=[pltpu.VMEM((1, LANES), F32)],
        compiler_params=_cparams("arbitrary"),
        name="router",
    )(h, nw, wr3, tri)


def _slot_kernel(idx_ref, rank_ref, start_ref, o_ref):
    idx = idx_ref[...]
    lane = lax.broadcasted_iota(jnp.int32, idx.shape, 1)
    starts = start_ref[...]
    s1 = jnp.sum(jnp.where(lane == idx[:, 0:1], starts, 0.0), axis=-1, keepdims=True)
    s2 = jnp.sum(jnp.where(lane == idx[:, 1:2], starts, 0.0), axis=-1, keepdims=True)
    o_ref[...] = jnp.where(lane == 0, s1, jnp.where(lane == 1, s2, 0.0)).astype(jnp.int32) + rank_ref[...]


def _slots(idx, rank, start_row):
    t = idx.shape[0]
    row = pl.BlockSpec((ROW_BLOCK, LANES), lambda i: (i, 0))
    return pl.pallas_call(
        _slot_kernel,
        grid=(t // ROW_BLOCK,),
        in_specs=[row, row, pl.BlockSpec((1, LANES), lambda i: (0, 0))],
        out_specs=row,
        out_shape=jax.ShapeDtypeStruct((t, LANES), jnp.int32),
        compiler_params=_cparams("parallel"),
        name="moe_slots",
    )(idx, rank, start_row)


def _dispatch_kernel(pend_ref, padded_ref, nused_ref, dest_ref, x_ref, xs_ref, zero_ref, sem):
    rows = x_ref.shape[0]
    n_blk = xs_ref.shape[0] // MOE_ROWS

    @pl.when(pl.program_id(0) == 0)
    def _():
        zero_ref[...] = jnp.zeros_like(zero_ref)

        def fill_block(start):
            fill = pltpu.make_async_copy(zero_ref, xs_ref.at[pl.ds(start, MOE_ROWS), :], sem)
            fill.start()
            fill.wait()

        for e in range(N_EXPERTS):
            @pl.when(padded_ref[e] > 0)
            def _():
                fill_block(pl.multiple_of(pend_ref[e] - MOE_ROWS, MOE_ROWS))

            @pl.when(nused_ref[0] + e < n_blk)
            def _():
                fill_block(pl.multiple_of((nused_ref[0] + e) * MOE_ROWS, MOE_ROWS))

    def issue(t, carry):
        for k in range(2):
            slot = dest_ref[0, k, t]
            pltpu.make_async_copy(x_ref.at[pl.ds(t, 1), :], xs_ref.at[pl.ds(slot, 1), :], sem).start()
        return carry

    lax.fori_loop(0, rows, issue, 0, unroll=8)
    for k in range(2):
        pltpu.make_async_copy(x_ref, xs_ref.at[pl.ds(0, rows), :], sem).wait()


def _dispatch(pends, padded, n_used, dest_t, xn, cap):
    t, d = xn.shape
    grid_spec = pltpu.PrefetchScalarGridSpec(
        num_scalar_prefetch=3,
        grid=(t // ROW_BLOCK,),
        in_specs=[
            pl.BlockSpec((1, 2, ROW_BLOCK), lambda i, pe, pa, nu: (i, 0, 0), memory_space=pltpu.SMEM),
            pl.BlockSpec((ROW_BLOCK, d), lambda i, pe, pa, nu: (i, 0)),
        ],
        out_specs=pl.BlockSpec(memory_space=pl.ANY),
        scratch_shapes=[pltpu.VMEM((MOE_ROWS, d), F32), pltpu.SemaphoreType.DMA(())],
    )
    return pl.pallas_call(
        _dispatch_kernel,
        grid_spec=grid_spec,
        out_shape=jax.ShapeDtypeStruct((cap, d), F32),
        compiler_params=_cparams("arbitrary"),
        name="moe_dispatch",
    )(pends, padded, n_used, dest_t, xn)


def _expert_kernel(be_ref, nused_ref, x_ref, wg_ref, wu_ref, w2_ref, o_ref, xb_ref):
    j = pl.program_id(1)
    live = pl.program_id(0) < nused_ref[0]

    @pl.when(live)
    def _():
        @pl.when(j == 0)
        def _():
            xb_ref[...] = x_ref[...].astype(BF16)

        x = xb_ref[...]
        a = (_silu(_dot(x, wg_ref[0])) * _dot(x, wu_ref[0])).astype(BF16)
        y = _dot(a, w2_ref[0])

        @pl.when(j == 0)
        def _():
            o_ref[...] = y

        @pl.when(j > 0)
        def _():
            o_ref[...] += y

    @pl.when(jnp.logical_not(live) & (j == 0))
    def _():
        o_ref[...] = jnp.zeros_like(o_ref)


def _experts(blk_expert, n_used, xs, w1, w2):
    cap, d = xs.shape
    d_ff = w2.shape[1]
    tf = _ffn_tile(d_ff)
    nf = d_ff // tf
    blk = lambda i, j, be, nu: (i, 0)
    grid_spec = pltpu.PrefetchScalarGridSpec(
        num_scalar_prefetch=2,
        grid=(cap // MOE_ROWS, nf),
        in_specs=[
            pl.BlockSpec((MOE_ROWS, d), blk),
            pl.BlockSpec((1, d, tf), lambda i, j, be, nu: (be[i], 0, j)),
            pl.BlockSpec((1, d, tf), lambda i, j, be, nu: (be[i], 0, j + nf)),
            pl.BlockSpec((1, tf, d), lambda i, j, be, nu: (be[i], j, 0)),
        ],
        out_specs=pl.BlockSpec((MOE_ROWS, d), blk),
        scratch_shapes=[pltpu.VMEM((MOE_ROWS, d), BF16)],
    )
    return pl.pallas_call(
        _expert_kernel,
        grid_spec=grid_spec,
        out_shape=jax.ShapeDtypeStruct((cap, d), F32),
        compiler_params=_cparams("arbitrary", "arbitrary"),
        name="experts",
    )(blk_expert, n_used, xs, w1, w1, w2)


def _combine_kernel(dest_ref, h_ref, gate_ref, nw_ref, yb_ref, o_ref, buf_ref, sem, *, final_norm):
    rows = h_ref.shape[0]

    def issue(t, carry):
        for k in range(2):
            slot = dest_ref[0, k, t]
            pltpu.make_async_copy(yb_ref.at[pl.ds(slot, 1), :], buf_ref.at[k, pl.ds(t, 1), :], sem).start()
        return carry

    lax.fori_loop(0, rows, issue, 0, unroll=8)
    for k in range(2):
        pltpu.make_async_copy(yb_ref.at[pl.ds(0, rows), :], buf_ref.at[k], sem).wait()
    gate = gate_ref[...]
    out = h_ref[...] + gate[:, 0:1] * buf_ref[0] + gate[:, 1:2] * buf_ref[1]
    if final_norm:
        out = _rms(out, nw_ref[...])
    o_ref[...] = out


def _combine(dest_t, h, gate, nw, yb, final_norm):
    t, d = h.shape
    return pl.pallas_call(
        functools.partial(_combine_kernel, final_norm=final_norm),
        grid=(t // ROW_BLOCK,),
        in_specs=[
            pl.BlockSpec((1, 2, ROW_BLOCK), lambda i: (i, 0, 0), memory_space=pltpu.SMEM),
            pl.BlockSpec((ROW_BLOCK, d), lambda i: (i, 0)),
            pl.BlockSpec((ROW_BLOCK, LANES), lambda i: (i, 0)),
            pl.BlockSpec((1, d), lambda i: (0, 0)),
            pl.BlockSpec(memory_space=pl.ANY),
        ],
        out_specs=pl.BlockSpec((ROW_BLOCK, d), lambda i: (i, 0)),
        out_shape=jax.ShapeDtypeStruct((t, d), F32),
        scratch_shapes=[pltpu.VMEM((2, ROW_BLOCK, d), F32), pltpu.SemaphoreType.DMA(())],
        compiler_params=_cparams("arbitrary"),
        name="moe_combine",
    )(dest_t, h, gate, nw, yb)


def _final_norm_kernel(h_ref, nw_ref, o_ref):
    o_ref[...] = _rms(h_ref[...], nw_ref[...])


def _final_norm(h, nw):
    t, d = h.shape
    return pl.pallas_call(
        _final_norm_kernel,
        grid=(t // ROW_BLOCK,),
        in_specs=[pl.BlockSpec((ROW_BLOCK, d), lambda i: (i, 0)), pl.BlockSpec((1, d), lambda i: (0, 0))],
        out_specs=pl.BlockSpec((ROW_BLOCK, d), lambda i: (i, 0)),
        out_shape=jax.ShapeDtypeStruct((t, d), F32),
        compiler_params=_cparams("parallel"),
        name="final_norm",
    )(h, nw)


def _moe(h, nw, router_w, w1, w2, tri, out_nw, final_norm):
    t, d = h.shape
    assert MOE_ROWS == ROW_BLOCK
    r1, r2, r3 = _split3(jnp.pad(router_w, ((0, 0), (0, LANES - N_EXPERTS))))
    xn, idx, gate, rank, cnt = _router(h, nw, jnp.stack([r1, r2, r3]), tri)
    counts = cnt[0, :N_EXPERTS].astype(jnp.int32)
    padded = (counts + MOE_ROWS - 1) // MOE_ROWS * MOE_ROWS
    pends = jnp.cumsum(padded)
    starts = pends - padded
    cap = (2 * t // MOE_ROWS + N_EXPERTS) * MOE_ROWS
    n_blk = cap // MOE_ROWS
    blk_start = jnp.arange(n_blk, dtype=jnp.int32) * MOE_ROWS
    blk_expert = jnp.minimum(jnp.sum(blk_start[:, None] >= pends[None, :], axis=1), N_EXPERTS - 1).astype(jnp.int32)
    n_used = (pends[-1:] // MOE_ROWS).astype(jnp.int32)
    start_row = jnp.zeros((1, LANES), F32).at[0, :N_EXPERTS].set(starts.astype(F32))
    dest = _slots(idx, rank, start_row)
    dest_t = dest[:, :2].reshape(t // ROW_BLOCK, ROW_BLOCK, 2).transpose(0, 2, 1)
    xs = _dispatch(pends.astype(jnp.int32), padded.astype(jnp.int32), n_used, dest_t, xn, cap)
    yb = _experts(blk_expert, n_used, xs, w1, w2)
    return _combine(dest_t, h, gate, out_nw, yb, final_norm)


def _tri(n):
    return jnp.tril(jnp.ones((n, n), F32)).astype(BF16)


def kernel(x, mem, mix_norm_w, w_in, fox_f_bias, fox_norm_w, hg_lb_raw, hg_norm_w, m2_conv_w, m2_conv_b, m2_dt_bias, m2_a_log, m2_d, m2_norm_w, w_out, xa_norm_w, xa_mem_norm_w, xa_w_q, xa_w_kv, xa_w_o, ffn_norm_w, ffn_w1, ffn_w2, moe_router, moe_w1, moe_w2, final_norm_w):
    batch, seq, d = x.shape
    depth = w_in.shape[0]
    t = batch * seq
    fox_w = N_FOX_HEADS * HEAD_DIM
    hg_w = N_HG_HEADS * HEAD_DIM
    m2_w = M2_HEADS * HEAD_DIM
    conv_ch = m2_w + 2 * M2_GROUPS * M2_STATE
    in_splits = (fox_w, fox_w, fox_w, N_FOX_HEADS, hg_w, hg_w, hg_w, hg_w, m2_w, conv_ch, M2_HEADS)
    offs = [0]
    for s in in_splits:
        offs.append(offs[-1] + s)

    lb_p = jax.nn.softmax(hg_lb_raw.astype(F32), axis=0)
    hg_lb = jnp.cumsum(lb_p, axis=0) - lb_p[0]

    tri_seq = _tri(SEQ_BLOCK)
    tri_chunk = _tri(SCAN_CHUNK)
    head_of_lane = jnp.arange(hg_w) // HEAD_DIM
    seg_ones = (head_of_lane[:, None] == head_of_lane[None, :]).astype(BF16)
    expand = (jnp.arange(LANES)[:, None] - SUBLANES == jnp.arange(m2_w)[None, :] // HEAD_DIM).astype(BF16)

    def pad_lanes(v, offset=0, width=LANES):
        return jnp.zeros((1, width), F32).at[0, offset:offset + v.shape[0]].set(v)

    h = x.reshape(t, d)
    mem2 = mem.reshape(batch * mem.shape[1], d)
    for layer in range(depth):
        w = w_in[layer]
        small_w = jnp.zeros((d, LANES), F32)
        small_w = small_w.at[:, 0:N_FOX_HEADS].set(w[:, offs[3]:offs[4]])
        small_w = small_w.at[:, SUBLANES:SUBLANES + M2_HEADS].set(w[:, offs[10]:offs[11]])
        w_cat = jnp.concatenate(
            [w[:, offs[0]:offs[3]], small_w, w[:, offs[4]:offs[8]], w[:, offs[8]:offs[9]], w[:, offs[9]:offs[10]]],
            axis=1).astype(BF16)
        splits = (3 * fox_w, LANES, 4 * hg_w, m2_w, conv_ch)
        qkv, small, hg, z, xbc = _in_proj(h, mix_norm_w[layer][None, :], w_cat, splits)

        qa, ka, va = _fox_prep(qkv, small, pad_lanes(fox_f_bias[layer]), tri_seq, batch, seq)
        fox_nw = jnp.zeros((SUBLANES, LANES), F32).at[:fox_w // LANES].set(
            fox_norm_w[layer].reshape(fox_w // LANES, LANES))
        o_fox = _fox_attn(qa, ka, va, fox_nw, batch, seq)

        lb = hg_lb[layer]
        hg_params = jnp.zeros((SUBLANES, hg_w), F32)
        hg_params = hg_params.at[0].set(jnp.log(jnp.maximum(lb, LB_FLOOR)))
        hg_params = hg_params.at[1].set(jnp.log1p(-lb))
        hg_params = hg_params.at[2].set(1.0 - lb)
        hg_params = hg_params.at[3].set(hg_norm_w[layer])
        o_hg = _hgrn2(hg, hg_params, tri_chunk, seg_ones, batch, seq)

        conv_w = jnp.zeros((SUBLANES, conv_ch), F32).at[:M2_CONV].set(m2_conv_w[layer])
        head_rows = jnp.zeros((SUBLANES, m2_w), F32)
        head_rows = head_rows.at[0].set(jnp.repeat(m2_dt_bias[layer], HEAD_DIM))
        head_rows = head_rows.at[1].set(jnp.repeat(m2_a_log[layer], HEAD_DIM))
        head_rows = head_rows.at[2].set(jnp.repeat(m2_d[layer], HEAD_DIM))
        small_rows = jnp.concatenate(
            [pad_lanes(m2_dt_bias[layer], SUBLANES), pad_lanes(m2_a_log[layer], SUBLANES),
             jnp.zeros((SUBLANES - 2, LANES), F32)], axis=0)
        o_m2 = _ssd(z, xbc, small, conv_w, m2_conv_b[layer][None, :], head_rows, small_rows,
                    m2_norm_w[layer][None, :], tri_chunk, expand, batch, seq)

        h = _out_proj(h, o_fox, o_hg, o_m2, w_out[layer].astype(BF16))

        kv = _norm_proj(mem2, xa_mem_norm_w[layer][None, :], xa_w_kv[layer].astype(BF16), BF16, mem.shape[1])
        h = _xattn(h, xa_norm_w[layer][None, :], xa_w_q[layer].astype(BF16), kv, xa_w_o[layer].astype(BF16),
                   batch, seq)

        nw = ffn_norm_w[layer][None, :]
        if layer % 2 == 0:
            h = _ffn(h, nw, ffn_w1[layer // 2].astype(BF16), ffn_w2[layer // 2].astype(BF16))
        else:
            last = layer == depth - 1
            h = _moe(h, nw, moe_router[layer // 2], moe_w1[layer // 2].astype(BF16),
                     moe_w2[layer // 2].astype(BF16), tri_seq, final_norm_w[None, :], last)
    if depth % 2:
        h = _final_norm(h, final_norm_w[None, :])
    return h.reshape(batch, seq, d)
```

```python
import functools
import math

import jax
import jax.numpy as jnp
from jax import lax
from jax.experimental import pallas as pl
from jax.experimental.pallas import tpu as pltpu

F32 = jnp.float32
BF16 = jnp.bfloat16

EPS = 1e-6
MASK_VALUE = -1e9
LB_FLOOR = 1e-30
HEAD_DIM = 64
N_FOX_HEADS = 4
N_HG_HEADS = 4
M2_HEADS = 8
M2_GROUPS = 2
M2_STATE = 128
M2_CONV = 4
XA_HEADS = 4
N_EXPERTS = 8

LANES = 128
SUBLANES = 8
VMEM_LIMIT_BYTES = 56 * 1024 * 1024

ROW_BLOCK = 512
SEQ_BLOCK = 512
KV_BLOCK = 512
SCAN_CHUNK = 128
SUB_BLOCK = 16
HG_SAFE_DECAY = 60.0
MOE_ROWS = 512


def _cparams(*sem):
    return pltpu.CompilerParams(dimension_semantics=sem, vmem_limit_bytes=VMEM_LIMIT_BYTES)


def _split3(x):
    x1 = x.astype(BF16)
    r1 = x - x1.astype(F32)
    x2 = r1.astype(BF16)
    x3 = (r1 - x2.astype(F32)).astype(BF16)
    return x1, x2, x3


def _dot(a, b):
    return jnp.dot(a, b, preferred_element_type=F32)


def _dot_nt(a, b):
    return lax.dot_general(a, b, (((1,), (1,)), ((), ())), preferred_element_type=F32)


def _exact_left(mat01, x):
    x1, x2, x3 = _split3(x)
    return _dot(mat01, x1) + _dot(mat01, x2) + _dot(mat01, x3)


def _exact_right(x, mat01):
    x1, x2, x3 = _split3(x)
    return _dot(x1, mat01) + _dot(x2, mat01) + _dot(x3, mat01)


def _rms(x, w):
    return x * lax.rsqrt(jnp.mean(x * x, axis=-1, keepdims=True) + EPS) * w


def _silu(x):
    return x * jax.nn.sigmoid(x)


def _log_sigmoid(x):
    return jnp.minimum(x, 0.0) - jnp.log1p(jnp.exp(-jnp.abs(x)))


def _softplus(x):
    return jnp.maximum(x, 0.0) + jnp.log1p(jnp.exp(-jnp.abs(x)))


def _masked_exp(x, mask):
    return jnp.where(mask, jnp.exp(jnp.where(mask, x, 0.0)), 0.0)


def _in_proj_kernel(x_ref, nw_ref, w_ref, fox_ref, small_ref, hg_ref, z_ref, xbc_ref, *, splits):
    xn = _rms(x_ref[...], nw_ref[...]).astype(BF16)
    outs = (fox_ref, small_ref, hg_ref, z_ref, xbc_ref)
    off = 0
    for o_ref, width in zip(outs, splits):
        o_ref[...] = _dot(xn, w_ref[:, off:off + width]).astype(o_ref.dtype)
        off += width


def _in_proj(h, nw, w_cat, splits):
    t, d = h.shape
    out_dtypes = (BF16, F32, F32, F32, F32)
    return pl.pallas_call(
        functools.partial(_in_proj_kernel, splits=splits),
        grid=(t // ROW_BLOCK,),
        in_specs=[
            pl.BlockSpec((ROW_BLOCK, d), lambda i: (i, 0)),
            pl.BlockSpec((1, d), lambda i: (0, 0)),
            pl.BlockSpec(w_cat.shape, lambda i: (0, 0)),
        ],
        out_specs=[pl.BlockSpec((ROW_BLOCK, w), lambda i: (i, 0)) for w in splits],
        out_shape=[jax.ShapeDtypeStruct((t, w), dt) for w, dt in zip(splits, out_dtypes)],
        compiler_params=_cparams("parallel"),
        name="in_proj",
    )(h, nw, w_cat)


def _norm_proj_kernel(x_ref, nw_ref, w_ref, o_ref):
    xn = _rms(x_ref[...], nw_ref[...]).astype(BF16)
    o_ref[...] = _dot(xn, w_ref[...]).astype(o_ref.dtype)


def _norm_proj(x, nw, w, out_dtype, rows):
    t, d = x.shape
    n = w.shape[1]
    return pl.pallas_call(
        _norm_proj_kernel,
        grid=(t // rows,),
        in_specs=[
            pl.BlockSpec((rows, d), lambda i: (i, 0)),
            pl.BlockSpec((1, d), lambda i: (0, 0)),
            pl.BlockSpec(w.shape, lambda i: (0, 0)),
        ],
        out_specs=pl.BlockSpec((rows, n), lambda i: (i, 0)),
        out_shape=jax.ShapeDtypeStruct((t, n), out_dtype),
        compiler_params=_cparams("parallel"),
        name="norm_proj",
    )(x, nw, w)


def _fox_prep_kernel(qkv_ref, small_ref, bias_ref, tri_ref, q_ref, k_ref, v_ref, carry_ref):
    @pl.when(pl.program_id(1) == 0)
    def _():
        carry_ref[...] = jnp.zeros_like(carry_ref)

    rows = qkv_ref.shape[0]
    log_f = _log_sigmoid(small_ref[...] + bias_ref[...])
    c = _exact_left(tri_ref[...], log_f) + carry_ref[...]
    carry_ref[...] = c[rows - 1:rows, :]

    lane = lax.broadcasted_iota(jnp.int32, (rows, LANES), 1)
    width = N_FOX_HEADS * HEAD_DIM
    scale = HEAD_DIM ** -0.5
    for hd in range(N_FOX_HEADS):
        tile = (hd * HEAD_DIM) // LANES
        ch = c[:, hd:hd + 1]
        c1 = ch.astype(BF16).astype(F32)
        r1 = ch - c1
        c2 = r1.astype(BF16).astype(F32)
        c3 = r1 - c2

        def head_tile(base):
            x = qkv_ref[:, base + tile * LANES: base + (tile + 1) * LANES].astype(F32)
            if (hd * HEAD_DIM) % LANES:
                x = pltpu.roll(x, LANES - (hd * HEAD_DIM) % LANES, axis=1)
            return x

        def augment(x, first, second):
            out = jnp.where(lane < HEAD_DIM, x, 0.0)
            for j, val in enumerate(first + second):
                out = jnp.where(lane == HEAD_DIM + j, val, out)
            return out.astype(BF16)

        ones = (1.0, 1.0, 1.0)
        q_ref[hd] = augment(head_tile(0) * scale, ones, (c1, c2, c3))
        k_ref[hd] = augment(head_tile(width), (-c1, -c2, -c3), ones)
        v_aug = jnp.where(lane < HEAD_DIM, head_tile(2 * width), jnp.where(lane == HEAD_DIM, 1.0, 0.0))
        v_ref[hd] = v_aug.T.astype(BF16)


def _fox_prep(qkv, small, bias_row, tri, batch, seq):
    t = qkv.shape[0]
    nb = seq // SEQ_BLOCK
    row = lambda b, c: (b * nb + c, 0)
    head_spec = pl.BlockSpec((N_FOX_HEADS, SEQ_BLOCK, LANES), lambda b, c: (0, b * nb + c, 0))
    head_shape = jax.ShapeDtypeStruct((N_FOX_HEADS, t, LANES), BF16)
    vt_spec = pl.BlockSpec((N_FOX_HEADS, LANES, SEQ_BLOCK), lambda b, c: (0, 0, b * nb + c))
    vt_shape = jax.ShapeDtypeStruct((N_FOX_HEADS, LANES, t), BF16)
    return pl.pallas_call(
        _fox_prep_kernel,
        grid=(batch, nb),
        in_specs=[
            pl.BlockSpec((SEQ_BLOCK, qkv.shape[1]), row),
            pl.BlockSpec((SEQ_BLOCK, LANES), row),
            pl.BlockSpec((1, LANES), lambda b, c: (0, 0)),
            pl.BlockSpec((SEQ_BLOCK, SEQ_BLOCK), lambda b, c: (0, 0)),
        ],
        out_specs=[head_spec, head_spec, vt_spec],
        out_shape=[head_shape, head_shape, vt_shape],
        scratch_shapes=[pltpu.VMEM((1, LANES), F32)],
        compiler_params=_cparams("parallel", "arbitrary"),
        name="fox_prep",
    )(qkv, small, bias_row, tri)


def _fox_attn_kernel(q_ref, k_ref, vt_ref, nw_ref, o_ref):
    i = pl.program_id(1)
    tq = q_ref.shape[1]
    tk = KV_BLOCK
    per_q = tq // tk
    kv_idx = lax.broadcasted_iota(jnp.int32, (tk, tq), 0)
    q_idx = lax.broadcasted_iota(jnp.int32, (tk, tq), 1)

    def step(j, carry, diag_offset):
        start = pl.multiple_of(j * tk, tk)
        logits = [_dot_nt(k_ref[hd, pl.ds(start, tk), :], q_ref[hd]) for hd in range(N_FOX_HEADS)]
        new = []
        for hd in range(N_FOX_HEADS):
            m_old, acc = carry[hd]
            s = logits[hd]
            if diag_offset is not None:
                s = jnp.where(kv_idx + diag_offset <= q_idx, s, MASK_VALUE)
            m_new = jnp.maximum(m_old, jnp.max(s, axis=0, keepdims=True))
            p = jnp.exp(s - m_new).astype(BF16)
            acc = jnp.exp(m_old - m_new) * acc + _dot(vt_ref[hd, :, pl.ds(start, tk)], p)
            new.append((m_new, acc))
        return tuple(new)

    init = tuple((jnp.full((1, tq), -jnp.inf, F32), jnp.zeros((LANES, tq), F32)) for _ in range(N_FOX_HEADS))
    carry = lax.fori_loop(0, i * per_q, lambda j, c: step(j, c, None), init)
    for a in range(per_q):
        carry = step(i * per_q + a, carry, a * tk)

    normed = []
    for hd in range(N_FOX_HEADS):
        acc = carry[hd][1]
        o = acc[:HEAD_DIM] / acc[HEAD_DIM:HEAD_DIM + 1]
        ms = jnp.mean(o * o, axis=0, keepdims=True)
        normed.append(o * lax.rsqrt(ms + EPS))
    tiles = []
    for pair in range(N_FOX_HEADS // 2):
        both = jnp.concatenate(normed[2 * pair:2 * pair + 2], axis=0)
        tiles.append(both.T * nw_ref[pair:pair + 1, :])
    o_ref[...] = jnp.concatenate(tiles, axis=1).astype(o_ref.dtype)


def _fox_attn(qa, ka, vt, nw_pairs, batch, seq):
    t = qa.shape[1]
    nq = seq // SEQ_BLOCK
    return pl.pallas_call(
        _fox_attn_kernel,
        grid=(batch, nq),
        in_specs=[
            pl.BlockSpec((N_FOX_HEADS, SEQ_BLOCK, LANES), lambda b, i: (0, b * nq + i, 0)),
            pl.BlockSpec((N_FOX_HEADS, seq, LANES), lambda b, i: (0, b, 0)),
            pl.BlockSpec((N_FOX_HEADS, LANES, seq), lambda b, i: (0, 0, b)),
            pl.BlockSpec((SUBLANES, LANES), lambda b, i: (0, 0)),
        ],
        out_specs=pl.BlockSpec((SEQ_BLOCK, N_FOX_HEADS * HEAD_DIM), lambda b, i: (b * nq + i, 0)),
        out_shape=jax.ShapeDtypeStruct((t, N_FOX_HEADS * HEAD_DIM), BF16),
        compiler_params=_cparams("parallel", "arbitrary"),
        name="fox_attn",
    )(qa, ka, vt, nw_pairs)


def _hgrn2_kernel(hq_ref, hf_ref, hi_ref, hg_ref, par_ref, tri_ref, seg_ref, o_ref, st_ref):
    @pl.when(pl.program_id(1) == 0)
    def _():
        st_ref[...] = jnp.zeros_like(st_ref)

    rows, width = hq_ref.shape
    log_lb, log1m_lb, one_m_lb, nw = (par_ref[r:r + 1, :] for r in range(4))
    f_raw = hf_ref[...]
    q = _silu(hq_ref[...]) * (HEAD_DIM ** -0.5)
    b = log1m_lb + _log_sigmoid(f_raw)
    g = jnp.maximum(log_lb, b) + jnp.log1p(jnp.exp(-jnp.abs(log_lb - b)))
    k = one_m_lb * jax.nn.sigmoid(-f_raw)
    v = hi_ref[...]
    cum = _exact_left(tri_ref[...], g)

    lane = lax.broadcasted_iota(jnp.int32, (1, width), 1)
    head_masks = [(lane >= hd * HEAD_DIM) & (lane < (hd + 1) * HEAD_DIM) for hd in range(N_HG_HEADS)]
    seg = seg_ref[...]
    k_b = k.astype(BF16)
    v_b = v.astype(BF16)

    st = st_ref[...]
    o_state = _dot_nt((q * jnp.exp(cum)).astype(BF16), st.astype(BF16))

    n_sub = rows // SUB_BLOCK
    refs = [cum[i * SUB_BLOCK - 1:i * SUB_BLOCK] if i else jnp.zeros((1, width), F32) for i in range(n_sub)]
    local = [cum[i * SUB_BLOCK:(i + 1) * SUB_BLOCK] - refs[i] for i in range(n_sub)]

    def stack_heads(x):
        return jnp.concatenate([jnp.where(mk, x, 0.0) for mk in head_masks], axis=0).astype(BF16)

    def unstack_heads(base, p4):
        for hd, mk in enumerate(head_masks):
            base = base + jnp.where(mk, p4[hd * SUB_BLOCK:(hd + 1) * SUB_BLOCK], 0.0)
        return base

    def factored():
        scores = []
        for i in range(n_sub):
            hi = (i + 1) * SUB_BLOCK
            qs = q[hi - SUB_BLOCK:hi] * jnp.exp(local[i])
            ks = (k[:hi] * jnp.exp(refs[i] - cum[:hi])).astype(BF16)
            scores.append(_dot_nt(stack_heads(qs), ks))
        blocks = []
        for i in range(n_sub):
            hi = (i + 1) * SUB_BLOCK
            t_in = lax.broadcasted_iota(jnp.int32, (N_HG_HEADS * SUB_BLOCK, hi), 0) & (SUB_BLOCK - 1)
            s_in = lax.broadcasted_iota(jnp.int32, (N_HG_HEADS * SUB_BLOCK, hi), 1)
            sc = jnp.where(s_in <= t_in + (hi - SUB_BLOCK), scores[i], 0.0).astype(BF16)
            blocks.append(unstack_heads(o_state[hi - SUB_BLOCK:hi], _dot(sc, v_b[:hi])))
        return jnp.concatenate(blocks, axis=0)

    def pairwise():
        t_idx = lax.broadcasted_iota(jnp.int32, (SUB_BLOCK, 1), 0)
        blocks = []
        for i in range(n_sub):
            r0 = i * SUB_BLOCK
            q_i = q[r0:r0 + SUB_BLOCK]
            cum_i = cum[r0:r0 + SUB_BLOCK]
            o_i = o_state[r0:r0 + SUB_BLOCK]
            if i > 0:
                ks = (k[:r0] * jnp.exp(refs[i] - cum[:r0])).astype(BF16)
                sc = _dot_nt(stack_heads(q_i * jnp.exp(local[i])), ks)
                o_i = unstack_heads(o_i, _dot(sc.astype(BF16), v_b[:r0]))
            terms = []
            for s in range(SUB_BLOCK):
                keep = t_idx >= s
                e = _masked_exp(cum_i - cum[r0 + s:r0 + s + 1], keep)
                terms.append((q_i * e * k[r0 + s:r0 + s + 1]).astype(BF16))
            sums = _dot(jnp.concatenate(terms, axis=0), seg)
            for s in range(SUB_BLOCK):
                o_i = o_i + sums[s * SUB_BLOCK:(s + 1) * SUB_BLOCK] * v[r0 + s:r0 + s + 1]
            blocks.append(o_i)
        return jnp.concatenate(blocks, axis=0)

    worst = functools.reduce(jnp.minimum, local)
    o = lax.cond(jnp.min(worst) >= -HG_SAFE_DECAY, factored, pairwise)

    last = cum[rows - 1:rows]
    kd = (k * jnp.exp(last - cum)).astype(BF16)
    upd = _dot(v.T.astype(BF16), kd)
    st_ref[...] = st * jnp.exp(last) + jnp.where(seg > 0, upd, 0.0)

    sq = o * o
    s1 = sq.astype(BF16)
    s2 = (sq - s1.astype(F32)).astype(BF16)
    ms = (_dot(s1, seg) + _dot(s2, seg)) * (1.0 / HEAD_DIM)
    o_ref[...] = (o * lax.rsqrt(ms + EPS) * nw * _silu(hg_ref[...])).astype(o_ref.dtype)


def _hgrn2(hg, params, tri, seg, batch, seq):
    t = hg.shape[0]
    width = N_HG_HEADS * HEAD_DIM
    nc = seq // SCAN_CHUNK
    col = lambda c: pl.BlockSpec((SCAN_CHUNK, width), lambda b, j, c=c: (b * nc + j, c))
    return pl.pallas_call(
        _hgrn2_kernel,
        grid=(batch, nc),
        in_specs=[
            col(0), col(1), col(2), col(3),
            pl.BlockSpec(params.shape, lambda b, j: (0, 0)),
            pl.BlockSpec(tri.shape, lambda b, j: (0, 0)),
            pl.BlockSpec(seg.shape, lambda b, j: (0, 0)),
        ],
        out_specs=pl.BlockSpec((SCAN_CHUNK, width), lambda b, j: (b * nc + j, 0)),
        out_shape=jax.ShapeDtypeStruct((t, width), BF16),
        scratch_shapes=[pltpu.VMEM((width, width), F32)],
        compiler_params=_cparams("parallel", "arbitrary"),
        name="hgrn2",
    )(hg, hg, hg, hg, params, tri, seg)


def _ssd_kernel(z_ref, xbc_ref, small_ref, cw_ref, cb_ref, hp_ref, sp_ref, nw_ref, tri_ref, exp_ref,
                o_ref, buf_ref, st_ref):
    rows = z_ref.shape[0]
    width = z_ref.shape[1]
    halo = SUBLANES

    @pl.when(pl.program_id(1) == 0)
    def _():
        buf_ref[0:halo, :] = jnp.zeros((halo, buf_ref.shape[1]), F32)
        st_ref[...] = jnp.zeros_like(st_ref)

    buf_ref[halo:halo + rows, :] = xbc_ref[...]
    conv = cb_ref[...]
    for i in range(M2_CONV):
        off = halo - (M2_CONV - 1) + i
        conv = conv + cw_ref[i:i + 1, :] * buf_ref[off:off + rows, :]
    tail = buf_ref[rows:rows + halo, :]
    buf_ref[0:halo, :] = tail
    xc = _silu(conv)
    xs = xc[:, :width]
    gn = M2_GROUPS * M2_STATE
    bm = xc[:, width:width + gn]
    cm = xc[:, width + gn:width + 2 * gn]

    dt_bias_x, a_log_x, d_x = (hp_ref[r:r + 1, :] for r in range(3))
    dt_bias_c, a_log_c = (sp_ref[r:r + 1, :] for r in range(2))
    small = small_ref[...]
    tri = tri_ref[...]
    dt_x = _softplus(_exact_right(small, exp_ref[...]) + dt_bias_x)
    a_cum_x = _exact_left(tri, dt_x * -jnp.exp(a_log_x))
    a_cum_c = _exact_left(tri, _softplus(small + dt_bias_c) * -jnp.exp(a_log_c))
    a_cum_t = a_cum_c.T

    row = lax.broadcasted_iota(jnp.int32, (rows, rows), 0)
    col = lax.broadcasted_iota(jnp.int32, (rows, rows), 1)
    causal = col <= row
    lane = lax.broadcasted_iota(jnp.int32, (1, LANES), 1)
    low = lane < HEAD_DIM

    xdt = xs * dt_x
    a_last = a_cum_x[rows - 1:rows]
    xd = (xdt * jnp.exp(a_last - a_cum_x)).astype(BF16)
    decay_out = jnp.exp(a_cum_x)
    st = st_ref[...]
    heads_per_group = M2_HEADS // M2_GROUPS
    gw = heads_per_group * HEAD_DIM
    y_tiles = []
    new_states = []
    for g in range(M2_GROUPS):
        b_g = bm[:, g * M2_STATE:(g + 1) * M2_STATE]
        c_g = cm[:, g * M2_STATE:(g + 1) * M2_STATE].astype(BF16)
        cb = _dot_nt(c_g, b_g.astype(BF16))
        y_off = _dot(c_g, st[:, g * gw:(g + 1) * gw].astype(BF16)) * decay_out[:, g * gw:(g + 1) * gw]
        new_states.append(_dot(b_g.T.astype(BF16), xd[:, g * gw:(g + 1) * gw]))
        for pair in range(heads_per_group // 2):
            tile = g * (heads_per_group // 2) + pair
            x_tile = xdt[:, tile * LANES:(tile + 1) * LANES]
            y_pair = y_off[:, pair * LANES:(pair + 1) * LANES]
            for half in range(2):
                hd = 2 * tile + half
                seg = a_cum_c[:, SUBLANES + hd:SUBLANES + hd + 1] - a_cum_t[SUBLANES + hd:SUBLANES + hd + 1, :]
                m_h = (cb * _masked_exp(seg, causal)).astype(BF16)
                x_h = jnp.where(low if half == 0 else ~low, x_tile, 0.0).astype(BF16)
                y_pair = y_pair + _dot(m_h, x_h)
            y_tiles.append(y_pair)
    st_ref[...] = st * jnp.exp(a_last) + jnp.concatenate(new_states, axis=1)

    y = jnp.concatenate(y_tiles, axis=1) + d_x * xs
    y = y * _silu(z_ref[...])
    nw = nw_ref[...]
    outs = []
    for g in range(M2_GROUPS):
        outs.append(_rms(y[:, g * gw:(g + 1) * gw], nw[:, g * gw:(g + 1) * gw]))
    o_ref[...] = jnp.concatenate(outs, axis=1).astype(o_ref.dtype)


def _ssd(z, xbc, small, conv_w, conv_b, head_rows, small_rows, norm_w, tri, expand, batch, seq):
    t, width = z.shape
    ch = xbc.shape[1]
    nc = seq // SCAN_CHUNK
    row = lambda b, j: (b * nc + j, 0)
    const = lambda a: pl.BlockSpec(a.shape, lambda b, j: (0, 0))
    return pl.pallas_call(
        _ssd_kernel,
        grid=(batch, nc),
        in_specs=[
            pl.BlockSpec((SCAN_CHUNK, width), row),
            pl.BlockSpec((SCAN_CHUNK, ch), row),
            pl.BlockSpec((SCAN_CHUNK, LANES), row),
            const(conv_w), const(conv_b), const(head_rows), const(small_rows), const(norm_w),
            const(tri), const(expand),
        ],
        out_specs=pl.BlockSpec((SCAN_CHUNK, width), row),
        out_shape=jax.ShapeDtypeStruct((t, width), BF16),
        scratch_shapes=[
            pltpu.VMEM((SCAN_CHUNK + SUBLANES, ch), F32),
            pltpu.VMEM((M2_STATE, width), F32),
        ],
        compiler_params=_cparams("parallel", "arbitrary"),
        name="ssd",
    )(z, xbc, small, conv_w, conv_b, head_rows, small_rows, norm_w, tri, expand)


def _resident(a):
    zeros = (0,) * a.ndim
    return pl.BlockSpec(a.shape, lambda *_: zeros, pipeline_mode=pl.Buffered(1))


def _xattn_kernel(h_ref, a_ref, b_ref, c_ref, wout_ref, nw_ref, wq_ref, kv_ref, wo_ref, o_ref):
    wa = a_ref.shape[1]
    wb = b_ref.shape[1]
    h = h_ref[...] + _dot(a_ref[...], wout_ref[0:wa, :])
    h = h + _dot(b_ref[...], wout_ref[wa:wa + wb, :])
    h = h + _dot(c_ref[...], wout_ref[wa + wb:, :])
    d = h.shape[1]
    hd = d // XA_HEADS
    q = (_dot(_rms(h, nw_ref[...]).astype(BF16), wq_ref[...]) * (hd ** -0.5)).astype(BF16)
    outs = []
    for a in range(XA_HEADS):
        k_a = kv_ref[:, a * hd:(a + 1) * hd]
        v_a = kv_ref[:, d + a * hd:d + (a + 1) * hd]
        s = _dot_nt(q[:, a * hd:(a + 1) * hd], k_a)
        p = jnp.exp(s - jnp.max(s, axis=-1, keepdims=True))
        p = p / jnp.sum(p, axis=-1, keepdims=True)
        outs.append(_dot(p.astype(BF16), v_a).astype(BF16))
    o_ref[...] = h + _dot(jnp.concatenate(outs, axis=1), wo_ref[...])


def _xattn(h, o_fox, o_hg, o_m2, w_out, nw, wq, kv, wo, batch, seq):
    t, d = h.shape
    n_mem = kv.shape[0] // batch
    nb = seq // ROW_BLOCK
    row = lambda x: pl.BlockSpec((ROW_BLOCK, x.shape[1]), lambda b, i: (b * nb + i, 0))
    return pl.pallas_call(
        _xattn_kernel,
        grid=(batch, nb),
        in_specs=[
            row(h), row(o_fox), row(o_hg), row(o_m2),
            _resident(w_out), _resident(nw), _resident(wq),
            pl.BlockSpec((n_mem, kv.shape[1]), lambda b, i: (b, 0)),
            _resident(wo),
        ],
        out_specs=row(h),
        out_shape=jax.ShapeDtypeStruct((t, d), F32),
        compiler_params=_cparams("parallel", "parallel"),
        name="xattn",
    )(h, o_fox, o_hg, o_m2, w_out, nw, wq, kv, wo)


def _swiglu(x, w1_ref, w2_ref, d_ff):
    a = (_silu(_dot(x, w1_ref[:, :d_ff])) * _dot(x, w1_ref[:, d_ff:])).astype(BF16)
    return _dot(a, w2_ref[...])


def _ffn_kernel(h_ref, nw_ref, w1_ref, w2_ref, o_ref):
    h = h_ref[...]
    o_ref[...] = h + _swiglu(_rms(h, nw_ref[...]).astype(BF16), w1_ref, w2_ref, w2_ref.shape[0])


def _ffn(h, nw, w1, w2):
    t, d = h.shape
    return pl.pallas_call(
        _ffn_kernel,
        grid=(t // ROW_BLOCK,),
        in_specs=[pl.BlockSpec((ROW_BLOCK, d), lambda i: (i, 0)), _resident(nw), _resident(w1), _resident(w2)],
        out_specs=pl.BlockSpec((ROW_BLOCK, d), lambda i: (i, 0)),
        out_shape=jax.ShapeDtypeStruct((t, d), F32),
        compiler_params=_cparams("parallel"),
        name="ffn",
    )(h, nw, w1, w2)


def _router_kernel(h_ref, nw_ref, wr_ref, tri_ref, xn_ref, idx_ref, gate_ref, rank_ref, cnt_ref, carry_ref):
    @pl.when(pl.program_id(0) == 0)
    def _():
        carry_ref[...] = jnp.zeros_like(carry_ref)

    xn = _rms(h_ref[...], nw_ref[...])
    xn_ref[...] = xn
    x1 = xn.astype(BF16)
    x2 = (xn - x1.astype(F32)).astype(BF16)
    w1, w2, w3 = wr_ref[0], wr_ref[1], wr_ref[2]
    logits = _dot(x1, w1) + _dot(x1, w2) + _dot(x2, w1) + _dot(x1, w3) + _dot(x2, w2)
    lane = lax.broadcasted_iota(jnp.int32, logits.shape, 1)
    logits = jnp.where(lane < N_EXPERTS, logits, -jnp.inf)
    m1 = jnp.max(logits, axis=-1, keepdims=True)
    i1 = jnp.min(jnp.where(logits == m1, lane, LANES), axis=-1, keepdims=True)
    rest = jnp.where(lane == i1, -jnp.inf, logits)
    m2 = jnp.max(rest, axis=-1, keepdims=True)
    i2 = jnp.min(jnp.where(rest == m2, lane, LANES), axis=-1, keepdims=True)
    e2 = jnp.exp(m2 - m1)
    denom = 1.0 + e2
    idx_ref[...] = jnp.where(lane == 0, i1, jnp.where(lane == 1, i2, 0))
    gate_ref[...] = jnp.where(lane == 0, 1.0 / denom, jnp.where(lane == 1, e2 / denom, 0.0))

    hit1 = lane == i1
    hit2 = lane == i2
    member = jnp.where(hit1, 1.0, jnp.where(hit2, 1.0, 0.0))
    incl = _dot(tri_ref[...], member.astype(BF16)) + carry_ref[...]
    excl = incl - member
    r1 = jnp.sum(jnp.where(hit1, excl, 0.0), axis=-1, keepdims=True)
    r2 = jnp.sum(jnp.where(hit2, excl, 0.0), axis=-1, keepdims=True)
    rows = member.shape[0]
    carry_ref[...] = incl[rows - 1:rows, :]
    rank_ref[...] = jnp.where(lane == 0, r1, jnp.where(lane == 1, r2, 0.0)).astype(jnp.int32)
    cnt_ref[...] = jnp.broadcast_to(incl[rows - 1:rows, :], cnt_ref.shape)


def _router(h, nw, wr3, tri):
    t, d = h.shape
    row = lambda w: pl.BlockSpec((ROW_BLOCK, w), lambda i: (i, 0))
    return pl.pallas_call(
        _router_kernel,
        grid=(t // ROW_BLOCK,),
        in_specs=[
            row(d),
            pl.BlockSpec((1, d), lambda i: (0, 0)),
            pl.BlockSpec(wr3.shape, lambda i: (0, 0, 0)),
            pl.BlockSpec(tri.shape, lambda i: (0, 0)),
        ],
        out_specs=[row(d), row(LANES), row(LANES), row(LANES), pl.BlockSpec((SUBLANES, LANES), lambda i: (0, 0))],
        out_shape=[
            jax.ShapeDtypeStruct((t, d), F32),
            jax.ShapeDtypeStruct((t, LANES), jnp.int32),
            jax.ShapeDtypeStruct((t, LANES), F32),
            jax.ShapeDtypeStruct((t, LANES), jnp.int32),
            jax.ShapeDtypeStruct((SUBLANES, LANES), F32),
        ],
        scratch_shapes=[pltpu.VMEM((1, LANES), F32)],
        compiler_params=_cparams("arbitrary"),
        name="router",
    )(h, nw, wr3, tri)


def _slot_kernel(idx_ref, rank_ref, start_ref, o_ref):
    idx = idx_ref[...]
    lane = lax.broadcasted_iota(jnp.int32, idx.shape, 1)
    starts = start_ref[...]
    s1 = jnp.sum(jnp.where(lane == idx[:, 0:1], starts, 0.0), axis=-1, keepdims=True)
    s2 = jnp.sum(jnp.where(lane == idx[:, 1:2], starts, 0.0), axis=-1, keepdims=True)
    o_ref[...] = jnp.where(lane == 0, s1, jnp.where(lane == 1, s2, 0.0)).astype(jnp.int32) + rank_ref[...]


def _slots(idx, rank, start_row):
    t = idx.shape[0]
    row = pl.BlockSpec((ROW_BLOCK, LANES), lambda i: (i, 0))
    return pl.pallas_call(
        _slot_kernel,
        grid=(t // ROW_BLOCK,),
        in_specs=[row, row, pl.BlockSpec((1, LANES), lambda i: (0, 0))],
        out_specs=row,
        out_shape=jax.ShapeDtypeStruct((t, LANES), jnp.int32),
        compiler_params=_cparams("parallel"),
        name="moe_slots",
    )(idx, rank, start_row)


def _dispatch_kernel(pend_ref, padded_ref, nused_ref, dest_ref, x_ref, xs_ref, zero_ref, sem):
    rows = x_ref.shape[0]
    n_blk = xs_ref.shape[0] // MOE_ROWS

    @pl.when(pl.program_id(0) == 0)
    def _():
        zero_ref[...] = jnp.zeros_like(zero_ref)

        def fill_block(start):
            fill = pltpu.make_async_copy(zero_ref, xs_ref.at[pl.ds(start, MOE_ROWS), :], sem)
            fill.start()
            fill.wait()

        for e in range(N_EXPERTS):
            @pl.when(padded_ref[e] > 0)
            def _():
                fill_block(pl.multiple_of(pend_ref[e] - MOE_ROWS, MOE_ROWS))

            @pl.when(nused_ref[0] + e < n_blk)
            def _():
                fill_block(pl.multiple_of((nused_ref[0] + e) * MOE_ROWS, MOE_ROWS))

    def issue(group, carry):
        base = pl.multiple_of(group * SUBLANES, SUBLANES)
        for s in range(SUBLANES):
            for k in range(2):
                slot = dest_ref[0, 0, k * rows + base + s]
                pltpu.make_async_copy(x_ref.at[pl.ds(base + s, 1), :], xs_ref.at[pl.ds(slot, 1), :], sem).start()
        return carry

    lax.fori_loop(0, rows // SUBLANES, issue, 0)
    for k in range(2):
        pltpu.make_async_copy(x_ref, xs_ref.at[pl.ds(0, rows), :], sem).wait()


def _dispatch(pends, padded, n_used, dest_t, xn, cap):
    t, d = xn.shape
    grid_spec = pltpu.PrefetchScalarGridSpec(
        num_scalar_prefetch=3,
        grid=(t // ROW_BLOCK,),
        in_specs=[
            pl.BlockSpec((1, 1, 2 * ROW_BLOCK), lambda i, pe, pa, nu: (i, 0, 0), memory_space=pltpu.SMEM),
            pl.BlockSpec((ROW_BLOCK, d), lambda i, pe, pa, nu: (i, 0)),
        ],
        out_specs=pl.BlockSpec(memory_space=pl.ANY),
        scratch_shapes=[pltpu.VMEM((MOE_ROWS, d), F32), pltpu.SemaphoreType.DMA(())],
    )
    return pl.pallas_call(
        _dispatch_kernel,
        grid_spec=grid_spec,
        out_shape=jax.ShapeDtypeStruct((cap, d), F32),
        compiler_params=_cparams("arbitrary"),
        name="moe_dispatch",
    )(pends, padded, n_used, dest_t, xn)


def _expert_kernel(be_ref, nused_ref, x_ref, w1_ref, w2_ref, o_ref):
    live = pl.program_id(0) < nused_ref[0]

    @pl.when(live)
    def _():
        o_ref[...] = _swiglu(x_ref[...].astype(BF16), w1_ref.at[0], w2_ref.at[0], w2_ref.shape[1])

    @pl.when(jnp.logical_not(live))
    def _():
        o_ref[...] = jnp.zeros_like(o_ref)


def _experts(blk_expert, n_used, xs, w1, w2):
    cap, d = xs.shape
    blk = lambda i, be, nu: (i, 0)
    expert = lambda shape: pl.BlockSpec((1,) + shape[1:], lambda i, be, nu: (be[i], 0, 0),
                                        pipeline_mode=pl.Buffered(1))
    grid_spec = pltpu.PrefetchScalarGridSpec(
        num_scalar_prefetch=2,
        grid=(cap // MOE_ROWS,),
        in_specs=[pl.BlockSpec((MOE_ROWS, d), blk), expert(w1.shape), expert(w2.shape)],
        out_specs=pl.BlockSpec((MOE_ROWS, d), blk),
    )
    return pl.pallas_call(
        _expert_kernel,
        grid_spec=grid_spec,
        out_shape=jax.ShapeDtypeStruct((cap, d), F32),
        compiler_params=_cparams("arbitrary"),
        name="experts",
    )(blk_expert, n_used, xs, w1, w2)


def _combine_kernel(dest_ref, h_ref, gate_ref, nw_ref, yb_ref, o_ref, buf_ref, sem, *, final_norm):
    rows = h_ref.shape[0]

    def issue(group, carry):
        base = pl.multiple_of(group * SUBLANES, SUBLANES)
        for s in range(SUBLANES):
            for k in range(2):
                slot = dest_ref[0, 0, k * rows + base + s]
                pltpu.make_async_copy(
                    yb_ref.at[pl.ds(slot, 1), :], buf_ref.at[k, pl.ds(base + s, 1), :], sem).start()
        return carry

    lax.fori_loop(0, rows // SUBLANES, issue, 0)
    for k in range(2):
        pltpu.make_async_copy(yb_ref.at[pl.ds(0, rows), :], buf_ref.at[k], sem).wait()
    gate = gate_ref[...]
    out = h_ref[...] + gate[:, 0:1] * buf_ref[0] + gate[:, 1:2] * buf_ref[1]
    if final_norm:
        out = _rms(out, nw_ref[...])
    o_ref[...] = out


def _combine(dest_t, h, gate, nw, yb, final_norm):
    t, d = h.shape
    return pl.pallas_call(
        functools.partial(_combine_kernel, final_norm=final_norm),
        grid=(t // ROW_BLOCK,),
        in_specs=[
            pl.BlockSpec((1, 1, 2 * ROW_BLOCK), lambda i: (i, 0, 0), memory_space=pltpu.SMEM),
            pl.BlockSpec((ROW_BLOCK, d), lambda i: (i, 0)),
            pl.BlockSpec((ROW_BLOCK, LANES), lambda i: (i, 0)),
            pl.BlockSpec((1, d), lambda i: (0, 0)),
            pl.BlockSpec(memory_space=pl.ANY),
        ],
        out_specs=pl.BlockSpec((ROW_BLOCK, d), lambda i: (i, 0)),
        out_shape=jax.ShapeDtypeStruct((t, d), F32),
        scratch_shapes=[pltpu.VMEM((2, ROW_BLOCK, d), F32), pltpu.SemaphoreType.DMA(())],
        compiler_params=_cparams("arbitrary"),
        name="moe_combine",
    )(dest_t, h, gate, nw, yb)


def _final_norm_kernel(h_ref, nw_ref, o_ref):
    o_ref[...] = _rms(h_ref[...], nw_ref[...])


def _final_norm(h, nw):
    t, d = h.shape
    return pl.pallas_call(
        _final_norm_kernel,
        grid=(t // ROW_BLOCK,),
        in_specs=[pl.BlockSpec((ROW_BLOCK, d), lambda i: (i, 0)), pl.BlockSpec((1, d), lambda i: (0, 0))],
        out_specs=pl.BlockSpec((ROW_BLOCK, d), lambda i: (i, 0)),
        out_shape=jax.ShapeDtypeStruct((t, d), F32),
        compiler_params=_cparams("parallel"),
        name="final_norm",
    )(h, nw)


def _moe(h, nw, router_w, w1, w2, tri, out_nw, final_norm):
    t, d = h.shape
    assert MOE_ROWS == ROW_BLOCK
    r1, r2, r3 = _split3(jnp.pad(router_w, ((0, 0), (0, LANES - N_EXPERTS))))
    xn, idx, gate, rank, cnt = _router(h, nw, jnp.stack([r1, r2, r3]), tri)
    counts = cnt[0, :N_EXPERTS].astype(jnp.int32)
    padded = (counts + MOE_ROWS - 1) // MOE_ROWS * MOE_ROWS
    pends = jnp.cumsum(padded)
    starts = pends - padded
    cap = (2 * t // MOE_ROWS + N_EXPERTS) * MOE_ROWS
    n_blk = cap // MOE_ROWS
    blk_start = jnp.arange(n_blk, dtype=jnp.int32) * MOE_ROWS
    blk_expert = jnp.minimum(jnp.sum(blk_start[:, None] >= pends[None, :], axis=1), N_EXPERTS - 1).astype(jnp.int32)
    n_used = (pends[-1:] // MOE_ROWS).astype(jnp.int32)
    start_row = jnp.zeros((1, LANES), F32).at[0, :N_EXPERTS].set(starts.astype(F32))
    dest = _slots(idx, rank, start_row)
    dest_t = dest[:, :2].reshape(t // ROW_BLOCK, ROW_BLOCK, 2).transpose(0, 2, 1).reshape(
        t // ROW_BLOCK, 1, 2 * ROW_BLOCK)
    xs = _dispatch(pends.astype(jnp.int32), padded.astype(jnp.int32), n_used, dest_t, xn, cap)
    yb = _experts(blk_expert, n_used, xs, w1, w2)
    return _combine(dest_t, h, gate, out_nw, yb, final_norm)


def _tri(n):
    return jnp.tril(jnp.ones((n, n), F32)).astype(BF16)


def kernel(x, mem, mix_norm_w, w_in, fox_f_bias, fox_norm_w, hg_lb_raw, hg_norm_w, m2_conv_w, m2_conv_b, m2_dt_bias, m2_a_log, m2_d, m2_norm_w, w_out, xa_norm_w, xa_mem_norm_w, xa_w_q, xa_w_kv, xa_w_o, ffn_norm_w, ffn_w1, ffn_w2, moe_router, moe_w1, moe_w2, final_norm_w):
    batch, seq, d = x.shape
    depth = w_in.shape[0]
    t = batch * seq
    fox_w = N_FOX_HEADS * HEAD_DIM
    hg_w = N_HG_HEADS * HEAD_DIM
    m2_w = M2_HEADS * HEAD_DIM
    conv_ch = m2_w + 2 * M2_GROUPS * M2_STATE
    in_splits = (fox_w, fox_w, fox_w, N_FOX_HEADS, hg_w, hg_w, hg_w, hg_w, m2_w, conv_ch, M2_HEADS)
    offs = [0]
    for s in in_splits:
        offs.append(offs[-1] + s)

    lb_p = jax.nn.softmax(hg_lb_raw.astype(F32), axis=0)
    hg_lb = jnp.cumsum(lb_p, axis=0) - lb_p[0]

    tri_seq = _tri(SEQ_BLOCK)
    tri_chunk = _tri(SCAN_CHUNK)
    head_of_lane = jnp.arange(hg_w) // HEAD_DIM
    seg_ones = (head_of_lane[:, None] == head_of_lane[None, :]).astype(BF16)
    expand = (jnp.arange(LANES)[:, None] - SUBLANES == jnp.arange(m2_w)[None, :] // HEAD_DIM).astype(BF16)

    def pad_lanes(v, offset=0, width=LANES):
        return jnp.zeros((1, width), F32).at[0, offset:offset + v.shape[0]].set(v)

    h = x.reshape(t, d)
    mem2 = mem.reshape(batch * mem.shape[1], d)
    for layer in range(depth):
        w = w_in[layer]
        small_w = jnp.zeros((d, LANES), F32)
        small_w = small_w.at[:, 0:N_FOX_HEADS].set(w[:, offs[3]:offs[4]])
        small_w = small_w.at[:, SUBLANES:SUBLANES + M2_HEADS].set(w[:, offs[10]:offs[11]])
        w_cat = jnp.concatenate(
            [w[:, offs[0]:offs[3]], small_w, w[:, offs[4]:offs[8]], w[:, offs[8]:offs[9]], w[:, offs[9]:offs[10]]],
            axis=1).astype(BF16)
        splits = (3 * fox_w, LANES, 4 * hg_w, m2_w, conv_ch)
        qkv, small, hg, z, xbc = _in_proj(h, mix_norm_w[layer][None, :], w_cat, splits)

        qa, ka, va = _fox_prep(qkv, small, pad_lanes(fox_f_bias[layer]), tri_seq, batch, seq)
        fox_nw = jnp.zeros((SUBLANES, LANES), F32).at[:fox_w // LANES].set(
            fox_norm_w[layer].reshape(fox_w // LANES, LANES))
        o_fox = _fox_attn(qa, ka, va, fox_nw, batch, seq)

        lb = hg_lb[layer]
        hg_params = jnp.zeros((SUBLANES, hg_w), F32)
        hg_params = hg_params.at[0].set(jnp.log(jnp.maximum(lb, LB_FLOOR)))
        hg_params = hg_params.at[1].set(jnp.log1p(-lb))
        hg_params = hg_params.at[2].set(1.0 - lb)
        hg_params = hg_params.at[3].set(hg_norm_w[layer])
        o_hg = _hgrn2(hg, hg_params, tri_chunk, seg_ones, batch, seq)

        conv_w = jnp.zeros((SUBLANES, conv_ch), F32).at[:M2_CONV].set(m2_conv_w[layer])
        head_rows = jnp.zeros((SUBLANES, m2_w), F32)
        head_rows = head_rows.at[0].set(jnp.repeat(m2_dt_bias[layer], HEAD_DIM))
        head_rows = head_rows.at[1].set(jnp.repeat(m2_a_log[layer], HEAD_DIM))
        head_rows = head_rows.at[2].set(jnp.repeat(m2_d[layer], HEAD_DIM))
        small_rows = jnp.concatenate(
            [pad_lanes(m2_dt_bias[layer], SUBLANES), pad_lanes(m2_a_log[layer], SUBLANES),
             jnp.zeros((SUBLANES - 2, LANES), F32)], axis=0)
        o_m2 = _ssd(z, xbc, small, conv_w, m2_conv_b[layer][None, :], head_rows, small_rows,
                    m2_norm_w[layer][None, :], tri_chunk, expand, batch, seq)

        kv = _norm_proj(mem2, xa_mem_norm_w[layer][None, :], xa_w_kv[layer].astype(BF16), BF16, mem.shape[1])
        h = _xattn(h, o_fox, o_hg, o_m2, w_out[layer].astype(BF16), xa_norm_w[layer][None, :],
                   xa_w_q[layer].astype(BF16), kv, xa_w_o[layer].astype(BF16), batch, seq)

        nw = ffn_norm_w[layer][None, :]
        if layer % 2 == 0:
            h = _ffn(h, nw, ffn_w1[layer // 2].astype(BF16), ffn_w2[layer // 2].astype(BF16))
        else:
            last = layer == depth - 1
            h = _moe(h, nw, moe_router[layer // 2], moe_w1[layer // 2].astype(BF16),
                     moe_w2[layer // 2].astype(BF16), tri_seq, final_norm_w[None, :], last)
    if depth % 2:
        h = _final_norm(h, final_norm_w[None, :])
    return h.reshape(batch, seq, d)
```

```python
import functools
import math

import jax
import jax.numpy as jnp
from jax import lax
from jax.experimental import pallas as pl
from jax.experimental.pallas import tpu as pltpu

F32 = jnp.float32
BF16 = jnp.bfloat16

EPS = 1e-6
MASK_VALUE = -1e9
LB_FLOOR = 1e-30
HEAD_DIM = 64
N_FOX_HEADS = 4
N_HG_HEADS = 4
M2_HEADS = 8
M2_GROUPS = 2
M2_STATE = 128
M2_CONV = 4
XA_HEADS = 4
N_EXPERTS = 8

LANES = 128
SUBLANES = 8
VMEM_LIMIT_BYTES = 56 * 1024 * 1024

ROW_BLOCK = 512
SEQ_BLOCK = 512
SCAN_CHUNK = 128
SCAN_CHUNKS_PER_STEP = 2
SUB_BLOCK = 16
HG_SAFE_DECAY = 60.0
MOE_ROWS = 512


def _cparams(*sem):
    return pltpu.CompilerParams(dimension_semantics=sem, vmem_limit_bytes=VMEM_LIMIT_BYTES)


def _split3(x):
    x1 = x.astype(BF16)
    r1 = x - x1.astype(F32)
    x2 = r1.astype(BF16)
    x3 = (r1 - x2.astype(F32)).astype(BF16)
    return x1, x2, x3


def _dot(a, b):
    return jnp.dot(a, b, preferred_element_type=F32)


def _dot_nt(a, b):
    return lax.dot_general(a, b, (((1,), (1,)), ((), ())), preferred_element_type=F32)


def _exact_left(mat01, x):
    x1, x2, x3 = _split3(x)
    return _dot(mat01, x1) + _dot(mat01, x2) + _dot(mat01, x3)


def _exact_right(x, mat01):
    x1, x2, x3 = _split3(x)
    return _dot(x1, mat01) + _dot(x2, mat01) + _dot(x3, mat01)


def _rms(x, w):
    return x * lax.rsqrt(jnp.mean(x * x, axis=-1, keepdims=True) + EPS) * w


def _sigmoid(x):
    return 1.0 / (1.0 + jnp.exp(-x))


def _silu(x):
    return x * _sigmoid(x)


def _log1p_exp(x):
    return jnp.log(1.0 + jnp.exp(x))


def _log_sigmoid(x):
    return jnp.minimum(x, 0.0) - _log1p_exp(-jnp.abs(x))


def _softplus(x):
    return jnp.maximum(x, 0.0) + _log1p_exp(-jnp.abs(x))


def _masked_exp(x, mask):
    return jnp.where(mask, jnp.exp(jnp.where(mask, x, 0.0)), 0.0)


def _in_proj_kernel(x_ref, nw_ref, w_ref, fox_ref, small_ref, hg_ref, z_ref, xbc_ref, *, splits):
    xn = _rms(x_ref[...], nw_ref[...]).astype(BF16)
    outs = (fox_ref, small_ref, hg_ref, z_ref, xbc_ref)
    off = 0
    for o_ref, width in zip(outs, splits):
        o_ref[...] = _dot(xn, w_ref[:, off:off + width]).astype(o_ref.dtype)
        off += width


def _in_proj(h, nw, w_cat, splits):
    t, d = h.shape
    out_dtypes = (BF16, F32, F32, F32, F32)
    return pl.pallas_call(
        functools.partial(_in_proj_kernel, splits=splits),
        grid=(t // ROW_BLOCK,),
        in_specs=[
            pl.BlockSpec((ROW_BLOCK, d), lambda i: (i, 0)),
            pl.BlockSpec((1, d), lambda i: (0, 0)),
            pl.BlockSpec(w_cat.shape, lambda i: (0, 0)),
        ],
        out_specs=[pl.BlockSpec((ROW_BLOCK, w), lambda i: (i, 0)) for w in splits],
        out_shape=[jax.ShapeDtypeStruct((t, w), dt) for w, dt in zip(splits, out_dtypes)],
        compiler_params=_cparams("parallel"),
        name="in_proj",
    )(h, nw, w_cat)


def _norm_proj_kernel(x_ref, nw_ref, w_ref, o_ref):
    xn = _rms(x_ref[...], nw_ref[...]).astype(BF16)
    o_ref[...] = _dot(xn, w_ref[...]).astype(o_ref.dtype)


def _norm_proj(x, nw, w, out_dtype, rows):
    t, d = x.shape
    n = w.shape[1]
    return pl.pallas_call(
        _norm_proj_kernel,
        grid=(t // rows,),
        in_specs=[
            pl.BlockSpec((rows, d), lambda i: (i, 0)),
            pl.BlockSpec((1, d), lambda i: (0, 0)),
            pl.BlockSpec(w.shape, lambda i: (0, 0)),
        ],
        out_specs=pl.BlockSpec((rows, n), lambda i: (i, 0)),
        out_shape=jax.ShapeDtypeStruct((t, n), out_dtype),
        compiler_params=_cparams("parallel"),
        name="norm_proj",
    )(x, nw, w)


def _fox_prep_kernel(qkv_ref, small_ref, bias_ref, tri_ref, q_ref, k_ref, v_ref, carry_ref):
    @pl.when(pl.program_id(1) == 0)
    def _():
        carry_ref[...] = jnp.zeros_like(carry_ref)

    rows = qkv_ref.shape[0]
    log_f = _log_sigmoid(small_ref[...] + bias_ref[...])
    c = _exact_left(tri_ref[...], log_f) + carry_ref[...]
    carry_ref[...] = c[rows - 1:rows, :]

    lane = lax.broadcasted_iota(jnp.int32, (rows, LANES), 1)
    width = N_FOX_HEADS * HEAD_DIM
    scale = HEAD_DIM ** -0.5
    for hd in range(N_FOX_HEADS):
        tile = (hd * HEAD_DIM) // LANES
        ch = c[:, hd:hd + 1]
        c1 = ch.astype(BF16).astype(F32)
        r1 = ch - c1
        c2 = r1.astype(BF16).astype(F32)
        c3 = r1 - c2

        def head_tile(base):
            x = qkv_ref[:, base + tile * LANES: base + (tile + 1) * LANES].astype(F32)
            if (hd * HEAD_DIM) % LANES:
                x = pltpu.roll(x, LANES - (hd * HEAD_DIM) % LANES, axis=1)
            return x

        def augment(x, first, second):
            out = jnp.where(lane < HEAD_DIM, x, 0.0)
            for j, val in enumerate(first + second):
                out = jnp.where(lane == HEAD_DIM + j, val, out)
            return out.astype(BF16)

        ones = (1.0, 1.0, 1.0)
        q_ref[hd] = augment(head_tile(0) * scale, ones, (c1, c2, c3))
        k_ref[hd] = augment(head_tile(width), (-c1, -c2, -c3), ones)
        v_aug = jnp.where(lane < HEAD_DIM, head_tile(2 * width), jnp.where(lane == HEAD_DIM, 1.0, 0.0))
        v_ref[hd] = v_aug.T.astype(BF16)


def _fox_prep(qkv, small, bias_row, tri, batch, seq):
    t = qkv.shape[0]
    nb = seq // SEQ_BLOCK
    row = lambda b, c: (b * nb + c, 0)
    head_spec = pl.BlockSpec((N_FOX_HEADS, SEQ_BLOCK, LANES), lambda b, c: (0, b * nb + c, 0))
    head_shape = jax.ShapeDtypeStruct((N_FOX_HEADS, t, LANES), BF16)
    vt_spec = pl.BlockSpec((N_FOX_HEADS, LANES, SEQ_BLOCK), lambda b, c: (0, 0, b * nb + c))
    vt_shape = jax.ShapeDtypeStruct((N_FOX_HEADS, LANES, t), BF16)
    return pl.pallas_call(
        _fox_prep_kernel,
        grid=(batch, nb),
        in_specs=[
            pl.BlockSpec((SEQ_BLOCK, qkv.shape[1]), row),
            pl.BlockSpec((SEQ_BLOCK, LANES), row),
            pl.BlockSpec((1, LANES), lambda b, c: (0, 0)),
            pl.BlockSpec((SEQ_BLOCK, SEQ_BLOCK), lambda b, c: (0, 0)),
        ],
        out_specs=[head_spec, head_spec, vt_spec],
        out_shape=[head_shape, head_shape, vt_shape],
        scratch_shapes=[pltpu.VMEM((1, LANES), F32)],
        compiler_params=_cparams("parallel", "arbitrary"),
        name="fox_prep",
    )(qkv, small, bias_row, tri)


def _fox_attn_kernel(q_ref, k_ref, vt_ref, nw_ref, o_ref, m_ref, acc_ref):
    i = pl.program_id(1)
    tq = q_ref.shape[1]
    kv_idx = lax.broadcasted_iota(jnp.int32, (tq, tq), 0)
    q_idx = lax.broadcasted_iota(jnp.int32, (tq, tq), 1)
    heads = range(N_FOX_HEADS)
    m_ref[...] = jnp.full(m_ref.shape, -jnp.inf, F32)
    acc_ref[...] = jnp.zeros_like(acc_ref)

    def process(blocks, diagonal):
        starts = [pl.multiple_of(j * tq, tq) for j in blocks]
        logits = [[_dot_nt(k_ref[hd, pl.ds(start, tq), :], q_ref[hd]) for hd in heads] for start in starts]
        for start, block_logits in zip(starts, logits):
            for hd in heads:
                s = block_logits[hd]
                if diagonal:
                    s = jnp.where(kv_idx <= q_idx, s, MASK_VALUE)
                m_old = m_ref[hd]
                m_new = jnp.maximum(m_old, jnp.max(s, axis=0, keepdims=True))
                p = jnp.exp(s - m_new).astype(BF16)
                acc_ref[hd] = jnp.exp(m_old - m_new) * acc_ref[hd] + _dot(vt_ref[hd, :, pl.ds(start, tq)], p)
                m_ref[hd] = m_new

    def pair(jj, carry):
        process([2 * jj, 2 * jj + 1], False)
        return carry

    lax.fori_loop(0, i // 2, pair, 0)

    @pl.when(i % 2 == 1)
    def _():
        process([i - 1], False)

    process([i], True)

    normed = []
    for hd in heads:
        acc = acc_ref[hd]
        o = acc[:HEAD_DIM] / acc[HEAD_DIM:HEAD_DIM + 1]
        ms = jnp.mean(o * o, axis=0, keepdims=True)
        normed.append(o * lax.rsqrt(ms + EPS))
    tiles = []
    for pair in range(N_FOX_HEADS // 2):
        both = jnp.concatenate(normed[2 * pair:2 * pair + 2], axis=0)
        tiles.append(both.T * nw_ref[pair:pair + 1, :])
    o_ref[...] = jnp.concatenate(tiles, axis=1).astype(o_ref.dtype)


def _fox_attn(qa, ka, vt, nw_pairs, batch, seq):
    t = qa.shape[1]
    nq = seq // SEQ_BLOCK
    return pl.pallas_call(
        _fox_attn_kernel,
        grid=(batch, nq),
        in_specs=[
            pl.BlockSpec((N_FOX_HEADS, SEQ_BLOCK, LANES), lambda b, i: (0, b * nq + i, 0)),
            pl.BlockSpec((N_FOX_HEADS, seq, LANES), lambda b, i: (0, b, 0)),
            pl.BlockSpec((N_FOX_HEADS, LANES, seq), lambda b, i: (0, 0, b)),
            pl.BlockSpec((SUBLANES, LANES), lambda b, i: (0, 0)),
        ],
        out_specs=pl.BlockSpec((SEQ_BLOCK, N_FOX_HEADS * HEAD_DIM), lambda b, i: (b * nq + i, 0)),
        out_shape=jax.ShapeDtypeStruct((t, N_FOX_HEADS * HEAD_DIM), BF16),
        scratch_shapes=[
            pltpu.VMEM((N_FOX_HEADS, 1, SEQ_BLOCK), F32),
            pltpu.VMEM((N_FOX_HEADS, LANES, SEQ_BLOCK), F32),
        ],
        compiler_params=_cparams("parallel", "arbitrary"),
        name="fox_attn",
    )(qa, ka, vt, nw_pairs)


def _chunk_views(refs, c):
    return [r.at[pl.ds(c * SCAN_CHUNK, SCAN_CHUNK)] for r in refs]


def _hgrn2_kernel(hq_ref, hf_ref, hi_ref, hg_ref, par_ref, tri_ref, seg_ref, o_ref, st_ref):
    @pl.when(pl.program_id(1) == 0)
    def _():
        st_ref[...] = jnp.zeros_like(st_ref)

    for c in range(SCAN_CHUNKS_PER_STEP):
        hq, hf, hi, hg, o = _chunk_views((hq_ref, hf_ref, hi_ref, hg_ref, o_ref), c)
        _hgrn2_chunk(hq, hf, hi, hg, par_ref, tri_ref, seg_ref, o, st_ref)


def _hgrn2_chunk(hq_ref, hf_ref, hi_ref, hg_ref, par_ref, tri_ref, seg_ref, o_ref, st_ref):
    rows, width = hq_ref.shape
    log_lb, log1m_lb, one_m_lb, nw = (par_ref[r:r + 1, :] for r in range(4))
    f_raw = hf_ref[...]
    q = _silu(hq_ref[...]) * (HEAD_DIM ** -0.5)
    b = log1m_lb + _log_sigmoid(f_raw)
    g = jnp.maximum(log_lb, b) + _log1p_exp(-jnp.abs(log_lb - b))
    k = one_m_lb * _sigmoid(-f_raw)
    v = hi_ref[...]
    cum = _exact_left(tri_ref[...], g)

    lane = lax.broadcasted_iota(jnp.int32, (1, width), 1)
    head_masks = [(lane >= hd * HEAD_DIM) & (lane < (hd + 1) * HEAD_DIM) for hd in range(N_HG_HEADS)]
    seg = seg_ref[...]
    v_b = v.astype(BF16)

    st = st_ref[...]
    o_state = _dot_nt((q * jnp.exp(cum)).astype(BF16), st.astype(BF16))

    n_sub = rows // SUB_BLOCK
    refs = [cum[i * SUB_BLOCK - 1:i * SUB_BLOCK] if i else jnp.zeros((1, width), F32) for i in range(n_sub)]
    local = [cum[i * SUB_BLOCK:(i + 1) * SUB_BLOCK] - refs[i] for i in range(n_sub)]

    def stack_heads(x):
        return jnp.concatenate([jnp.where(mk, x, 0.0) for mk in head_masks], axis=0).astype(BF16)

    def unstack_heads(base, p4):
        for hd, mk in enumerate(head_masks):
            base = base + jnp.where(mk, p4[hd * SUB_BLOCK:(hd + 1) * SUB_BLOCK], 0.0)
        return base

    def factored():
        scores = []
        for i in range(n_sub):
            hi = (i + 1) * SUB_BLOCK
            qs = q[hi - SUB_BLOCK:hi] * jnp.exp(local[i])
            ks = (k[:hi] * jnp.exp(refs[i] - cum[:hi])).astype(BF16)
            scores.append(_dot_nt(stack_heads(qs), ks))
        blocks = []
        for i in range(n_sub):
            hi = (i + 1) * SUB_BLOCK
            t_in = lax.broadcasted_iota(jnp.int32, (N_HG_HEADS * SUB_BLOCK, hi), 0) & (SUB_BLOCK - 1)
            s_in = lax.broadcasted_iota(jnp.int32, (N_HG_HEADS * SUB_BLOCK, hi), 1)
            sc = jnp.where(s_in <= t_in + (hi - SUB_BLOCK), scores[i], 0.0).astype(BF16)
            blocks.append(unstack_heads(o_state[hi - SUB_BLOCK:hi], _dot(sc, v_b[:hi])))
        return jnp.concatenate(blocks, axis=0)

    def pairwise():
        t_idx = lax.broadcasted_iota(jnp.int32, (SUB_BLOCK, 1), 0)
        blocks = []
        for i in range(n_sub):
            r0 = i * SUB_BLOCK
            q_i = q[r0:r0 + SUB_BLOCK]
            cum_i = cum[r0:r0 + SUB_BLOCK]
            o_i = o_state[r0:r0 + SUB_BLOCK]
            if i > 0:
                ks = (k[:r0] * jnp.exp(refs[i] - cum[:r0])).astype(BF16)
                sc = _dot_nt(stack_heads(q_i * jnp.exp(local[i])), ks)
                o_i = unstack_heads(o_i, _dot(sc.astype(BF16), v_b[:r0]))
            terms = []
            for s in range(SUB_BLOCK):
                keep = t_idx >= s
                e = _masked_exp(cum_i - cum[r0 + s:r0 + s + 1], keep)
                terms.append((q_i * e * k[r0 + s:r0 + s + 1]).astype(BF16))
            sums = _dot(jnp.concatenate(terms, axis=0), seg)
            for s in range(SUB_BLOCK):
                o_i = o_i + sums[s * SUB_BLOCK:(s + 1) * SUB_BLOCK] * v[r0 + s:r0 + s + 1]
            blocks.append(o_i)
        return jnp.concatenate(blocks, axis=0)

    worst = functools.reduce(jnp.minimum, local)
    o = lax.cond(jnp.min(worst) >= -HG_SAFE_DECAY, factored, pairwise)

    last = cum[rows - 1:rows]
    kd = (k * jnp.exp(last - cum)).astype(BF16)
    upd = _dot(v.T.astype(BF16), kd)
    st_ref[...] = st * jnp.exp(last) + jnp.where(seg > 0, upd, 0.0)

    sq = o * o
    s1 = sq.astype(BF16)
    s2 = (sq - s1.astype(F32)).astype(BF16)
    ms = (_dot(s1, seg) + _dot(s2, seg)) * (1.0 / HEAD_DIM)
    o_ref[...] = (o * lax.rsqrt(ms + EPS) * nw * _silu(hg_ref[...])).astype(o_ref.dtype)


def _hgrn2(hg, params, tri, seg, batch, seq):
    t = hg.shape[0]
    width = N_HG_HEADS * HEAD_DIM
    step_rows = SCAN_CHUNK * SCAN_CHUNKS_PER_STEP
    nc = seq // step_rows
    col = lambda c: pl.BlockSpec((step_rows, width), lambda b, j, c=c: (b * nc + j, c))
    return pl.pallas_call(
        _hgrn2_kernel,
        grid=(batch, nc),
        in_specs=[
            col(0), col(1), col(2), col(3),
            pl.BlockSpec(params.shape, lambda b, j: (0, 0)),
            pl.BlockSpec(tri.shape, lambda b, j: (0, 0)),
            pl.BlockSpec(seg.shape, lambda b, j: (0, 0)),
        ],
        out_specs=pl.BlockSpec((step_rows, width), lambda b, j: (b * nc + j, 0)),
        out_shape=jax.ShapeDtypeStruct((t, width), BF16),
        scratch_shapes=[pltpu.VMEM((width, width), F32)],
        compiler_params=_cparams("parallel", "arbitrary"),
        name="hgrn2",
    )(hg, hg, hg, hg, params, tri, seg)


def _ssd_kernel(z_ref, xbc_ref, small_ref, cw_ref, cb_ref, hp_ref, sp_ref, nw_ref, tri_ref, exp_ref,
                o_ref, buf_ref, st_ref):
    @pl.when(pl.program_id(1) == 0)
    def _():
        buf_ref[0:SUBLANES, :] = jnp.zeros((SUBLANES, buf_ref.shape[1]), F32)
        st_ref[...] = jnp.zeros_like(st_ref)

    for c in range(SCAN_CHUNKS_PER_STEP):
        z, xbc, small, o = _chunk_views((z_ref, xbc_ref, small_ref, o_ref), c)
        _ssd_chunk(z, xbc, small, cw_ref, cb_ref, hp_ref, sp_ref, nw_ref, tri_ref, exp_ref, o, buf_ref, st_ref)


def _ssd_chunk(z_ref, xbc_ref, small_ref, cw_ref, cb_ref, hp_ref, sp_ref, nw_ref, tri_ref, exp_ref,
               o_ref, buf_ref, st_ref):
    rows = z_ref.shape[0]
    width = z_ref.shape[1]
    halo = SUBLANES

    buf_ref[halo:halo + rows, :] = xbc_ref[...]
    conv = cb_ref[...]
    for i in range(M2_CONV):
        off = halo - (M2_CONV - 1) + i
        conv = conv + cw_ref[i:i + 1, :] * buf_ref[off:off + rows, :]
    tail = buf_ref[rows:rows + halo, :]
    buf_ref[0:halo, :] = tail
    xc = _silu(conv)
    xs = xc[:, :width]
    gn = M2_GROUPS * M2_STATE
    bm = xc[:, width:width + gn]
    cm = xc[:, width + gn:width + 2 * gn]

    d_x = hp_ref[0:1, :]
    dt_bias_c, a_log_c = (sp_ref[r:r + 1, :] for r in range(2))
    small = small_ref[...]
    tri = tri_ref[...]
    dt_c = _softplus(small + dt_bias_c)
    a_cum_c = _exact_left(tri, dt_c * -jnp.exp(a_log_c))
    a_cum_t = a_cum_c.T
    expand = exp_ref[...]
    dt_x = _exact_right(dt_c, expand)
    a_cum_x = _exact_right(a_cum_c, expand)

    row = lax.broadcasted_iota(jnp.int32, (rows, rows), 0)
    col = lax.broadcasted_iota(jnp.int32, (rows, rows), 1)
    causal = col <= row
    lane = lax.broadcasted_iota(jnp.int32, (1, LANES), 1)
    low = lane < HEAD_DIM

    xdt = xs * dt_x
    a_last = a_cum_x[rows - 1:rows]
    xd = (xdt * jnp.exp(a_last - a_cum_x)).astype(BF16)
    decay_out = jnp.exp(a_cum_x)
    st = st_ref[...]
    heads_per_group = M2_HEADS // M2_GROUPS
    gw = heads_per_group * HEAD_DIM
    cbs, y_offs, new_states = [], [], []
    for g in range(M2_GROUPS):
        b_g = bm[:, g * M2_STATE:(g + 1) * M2_STATE]
        c_g = cm[:, g * M2_STATE:(g + 1) * M2_STATE].astype(BF16)
        cbs.append(jnp.where(causal, _dot_nt(c_g, b_g.astype(BF16)), 0.0))
        y_offs.append(_dot(c_g, st[:, g * gw:(g + 1) * gw].astype(BF16)) * decay_out[:, g * gw:(g + 1) * gw])
        new_states.append(_dot(b_g.T.astype(BF16), xd[:, g * gw:(g + 1) * gw]))
    st_ref[...] = st * jnp.exp(a_last) + jnp.concatenate(new_states, axis=1)

    y_tiles = []
    for tile in range(M2_HEADS // 2):
        g = (2 * tile) // heads_per_group
        pair = tile - g * (heads_per_group // 2)
        x_tile = xdt[:, tile * LANES:(tile + 1) * LANES]
        y_pair = y_offs[g][:, pair * LANES:(pair + 1) * LANES]
        for half in range(2):
            hd = 2 * tile + half
            seg = a_cum_c[:, SUBLANES + hd:SUBLANES + hd + 1] - a_cum_t[SUBLANES + hd:SUBLANES + hd + 1, :]
            m_h = (cbs[g] * jnp.exp(jnp.minimum(seg, 0.0))).astype(BF16)
            x_h = jnp.where(low if half == 0 else ~low, x_tile, 0.0).astype(BF16)
            y_pair = y_pair + _dot(m_h, x_h)
        y_tiles.append(y_pair)

    y = jnp.concatenate(y_tiles, axis=1) + d_x * xs
    y = y * _silu(z_ref[...])
    nw = nw_ref[...]
    outs = []
    for g in range(M2_GROUPS):
        outs.append(_rms(y[:, g * gw:(g + 1) * gw], nw[:, g * gw:(g + 1) * gw]))
    o_ref[...] = jnp.concatenate(outs, axis=1).astype(o_ref.dtype)


def _ssd(z, xbc, small, conv_w, conv_b, head_rows, small_rows, norm_w, tri, expand, batch, seq):
    t, width = z.shape
    ch = xbc.shape[1]
    step_rows = SCAN_CHUNK * SCAN_CHUNKS_PER_STEP
    nc = seq // step_rows
    row = lambda b, j: (b * nc + j, 0)
    const = lambda a: pl.BlockSpec(a.shape, lambda b, j: (0, 0))
    return pl.pallas_call(
        _ssd_kernel,
        grid=(batch, nc),
        in_specs=[
            pl.BlockSpec((step_rows, width), row),
            pl.BlockSpec((step_rows, ch), row),
            pl.BlockSpec((step_rows, LANES), row),
            const(conv_w), const(conv_b), const(head_rows), const(small_rows), const(norm_w),
            const(tri), const(expand),
        ],
        out_specs=pl.BlockSpec((step_rows, width), row),
        out_shape=jax.ShapeDtypeStruct((t, width), BF16),
        scratch_shapes=[
            pltpu.VMEM((SCAN_CHUNK + SUBLANES, ch), F32),
            pltpu.VMEM((M2_STATE, width), F32),
        ],
        compiler_params=_cparams("parallel", "arbitrary"),
        name="ssd",
    )(z, xbc, small, conv_w, conv_b, head_rows, small_rows, norm_w, tri, expand)


def _resident(a):
    zeros = (0,) * a.ndim
    return pl.BlockSpec(a.shape, lambda *_: zeros, pipeline_mode=pl.Buffered(1))


def _xattn_kernel(h_ref, a_ref, b_ref, c_ref, wout_ref, nw_ref, wq_ref, kv_ref, wo_ref, o_ref):
    wa = a_ref.shape[1]
    wb = b_ref.shape[1]
    h = h_ref[...] + _dot(a_ref[...], wout_ref[0:wa, :])
    h = h + _dot(b_ref[...], wout_ref[wa:wa + wb, :])
    h = h + _dot(c_ref[...], wout_ref[wa + wb:, :])
    d = h.shape[1]
    hd = d // XA_HEADS
    q = (_dot(_rms(h, nw_ref[...]).astype(BF16), wq_ref[...]) * (hd ** -0.5)).astype(BF16)
    outs = []
    for a in range(XA_HEADS):
        k_a = kv_ref[:, a * hd:(a + 1) * hd]
        v_a = kv_ref[:, d + a * hd:d + (a + 1) * hd]
        s = _dot_nt(q[:, a * hd:(a + 1) * hd], k_a)
        p = jnp.exp(s - jnp.max(s, axis=-1, keepdims=True))
        p = p / jnp.sum(p, axis=-1, keepdims=True)
        outs.append(_dot(p.astype(BF16), v_a).astype(BF16))
    o_ref[...] = h + _dot(jnp.concatenate(outs, axis=1), wo_ref[...])


def _xattn(h, o_fox, o_hg, o_m2, w_out, nw, wq, kv, wo, batch, seq):
    t, d = h.shape
    n_mem = kv.shape[0] // batch
    nb = seq // ROW_BLOCK
    row = lambda x: pl.BlockSpec((ROW_BLOCK, x.shape[1]), lambda b, i: (b * nb + i, 0))
    return pl.pallas_call(
        _xattn_kernel,
        grid=(batch, nb),
        in_specs=[
            row(h), row(o_fox), row(o_hg), row(o_m2),
            _resident(w_out), _resident(nw), _resident(wq),
            pl.BlockSpec((n_mem, kv.shape[1]), lambda b, i: (b, 0)),
            _resident(wo),
        ],
        out_specs=row(h),
        out_shape=jax.ShapeDtypeStruct((t, d), F32),
        compiler_params=_cparams("parallel", "parallel"),
        name="xattn",
    )(h, o_fox, o_hg, o_m2, w_out, nw, wq, kv, wo)


def _swiglu(x, w1_ref, w2_ref, d_ff):
    a = (_silu(_dot(x, w1_ref[:, :d_ff])) * _dot(x, w1_ref[:, d_ff:])).astype(BF16)
    return _dot(a, w2_ref[...])


def _ffn_kernel(h_ref, nw_ref, w1_ref, w2_ref, o_ref):
    h = h_ref[...]
    o_ref[...] = h + _swiglu(_rms(h, nw_ref[...]).astype(BF16), w1_ref, w2_ref, w2_ref.shape[0])


def _ffn(h, nw, w1, w2):
    t, d = h.shape
    return pl.pallas_call(
        _ffn_kernel,
        grid=(t // ROW_BLOCK,),
        in_specs=[pl.BlockSpec((ROW_BLOCK, d), lambda i: (i, 0)), _resident(nw), _resident(w1), _resident(w2)],
        out_specs=pl.BlockSpec((ROW_BLOCK, d), lambda i: (i, 0)),
        out_shape=jax.ShapeDtypeStruct((t, d), F32),
        compiler_params=_cparams("parallel"),
        name="ffn",
    )(h, nw, w1, w2)


def _router_kernel(h_ref, nw_ref, wr_ref, tri_ref, xn_ref, idx_ref, gate_ref, rank_ref, cnt_ref, carry_ref):
    @pl.when(pl.program_id(0) == 0)
    def _():
        carry_ref[...] = jnp.zeros_like(carry_ref)

    xn = _rms(h_ref[...], nw_ref[...])
    xn_ref[...] = xn
    x1 = xn.astype(BF16)
    x2 = (xn - x1.astype(F32)).astype(BF16)
    w1, w2, w3 = wr_ref[0], wr_ref[1], wr_ref[2]
    logits = _dot(x1, w1) + _dot(x1, w2) + _dot(x2, w1) + _dot(x1, w3) + _dot(x2, w2)
    lane = lax.broadcasted_iota(jnp.int32, logits.shape, 1)
    logits = jnp.where(lane < N_EXPERTS, logits, -jnp.inf)
    m1 = jnp.max(logits, axis=-1, keepdims=True)
    i1 = jnp.min(jnp.where(logits == m1, lane, LANES), axis=-1, keepdims=True)
    rest = jnp.where(lane == i1, -jnp.inf, logits)
    m2 = jnp.max(rest, axis=-1, keepdims=True)
    i2 = jnp.min(jnp.where(rest == m2, lane, LANES), axis=-1, keepdims=True)
    e2 = jnp.exp(m2 - m1)
    denom = 1.0 + e2
    idx_ref[...] = jnp.where(lane == 0, i1, jnp.where(lane == 1, i2, 0))
    gate_ref[...] = jnp.where(lane == 0, 1.0 / denom, jnp.where(lane == 1, e2 / denom, 0.0))

    hit1 = lane == i1
    hit2 = lane == i2
    member = jnp.where(hit1, 1.0, jnp.where(hit2, 1.0, 0.0))
    incl = _dot(tri_ref[...], member.astype(BF16)) + carry_ref[...]
    excl = incl - member
    r1 = jnp.sum(jnp.where(hit1, excl, 0.0), axis=-1, keepdims=True)
    r2 = jnp.sum(jnp.where(hit2, excl, 0.0), axis=-1, keepdims=True)
    rows = member.shape[0]
    carry_ref[...] = incl[rows - 1:rows, :]
    rank_ref[...] = jnp.where(lane == 0, r1, jnp.where(lane == 1, r2, 0.0)).astype(jnp.int32)
    cnt_ref[...] = jnp.broadcast_to(incl[rows - 1:rows, :], cnt_ref.shape)


def _router(h, nw, wr3, tri):
    t, d = h.shape
    row = lambda w: pl.BlockSpec((ROW_BLOCK, w), lambda i: (i, 0))
    return pl.pallas_call(
        _router_kernel,
        grid=(t // ROW_BLOCK,),
        in_specs=[
            row(d),
            pl.BlockSpec((1, d), lambda i: (0, 0)),
            pl.BlockSpec(wr3.shape, lambda i: (0, 0, 0)),
            pl.BlockSpec(tri.shape, lambda i: (0, 0)),
        ],
        out_specs=[row(d), row(LANES), row(LANES), row(LANES), pl.BlockSpec((SUBLANES, LANES), lambda i: (0, 0))],
        out_shape=[
            jax.ShapeDtypeStruct((t, d), F32),
            jax.ShapeDtypeStruct((t, LANES), jnp.int32),
            jax.ShapeDtypeStruct((t, LANES), F32),
            jax.ShapeDtypeStruct((t, LANES), jnp.int32),
            jax.ShapeDtypeStruct((SUBLANES, LANES), F32),
        ],
        scratch_shapes=[pltpu.VMEM((1, LANES), F32)],
        compiler_params=_cparams("arbitrary"),
        name="router",
    )(h, nw, wr3, tri)


def _slot_kernel(idx_ref, rank_ref, start_ref, o_ref):
    idx = idx_ref[...]
    lane = lax.broadcasted_iota(jnp.int32, idx.shape, 1)
    starts = start_ref[...]
    s1 = jnp.sum(jnp.where(lane == idx[:, 0:1], starts, 0.0), axis=-1, keepdims=True)
    s2 = jnp.sum(jnp.where(lane == idx[:, 1:2], starts, 0.0), axis=-1, keepdims=True)
    o_ref[...] = jnp.where(lane == 0, s1, jnp.where(lane == 1, s2, 0.0)).astype(jnp.int32) + rank_ref[...]


def _slots(idx, rank, start_row):
    t = idx.shape[0]
    row = pl.BlockSpec((ROW_BLOCK, LANES), lambda i: (i, 0))
    return pl.pallas_call(
        _slot_kernel,
        grid=(t // ROW_BLOCK,),
        in_specs=[row, row, pl.BlockSpec((1, LANES), lambda i: (0, 0))],
        out_specs=row,
        out_shape=jax.ShapeDtypeStruct((t, LANES), jnp.int32),
        compiler_params=_cparams("parallel"),
        name="moe_slots",
    )(idx, rank, start_row)


def _dispatch_kernel(pend_ref, padded_ref, nused_ref, dest_ref, x_ref, xs_ref, zero_ref, sem):
    rows = x_ref.shape[0]
    n_blk = xs_ref.shape[0] // MOE_ROWS

    @pl.when(pl.program_id(0) == 0)
    def _():
        zero_ref[...] = jnp.zeros_like(zero_ref)

        def fill_block(start):
            fill = pltpu.make_async_copy(zero_ref, xs_ref.at[pl.ds(start, MOE_ROWS), :], sem)
            fill.start()
            fill.wait()

        for e in range(N_EXPERTS):
            @pl.when(padded_ref[e] > 0)
            def _():
                fill_block(pl.multiple_of(pend_ref[e] - MOE_ROWS, MOE_ROWS))

            @pl.when(nused_ref[0] + e < n_blk)
            def _():
                fill_block(pl.multiple_of((nused_ref[0] + e) * MOE_ROWS, MOE_ROWS))

    def issue(group, carry):
        base = pl.multiple_of(group * SUBLANES, SUBLANES)
        for s in range(SUBLANES):
            for k in range(2):
                slot = dest_ref[0, 0, k * rows + base + s]
                pltpu.make_async_copy(x_ref.at[pl.ds(base + s, 1), :], xs_ref.at[pl.ds(slot, 1), :], sem).start()
        return carry

    lax.fori_loop(0, rows // SUBLANES, issue, 0)
    for k in range(2):
        pltpu.make_async_copy(x_ref, xs_ref.at[pl.ds(0, rows), :], sem).wait()


def _dispatch(pends, padded, n_used, dest_t, xn, cap):
    t, d = xn.shape
    grid_spec = pltpu.PrefetchScalarGridSpec(
        num_scalar_prefetch=3,
        grid=(t // ROW_BLOCK,),
        in_specs=[
            pl.BlockSpec((1, 1, 2 * ROW_BLOCK), lambda i, pe, pa, nu: (i, 0, 0), memory_space=pltpu.SMEM),
            pl.BlockSpec((ROW_BLOCK, d), lambda i, pe, pa, nu: (i, 0)),
        ],
        out_specs=pl.BlockSpec(memory_space=pl.ANY),
        scratch_shapes=[pltpu.VMEM((MOE_ROWS, d), F32), pltpu.SemaphoreType.DMA(())],
    )
    return pl.pallas_call(
        _dispatch_kernel,
        grid_spec=grid_spec,
        out_shape=jax.ShapeDtypeStruct((cap, d), F32),
        compiler_params=_cparams("arbitrary"),
        name="moe_dispatch",
    )(pends, padded, n_used, dest_t, xn)


def _expert_kernel(be_ref, nused_ref, x_ref, w1_ref, w2_ref, o_ref):
    live = pl.program_id(0) < nused_ref[0]

    @pl.when(live)
    def _():
        o_ref[...] = _swiglu(x_ref[...].astype(BF16), w1_ref.at[0], w2_ref.at[0], w2_ref.shape[1])

    @pl.when(jnp.logical_not(live))
    def _():
        o_ref[...] = jnp.zeros_like(o_ref)


def _experts(blk_expert, n_used, xs, w1, w2):
    cap, d = xs.shape
    blk = lambda i, be, nu: (i, 0)
    expert = lambda shape: pl.BlockSpec((1,) + shape[1:], lambda i, be, nu: (be[i], 0, 0),
                                        pipeline_mode=pl.Buffered(1))
    grid_spec = pltpu.PrefetchScalarGridSpec(
        num_scalar_prefetch=2,
        grid=(cap // MOE_ROWS,),
        in_specs=[pl.BlockSpec((MOE_ROWS, d), blk), expert(w1.shape), expert(w2.shape)],
        out_specs=pl.BlockSpec((MOE_ROWS, d), blk),
    )
    return pl.pallas_call(
        _expert_kernel,
        grid_spec=grid_spec,
        out_shape=jax.ShapeDtypeStruct((cap, d), F32),
        compiler_params=_cparams("arbitrary"),
        name="experts",
    )(blk_expert, n_used, xs, w1, w2)


def _combine_kernel(dest_ref, h_ref, gate_ref, nw_ref, yb_ref, o_ref, buf_ref, sem, *, final_norm):
    rows = h_ref.shape[0]

    def issue(group, carry):
        base = pl.multiple_of(group * SUBLANES, SUBLANES)
        for s in range(SUBLANES):
            for k in range(2):
                slot = dest_ref[0, 0, k * rows + base + s]
                pltpu.make_async_copy(
                    yb_ref.at[pl.ds(slot, 1), :], buf_ref.at[k, pl.ds(base + s, 1), :], sem).start()
        return carry

    lax.fori_loop(0, rows // SUBLANES, issue, 0)
    for k in range(2):
        pltpu.make_async_copy(yb_ref.at[pl.ds(0, rows), :], buf_ref.at[k], sem).wait()
    gate = gate_ref[...]
    out = h_ref[...] + gate[:, 0:1] * buf_ref[0] + gate[:, 1:2] * buf_ref[1]
    if final_norm:
        out = _rms(out, nw_ref[...])
    o_ref[...] = out


def _combine(dest_t, h, gate, nw, yb, final_norm):
    t, d = h.shape
    return pl.pallas_call(
        functools.partial(_combine_kernel, final_norm=final_norm),
        grid=(t // ROW_BLOCK,),
        in_specs=[
            pl.BlockSpec((1, 1, 2 * ROW_BLOCK), lambda i: (i, 0, 0), memory_space=pltpu.SMEM),
            pl.BlockSpec((ROW_BLOCK, d), lambda i: (i, 0)),
            pl.BlockSpec((ROW_BLOCK, LANES), lambda i: (i, 0)),
            pl.BlockSpec((1, d), lambda i: (0, 0)),
            pl.BlockSpec(memory_space=pl.ANY),
        ],
        out_specs=pl.BlockSpec((ROW_BLOCK, d), lambda i: (i, 0)),
        out_shape=jax.ShapeDtypeStruct((t, d), F32),
        scratch_shapes=[pltpu.VMEM((2, ROW_BLOCK, d), F32), pltpu.SemaphoreType.DMA(())],
        compiler_params=_cparams("arbitrary"),
        name="moe_combine",
    )(dest_t, h, gate, nw, yb)


def _final_norm_kernel(h_ref, nw_ref, o_ref):
    o_ref[...] = _rms(h_ref[...], nw_ref[...])


def _final_norm(h, nw):
    t, d = h.shape
    return pl.pallas_call(
        _final_norm_kernel,
        grid=(t // ROW_BLOCK,),
        in_specs=[pl.BlockSpec((ROW_BLOCK, d), lambda i: (i, 0)), pl.BlockSpec((1, d), lambda i: (0, 0))],
        out_specs=pl.BlockSpec((ROW_BLOCK, d), lambda i: (i, 0)),
        out_shape=jax.ShapeDtypeStruct((t, d), F32),
        compiler_params=_cparams("parallel"),
        name="final_norm",
    )(h, nw)


def _moe(h, nw, router_w, w1, w2, tri, out_nw, final_norm):
    t, d = h.shape
    assert MOE_ROWS == ROW_BLOCK
    r1, r2, r3 = _split3(jnp.pad(router_w, ((0, 0), (0, LANES - N_EXPERTS))))
    xn, idx, gate, rank, cnt = _router(h, nw, jnp.stack([r1, r2, r3]), tri)
    counts = cnt[0, :N_EXPERTS].astype(jnp.int32)
    padded = (counts + MOE_ROWS - 1) // MOE_ROWS * MOE_ROWS
    pends = jnp.cumsum(padded)
    starts = pends - padded
    cap = (2 * t // MOE_ROWS + N_EXPERTS) * MOE_ROWS
    n_blk = cap // MOE_ROWS
    blk_start = jnp.arange(n_blk, dtype=jnp.int32) * MOE_ROWS
    blk_expert = jnp.minimum(jnp.sum(blk_start[:, None] >= pends[None, :], axis=1), N_EXPERTS - 1).astype(jnp.int32)
    n_used = (pends[-1:] // MOE_ROWS).astype(jnp.int32)
    start_row = jnp.zeros((1, LANES), F32).at[0, :N_EXPERTS].set(starts.astype(F32))
    dest = _slots(idx, rank, start_row)
    dest_t = dest[:, :2].reshape(t // ROW_BLOCK, ROW_BLOCK, 2).transpose(0, 2, 1).reshape(
        t // ROW_BLOCK, 1, 2 * ROW_BLOCK)
    xs = _dispatch(pends.astype(jnp.int32), padded.astype(jnp.int32), n_used, dest_t, xn, cap)
    yb = _experts(blk_expert, n_used, xs, w1, w2)
    return _combine(dest_t, h, gate, out_nw, yb, final_norm)


def _tri(n):
    return jnp.tril(jnp.ones((n, n), F32)).astype(BF16)


def kernel(x, mem, mix_norm_w, w_in, fox_f_bias, fox_norm_w, hg_lb_raw, hg_norm_w, m2_conv_w, m2_conv_b, m2_dt_bias, m2_a_log, m2_d, m2_norm_w, w_out, xa_norm_w, xa_mem_norm_w, xa_w_q, xa_w_kv, xa_w_o, ffn_norm_w, ffn_w1, ffn_w2, moe_router, moe_w1, moe_w2, final_norm_w):
    batch, seq, d = x.shape
    depth = w_in.shape[0]
    t = batch * seq
    fox_w = N_FOX_HEADS * HEAD_DIM
    hg_w = N_HG_HEADS * HEAD_DIM
    m2_w = M2_HEADS * HEAD_DIM
    conv_ch = m2_w + 2 * M2_GROUPS * M2_STATE
    in_splits = (fox_w, fox_w, fox_w, N_FOX_HEADS, hg_w, hg_w, hg_w, hg_w, m2_w, conv_ch, M2_HEADS)
    offs = [0]
    for s in in_splits:
        offs.append(offs[-1] + s)

    lb_p = jax.nn.softmax(hg_lb_raw.astype(F32), axis=0)
    hg_lb = jnp.cumsum(lb_p, axis=0) - lb_p[0]

    tri_seq = _tri(SEQ_BLOCK)
    tri_chunk = _tri(SCAN_CHUNK)
    head_of_lane = jnp.arange(hg_w) // HEAD_DIM
    seg_ones = (head_of_lane[:, None] == head_of_lane[None, :]).astype(BF16)
    expand = (jnp.arange(LANES)[:, None] - SUBLANES == jnp.arange(m2_w)[None, :] // HEAD_DIM).astype(BF16)

    def pad_lanes(v, offset=0, width=LANES):
        return jnp.zeros((1, width), F32).at[0, offset:offset + v.shape[0]].set(v)

    h = x.reshape(t, d)
    mem2 = mem.reshape(batch * mem.shape[1], d)
    for layer in range(depth):
        w = w_in[layer]
        small_w = jnp.zeros((d, LANES), F32)
        small_w = small_w.at[:, 0:N_FOX_HEADS].set(w[:, offs[3]:offs[4]])
        small_w = small_w.at[:, SUBLANES:SUBLANES + M2_HEADS].set(w[:, offs[10]:offs[11]])
        w_cat = jnp.concatenate(
            [w[:, offs[0]:offs[3]], small_w, w[:, offs[4]:offs[8]], w[:, offs[8]:offs[9]], w[:, offs[9]:offs[10]]],
            axis=1).astype(BF16)
        splits = (3 * fox_w, LANES, 4 * hg_w, m2_w, conv_ch)
        qkv, small, hg, z, xbc = _in_proj(h, mix_norm_w[layer][None, :], w_cat, splits)

        qa, ka, va = _fox_prep(qkv, small, pad_lanes(fox_f_bias[layer]), tri_seq, batch, seq)
        fox_nw = jnp.zeros((SUBLANES, LANES), F32).at[:fox_w // LANES].set(
            fox_norm_w[layer].reshape(fox_w // LANES, LANES))
        o_fox = _fox_attn(qa, ka, va, fox_nw, batch, seq)

        lb = hg_lb[layer]
        hg_params = jnp.zeros((SUBLANES, hg_w), F32)
        hg_params = hg_params.at[0].set(jnp.log(jnp.maximum(lb, LB_FLOOR)))
        hg_params = hg_params.at[1].set(jnp.log1p(-lb))
        hg_params = hg_params.at[2].set(1.0 - lb)
        hg_params = hg_params.at[3].set(hg_norm_w[layer])
        o_hg = _hgrn2(hg, hg_params, tri_chunk, seg_ones, batch, seq)

        conv_w = jnp.zeros((SUBLANES, conv_ch), F32).at[:M2_CONV].set(m2_conv_w[layer])
        head_rows = jnp.zeros((SUBLANES, m2_w), F32).at[0].set(jnp.repeat(m2_d[layer], HEAD_DIM))
        small_rows = jnp.concatenate(
            [pad_lanes(m2_dt_bias[layer], SUBLANES), pad_lanes(m2_a_log[layer], SUBLANES),
             jnp.zeros((SUBLANES - 2, LANES), F32)], axis=0)
        o_m2 = _ssd(z, xbc, small, conv_w, m2_conv_b[layer][None, :], head_rows, small_rows,
                    m2_norm_w[layer][None, :], tri_chunk, expand, batch, seq)

        kv = _norm_proj(mem2, xa_mem_norm_w[layer][None, :], xa_w_kv[layer].astype(BF16), BF16, mem.shape[1])
        h = _xattn(h, o_fox, o_hg, o_m2, w_out[layer].astype(BF16), xa_norm_w[layer][None, :],
                   xa_w_q[layer].astype(BF16), kv, xa_w_o[layer].astype(BF16), batch, seq)

        nw = ffn_norm_w[layer][None, :]
        if layer % 2 == 0:
            h = _ffn(h, nw, ffn_w1[layer // 2].astype(BF16), ffn_w2[layer // 2].astype(BF16))
        else:
            last = layer == depth - 1
            h = _moe(h, nw, moe_router[layer // 2], moe_w1[layer // 2].astype(BF16),
                     moe_w2[layer // 2].astype(BF16), tri_seq, final_norm_w[None, :], last)
    if depth % 2:
        h = _final_norm(h, final_norm_w[None, :])
    return h.reshape(batch, seq, d)
```

```python
import functools
import math

import jax
import jax.numpy as jnp
from jax import lax
from jax.experimental import pallas as pl
from jax.experimental.pallas import tpu as pltpu

F32 = jnp.float32
BF16 = jnp.bfloat16

EPS = 1e-6
MASK_VALUE = -1e9
LB_FLOOR = 1e-30
HEAD_DIM = 64
N_FOX_HEADS = 4
N_HG_HEADS = 4
M2_HEADS = 8
M2_GROUPS = 2
M2_STATE = 128
M2_CONV = 4
XA_HEADS = 4
N_EXPERTS = 8

LANES = 128
SUBLANES = 8
VMEM_LIMIT_BYTES = 56 * 1024 * 1024

ROW_BLOCK = 512
SEQ_BLOCK = 512
SCAN_CHUNK = 128
SCAN_CHUNKS_PER_STEP = 2
SUB_BLOCK = 16
HG_SAFE_DECAY = 60.0
MOE_ROWS = 512


def _cparams(*sem):
    return pltpu.CompilerParams(dimension_semantics=sem, vmem_limit_bytes=VMEM_LIMIT_BYTES)


def _split3(x):
    x1 = x.astype(BF16)
    r1 = x - x1.astype(F32)
    x2 = r1.astype(BF16)
    x3 = (r1 - x2.astype(F32)).astype(BF16)
    return x1, x2, x3


def _dot(a, b):
    return jnp.dot(a, b, preferred_element_type=F32)


def _dot_nt(a, b):
    return lax.dot_general(a, b, (((1,), (1,)), ((), ())), preferred_element_type=F32)


def _exact_left(mat01, x):
    x1, x2, x3 = _split3(x)
    return _dot(mat01, x1) + _dot(mat01, x2) + _dot(mat01, x3)


def _exact_right(x, mat01):
    x1, x2, x3 = _split3(x)
    return _dot(x1, mat01) + _dot(x2, mat01) + _dot(x3, mat01)


def _rms(x, w):
    return x * lax.rsqrt(jnp.mean(x * x, axis=-1, keepdims=True) + EPS) * w


def _sigmoid(x):
    return 1.0 / (1.0 + jnp.exp(-x))


def _silu(x):
    return x * _sigmoid(x)


def _log1p_exp(x):
    return jnp.log(1.0 + jnp.exp(x))


def _log_sigmoid(x):
    return jnp.minimum(x, 0.0) - _log1p_exp(-jnp.abs(x))


def _softplus(x):
    return jnp.maximum(x, 0.0) + _log1p_exp(-jnp.abs(x))


def _masked_exp(x, mask):
    return jnp.where(mask, jnp.exp(jnp.where(mask, x, 0.0)), 0.0)


def _resident(a):
    zeros = (0,) * a.ndim
    return pl.BlockSpec(a.shape, lambda *_: zeros, pipeline_mode=pl.Buffered(1))


def _layer_resident(stacked, layer):
    index = (layer,) + (0,) * (stacked.ndim - 1)
    return pl.BlockSpec((None,) + stacked.shape[1:], lambda *_: index, pipeline_mode=pl.Buffered(1))


def _in_proj_kernel(x_ref, nw_ref, *refs):
    n = len(refs) // 2
    xn = _rms(x_ref[...], nw_ref[...]).astype(BF16)
    for w_ref, o_ref in zip(refs[:n], refs[n:]):
        o_ref[...] = _dot(xn, w_ref[...]).astype(o_ref.dtype)


def _in_proj(h, nw, weights, layer, out_dtypes):
    t, d = h.shape
    widths = [w.shape[-1] for w in weights]
    return pl.pallas_call(
        _in_proj_kernel,
        grid=(t // ROW_BLOCK,),
        in_specs=[pl.BlockSpec((ROW_BLOCK, d), lambda i: (i, 0)), _resident(nw)]
        + [_layer_resident(w, layer) for w in weights],
        out_specs=[pl.BlockSpec((ROW_BLOCK, w), lambda i: (i, 0)) for w in widths],
        out_shape=[jax.ShapeDtypeStruct((t, w), dt) for w, dt in zip(widths, out_dtypes)],
        compiler_params=_cparams("parallel"),
        name="in_proj",
    )(h, nw, *weights)


def _norm_proj_kernel(x_ref, nw_ref, w_ref, o_ref):
    xn = _rms(x_ref[...], nw_ref[...]).astype(BF16)
    o_ref[...] = _dot(xn, w_ref[...]).astype(o_ref.dtype)


def _norm_proj(x, nw, w, layer, out_dtype, rows):
    t, d = x.shape
    n = w.shape[-1]
    return pl.pallas_call(
        _norm_proj_kernel,
        grid=(t // rows,),
        in_specs=[pl.BlockSpec((rows, d), lambda i: (i, 0)), _resident(nw), _layer_resident(w, layer)],
        out_specs=pl.BlockSpec((rows, n), lambda i: (i, 0)),
        out_shape=jax.ShapeDtypeStruct((t, n), out_dtype),
        compiler_params=_cparams("parallel"),
        name="norm_proj",
    )(x, nw, w)


def _fox_prep_kernel(qkv_ref, small_ref, bias_ref, tri_ref, q_ref, k_ref, v_ref, carry_ref):
    @pl.when(pl.program_id(1) == 0)
    def _():
        carry_ref[...] = jnp.zeros_like(carry_ref)

    rows = qkv_ref.shape[0]
    log_f = _log_sigmoid(small_ref[...] + bias_ref[...])
    c = _exact_left(tri_ref[...], log_f) + carry_ref[...]
    carry_ref[...] = c[rows - 1:rows, :]

    lane = lax.broadcasted_iota(jnp.int32, (rows, LANES), 1)
    width = N_FOX_HEADS * HEAD_DIM
    scale = HEAD_DIM ** -0.5
    for hd in range(N_FOX_HEADS):
        tile = (hd * HEAD_DIM) // LANES
        ch = c[:, hd:hd + 1]
        c1 = ch.astype(BF16).astype(F32)
        r1 = ch - c1
        c2 = r1.astype(BF16).astype(F32)
        c3 = r1 - c2

        def head_tile(base):
            x = qkv_ref[:, base + tile * LANES: base + (tile + 1) * LANES].astype(F32)
            if (hd * HEAD_DIM) % LANES:
                x = pltpu.roll(x, LANES - (hd * HEAD_DIM) % LANES, axis=1)
            return x

        def augment(x, first, second):
            out = jnp.where(lane < HEAD_DIM, x, 0.0)
            for j, val in enumerate(first + second):
                out = jnp.where(lane == HEAD_DIM + j, val, out)
            return out.astype(BF16)

        ones = (1.0, 1.0, 1.0)
        q_ref[hd] = augment(head_tile(0) * scale, ones, (c1, c2, c3))
        k_ref[hd] = augment(head_tile(width), (-c1, -c2, -c3), ones)
        v_aug = jnp.where(lane < HEAD_DIM, head_tile(2 * width), jnp.where(lane == HEAD_DIM, 1.0, 0.0))
        v_ref[hd] = v_aug.T.astype(BF16)


def _fox_prep(qkv, small, bias_row, tri, batch, seq):
    t = qkv.shape[0]
    nb = seq // SEQ_BLOCK
    row = lambda b, c: (b * nb + c, 0)
    head_spec = pl.BlockSpec((N_FOX_HEADS, SEQ_BLOCK, LANES), lambda b, c: (0, b * nb + c, 0))
    head_shape = jax.ShapeDtypeStruct((N_FOX_HEADS, t, LANES), BF16)
    vt_spec = pl.BlockSpec((N_FOX_HEADS, LANES, SEQ_BLOCK), lambda b, c: (0, 0, b * nb + c))
    vt_shape = jax.ShapeDtypeStruct((N_FOX_HEADS, LANES, t), BF16)
    return pl.pallas_call(
        _fox_prep_kernel,
        grid=(batch, nb),
        in_specs=[
            pl.BlockSpec((SEQ_BLOCK, qkv.shape[1]), row),
            pl.BlockSpec((SEQ_BLOCK, LANES), row),
            pl.BlockSpec((1, LANES), lambda b, c: (0, 0)),
            pl.BlockSpec((SEQ_BLOCK, SEQ_BLOCK), lambda b, c: (0, 0)),
        ],
        out_specs=[head_spec, head_spec, vt_spec],
        out_shape=[head_shape, head_shape, vt_shape],
        scratch_shapes=[pltpu.VMEM((1, LANES), F32)],
        compiler_params=_cparams("parallel", "arbitrary"),
        name="fox_prep",
    )(qkv, small, bias_row, tri)


def _fox_attn_kernel(q_ref, k_ref, vt_ref, nw_ref, o_ref, m_ref, acc_ref):
    i = pl.program_id(1)
    tq = q_ref.shape[1]
    kv_idx = lax.broadcasted_iota(jnp.int32, (tq, tq), 0)
    q_idx = lax.broadcasted_iota(jnp.int32, (tq, tq), 1)
    heads = range(N_FOX_HEADS)
    m_ref[...] = jnp.full(m_ref.shape, -jnp.inf, F32)
    acc_ref[...] = jnp.zeros_like(acc_ref)

    def process(blocks, diagonal):
        starts = [pl.multiple_of(j * tq, tq) for j in blocks]
        logits = [[_dot_nt(k_ref[hd, pl.ds(start, tq), :], q_ref[hd]) for hd in heads] for start in starts]
        for start, block_logits in zip(starts, logits):
            for hd in heads:
                s = block_logits[hd]
                if diagonal:
                    s = jnp.where(kv_idx <= q_idx, s, MASK_VALUE)
                m_old = m_ref[hd]
                m_new = jnp.maximum(m_old, jnp.max(s, axis=0, keepdims=True))
                p = jnp.exp(s - m_new).astype(BF16)
                acc_ref[hd] = jnp.exp(m_old - m_new) * acc_ref[hd] + _dot(vt_ref[hd, :, pl.ds(start, tq)], p)
                m_ref[hd] = m_new

    def pair(jj, carry):
        process([2 * jj, 2 * jj + 1], False)
        return carry

    lax.fori_loop(0, i // 2, pair, 0)

    @pl.when(i % 2 == 1)
    def _():
        process([i - 1], False)

    process([i], True)

    normed = []
    for hd in heads:
        acc = acc_ref[hd]
        o = acc[:HEAD_DIM] / acc[HEAD_DIM:HEAD_DIM + 1]
        ms = jnp.mean(o * o, axis=0, keepdims=True)
        normed.append(o * lax.rsqrt(ms + EPS))
    tiles = []
    for pair in range(N_FOX_HEADS // 2):
        both = jnp.concatenate(normed[2 * pair:2 * pair + 2], axis=0)
        tiles.append(both.T * nw_ref[pair:pair + 1, :])
    o_ref[...] = jnp.concatenate(tiles, axis=1).astype(o_ref.dtype)


def _fox_attn(qa, ka, vt, nw_pairs, batch, seq):
    t = qa.shape[1]
    nq = seq // SEQ_BLOCK
    return pl.pallas_call(
        _fox_attn_kernel,
        grid=(batch, nq),
        in_specs=[
            pl.BlockSpec((N_FOX_HEADS, SEQ_BLOCK, LANES), lambda b, i: (0, b * nq + i, 0)),
            pl.BlockSpec((N_FOX_HEADS, seq, LANES), lambda b, i: (0, b, 0)),
            pl.BlockSpec((N_FOX_HEADS, LANES, seq), lambda b, i: (0, 0, b)),
            pl.BlockSpec((SUBLANES, LANES), lambda b, i: (0, 0)),
        ],
        out_specs=pl.BlockSpec((SEQ_BLOCK, N_FOX_HEADS * HEAD_DIM), lambda b, i: (b * nq + i, 0)),
        out_shape=jax.ShapeDtypeStruct((t, N_FOX_HEADS * HEAD_DIM), BF16),
        scratch_shapes=[
            pltpu.VMEM((N_FOX_HEADS, 1, SEQ_BLOCK), F32),
            pltpu.VMEM((N_FOX_HEADS, LANES, SEQ_BLOCK), F32),
        ],
        compiler_params=_cparams("parallel", "arbitrary"),
        name="fox_attn",
    )(qa, ka, vt, nw_pairs)


def _chunk_views(refs, c):
    return [r.at[pl.ds(c * SCAN_CHUNK, SCAN_CHUNK)] for r in refs]


def _hgrn2_kernel(hq_ref, hf_ref, hi_ref, hg_ref, par_ref, tri_ref, seg_ref, o_ref, st_ref):
    @pl.when(pl.program_id(1) == 0)
    def _():
        st_ref[...] = jnp.zeros_like(st_ref)

    for c in range(SCAN_CHUNKS_PER_STEP):
        hq, hf, hi, hg, o = _chunk_views((hq_ref, hf_ref, hi_ref, hg_ref, o_ref), c)
        _hgrn2_chunk(hq, hf, hi, hg, par_ref, tri_ref, seg_ref, o, st_ref)


def _hgrn2_chunk(hq_ref, hf_ref, hi_ref, hg_ref, par_ref, tri_ref, seg_ref, o_ref, st_ref):
    rows, width = hq_ref.shape
    log_lb, log1m_lb, one_m_lb, nw = (par_ref[r:r + 1, :] for r in range(4))
    f_raw = hf_ref[...]
    q = _silu(hq_ref[...]) * (HEAD_DIM ** -0.5)
    b = log1m_lb + _log_sigmoid(f_raw)
    g = jnp.maximum(log_lb, b) + _log1p_exp(-jnp.abs(log_lb - b))
    k = one_m_lb * _sigmoid(-f_raw)
    v = hi_ref[...]
    cum = _exact_left(tri_ref[...], g)

    lane = lax.broadcasted_iota(jnp.int32, (1, width), 1)
    head_masks = [(lane >= hd * HEAD_DIM) & (lane < (hd + 1) * HEAD_DIM) for hd in range(N_HG_HEADS)]
    seg = seg_ref[...]
    v_b = v.astype(BF16)

    st = st_ref[...]
    o_state = _dot_nt((q * jnp.exp(cum)).astype(BF16), st.astype(BF16))

    n_sub = rows // SUB_BLOCK
    refs = [cum[i * SUB_BLOCK - 1:i * SUB_BLOCK] if i else jnp.zeros((1, width), F32) for i in range(n_sub)]
    local = [cum[i * SUB_BLOCK:(i + 1) * SUB_BLOCK] - refs[i] for i in range(n_sub)]

    def stack_heads(x):
        return jnp.concatenate([jnp.where(mk, x, 0.0) for mk in head_masks], axis=0).astype(BF16)

    def unstack_heads(base, p4):
        for hd, mk in enumerate(head_masks):
            base = base + jnp.where(mk, p4[hd * SUB_BLOCK:(hd + 1) * SUB_BLOCK], 0.0)
        return base

    def factored():
        scores = []
        for i in range(n_sub):
            hi = (i + 1) * SUB_BLOCK
            qs = q[hi - SUB_BLOCK:hi] * jnp.exp(local[i])
            ks = (k[:hi] * jnp.exp(refs[i] - cum[:hi])).astype(BF16)
            scores.append(_dot_nt(stack_heads(qs), ks))
        blocks = []
        for i in range(n_sub):
            hi = (i + 1) * SUB_BLOCK
            t_in = lax.broadcasted_iota(jnp.int32, (N_HG_HEADS * SUB_BLOCK, hi), 0) & (SUB_BLOCK - 1)
            s_in = lax.broadcasted_iota(jnp.int32, (N_HG_HEADS * SUB_BLOCK, hi), 1)
            sc = jnp.where(s_in <= t_in + (hi - SUB_BLOCK), scores[i], 0.0).astype(BF16)
            blocks.append(unstack_heads(o_state[hi - SUB_BLOCK:hi], _dot(sc, v_b[:hi])))
        return jnp.concatenate(blocks, axis=0)

    def pairwise():
        t_idx = lax.broadcasted_iota(jnp.int32, (SUB_BLOCK, 1), 0)
        blocks = []
        for i in range(n_sub):
            r0 = i * SUB_BLOCK
            q_i = q[r0:r0 + SUB_BLOCK]
            cum_i = cum[r0:r0 + SUB_BLOCK]
            o_i = o_state[r0:r0 + SUB_BLOCK]
            if i > 0:
                ks = (k[:r0] * jnp.exp(refs[i] - cum[:r0])).astype(BF16)
                sc = _dot_nt(stack_heads(q_i * jnp.exp(local[i])), ks)
                o_i = unstack_heads(o_i, _dot(sc.astype(BF16), v_b[:r0]))
            terms = []
            for s in range(SUB_BLOCK):
                keep = t_idx >= s
                e = _masked_exp(cum_i - cum[r0 + s:r0 + s + 1], keep)
                terms.append((q_i * e * k[r0 + s:r0 + s + 1]).astype(BF16))
            sums = _dot(jnp.concatenate(terms, axis=0), seg)
            for s in range(SUB_BLOCK):
                o_i = o_i + sums[s * SUB_BLOCK:(s + 1) * SUB_BLOCK] * v[r0 + s:r0 + s + 1]
            blocks.append(o_i)
        return jnp.concatenate(blocks, axis=0)

    worst = functools.reduce(jnp.minimum, local)
    o = lax.cond(jnp.min(worst) >= -HG_SAFE_DECAY, factored, pairwise)

    last = cum[rows - 1:rows]
    kd = (k * jnp.exp(last - cum)).astype(BF16)
    upd = _dot(v.T.astype(BF16), kd)
    st_ref[...] = st * jnp.exp(last) + jnp.where(seg > 0, upd, 0.0)

    sq = o * o
    s1 = sq.astype(BF16)
    s2 = (sq - s1.astype(F32)).astype(BF16)
    ms = (_dot(s1, seg) + _dot(s2, seg)) * (1.0 / HEAD_DIM)
    o_ref[...] = (o * lax.rsqrt(ms + EPS) * nw * _silu(hg_ref[...])).astype(o_ref.dtype)


def _hgrn2(hg, params, tri, seg, batch, seq):
    t = hg.shape[0]
    width = N_HG_HEADS * HEAD_DIM
    step_rows = SCAN_CHUNK * SCAN_CHUNKS_PER_STEP
    nc = seq // step_rows
    col = lambda c: pl.BlockSpec((step_rows, width), lambda b, j, c=c: (b * nc + j, c))
    return pl.pallas_call(
        _hgrn2_kernel,
        grid=(batch, nc),
        in_specs=[
            col(0), col(1), col(2), col(3),
            pl.BlockSpec(params.shape, lambda b, j: (0, 0)),
            pl.BlockSpec(tri.shape, lambda b, j: (0, 0)),
            pl.BlockSpec(seg.shape, lambda b, j: (0, 0)),
        ],
        out_specs=pl.BlockSpec((step_rows, width), lambda b, j: (b * nc + j, 0)),
        out_shape=jax.ShapeDtypeStruct((t, width), BF16),
        scratch_shapes=[pltpu.VMEM((width, width), F32)],
        compiler_params=_cparams("parallel", "arbitrary"),
        name="hgrn2",
    )(hg, hg, hg, hg, params, tri, seg)


def _ssd_kernel(z_ref, xbc_ref, small_ref, cw_ref, cb_ref, hp_ref, sp_ref, nw_ref, tri_ref, exp_ref,
                o_ref, buf_ref, st_ref):
    @pl.when(pl.program_id(1) == 0)
    def _():
        buf_ref[0:SUBLANES, :] = jnp.zeros((SUBLANES, buf_ref.shape[1]), F32)
        st_ref[...] = jnp.zeros_like(st_ref)

    for c in range(SCAN_CHUNKS_PER_STEP):
        z, xbc, small, o = _chunk_views((z_ref, xbc_ref, small_ref, o_ref), c)
        _ssd_chunk(z, xbc, small, cw_ref, cb_ref, hp_ref, sp_ref, nw_ref, tri_ref, exp_ref, o, buf_ref, st_ref)


def _ssd_chunk(z_ref, xbc_ref, small_ref, cw_ref, cb_ref, hp_ref, sp_ref, nw_ref, tri_ref, exp_ref,
               o_ref, buf_ref, st_ref):
    rows = z_ref.shape[0]
    width = z_ref.shape[1]
    halo = SUBLANES

    buf_ref[halo:halo + rows, :] = xbc_ref[...]
    conv = cb_ref[...]
    for i in range(M2_CONV):
        off = halo - (M2_CONV - 1) + i
        conv = conv + cw_ref[i:i + 1, :] * buf_ref[off:off + rows, :]
    tail = buf_ref[rows:rows + halo, :]
    buf_ref[0:halo, :] = tail
    xc = _silu(conv)
    xs = xc[:, :width]
    gn = M2_GROUPS * M2_STATE
    bm = xc[:, width:width + gn]
    cm = xc[:, width + gn:width + 2 * gn]

    d_x = hp_ref[0:1, :]
    dt_bias_c, a_log_c = (sp_ref[r:r + 1, :] for r in range(2))
    small = small_ref[...]
    tri = tri_ref[...]
    dt_c = _softplus(small + dt_bias_c)
    a_cum_c = _exact_left(tri, dt_c * -jnp.exp(a_log_c))
    a_cum_t = a_cum_c.T
    expand = exp_ref[...]
    dt_x = _exact_right(dt_c, expand)
    a_cum_x = _exact_right(a_cum_c, expand)

    row = lax.broadcasted_iota(jnp.int32, (rows, rows), 0)
    col = lax.broadcasted_iota(jnp.int32, (rows, rows), 1)
    causal = col <= row
    lane = lax.broadcasted_iota(jnp.int32, (1, LANES), 1)
    low = lane < HEAD_DIM

    xdt = xs * dt_x
    a_last = a_cum_x[rows - 1:rows]
    xd = (xdt * jnp.exp(a_last - a_cum_x)).astype(BF16)
    decay_out = jnp.exp(a_cum_x)
    st = st_ref[...]
    heads_per_group = M2_HEADS // M2_GROUPS
    gw = heads_per_group * HEAD_DIM
    cbs, y_offs, new_states = [], [], []
    for g in range(M2_GROUPS):
        b_g = bm[:, g * M2_STATE:(g + 1) * M2_STATE]
        c_g = cm[:, g * M2_STATE:(g + 1) * M2_STATE].astype(BF16)
        cbs.append(jnp.where(causal, _dot_nt(c_g, b_g.astype(BF16)), 0.0))
        y_offs.append(_dot(c_g, st[:, g * gw:(g + 1) * gw].astype(BF16)) * decay_out[:, g * gw:(g + 1) * gw])
        new_states.append(_dot(b_g.T.astype(BF16), xd[:, g * gw:(g + 1) * gw]))
    st_ref[...] = st * jnp.exp(a_last) + jnp.concatenate(new_states, axis=1)

    y_tiles = []
    for tile in range(M2_HEADS // 2):
        g = (2 * tile) // heads_per_group
        pair = tile - g * (heads_per_group // 2)
        x_tile = xdt[:, tile * LANES:(tile + 1) * LANES]
        y_pair = y_offs[g][:, pair * LANES:(pair + 1) * LANES]
        for half in range(2):
            hd = 2 * tile + half
            seg = a_cum_c[:, SUBLANES + hd:SUBLANES + hd + 1] - a_cum_t[SUBLANES + hd:SUBLANES + hd + 1, :]
            m_h = (cbs[g] * jnp.exp(jnp.minimum(seg, 0.0))).astype(BF16)
            x_h = jnp.where(low if half == 0 else ~low, x_tile, 0.0).astype(BF16)
            y_pair = y_pair + _dot(m_h, x_h)
        y_tiles.append(y_pair)

    y = jnp.concatenate(y_tiles, axis=1) + d_x * xs
    y = y * _silu(z_ref[...])
    nw = nw_ref[...]
    outs = []
    for g in range(M2_GROUPS):
        outs.append(_rms(y[:, g * gw:(g + 1) * gw], nw[:, g * gw:(g + 1) * gw]))
    o_ref[...] = jnp.concatenate(outs, axis=1).astype(o_ref.dtype)


def _ssd(z, xbc, small, conv_w, conv_b, head_rows, small_rows, norm_w, tri, expand, batch, seq):
    t, width = z.shape
    ch = xbc.shape[1]
    step_rows = SCAN_CHUNK * SCAN_CHUNKS_PER_STEP
    nc = seq // step_rows
    row = lambda b, j: (b * nc + j, 0)
    const = lambda a: pl.BlockSpec(a.shape, lambda b, j: (0, 0))
    return pl.pallas_call(
        _ssd_kernel,
        grid=(batch, nc),
        in_specs=[
            pl.BlockSpec((step_rows, width), row),
            pl.BlockSpec((step_rows, ch), row),
            pl.BlockSpec((step_rows, LANES), row),
            const(conv_w), const(conv_b), const(head_rows), const(small_rows), const(norm_w),
            const(tri), const(expand),
        ],
        out_specs=pl.BlockSpec((step_rows, width), row),
        out_shape=jax.ShapeDtypeStruct((t, width), BF16),
        scratch_shapes=[
            pltpu.VMEM((SCAN_CHUNK + SUBLANES, ch), F32),
            pltpu.VMEM((M2_STATE, width), F32),
        ],
        compiler_params=_cparams("parallel", "arbitrary"),
        name="ssd",
    )(z, xbc, small, conv_w, conv_b, head_rows, small_rows, norm_w, tri, expand)


def _xattn_kernel(h_ref, a_ref, b_ref, c_ref, wout_ref, nw_ref, wq_ref, kv_ref, wo_ref, o_ref):
    wa = a_ref.shape[1]
    wb = b_ref.shape[1]
    h = h_ref[...] + _dot(a_ref[...], wout_ref[0:wa, :])
    h = h + _dot(b_ref[...], wout_ref[wa:wa + wb, :])
    h = h + _dot(c_ref[...], wout_ref[wa + wb:, :])
    d = h.shape[1]
    hd = d // XA_HEADS
    q = (_dot(_rms(h, nw_ref[...]).astype(BF16), wq_ref[...]) * (hd ** -0.5)).astype(BF16)
    outs = []
    for a in range(XA_HEADS):
        k_a = kv_ref[:, a * hd:(a + 1) * hd]
        v_a = kv_ref[:, d + a * hd:d + (a + 1) * hd]
        s = _dot_nt(q[:, a * hd:(a + 1) * hd], k_a)
        p = jnp.exp(s - jnp.max(s, axis=-1, keepdims=True))
        p = p / jnp.sum(p, axis=-1, keepdims=True)
        outs.append(_dot(p.astype(BF16), v_a).astype(BF16))
    o_ref[...] = h + _dot(jnp.concatenate(outs, axis=1), wo_ref[...])


def _xattn(h, o_fox, o_hg, o_m2, w_out, nw, wq, kv, wo, layer, batch, seq):
    t, d = h.shape
    n_mem = kv.shape[0] // batch
    nb = seq // ROW_BLOCK
    row = lambda x: pl.BlockSpec((ROW_BLOCK, x.shape[1]), lambda b, i: (b * nb + i, 0))
    return pl.pallas_call(
        _xattn_kernel,
        grid=(batch, nb),
        in_specs=[
            row(h), row(o_fox), row(o_hg), row(o_m2),
            _layer_resident(w_out, layer), _resident(nw), _layer_resident(wq, layer),
            pl.BlockSpec((n_mem, kv.shape[1]), lambda b, i: (b, 0)),
            _layer_resident(wo, layer),
        ],
        out_specs=row(h),
        out_shape=jax.ShapeDtypeStruct((t, d), F32),
        compiler_params=_cparams("parallel", "parallel"),
        name="xattn",
    )(h, o_fox, o_hg, o_m2, w_out, nw, wq, kv, wo)


def _swiglu(x, w1_ref, w2_ref, d_ff):
    a = (_silu(_dot(x, w1_ref[:, :d_ff])) * _dot(x, w1_ref[:, d_ff:])).astype(BF16)
    return _dot(a, w2_ref[...])


def _ffn_kernel(h_ref, nw_ref, w1_ref, w2_ref, o_ref):
    h = h_ref[...]
    o_ref[...] = h + _swiglu(_rms(h, nw_ref[...]).astype(BF16), w1_ref, w2_ref, w2_ref.shape[0])


def _ffn(h, nw, w1, w2, layer):
    t, d = h.shape
    return pl.pallas_call(
        _ffn_kernel,
        grid=(t // ROW_BLOCK,),
        in_specs=[pl.BlockSpec((ROW_BLOCK, d), lambda i: (i, 0)), _resident(nw),
                  _layer_resident(w1, layer), _layer_resident(w2, layer)],
        out_specs=pl.BlockSpec((ROW_BLOCK, d), lambda i: (i, 0)),
        out_shape=jax.ShapeDtypeStruct((t, d), F32),
        compiler_params=_cparams("parallel"),
        name="ffn",
    )(h, nw, w1, w2)


def _router_kernel(h_ref, nw_ref, wr_ref, tri_ref, xn_ref, idx_ref, gate_ref, rank_ref, cnt_ref, carry_ref):
    @pl.when(pl.program_id(0) == 0)
    def _():
        carry_ref[...] = jnp.zeros_like(carry_ref)

    xn = _rms(h_ref[...], nw_ref[...])
    xn_ref[...] = xn
    x1 = xn.astype(BF16)
    x2 = (xn - x1.astype(F32)).astype(BF16)
    w1, w2, w3 = wr_ref[0], wr_ref[1], wr_ref[2]
    logits = _dot(x1, w1) + _dot(x1, w2) + _dot(x2, w1) + _dot(x1, w3) + _dot(x2, w2)
    lane = lax.broadcasted_iota(jnp.int32, logits.shape, 1)
    logits = jnp.where(lane < N_EXPERTS, logits, -jnp.inf)
    m1 = jnp.max(logits, axis=-1, keepdims=True)
    i1 = jnp.min(jnp.where(logits == m1, lane, LANES), axis=-1, keepdims=True)
    rest = jnp.where(lane == i1, -jnp.inf, logits)
    m2 = jnp.max(rest, axis=-1, keepdims=True)
    i2 = jnp.min(jnp.where(rest == m2, lane, LANES), axis=-1, keepdims=True)
    e2 = jnp.exp(m2 - m1)
    denom = 1.0 + e2
    idx_ref[...] = jnp.where(lane == 0, i1, jnp.where(lane == 1, i2, 0))
    gate_ref[...] = jnp.where(lane == 0, 1.0 / denom, jnp.where(lane == 1, e2 / denom, 0.0))

    hit1 = lane == i1
    hit2 = lane == i2
    member = jnp.where(hit1, 1.0, jnp.where(hit2, 1.0, 0.0))
    incl = _dot(tri_ref[...], member.astype(BF16)) + carry_ref[...]
    excl = incl - member
    r1 = jnp.sum(jnp.where(hit1, excl, 0.0), axis=-1, keepdims=True)
    r2 = jnp.sum(jnp.where(hit2, excl, 0.0), axis=-1, keepdims=True)
    rows = member.shape[0]
    carry_ref[...] = incl[rows - 1:rows, :]
    rank_ref[...] = jnp.where(lane == 0, r1, jnp.where(lane == 1, r2, 0.0)).astype(jnp.int32)
    cnt_ref[...] = jnp.broadcast_to(incl[rows - 1:rows, :], cnt_ref.shape)


def _router(h, nw, wr3, tri):
    t, d = h.shape
    row = lambda w: pl.BlockSpec((ROW_BLOCK, w), lambda i: (i, 0))
    return pl.pallas_call(
        _router_kernel,
        grid=(t // ROW_BLOCK,),
        in_specs=[
            row(d),
            pl.BlockSpec((1, d), lambda i: (0, 0)),
            pl.BlockSpec(wr3.shape, lambda i: (0, 0, 0)),
            pl.BlockSpec(tri.shape, lambda i: (0, 0)),
        ],
        out_specs=[row(d), row(LANES), row(LANES), row(LANES), pl.BlockSpec((SUBLANES, LANES), lambda i: (0, 0))],
        out_shape=[
            jax.ShapeDtypeStruct((t, d), F32),
            jax.ShapeDtypeStruct((t, LANES), jnp.int32),
            jax.ShapeDtypeStruct((t, LANES), F32),
            jax.ShapeDtypeStruct((t, LANES), jnp.int32),
            jax.ShapeDtypeStruct((SUBLANES, LANES), F32),
        ],
        scratch_shapes=[pltpu.VMEM((1, LANES), F32)],
        compiler_params=_cparams("arbitrary"),
        name="router",
    )(h, nw, wr3, tri)


def _slot_kernel(idx_ref, rank_ref, start_ref, o_ref):
    idx = idx_ref[...]
    lane = lax.broadcasted_iota(jnp.int32, idx.shape, 1)
    starts = start_ref[...]
    s1 = jnp.sum(jnp.where(lane == idx[:, 0:1], starts, 0.0), axis=-1, keepdims=True)
    s2 = jnp.sum(jnp.where(lane == idx[:, 1:2], starts, 0.0), axis=-1, keepdims=True)
    o_ref[...] = jnp.where(lane == 0, s1, jnp.where(lane == 1, s2, 0.0)).astype(jnp.int32) + rank_ref[...]


def _slots(idx, rank, start_row):
    t = idx.shape[0]
    row = pl.BlockSpec((ROW_BLOCK, LANES), lambda i: (i, 0))
    return pl.pallas_call(
        _slot_kernel,
        grid=(t // ROW_BLOCK,),
        in_specs=[row, row, pl.BlockSpec((1, LANES), lambda i: (0, 0))],
        out_specs=row,
        out_shape=jax.ShapeDtypeStruct((t, LANES), jnp.int32),
        compiler_params=_cparams("parallel"),
        name="moe_slots",
    )(idx, rank, start_row)


def _dispatch_kernel(pend_ref, padded_ref, nused_ref, dest_ref, x_ref, xs_ref, zero_ref, sem):
    rows = x_ref.shape[0]
    n_blk = xs_ref.shape[0] // MOE_ROWS

    @pl.when(pl.program_id(0) == 0)
    def _():
        zero_ref[...] = jnp.zeros_like(zero_ref)

        def fill_block(start):
            fill = pltpu.make_async_copy(zero_ref, xs_ref.at[pl.ds(start, MOE_ROWS), :], sem)
            fill.start()
            fill.wait()

        for e in range(N_EXPERTS):
            @pl.when(padded_ref[e] > 0)
            def _():
                fill_block(pl.multiple_of(pend_ref[e] - MOE_ROWS, MOE_ROWS))

            @pl.when(nused_ref[0] + e < n_blk)
            def _():
                fill_block(pl.multiple_of((nused_ref[0] + e) * MOE_ROWS, MOE_ROWS))

    def issue(group, carry):
        base = pl.multiple_of(group * SUBLANES, SUBLANES)
        for s in range(SUBLANES):
            for k in range(2):
                slot = dest_ref[0, 0, k * rows + base + s]
                pltpu.make_async_copy(x_ref.at[pl.ds(base + s, 1), :], xs_ref.at[pl.ds(slot, 1), :], sem).start()
        return carry

    lax.fori_loop(0, rows // SUBLANES, issue, 0)
    for k in range(2):
        pltpu.make_async_copy(x_ref, xs_ref.at[pl.ds(0, rows), :], sem).wait()


def _dispatch(pends, padded, n_used, dest_t, xn, cap):
    t, d = xn.shape
    grid_spec = pltpu.PrefetchScalarGridSpec(
        num_scalar_prefetch=3,
        grid=(t // ROW_BLOCK,),
        in_specs=[
            pl.BlockSpec((1, 1, 2 * ROW_BLOCK), lambda i, pe, pa, nu: (i, 0, 0), memory_space=pltpu.SMEM),
            pl.BlockSpec((ROW_BLOCK, d), lambda i, pe, pa, nu: (i, 0)),
        ],
        out_specs=pl.BlockSpec(memory_space=pl.ANY),
        scratch_shapes=[pltpu.VMEM((MOE_ROWS, d), F32), pltpu.SemaphoreType.DMA(())],
    )
    return pl.pallas_call(
        _dispatch_kernel,
        grid_spec=grid_spec,
        out_shape=jax.ShapeDtypeStruct((cap, d), F32),
        compiler_params=_cparams("arbitrary"),
        name="moe_dispatch",
    )(pends, padded, n_used, dest_t, xn)


def _expert_kernel(be_ref, nused_ref, x_ref, w1_ref, w2_ref, o_ref):
    live = pl.program_id(0) < nused_ref[0]

    @pl.when(live)
    def _():
        o_ref[...] = _swiglu(x_ref[...].astype(BF16), w1_ref, w2_ref, w2_ref.shape[0])

    @pl.when(jnp.logical_not(live))
    def _():
        o_ref[...] = jnp.zeros_like(o_ref)


def _experts(blk_expert, n_used, xs, w1, w2, layer):
    cap, d = xs.shape
    blk = lambda i, be, nu: (i, 0)
    expert = lambda w: pl.BlockSpec((None, None) + w.shape[2:], lambda i, be, nu: (layer, be[i], 0, 0),
                                    pipeline_mode=pl.Buffered(1))
    grid_spec = pltpu.PrefetchScalarGridSpec(
        num_scalar_prefetch=2,
        grid=(cap // MOE_ROWS,),
        in_specs=[pl.BlockSpec((MOE_ROWS, d), blk), expert(w1), expert(w2)],
        out_specs=pl.BlockSpec((MOE_ROWS, d), blk),
    )
    return pl.pallas_call(
        _expert_kernel,
        grid_spec=grid_spec,
        out_shape=jax.ShapeDtypeStruct((cap, d), F32),
        compiler_params=_cparams("arbitrary"),
        name="experts",
    )(blk_expert, n_used, xs, w1, w2)


def _combine_kernel(dest_ref, h_ref, gate_ref, nw_ref, yb_ref, o_ref, buf_ref, sem, *, final_norm):
    rows = h_ref.shape[0]

    def issue(group, carry):
        base = pl.multiple_of(group * SUBLANES, SUBLANES)
        for s in range(SUBLANES):
            for k in range(2):
                slot = dest_ref[0, 0, k * rows + base + s]
                pltpu.make_async_copy(
                    yb_ref.at[pl.ds(slot, 1), :], buf_ref.at[k, pl.ds(base + s, 1), :], sem).start()
        return carry

    lax.fori_loop(0, rows // SUBLANES, issue, 0)
    for k in range(2):
        pltpu.make_async_copy(yb_ref.at[pl.ds(0, rows), :], buf_ref.at[k], sem).wait()
    gate = gate_ref[...]
    out = h_ref[...] + gate[:, 0:1] * buf_ref[0] + gate[:, 1:2] * buf_ref[1]
    if final_norm:
        out = _rms(out, nw_ref[...])
    o_ref[...] = out


def _combine(dest_t, h, gate, nw, yb, final_norm):
    t, d = h.shape
    return pl.pallas_call(
        functools.partial(_combine_kernel, final_norm=final_norm),
        grid=(t // ROW_BLOCK,),
        in_specs=[
            pl.BlockSpec((1, 1, 2 * ROW_BLOCK), lambda i: (i, 0, 0), memory_space=pltpu.SMEM),
            pl.BlockSpec((ROW_BLOCK, d), lambda i: (i, 0)),
            pl.BlockSpec((ROW_BLOCK, LANES), lambda i: (i, 0)),
            pl.BlockSpec((1, d), lambda i: (0, 0)),
            pl.BlockSpec(memory_space=pl.ANY),
        ],
        out_specs=pl.BlockSpec((ROW_BLOCK, d), lambda i: (i, 0)),
        out_shape=jax.ShapeDtypeStruct((t, d), F32),
        scratch_shapes=[pltpu.VMEM((2, ROW_BLOCK, d), F32), pltpu.SemaphoreType.DMA(())],
        compiler_params=_cparams("arbitrary"),
        name="moe_combine",
    )(dest_t, h, gate, nw, yb)


def _final_norm_kernel(h_ref, nw_ref, o_ref):
    o_ref[...] = _rms(h_ref[...], nw_ref[...])


def _final_norm(h, nw):
    t, d = h.shape
    return pl.pallas_call(
        _final_norm_kernel,
        grid=(t // ROW_BLOCK,),
        in_specs=[pl.BlockSpec((ROW_BLOCK, d), lambda i: (i, 0)), pl.BlockSpec((1, d), lambda i: (0, 0))],
        out_specs=pl.BlockSpec((ROW_BLOCK, d), lambda i: (i, 0)),
        out_shape=jax.ShapeDtypeStruct((t, d), F32),
        compiler_params=_cparams("parallel"),
        name="final_norm",
    )(h, nw)


def _moe(h, nw, router_w, w1, w2, layer, tri, out_nw, final_norm):
    t, d = h.shape
    assert MOE_ROWS == ROW_BLOCK
    r1, r2, r3 = _split3(jnp.pad(router_w, ((0, 0), (0, LANES - N_EXPERTS))))
    xn, idx, gate, rank, cnt = _router(h, nw, jnp.stack([r1, r2, r3]), tri)
    counts = cnt[0, :N_EXPERTS].astype(jnp.int32)
    padded = (counts + MOE_ROWS - 1) // MOE_ROWS * MOE_ROWS
    pends = jnp.cumsum(padded)
    starts = pends - padded
    cap = (2 * t // MOE_ROWS + N_EXPERTS) * MOE_ROWS
    n_blk = cap // MOE_ROWS
    blk_start = jnp.arange(n_blk, dtype=jnp.int32) * MOE_ROWS
    blk_expert = jnp.minimum(jnp.sum(blk_start[:, None] >= pends[None, :], axis=1), N_EXPERTS - 1).astype(jnp.int32)
    n_used = (pends[-1:] // MOE_ROWS).astype(jnp.int32)
    start_row = jnp.zeros((1, LANES), F32).at[0, :N_EXPERTS].set(starts.astype(F32))
    dest = _slots(idx, rank, start_row)
    dest_t = dest[:, :2].reshape(t // ROW_BLOCK, ROW_BLOCK, 2).transpose(0, 2, 1).reshape(
        t // ROW_BLOCK, 1, 2 * ROW_BLOCK)
    xs = _dispatch(pends.astype(jnp.int32), padded.astype(jnp.int32), n_used, dest_t, xn, cap)
    yb = _experts(blk_expert, n_used, xs, w1, w2, layer)
    return _combine(dest_t, h, gate, out_nw, yb, final_norm)


def _tri(n):
    return jnp.tril(jnp.ones((n, n), F32)).astype(BF16)


def kernel(x, mem, mix_norm_w, w_in, fox_f_bias, fox_norm_w, hg_lb_raw, hg_norm_w, m2_conv_w, m2_conv_b, m2_dt_bias, m2_a_log, m2_d, m2_norm_w, w_out, xa_norm_w, xa_mem_norm_w, xa_w_q, xa_w_kv, xa_w_o, ffn_norm_w, ffn_w1, ffn_w2, moe_router, moe_w1, moe_w2, final_norm_w):
    batch, seq, d = x.shape
    depth = w_in.shape[0]
    t = batch * seq
    fox_w = N_FOX_HEADS * HEAD_DIM
    hg_w = N_HG_HEADS * HEAD_DIM
    m2_w = M2_HEADS * HEAD_DIM
    conv_ch = m2_w + 2 * M2_GROUPS * M2_STATE
    in_splits = (fox_w, fox_w, fox_w, N_FOX_HEADS, hg_w, hg_w, hg_w, hg_w, m2_w, conv_ch, M2_HEADS)
    offs = [0]
    for s in in_splits:
        offs.append(offs[-1] + s)

    lb_p = jax.nn.softmax(hg_lb_raw.astype(F32), axis=0)
    hg_lb = jnp.cumsum(lb_p, axis=0) - lb_p[0]

    tri_seq = _tri(SEQ_BLOCK)
    tri_chunk = _tri(SCAN_CHUNK)
    head_of_lane = jnp.arange(hg_w) // HEAD_DIM
    seg_ones = (head_of_lane[:, None] == head_of_lane[None, :]).astype(BF16)
    expand = (jnp.arange(LANES)[:, None] - SUBLANES == jnp.arange(m2_w)[None, :] // HEAD_DIM).astype(BF16)

    def pad_lanes(v, offset=0, width=LANES):
        return jnp.zeros((1, width), F32).at[0, offset:offset + v.shape[0]].set(v)

    small_w = jnp.zeros((depth, d, LANES), F32)
    small_w = small_w.at[:, :, 0:N_FOX_HEADS].set(w_in[:, :, offs[3]:offs[4]])
    small_w = small_w.at[:, :, SUBLANES:SUBLANES + M2_HEADS].set(w_in[:, :, offs[10]:offs[11]])
    in_weights = tuple(w.astype(BF16) for w in (
        w_in[:, :, offs[0]:offs[3]], small_w, w_in[:, :, offs[4]:offs[8]], w_in[:, :, offs[8]:offs[9]],
        w_in[:, :, offs[9]:offs[10]]))
    in_dtypes = (BF16, F32, F32, F32, F32)
    w_out_b, xa_wq_b, xa_wkv_b, xa_wo_b, ffn_w1_b, ffn_w2_b, moe_w1_b, moe_w2_b = (
        w.astype(BF16) for w in (w_out, xa_w_q, xa_w_kv, xa_w_o, ffn_w1, ffn_w2, moe_w1, moe_w2))

    h = x.reshape(t, d)
    mem2 = mem.reshape(batch * mem.shape[1], d)
    for layer in range(depth):
        qkv, small, hg, z, xbc = _in_proj(h, mix_norm_w[layer][None, :], in_weights, layer, in_dtypes)

        qa, ka, va = _fox_prep(qkv, small, pad_lanes(fox_f_bias[layer]), tri_seq, batch, seq)
        fox_nw = jnp.zeros((SUBLANES, LANES), F32).at[:fox_w // LANES].set(
            fox_norm_w[layer].reshape(fox_w // LANES, LANES))
        o_fox = _fox_attn(qa, ka, va, fox_nw, batch, seq)

        lb = hg_lb[layer]
        hg_params = jnp.zeros((SUBLANES, hg_w), F32)
        hg_params = hg_params.at[0].set(jnp.log(jnp.maximum(lb, LB_FLOOR)))
        hg_params = hg_params.at[1].set(jnp.log1p(-lb))
        hg_params = hg_params.at[2].set(1.0 - lb)
        hg_params = hg_params.at[3].set(hg_norm_w[layer])
        o_hg = _hgrn2(hg, hg_params, tri_chunk, seg_ones, batch, seq)

        conv_w = jnp.zeros((SUBLANES, conv_ch), F32).at[:M2_CONV].set(m2_conv_w[layer])
        head_rows = jnp.zeros((SUBLANES, m2_w), F32).at[0].set(jnp.repeat(m2_d[layer], HEAD_DIM))
        small_rows = jnp.concatenate(
            [pad_lanes(m2_dt_bias[layer], SUBLANES), pad_lanes(m2_a_log[layer], SUBLANES),
             jnp.zeros((SUBLANES - 2, LANES), F32)], axis=0)
        o_m2 = _ssd(z, xbc, small, conv_w, m2_conv_b[layer][None, :], head_rows, small_rows,
                    m2_norm_w[layer][None, :], tri_chunk, expand, batch, seq)

        kv = _norm_proj(mem2, xa_mem_norm_w[layer][None, :], xa_wkv_b, layer, BF16, mem.shape[1])
        h = _xattn(h, o_fox, o_hg, o_m2, w_out_b, xa_norm_w[layer][None, :], xa_wq_b, kv, xa_wo_b, layer,
                   batch, seq)

        nw = ffn_norm_w[layer][None, :]
        if layer % 2 == 0:
            h = _ffn(h, nw, ffn_w1_b, ffn_w2_b, layer // 2)
        else:
            last = layer == depth - 1
            h = _moe(h, nw, moe_router[layer // 2], moe_w1_b, moe_w2_b, layer // 2, tri_seq,
                     final_norm_w[None, :], last)
    if depth % 2:
        h = _final_norm(h, final_norm_w[None, :])
    return h.reshape(batch, seq, d)
```

```python
import functools
import math

import jax
import jax.numpy as jnp
from jax import lax
from jax.experimental import pallas as pl
from jax.experimental.pallas import tpu as pltpu

F32 = jnp.float32
BF16 = jnp.bfloat16

EPS = 1e-6
MASK_VALUE = -1e9
LB_FLOOR = 1e-30
HEAD_DIM = 64
N_FOX_HEADS = 4
N_HG_HEADS = 4
M2_HEADS = 8
M2_GROUPS = 2
M2_STATE = 128
M2_CONV = 4
XA_HEADS = 4
N_EXPERTS = 8

LANES = 128
SUBLANES = 8
VMEM_LIMIT_BYTES = 56 * 1024 * 1024

ROW_BLOCK = 512
SEQ_BLOCK = 512
SCAN_CHUNK = 128
SCAN_CHUNKS_PER_STEP = 2
SUB_BLOCK = 16
HG_SAFE_DECAY = 60.0
MOE_ROWS = 512


def _cparams(*sem):
    return pltpu.CompilerParams(dimension_semantics=sem, vmem_limit_bytes=VMEM_LIMIT_BYTES)


def _split3(x):
    x1 = x.astype(BF16)
    r1 = x - x1.astype(F32)
    x2 = r1.astype(BF16)
    x3 = (r1 - x2.astype(F32)).astype(BF16)
    return x1, x2, x3


def _dot(a, b):
    return jnp.dot(a, b, preferred_element_type=F32)


def _dot_nt(a, b):
    return lax.dot_general(a, b, (((1,), (1,)), ((), ())), preferred_element_type=F32)


def _exact_left(mat01, x):
    x1, x2, x3 = _split3(x)
    return _dot(mat01, x1) + _dot(mat01, x2) + _dot(mat01, x3)


def _exact_right(x, mat01):
    x1, x2, x3 = _split3(x)
    return _dot(x1, mat01) + _dot(x2, mat01) + _dot(x3, mat01)


def _rms(x, w):
    return x * lax.rsqrt(jnp.mean(x * x, axis=-1, keepdims=True) + EPS) * w


def _sigmoid(x):
    return 1.0 / (1.0 + jnp.exp(-x))


def _silu(x):
    return x * _sigmoid(x)


def _log1p_exp(x):
    return jnp.log(1.0 + jnp.exp(x))


def _log_sigmoid(x):
    return jnp.minimum(x, 0.0) - _log1p_exp(-jnp.abs(x))


def _softplus(x):
    return jnp.maximum(x, 0.0) + _log1p_exp(-jnp.abs(x))


def _masked_exp(x, mask):
    return jnp.where(mask, jnp.exp(jnp.where(mask, x, 0.0)), 0.0)


def _resident(a):
    zeros = (0,) * a.ndim
    return pl.BlockSpec(a.shape, lambda *_: zeros, pipeline_mode=pl.Buffered(1))


def _layer_resident(stacked, layer):
    index = (layer,) + (0,) * (stacked.ndim - 1)
    return pl.BlockSpec((None,) + stacked.shape[1:], lambda *_: index, pipeline_mode=pl.Buffered(1))


def _in_proj_kernel(x_ref, nw_ref, *refs):
    n = len(refs) // 2
    xn = _rms(x_ref[...], nw_ref[...]).astype(BF16)
    for w_ref, o_ref in zip(refs[:n], refs[n:]):
        o_ref[...] = _dot(xn, w_ref[...]).astype(o_ref.dtype)


def _in_proj(h, nw, weights, layer, out_dtypes):
    t, d = h.shape
    widths = [w.shape[-1] for w in weights]
    return pl.pallas_call(
        _in_proj_kernel,
        grid=(t // ROW_BLOCK,),
        in_specs=[pl.BlockSpec((ROW_BLOCK, d), lambda i: (i, 0)), _resident(nw)]
        + [_layer_resident(w, layer) for w in weights],
        out_specs=[pl.BlockSpec((ROW_BLOCK, w), lambda i: (i, 0)) for w in widths],
        out_shape=[jax.ShapeDtypeStruct((t, w), dt) for w, dt in zip(widths, out_dtypes)],
        compiler_params=_cparams("parallel"),
        name="in_proj",
    )(h, nw, *weights)


def _norm_proj_kernel(x_ref, nw_ref, w_ref, o_ref):
    xn = _rms(x_ref[...], nw_ref[...]).astype(BF16)
    o_ref[...] = _dot(xn, w_ref[...]).astype(o_ref.dtype)


def _norm_proj(x, nw, w, layer, out_dtype, rows):
    t, d = x.shape
    n = w.shape[-1]
    return pl.pallas_call(
        _norm_proj_kernel,
        grid=(t // rows,),
        in_specs=[pl.BlockSpec((rows, d), lambda i: (i, 0)), _resident(nw), _layer_resident(w, layer)],
        out_specs=pl.BlockSpec((rows, n), lambda i: (i, 0)),
        out_shape=jax.ShapeDtypeStruct((t, n), out_dtype),
        compiler_params=_cparams("parallel"),
        name="norm_proj",
    )(x, nw, w)


def _fox_prep_kernel(qkv_ref, small_ref, bias_ref, tri_ref, q_ref, k_ref, v_ref, carry_ref):
    @pl.when(pl.program_id(1) == 0)
    def _():
        carry_ref[...] = jnp.zeros_like(carry_ref)

    rows = qkv_ref.shape[0]
    log_f = _log_sigmoid(small_ref[...] + bias_ref[...])
    c = _exact_left(tri_ref[...], log_f) + carry_ref[...]
    carry_ref[...] = c[rows - 1:rows, :]

    lane = lax.broadcasted_iota(jnp.int32, (rows, LANES), 1)
    width = N_FOX_HEADS * HEAD_DIM
    scale = HEAD_DIM ** -0.5
    for hd in range(N_FOX_HEADS):
        tile = (hd * HEAD_DIM) // LANES
        ch = c[:, hd:hd + 1]
        c1 = ch.astype(BF16).astype(F32)
        r1 = ch - c1
        c2 = r1.astype(BF16).astype(F32)
        c3 = r1 - c2

        def head_tile(base):
            x = qkv_ref[:, base + tile * LANES: base + (tile + 1) * LANES].astype(F32)
            if (hd * HEAD_DIM) % LANES:
                x = pltpu.roll(x, LANES - (hd * HEAD_DIM) % LANES, axis=1)
            return x

        def augment(x, first, second):
            out = jnp.where(lane < HEAD_DIM, x, 0.0)
            for j, val in enumerate(first + second):
                out = jnp.where(lane == HEAD_DIM + j, val, out)
            return out.astype(BF16)

        ones = (1.0, 1.0, 1.0)
        q_ref[hd] = augment(head_tile(0) * scale, ones, (c1, c2, c3))
        k_ref[hd] = augment(head_tile(width), (-c1, -c2, -c3), ones)
        v_aug = jnp.where(lane < HEAD_DIM, head_tile(2 * width), jnp.where(lane == HEAD_DIM, 1.0, 0.0))
        v_ref[hd] = v_aug.T.astype(BF16)


def _fox_prep(qkv, small, bias_row, tri, batch, seq):
    t = qkv.shape[0]
    nb = seq // SEQ_BLOCK
    row = lambda b, c: (b * nb + c, 0)
    head_spec = pl.BlockSpec((N_FOX_HEADS, SEQ_BLOCK, LANES), lambda b, c: (0, b * nb + c, 0))
    head_shape = jax.ShapeDtypeStruct((N_FOX_HEADS, t, LANES), BF16)
    vt_spec = pl.BlockSpec((N_FOX_HEADS, LANES, SEQ_BLOCK), lambda b, c: (0, 0, b * nb + c))
    vt_shape = jax.ShapeDtypeStruct((N_FOX_HEADS, LANES, t), BF16)
    return pl.pallas_call(
        _fox_prep_kernel,
        grid=(batch, nb),
        in_specs=[
            pl.BlockSpec((SEQ_BLOCK, qkv.shape[1]), row),
            pl.BlockSpec((SEQ_BLOCK, LANES), row),
            pl.BlockSpec((1, LANES), lambda b, c: (0, 0)),
            pl.BlockSpec((SEQ_BLOCK, SEQ_BLOCK), lambda b, c: (0, 0)),
        ],
        out_specs=[head_spec, head_spec, vt_spec],
        out_shape=[head_shape, head_shape, vt_shape],
        scratch_shapes=[pltpu.VMEM((1, LANES), F32)],
        compiler_params=_cparams("parallel", "arbitrary"),
        name="fox_prep",
    )(qkv, small, bias_row, tri)


def _fox_attn_kernel(q_ref, k_ref, vt_ref, nw_ref, o_ref, m_ref, acc_ref, sa_ref, sb_ref):
    i = pl.program_id(1)
    tq = q_ref.shape[1]
    kv_idx = lax.broadcasted_iota(jnp.int32, (tq, tq), 0)
    q_idx = lax.broadcasted_iota(jnp.int32, (tq, tq), 1)
    heads = range(N_FOX_HEADS)
    m_ref[...] = jnp.full(m_ref.shape, -jnp.inf, F32)
    acc_ref[...] = jnp.zeros_like(acc_ref)

    def logits_into(dst_ref, j):
        start = pl.multiple_of(j * tq, tq)
        for hd in heads:
            dst_ref[hd] = _dot_nt(k_ref[hd, pl.ds(start, tq), :], q_ref[hd])

    def fold(src_ref, j, diagonal=False):
        start = pl.multiple_of(j * tq, tq)
        for hd in heads:
            s = src_ref[hd]
            if diagonal:
                s = jnp.where(kv_idx <= q_idx, s, MASK_VALUE)
            m_old = m_ref[hd]
            m_new = jnp.maximum(m_old, jnp.max(s, axis=0, keepdims=True))
            p = jnp.exp(s - m_new).astype(BF16)
            acc_ref[hd] = jnp.exp(m_old - m_new) * acc_ref[hd] + _dot(vt_ref[hd, :, pl.ds(start, tq)], p)
            m_ref[hd] = m_new

    logits_into(sa_ref, 0)

    def two_blocks(jj, carry):
        logits_into(sb_ref, 2 * jj + 1)
        fold(sa_ref, 2 * jj)
        logits_into(sa_ref, 2 * jj + 2)
        fold(sb_ref, 2 * jj + 1)
        return carry

    lax.fori_loop(0, i // 2, two_blocks, 0)

    @pl.when(i % 2 == 1)
    def _():
        logits_into(sb_ref, i)
        fold(sa_ref, i - 1)
        fold(sb_ref, i, diagonal=True)

    @pl.when(i % 2 == 0)
    def _():
        fold(sa_ref, i, diagonal=True)

    normed = []
    for hd in heads:
        acc = acc_ref[hd]
        o = acc[:HEAD_DIM] / acc[HEAD_DIM:HEAD_DIM + 1]
        ms = jnp.mean(o * o, axis=0, keepdims=True)
        normed.append(o * lax.rsqrt(ms + EPS))
    tiles = []
    for pair in range(N_FOX_HEADS // 2):
        both = jnp.concatenate(normed[2 * pair:2 * pair + 2], axis=0)
        tiles.append(both.T * nw_ref[pair:pair + 1, :])
    o_ref[...] = jnp.concatenate(tiles, axis=1).astype(o_ref.dtype)


def _fox_attn(qa, ka, vt, nw_pairs, batch, seq):
    t = qa.shape[1]
    nq = seq // SEQ_BLOCK
    return pl.pallas_call(
        _fox_attn_kernel,
        grid=(batch, nq),
        in_specs=[
            pl.BlockSpec((N_FOX_HEADS, SEQ_BLOCK, LANES), lambda b, i: (0, b * nq + i, 0)),
            pl.BlockSpec((N_FOX_HEADS, seq, LANES), lambda b, i: (0, b, 0)),
            pl.BlockSpec((N_FOX_HEADS, LANES, seq), lambda b, i: (0, 0, b)),
            pl.BlockSpec((SUBLANES, LANES), lambda b, i: (0, 0)),
        ],
        out_specs=pl.BlockSpec((SEQ_BLOCK, N_FOX_HEADS * HEAD_DIM), lambda b, i: (b * nq + i, 0)),
        out_shape=jax.ShapeDtypeStruct((t, N_FOX_HEADS * HEAD_DIM), BF16),
        scratch_shapes=[
            pltpu.VMEM((N_FOX_HEADS, 1, SEQ_BLOCK), F32),
            pltpu.VMEM((N_FOX_HEADS, LANES, SEQ_BLOCK), F32),
            pltpu.VMEM((N_FOX_HEADS, SEQ_BLOCK, SEQ_BLOCK), F32),
            pltpu.VMEM((N_FOX_HEADS, SEQ_BLOCK, SEQ_BLOCK), F32),
        ],
        compiler_params=_cparams("parallel", "arbitrary"),
        name="fox_attn",
    )(qa, ka, vt, nw_pairs)


def _chunk_views(refs, c):
    return [r.at[pl.ds(c * SCAN_CHUNK, SCAN_CHUNK)] for r in refs]


def _hgrn2_kernel(hq_ref, hf_ref, hi_ref, hg_ref, par_ref, tri_ref, seg_ref, o_ref, st_ref):
    @pl.when(pl.program_id(1) == 0)
    def _():
        st_ref[...] = jnp.zeros_like(st_ref)

    for c in range(SCAN_CHUNKS_PER_STEP):
        hq, hf, hi, hg, o = _chunk_views((hq_ref, hf_ref, hi_ref, hg_ref, o_ref), c)
        _hgrn2_chunk(hq, hf, hi, hg, par_ref, tri_ref, seg_ref, o, st_ref)


def _hgrn2_chunk(hq_ref, hf_ref, hi_ref, hg_ref, par_ref, tri_ref, seg_ref, o_ref, st_ref):
    rows, width = hq_ref.shape
    log_lb, log1m_lb, one_m_lb, nw = (par_ref[r:r + 1, :] for r in range(4))
    f_raw = hf_ref[...]
    q = _silu(hq_ref[...]) * (HEAD_DIM ** -0.5)
    b = log1m_lb + _log_sigmoid(f_raw)
    g = jnp.maximum(log_lb, b) + _log1p_exp(-jnp.abs(log_lb - b))
    k = one_m_lb * _sigmoid(-f_raw)
    v = hi_ref[...]
    cum = _exact_left(tri_ref[...], g)

    lane = lax.broadcasted_iota(jnp.int32, (1, width), 1)
    head_masks = [(lane >= hd * HEAD_DIM) & (lane < (hd + 1) * HEAD_DIM) for hd in range(N_HG_HEADS)]
    seg = seg_ref[...]
    v_b = v.astype(BF16)

    st = st_ref[...]
    o_state = _dot_nt((q * jnp.exp(cum)).astype(BF16), st.astype(BF16))

    n_sub = rows // SUB_BLOCK
    refs = [cum[i * SUB_BLOCK - 1:i * SUB_BLOCK] if i else jnp.zeros((1, width), F32) for i in range(n_sub)]
    local = [cum[i * SUB_BLOCK:(i + 1) * SUB_BLOCK] - refs[i] for i in range(n_sub)]

    def stack_heads(x):
        return jnp.concatenate([jnp.where(mk, x, 0.0) for mk in head_masks], axis=0).astype(BF16)

    def unstack_heads(base, p4):
        for hd, mk in enumerate(head_masks):
            base = base + jnp.where(mk, p4[hd * SUB_BLOCK:(hd + 1) * SUB_BLOCK], 0.0)
        return base

    def factored():
        scores = []
        for i in range(n_sub):
            hi = (i + 1) * SUB_BLOCK
            qs = q[hi - SUB_BLOCK:hi] * jnp.exp(local[i])
            ks = (k[:hi] * jnp.exp(refs[i] - cum[:hi])).astype(BF16)
            scores.append(_dot_nt(stack_heads(qs), ks))
        blocks = []
        for i in range(n_sub):
            hi = (i + 1) * SUB_BLOCK
            t_in = lax.broadcasted_iota(jnp.int32, (N_HG_HEADS * SUB_BLOCK, hi), 0) & (SUB_BLOCK - 1)
            s_in = lax.broadcasted_iota(jnp.int32, (N_HG_HEADS * SUB_BLOCK, hi), 1)
            sc = jnp.where(s_in <= t_in + (hi - SUB_BLOCK), scores[i], 0.0).astype(BF16)
            blocks.append(unstack_heads(o_state[hi - SUB_BLOCK:hi], _dot(sc, v_b[:hi])))
        return jnp.concatenate(blocks, axis=0)

    def pairwise():
        t_idx = lax.broadcasted_iota(jnp.int32, (SUB_BLOCK, 1), 0)
        blocks = []
        for i in range(n_sub):
            r0 = i * SUB_BLOCK
            q_i = q[r0:r0 + SUB_BLOCK]
            cum_i = cum[r0:r0 + SUB_BLOCK]
            o_i = o_state[r0:r0 + SUB_BLOCK]
            if i > 0:
                ks = (k[:r0] * jnp.exp(refs[i] - cum[:r0])).astype(BF16)
                sc = _dot_nt(stack_heads(q_i * jnp.exp(local[i])), ks)
                o_i = unstack_heads(o_i, _dot(sc.astype(BF16), v_b[:r0]))
            terms = []
            for s in range(SUB_BLOCK):
                keep = t_idx >= s
                e = _masked_exp(cum_i - cum[r0 + s:r0 + s + 1], keep)
                terms.append((q_i * e * k[r0 + s:r0 + s + 1]).astype(BF16))
            sums = _dot(jnp.concatenate(terms, axis=0), seg)
            for s in range(SUB_BLOCK):
                o_i = o_i + sums[s * SUB_BLOCK:(s + 1) * SUB_BLOCK] * v[r0 + s:r0 + s + 1]
            blocks.append(o_i)
        return jnp.concatenate(blocks, axis=0)

    worst = functools.reduce(jnp.minimum, local)
    o = lax.cond(jnp.min(worst) >= -HG_SAFE_DECAY, factored, pairwise)

    last = cum[rows - 1:rows]
    kd = (k * jnp.exp(last - cum)).astype(BF16)
    upd = _dot(v.T.astype(BF16), kd)
    st_ref[...] = st * jnp.exp(last) + jnp.where(seg > 0, upd, 0.0)

    sq = o * o
    s1 = sq.astype(BF16)
    s2 = (sq - s1.astype(F32)).astype(BF16)
    ms = (_dot(s1, seg) + _dot(s2, seg)) * (1.0 / HEAD_DIM)
    o_ref[...] = (o * lax.rsqrt(ms + EPS) * nw * _silu(hg_ref[...])).astype(o_ref.dtype)


def _hgrn2(hg, params, tri, seg, batch, seq):
    t = hg.shape[0]
    width = N_HG_HEADS * HEAD_DIM
    step_rows = SCAN_CHUNK * SCAN_CHUNKS_PER_STEP
    nc = seq // step_rows
    col = lambda c: pl.BlockSpec((step_rows, width), lambda b, j, c=c: (b * nc + j, c))
    return pl.pallas_call(
        _hgrn2_kernel,
        grid=(batch, nc),
        in_specs=[
            col(0), col(1), col(2), col(3),
            pl.BlockSpec(params.shape, lambda b, j: (0, 0)),
            pl.BlockSpec(tri.shape, lambda b, j: (0, 0)),
            pl.BlockSpec(seg.shape, lambda b, j: (0, 0)),
        ],
        out_specs=pl.BlockSpec((step_rows, width), lambda b, j: (b * nc + j, 0)),
        out_shape=jax.ShapeDtypeStruct((t, width), BF16),
        scratch_shapes=[pltpu.VMEM((width, width), F32)],
        compiler_params=_cparams("parallel", "arbitrary"),
        name="hgrn2",
    )(hg, hg, hg, hg, params, tri, seg)


def _ssd_kernel(z_ref, xbc_ref, small_ref, cw_ref, cb_ref, hp_ref, sp_ref, nw_ref, tri_ref, exp_ref,
                o_ref, buf_ref, st_ref):
    @pl.when(pl.program_id(1) == 0)
    def _():
        buf_ref[0:SUBLANES, :] = jnp.zeros((SUBLANES, buf_ref.shape[1]), F32)
        st_ref[...] = jnp.zeros_like(st_ref)

    for c in range(SCAN_CHUNKS_PER_STEP):
        z, xbc, small, o = _chunk_views((z_ref, xbc_ref, small_ref, o_ref), c)
        _ssd_chunk(z, xbc, small, cw_ref, cb_ref, hp_ref, sp_ref, nw_ref, tri_ref, exp_ref, o, buf_ref, st_ref)


def _ssd_chunk(z_ref, xbc_ref, small_ref, cw_ref, cb_ref, hp_ref, sp_ref, nw_ref, tri_ref, exp_ref,
               o_ref, buf_ref, st_ref):
    rows = z_ref.shape[0]
    width = z_ref.shape[1]
    halo = SUBLANES

    buf_ref[halo:halo + rows, :] = xbc_ref[...]
    conv = cb_ref[...]
    for i in range(M2_CONV):
        off = halo - (M2_CONV - 1) + i
        conv = conv + cw_ref[i:i + 1, :] * buf_ref[off:off + rows, :]
    tail = buf_ref[rows:rows + halo, :]
    buf_ref[0:halo, :] = tail
    xc = _silu(conv)
    xs = xc[:, :width]
    gn = M2_GROUPS * M2_STATE
    bm = xc[:, width:width + gn]
    cm = xc[:, width + gn:width + 2 * gn]

    d_x = hp_ref[0:1, :]
    dt_bias_c, a_log_c = (sp_ref[r:r + 1, :] for r in range(2))
    small = small_ref[...]
    tri = tri_ref[...]
    dt_c = _softplus(small + dt_bias_c)
    a_cum_c = _exact_left(tri, dt_c * -jnp.exp(a_log_c))
    a_cum_t = a_cum_c.T
    expand = exp_ref[...]
    dt_x = _exact_right(dt_c, expand)
    a_cum_x = _exact_right(a_cum_c, expand)

    row = lax.broadcasted_iota(jnp.int32, (rows, rows), 0)
    col = lax.broadcasted_iota(jnp.int32, (rows, rows), 1)
    causal = col <= row
    lane = lax.broadcasted_iota(jnp.int32, (1, LANES), 1)
    low = lane < HEAD_DIM

    xdt = xs * dt_x
    a_last = a_cum_x[rows - 1:rows]
    xd = (xdt * jnp.exp(a_last - a_cum_x)).astype(BF16)
    decay_out = jnp.exp(a_cum_x)
    st = st_ref[...]
    heads_per_group = M2_HEADS // M2_GROUPS
    gw = heads_per_group * HEAD_DIM
    cbs, y_offs, new_states = [], [], []
    for g in range(M2_GROUPS):
        b_g = bm[:, g * M2_STATE:(g + 1) * M2_STATE]
        c_g = cm[:, g * M2_STATE:(g + 1) * M2_STATE].astype(BF16)
        cbs.append(jnp.where(causal, _dot_nt(c_g, b_g.astype(BF16)), 0.0))
        y_offs.append(_dot(c_g, st[:, g * gw:(g + 1) * gw].astype(BF16)) * decay_out[:, g * gw:(g + 1) * gw])
        new_states.append(_dot(b_g.T.astype(BF16), xd[:, g * gw:(g + 1) * gw]))
    st_ref[...] = st * jnp.exp(a_last) + jnp.concatenate(new_states, axis=1)

    y_tiles = []
    for tile in range(M2_HEADS // 2):
        g = (2 * tile) // heads_per_group
        pair = tile - g * (heads_per_group // 2)
        x_tile = xdt[:, tile * LANES:(tile + 1) * LANES]
        y_pair = y_offs[g][:, pair * LANES:(pair + 1) * LANES]
        for half in range(2):
            hd = 2 * tile + half
            seg = a_cum_c[:, SUBLANES + hd:SUBLANES + hd + 1] - a_cum_t[SUBLANES + hd:SUBLANES + hd + 1, :]
            m_h = (cbs[g] * jnp.exp(jnp.minimum(seg, 0.0))).astype(BF16)
            x_h = jnp.where(low if half == 0 else ~low, x_tile, 0.0).astype(BF16)
            y_pair = y_pair + _dot(m_h, x_h)
        y_tiles.append(y_pair)

    y = jnp.concatenate(y_tiles, axis=1) + d_x * xs
    y = y * _silu(z_ref[...])
    nw = nw_ref[...]
    outs = []
    for g in range(M2_GROUPS):
        outs.append(_rms(y[:, g * gw:(g + 1) * gw], nw[:, g * gw:(g + 1) * gw]))
    o_ref[...] = jnp.concatenate(outs, axis=1).astype(o_ref.dtype)


def _ssd(z, xbc, small, conv_w, conv_b, head_rows, small_rows, norm_w, tri, expand, batch, seq):
    t, width = z.shape
    ch = xbc.shape[1]
    step_rows = SCAN_CHUNK * SCAN_CHUNKS_PER_STEP
    nc = seq // step_rows
    row = lambda b, j: (b * nc + j, 0)
    const = lambda a: pl.BlockSpec(a.shape, lambda b, j: (0, 0))
    return pl.pallas_call(
        _ssd_kernel,
        grid=(batch, nc),
        in_specs=[
            pl.BlockSpec((step_rows, width), row),
            pl.BlockSpec((step_rows, ch), row),
            pl.BlockSpec((step_rows, LANES), row),
            const(conv_w), const(conv_b), const(head_rows), const(small_rows), const(norm_w),
            const(tri), const(expand),
        ],
        out_specs=pl.BlockSpec((step_rows, width), row),
        out_shape=jax.ShapeDtypeStruct((t, width), BF16),
        scratch_shapes=[
            pltpu.VMEM((SCAN_CHUNK + SUBLANES, ch), F32),
            pltpu.VMEM((M2_STATE, width), F32),
        ],
        compiler_params=_cparams("parallel", "arbitrary"),
        name="ssd",
    )(z, xbc, small, conv_w, conv_b, head_rows, small_rows, norm_w, tri, expand)


def _xattn_kernel(h_ref, a_ref, b_ref, c_ref, wout_ref, nw_ref, wq_ref, kv_ref, wo_ref, o_ref):
    wa = a_ref.shape[1]
    wb = b_ref.shape[1]
    h = h_ref[...] + _dot(a_ref[...], wout_ref[0:wa, :])
    h = h + _dot(b_ref[...], wout_ref[wa:wa + wb, :])
    h = h + _dot(c_ref[...], wout_ref[wa + wb:, :])
    d = h.shape[1]
    hd = d // XA_HEADS
    q = (_dot(_rms(h, nw_ref[...]).astype(BF16), wq_ref[...]) * (hd ** -0.5)).astype(BF16)
    logits = [_dot_nt(q[:, a * hd:(a + 1) * hd], kv_ref[:, a * hd:(a + 1) * hd]) for a in range(XA_HEADS)]
    outs = []
    for a, s in enumerate(logits):
        p = jnp.exp(s - jnp.max(s, axis=-1, keepdims=True))
        p = p / jnp.sum(p, axis=-1, keepdims=True)
        outs.append(_dot(p.astype(BF16), kv_ref[:, d + a * hd:d + (a + 1) * hd]).astype(BF16))
    o_ref[...] = h + _dot(jnp.concatenate(outs, axis=1), wo_ref[...])


def _xattn(h, o_fox, o_hg, o_m2, w_out, nw, wq, kv, wo, layer, batch, seq):
    t, d = h.shape
    n_mem = kv.shape[0] // batch
    nb = seq // ROW_BLOCK
    row = lambda x: pl.BlockSpec((ROW_BLOCK, x.shape[1]), lambda b, i: (b * nb + i, 0))
    return pl.pallas_call(
        _xattn_kernel,
        grid=(batch, nb),
        in_specs=[
            row(h), row(o_fox), row(o_hg), row(o_m2),
            _layer_resident(w_out, layer), _resident(nw), _layer_resident(wq, layer),
            pl.BlockSpec((n_mem, kv.shape[1]), lambda b, i: (b, 0)),
            _layer_resident(wo, layer),
        ],
        out_specs=row(h),
        out_shape=jax.ShapeDtypeStruct((t, d), F32),
        compiler_params=_cparams("parallel", "parallel"),
        name="xattn",
    )(h, o_fox, o_hg, o_m2, w_out, nw, wq, kv, wo)


def _swiglu(x, w1_ref, w2_ref, d_ff):
    a = (_silu(_dot(x, w1_ref[:, :d_ff])) * _dot(x, w1_ref[:, d_ff:])).astype(BF16)
    return _dot(a, w2_ref[...])


def _ffn_kernel(h_ref, nw_ref, w1_ref, w2_ref, o_ref):
    h = h_ref[...]
    o_ref[...] = h + _swiglu(_rms(h, nw_ref[...]).astype(BF16), w1_ref, w2_ref, w2_ref.shape[0])


def _ffn(h, nw, w1, w2, layer):
    t, d = h.shape
    return pl.pallas_call(
        _ffn_kernel,
        grid=(t // ROW_BLOCK,),
        in_specs=[pl.BlockSpec((ROW_BLOCK, d), lambda i: (i, 0)), _resident(nw),
                  _layer_resident(w1, layer), _layer_resident(w2, layer)],
        out_specs=pl.BlockSpec((ROW_BLOCK, d), lambda i: (i, 0)),
        out_shape=jax.ShapeDtypeStruct((t, d), F32),
        compiler_params=_cparams("parallel"),
        name="ffn",
    )(h, nw, w1, w2)


def _router_kernel(h_ref, nw_ref, wr_ref, tri_ref, xn_ref, idx_ref, gate_ref, rank_ref, cnt_ref, carry_ref):
    @pl.when(pl.program_id(0) == 0)
    def _():
        carry_ref[...] = jnp.zeros_like(carry_ref)

    xn = _rms(h_ref[...], nw_ref[...])
    xn_ref[...] = xn
    x1 = xn.astype(BF16)
    x2 = (xn - x1.astype(F32)).astype(BF16)
    w12 = wr_ref[:, :2 * LANES]
    big = _dot(x1, w12) + _dot(x2, w12)
    logits = big[:, :LANES] + big[:, LANES:] + _dot(x1, wr_ref[:, 2 * LANES:])
    lane = lax.broadcasted_iota(jnp.int32, logits.shape, 1)
    logits = jnp.where(lane < N_EXPERTS, logits, -jnp.inf)
    m1 = jnp.max(logits, axis=-1, keepdims=True)
    i1 = jnp.min(jnp.where(logits == m1, lane, LANES), axis=-1, keepdims=True)
    rest = jnp.where(lane == i1, -jnp.inf, logits)
    m2 = jnp.max(rest, axis=-1, keepdims=True)
    i2 = jnp.min(jnp.where(rest == m2, lane, LANES), axis=-1, keepdims=True)
    e2 = jnp.exp(m2 - m1)
    denom = 1.0 + e2
    idx_ref[...] = jnp.where(lane == 0, i1, jnp.where(lane == 1, i2, 0))
    gate_ref[...] = jnp.where(lane == 0, 1.0 / denom, jnp.where(lane == 1, e2 / denom, 0.0))

    hit1 = lane == i1
    hit2 = lane == i2
    member = jnp.where(hit1, 1.0, jnp.where(hit2, 1.0, 0.0))
    incl = _dot(tri_ref[...], member.astype(BF16)) + carry_ref[...]
    excl = incl - member
    r1 = jnp.sum(jnp.where(hit1, excl, 0.0), axis=-1, keepdims=True)
    r2 = jnp.sum(jnp.where(hit2, excl, 0.0), axis=-1, keepdims=True)
    rows = member.shape[0]
    carry_ref[...] = incl[rows - 1:rows, :]
    rank_ref[...] = jnp.where(lane == 0, r1, jnp.where(lane == 1, r2, 0.0)).astype(jnp.int32)
    cnt_ref[...] = jnp.broadcast_to(incl[rows - 1:rows, :], cnt_ref.shape)


def _router(h, nw, wr, tri):
    t, d = h.shape
    row = lambda w: pl.BlockSpec((ROW_BLOCK, w), lambda i: (i, 0))
    return pl.pallas_call(
        _router_kernel,
        grid=(t // ROW_BLOCK,),
        in_specs=[row(d), _resident(nw), _resident(wr), _resident(tri)],
        out_specs=[row(d), row(LANES), row(LANES), row(LANES), pl.BlockSpec((SUBLANES, LANES), lambda i: (0, 0))],
        out_shape=[
            jax.ShapeDtypeStruct((t, d), F32),
            jax.ShapeDtypeStruct((t, LANES), jnp.int32),
            jax.ShapeDtypeStruct((t, LANES), F32),
            jax.ShapeDtypeStruct((t, LANES), jnp.int32),
            jax.ShapeDtypeStruct((SUBLANES, LANES), F32),
        ],
        scratch_shapes=[pltpu.VMEM((1, LANES), F32)],
        compiler_params=_cparams("arbitrary"),
        name="router",
    )(h, nw, wr, tri)


def _slot_kernel(idx_ref, rank_ref, start_ref, o_ref):
    idx = idx_ref[...]
    lane = lax.broadcasted_iota(jnp.int32, idx.shape, 1)
    starts = start_ref[...]
    s1 = jnp.sum(jnp.where(lane == idx[:, 0:1], starts, 0.0), axis=-1, keepdims=True)
    s2 = jnp.sum(jnp.where(lane == idx[:, 1:2], starts, 0.0), axis=-1, keepdims=True)
    o_ref[...] = jnp.where(lane == 0, s1, jnp.where(lane == 1, s2, 0.0)).astype(jnp.int32) + rank_ref[...]


def _slots(idx, rank, start_row):
    t = idx.shape[0]
    row = pl.BlockSpec((ROW_BLOCK, LANES), lambda i: (i, 0))
    return pl.pallas_call(
        _slot_kernel,
        grid=(t // ROW_BLOCK,),
        in_specs=[row, row, pl.BlockSpec((1, LANES), lambda i: (0, 0))],
        out_specs=row,
        out_shape=jax.ShapeDtypeStruct((t, LANES), jnp.int32),
        compiler_params=_cparams("parallel"),
        name="moe_slots",
    )(idx, rank, start_row)


def _dispatch_kernel(pend_ref, padded_ref, nused_ref, dest_ref, x_ref, xs_ref, zero_ref, sem):
    rows = x_ref.shape[0]
    n_blk = xs_ref.shape[0] // MOE_ROWS

    @pl.when(pl.program_id(0) == 0)
    def _():
        zero_ref[...] = jnp.zeros_like(zero_ref)

        def fill_block(start):
            fill = pltpu.make_async_copy(zero_ref, xs_ref.at[pl.ds(start, MOE_ROWS), :], sem)
            fill.start()
            fill.wait()

        for e in range(N_EXPERTS):
            @pl.when(padded_ref[e] > 0)
            def _():
                fill_block(pl.multiple_of(pend_ref[e] - MOE_ROWS, MOE_ROWS))

            @pl.when(nused_ref[0] + e < n_blk)
            def _():
                fill_block(pl.multiple_of((nused_ref[0] + e) * MOE_ROWS, MOE_ROWS))

    def issue(group, carry):
        base = pl.multiple_of(group * SUBLANES, SUBLANES)
        for s in range(SUBLANES):
            for k in range(2):
                slot = dest_ref[0, 0, k * rows + base + s]
                pltpu.make_async_copy(x_ref.at[pl.ds(base + s, 1), :], xs_ref.at[pl.ds(slot, 1), :], sem).start()
        return carry

    lax.fori_loop(0, rows // SUBLANES, issue, 0)
    for k in range(2):
        pltpu.make_async_copy(x_ref, xs_ref.at[pl.ds(0, rows), :], sem).wait()


def _dispatch(pends, padded, n_used, dest_t, xn, cap):
    t, d = xn.shape
    grid_spec = pltpu.PrefetchScalarGridSpec(
        num_scalar_prefetch=3,
        grid=(t // ROW_BLOCK,),
        in_specs=[
            pl.BlockSpec((1, 1, 2 * ROW_BLOCK), lambda i, pe, pa, nu: (i, 0, 0), memory_space=pltpu.SMEM),
            pl.BlockSpec((ROW_BLOCK, d), lambda i, pe, pa, nu: (i, 0)),
        ],
        out_specs=pl.BlockSpec(memory_space=pl.ANY),
        scratch_shapes=[pltpu.VMEM((MOE_ROWS, d), F32), pltpu.SemaphoreType.DMA(())],
    )
    return pl.pallas_call(
        _dispatch_kernel,
        grid_spec=grid_spec,
        out_shape=jax.ShapeDtypeStruct((cap, d), F32),
        compiler_params=_cparams("arbitrary"),
        name="moe_dispatch",
    )(pends, padded, n_used, dest_t, xn)


def _expert_kernel(be_ref, nused_ref, x_ref, w1_ref, w2_ref, o_ref):
    live = pl.program_id(0) < nused_ref[0]

    @pl.when(live)
    def _():
        o_ref[...] = _swiglu(x_ref[...].astype(BF16), w1_ref, w2_ref, w2_ref.shape[0])

    @pl.when(jnp.logical_not(live))
    def _():
        o_ref[...] = jnp.zeros_like(o_ref)


def _experts(blk_expert, n_used, xs, w1, w2, layer):
    cap, d = xs.shape
    blk = lambda i, be, nu: (i, 0)
    expert = lambda w: pl.BlockSpec((None, None) + w.shape[2:], lambda i, be, nu: (layer, be[i], 0, 0),
                                    pipeline_mode=pl.Buffered(1))
    grid_spec = pltpu.PrefetchScalarGridSpec(
        num_scalar_prefetch=2,
        grid=(cap // MOE_ROWS,),
        in_specs=[pl.BlockSpec((MOE_ROWS, d), blk), expert(w1), expert(w2)],
        out_specs=pl.BlockSpec((MOE_ROWS, d), blk),
    )
    return pl.pallas_call(
        _expert_kernel,
        grid_spec=grid_spec,
        out_shape=jax.ShapeDtypeStruct((cap, d), F32),
        compiler_params=_cparams("arbitrary"),
        name="experts",
    )(blk_expert, n_used, xs, w1, w2)


def _combine_kernel(dest_ref, h_ref, gate_ref, nw_ref, yb_ref, o_ref, buf_ref, sem, *, final_norm):
    rows = h_ref.shape[0]

    def issue(group, carry):
        base = pl.multiple_of(group * SUBLANES, SUBLANES)
        for s in range(SUBLANES):
            for k in range(2):
                slot = dest_ref[0, 0, k * rows + base + s]
                pltpu.make_async_copy(
                    yb_ref.at[pl.ds(slot, 1), :], buf_ref.at[k, pl.ds(base + s, 1), :], sem).start()
        return carry

    lax.fori_loop(0, rows // SUBLANES, issue, 0)
    for k in range(2):
        pltpu.make_async_copy(yb_ref.at[pl.ds(0, rows), :], buf_ref.at[k], sem).wait()
    gate = gate_ref[...]
    out = h_ref[...] + gate[:, 0:1] * buf_ref[0] + gate[:, 1:2] * buf_ref[1]
    if final_norm:
        out = _rms(out, nw_ref[...])
    o_ref[...] = out


def _combine(dest_t, h, gate, nw, yb, final_norm):
    t, d = h.shape
    return pl.pallas_call(
        functools.partial(_combine_kernel, final_norm=final_norm),
        grid=(t // ROW_BLOCK,),
        in_specs=[
            pl.BlockSpec((1, 1, 2 * ROW_BLOCK), lambda i: (i, 0, 0), memory_space=pltpu.SMEM),
            pl.BlockSpec((ROW_BLOCK, d), lambda i: (i, 0)),
            pl.BlockSpec((ROW_BLOCK, LANES), lambda i: (i, 0)),
            pl.BlockSpec((1, d), lambda i: (0, 0)),
            pl.BlockSpec(memory_space=pl.ANY),
        ],
        out_specs=pl.BlockSpec((ROW_BLOCK, d), lambda i: (i, 0)),
        out_shape=jax.ShapeDtypeStruct((t, d), F32),
        scratch_shapes=[pltpu.VMEM((2, ROW_BLOCK, d), F32), pltpu.SemaphoreType.DMA(())],
        compiler_params=_cparams("arbitrary"),
        name="moe_combine",
    )(dest_t, h, gate, nw, yb)


def _final_norm_kernel(h_ref, nw_ref, o_ref):
    o_ref[...] = _rms(h_ref[...], nw_ref[...])


def _final_norm(h, nw):
    t, d = h.shape
    return pl.pallas_call(
        _final_norm_kernel,
        grid=(t // ROW_BLOCK,),
        in_specs=[pl.BlockSpec((ROW_BLOCK, d), lambda i: (i, 0)), pl.BlockSpec((1, d), lambda i: (0, 0))],
        out_specs=pl.BlockSpec((ROW_BLOCK, d), lambda i: (i, 0)),
        out_shape=jax.ShapeDtypeStruct((t, d), F32),
        compiler_params=_cparams("parallel"),
        name="final_norm",
    )(h, nw)


def _moe(h, nw, router_w, w1, w2, layer, tri, out_nw, final_norm):
    t, d = h.shape
    assert MOE_ROWS == ROW_BLOCK
    r1, r2, r3 = _split3(jnp.pad(router_w, ((0, 0), (0, LANES - N_EXPERTS))))
    xn, idx, gate, rank, cnt = _router(h, nw, jnp.concatenate([r1, r2, r3], axis=1), tri)
    counts = cnt[0, :N_EXPERTS].astype(jnp.int32)
    padded = (counts + MOE_ROWS - 1) // MOE_ROWS * MOE_ROWS
    pends = jnp.cumsum(padded)
    starts = pends - padded
    cap = (2 * t // MOE_ROWS + N_EXPERTS) * MOE_ROWS
    n_blk = cap // MOE_ROWS
    blk_start = jnp.arange(n_blk, dtype=jnp.int32) * MOE_ROWS
    blk_expert = jnp.minimum(jnp.sum(blk_start[:, None] >= pends[None, :], axis=1), N_EXPERTS - 1).astype(jnp.int32)
    n_used = (pends[-1:] // MOE_ROWS).astype(jnp.int32)
    start_row = jnp.zeros((1, LANES), F32).at[0, :N_EXPERTS].set(starts.astype(F32))
    dest = _slots(idx, rank, start_row)
    dest_t = dest[:, :2].reshape(t // ROW_BLOCK, ROW_BLOCK, 2).transpose(0, 2, 1).reshape(
        t // ROW_BLOCK, 1, 2 * ROW_BLOCK)
    xs = _dispatch(pends.astype(jnp.int32), padded.astype(jnp.int32), n_used, dest_t, xn, cap)
    yb = _experts(blk_expert, n_used, xs, w1, w2, layer)
    return _combine(dest_t, h, gate, out_nw, yb, final_norm)


def _tri(n):
    return jnp.tril(jnp.ones((n, n), F32)).astype(BF16)


def kernel(x, mem, mix_norm_w, w_in, fox_f_bias, fox_norm_w, hg_lb_raw, hg_norm_w, m2_conv_w, m2_conv_b, m2_dt_bias, m2_a_log, m2_d, m2_norm_w, w_out, xa_norm_w, xa_mem_norm_w, xa_w_q, xa_w_kv, xa_w_o, ffn_norm_w, ffn_w1, ffn_w2, moe_router, moe_w1, moe_w2, final_norm_w):
    batch, seq, d = x.shape
    depth = w_in.shape[0]
    t = batch * seq
    fox_w = N_FOX_HEADS * HEAD_DIM
    hg_w = N_HG_HEADS * HEAD_DIM
    m2_w = M2_HEADS * HEAD_DIM
    conv_ch = m2_w + 2 * M2_GROUPS * M2_STATE
    in_splits = (fox_w, fox_w, fox_w, N_FOX_HEADS, hg_w, hg_w, hg_w, hg_w, m2_w, conv_ch, M2_HEADS)
    offs = [0]
    for s in in_splits:
        offs.append(offs[-1] + s)

    lb_p = jax.nn.softmax(hg_lb_raw.astype(F32), axis=0)
    hg_lb = jnp.cumsum(lb_p, axis=0) - lb_p[0]

    tri_seq = _tri(SEQ_BLOCK)
    tri_chunk = _tri(SCAN_CHUNK)
    head_of_lane = jnp.arange(hg_w) // HEAD_DIM
    seg_ones = (head_of_lane[:, None] == head_of_lane[None, :]).astype(BF16)
    expand = (jnp.arange(LANES)[:, None] - SUBLANES == jnp.arange(m2_w)[None, :] // HEAD_DIM).astype(BF16)

    def pad_lanes(v, offset=0, width=LANES):
        return jnp.zeros((1, width), F32).at[0, offset:offset + v.shape[0]].set(v)

    small_w = jnp.zeros((depth, d, LANES), F32)
    small_w = small_w.at[:, :, 0:N_FOX_HEADS].set(w_in[:, :, offs[3]:offs[4]])
    small_w = small_w.at[:, :, SUBLANES:SUBLANES + M2_HEADS].set(w_in[:, :, offs[10]:offs[11]])
    in_weights = tuple(w.astype(BF16) for w in (
        w_in[:, :, offs[0]:offs[3]], small_w, w_in[:, :, offs[4]:offs[8]], w_in[:, :, offs[8]:offs[9]],
        w_in[:, :, offs[9]:offs[10]]))
    in_dtypes = (BF16, F32, F32, F32, F32)
    w_out_b, xa_wq_b, xa_wkv_b, xa_wo_b, ffn_w1_b, ffn_w2_b, moe_w1_b, moe_w2_b = (
        w.astype(BF16) for w in (w_out, xa_w_q, xa_w_kv, xa_w_o, ffn_w1, ffn_w2, moe_w1, moe_w2))

    h = x.reshape(t, d)
    mem2 = mem.reshape(batch * mem.shape[1], d)
    for layer in range(depth):
        qkv, small, hg, z, xbc = _in_proj(h, mix_norm_w[layer][None, :], in_weights, layer, in_dtypes)

        qa, ka, va = _fox_prep(qkv, small, pad_lanes(fox_f_bias[layer]), tri_seq, batch, seq)
        fox_nw = jnp.zeros((SUBLANES, LANES), F32).at[:fox_w // LANES].set(
            fox_norm_w[layer].reshape(fox_w // LANES, LANES))
        o_fox = _fox_attn(qa, ka, va, fox_nw, batch, seq)

        lb = hg_lb[layer]
        hg_params = jnp.zeros((SUBLANES, hg_w), F32)
        hg_params = hg_params.at[0].set(jnp.log(jnp.maximum(lb, LB_FLOOR)))
        hg_params = hg_params.at[1].set(jnp.log1p(-lb))
        hg_params = hg_params.at[2].set(1.0 - lb)
        hg_params = hg_params.at[3].set(hg_norm_w[layer])
        o_hg = _hgrn2(hg, hg_params, tri_chunk, seg_ones, batch, seq)

        conv_w = jnp.zeros((SUBLANES, conv_ch), F32).at[:M2_CONV].set(m2_conv_w[layer])
        head_rows = jnp.zeros((SUBLANES, m2_w), F32).at[0].set(jnp.repeat(m2_d[layer], HEAD_DIM))
        small_rows = jnp.concatenate(
            [pad_lanes(m2_dt_bias[layer], SUBLANES), pad_lanes(m2_a_log[layer], SUBLANES),
             jnp.zeros((SUBLANES - 2, LANES), F32)], axis=0)
        o_m2 = _ssd(z, xbc, small, conv_w, m2_conv_b[layer][None, :], head_rows, small_rows,
                    m2_norm_w[layer][None, :], tri_chunk, expand, batch, seq)

        kv = _norm_proj(mem2, xa_mem_norm_w[layer][None, :], xa_wkv_b, layer, BF16, mem.shape[1])
        h = _xattn(h, o_fox, o_hg, o_m2, w_out_b, xa_norm_w[layer][None, :], xa_wq_b, kv, xa_wo_b, layer,
                   batch, seq)

        nw = ffn_norm_w[layer][None, :]
        if layer % 2 == 0:
            h = _ffn(h, nw, ffn_w1_b, ffn_w2_b, layer // 2)
        else:
            last = layer == depth - 1
            h = _moe(h, nw, moe_router[layer // 2], moe_w1_b, moe_w2_b, layer // 2, tri_seq,
                     final_norm_w[None, :], last)
    if depth % 2:
        h = _final_norm(h, final_norm_w[None, :])
    return h.reshape(batch, seq, d)
```

```python
import functools

import jax
import jax.numpy as jnp
from jax import lax
from jax.experimental import pallas as pl
from jax.experimental.pallas import tpu as pltpu

F32 = jnp.float32
BF16 = jnp.bfloat16

EPS = 1e-6
MASK_VALUE = -1e9
LB_FLOOR = 1e-30
HEAD_DIM = 64
N_FOX_HEADS = 4
N_HG_HEADS = 4
M2_HEADS = 8
M2_GROUPS = 2
M2_STATE = 128
M2_CONV = 4
XA_HEADS = 4
N_EXPERTS = 8

LANES = 128
SUBLANES = 8
VMEM_LIMIT_BYTES = 56 * 1024 * 1024

ROW_BLOCK = 512
SEQ_BLOCK = 512
SCAN_CHUNK = 128
SCAN_CHUNKS_PER_STEP = 2
SUB_BLOCK = 16
HG_SAFE_DECAY = 60.0
MOE_ROWS = 512
SLOT_ROWS = 4096


def _cparams(*sem):
    return pltpu.CompilerParams(dimension_semantics=sem, vmem_limit_bytes=VMEM_LIMIT_BYTES)


def _split3(x):
    x1 = x.astype(BF16)
    r1 = x - x1.astype(F32)
    x2 = r1.astype(BF16)
    x3 = (r1 - x2.astype(F32)).astype(BF16)
    return x1, x2, x3


def _dot(a, b):
    return jnp.dot(a, b, preferred_element_type=F32)


def _dot_nt(a, b):
    return lax.dot_general(a, b, (((1,), (1,)), ((), ())), preferred_element_type=F32)


def _exact_left(mat01, x):
    x1, x2, x3 = _split3(x)
    return _dot(mat01, x1) + _dot(mat01, x2) + _dot(mat01, x3)


def _exact_right(x, mat01):
    x1, x2, x3 = _split3(x)
    return _dot(x1, mat01) + _dot(x2, mat01) + _dot(x3, mat01)


def _rms(x, w):
    return x * lax.rsqrt(jnp.mean(x * x, axis=-1, keepdims=True) + EPS) * w


def _sigmoid(x):
    return 1.0 / (1.0 + jnp.exp(-x))


def _silu(x):
    return x * _sigmoid(x)


def _log1p_exp(x):
    return jnp.log(1.0 + jnp.exp(x))


def _log_sigmoid(x):
    return jnp.minimum(x, 0.0) - _log1p_exp(-jnp.abs(x))


def _softplus(x):
    return jnp.maximum(x, 0.0) + _log1p_exp(-jnp.abs(x))


def _masked_exp(x, mask):
    return jnp.where(mask, jnp.exp(jnp.where(mask, x, 0.0)), 0.0)


def _resident(a):
    zeros = (0,) * a.ndim
    return pl.BlockSpec(a.shape, lambda *_: zeros, pipeline_mode=pl.Buffered(1))


def _layer_resident(stacked, layer):
    index = (layer,) + (0,) * (stacked.ndim - 1)
    return pl.BlockSpec((None,) + stacked.shape[1:], lambda *_: index, pipeline_mode=pl.Buffered(1))


def _in_proj_kernel(x_ref, nw_ref, *refs):
    n = len(refs) // 2
    xn = _rms(x_ref[...], nw_ref[...]).astype(BF16)
    for w_ref, o_ref in zip(refs[:n], refs[n:]):
        o_ref[...] = _dot(xn, w_ref[...]).astype(o_ref.dtype)


def _in_proj(h, nw, weights, layer, out_dtypes):
    t, d = h.shape
    widths = [w.shape[-1] for w in weights]
    return pl.pallas_call(
        _in_proj_kernel,
        grid=(t // ROW_BLOCK,),
        in_specs=[pl.BlockSpec((ROW_BLOCK, d), lambda i: (i, 0)), _resident(nw)]
        + [_layer_resident(w, layer) for w in weights],
        out_specs=[pl.BlockSpec((ROW_BLOCK, w), lambda i: (i, 0)) for w in widths],
        out_shape=[jax.ShapeDtypeStruct((t, w), dt) for w, dt in zip(widths, out_dtypes)],
        compiler_params=_cparams("parallel"),
        name="in_proj",
    )(h, nw, *weights)


def _norm_proj_kernel(x_ref, nw_ref, w_ref, o_ref):
    xn = _rms(x_ref[...], nw_ref[...]).astype(BF16)
    o_ref[...] = _dot(xn, w_ref[...]).astype(o_ref.dtype)


def _norm_proj(x, nw, w, layer, out_dtype, rows):
    t, d = x.shape
    n = w.shape[-1]
    return pl.pallas_call(
        _norm_proj_kernel,
        grid=(t // rows,),
        in_specs=[pl.BlockSpec((rows, d), lambda i: (i, 0)), _resident(nw), _layer_resident(w, layer)],
        out_specs=pl.BlockSpec((rows, n), lambda i: (i, 0)),
        out_shape=jax.ShapeDtypeStruct((t, n), out_dtype),
        compiler_params=_cparams("parallel"),
        name="norm_proj",
    )(x, nw, w)


def _fox_prep_kernel(qkv_ref, small_ref, bias_ref, tri_ref, q_ref, k_ref, v_ref, carry_ref):
    @pl.when(pl.program_id(1) == 0)
    def _():
        carry_ref[...] = jnp.zeros_like(carry_ref)

    rows = qkv_ref.shape[0]
    log_f = _log_sigmoid(small_ref[...] + bias_ref[...])
    c = _exact_left(tri_ref[...], log_f) + carry_ref[...]
    carry_ref[...] = c[rows - 1:rows, :]

    lane = lax.broadcasted_iota(jnp.int32, (rows, LANES), 1)
    width = N_FOX_HEADS * HEAD_DIM
    scale = HEAD_DIM ** -0.5
    for hd in range(N_FOX_HEADS):
        tile = (hd * HEAD_DIM) // LANES
        ch = c[:, hd:hd + 1]
        c1 = ch.astype(BF16).astype(F32)
        r1 = ch - c1
        c2 = r1.astype(BF16).astype(F32)
        c3 = r1 - c2

        def head_tile(base):
            x = qkv_ref[:, base + tile * LANES: base + (tile + 1) * LANES].astype(F32)
            if (hd * HEAD_DIM) % LANES:
                x = pltpu.roll(x, LANES - (hd * HEAD_DIM) % LANES, axis=1)
            return x

        def augment(x, first, second):
            out = jnp.where(lane < HEAD_DIM, x, 0.0)
            for j, val in enumerate(first + second):
                out = jnp.where(lane == HEAD_DIM + j, val, out)
            return out.astype(BF16)

        ones = (1.0, 1.0, 1.0)
        q_ref[hd] = augment(head_tile(0) * scale, ones, (c1, c2, c3))
        k_ref[hd] = augment(head_tile(width), (-c1, -c2, -c3), ones)
        v_aug = jnp.where(lane < HEAD_DIM, head_tile(2 * width), jnp.where(lane == HEAD_DIM, 1.0, 0.0))
        v_ref[hd] = v_aug.T.astype(BF16)


def _fox_prep(qkv, small, bias_row, tri, batch, seq):
    t = qkv.shape[0]
    nb = seq // SEQ_BLOCK
    row = lambda b, c: (b * nb + c, 0)
    head_spec = pl.BlockSpec((N_FOX_HEADS, SEQ_BLOCK, LANES), lambda b, c: (0, b * nb + c, 0))
    head_shape = jax.ShapeDtypeStruct((N_FOX_HEADS, t, LANES), BF16)
    vt_spec = pl.BlockSpec((N_FOX_HEADS, LANES, SEQ_BLOCK), lambda b, c: (0, 0, b * nb + c))
    vt_shape = jax.ShapeDtypeStruct((N_FOX_HEADS, LANES, t), BF16)
    return pl.pallas_call(
        _fox_prep_kernel,
        grid=(batch, nb),
        in_specs=[
            pl.BlockSpec((SEQ_BLOCK, qkv.shape[1]), row),
            pl.BlockSpec((SEQ_BLOCK, LANES), row),
            pl.BlockSpec((1, LANES), lambda b, c: (0, 0)),
            pl.BlockSpec((SEQ_BLOCK, SEQ_BLOCK), lambda b, c: (0, 0)),
        ],
        out_specs=[head_spec, head_spec, vt_spec],
        out_shape=[head_shape, head_shape, vt_shape],
        scratch_shapes=[pltpu.VMEM((1, LANES), F32)],
        compiler_params=_cparams("parallel", "arbitrary"),
        name="fox_prep",
    )(qkv, small, bias_row, tri)


def _fox_attn_kernel(q_ref, k_ref, vt_ref, nw_ref, o_ref, m_ref, acc_ref, sa_ref, sb_ref):
    i = pl.program_id(1)
    tq = q_ref.shape[1]
    kv_idx = lax.broadcasted_iota(jnp.int32, (tq, tq), 0)
    q_idx = lax.broadcasted_iota(jnp.int32, (tq, tq), 1)
    heads = range(N_FOX_HEADS)
    m_ref[...] = jnp.full(m_ref.shape, -jnp.inf, F32)
    acc_ref[...] = jnp.zeros_like(acc_ref)

    def logits_into(dst_ref, j):
        start = pl.multiple_of(j * tq, tq)
        for hd in heads:
            dst_ref[hd] = _dot_nt(k_ref[hd, pl.ds(start, tq), :], q_ref[hd])

    def fold(src_ref, j, diagonal=False):
        start = pl.multiple_of(j * tq, tq)
        for hd in heads:
            s = src_ref[hd]
            if diagonal:
                s = jnp.where(kv_idx <= q_idx, s, MASK_VALUE)
            m_old = m_ref[hd]
            m_new = jnp.maximum(m_old, jnp.max(s, axis=0, keepdims=True))
            p = jnp.exp(s - m_new).astype(BF16)
            acc_ref[hd] = jnp.exp(m_old - m_new) * acc_ref[hd] + _dot(vt_ref[hd, :, pl.ds(start, tq)], p)
            m_ref[hd] = m_new

    logits_into(sa_ref, 0)

    def two_blocks(jj, carry):
        logits_into(sb_ref, 2 * jj + 1)
        fold(sa_ref, 2 * jj)
        logits_into(sa_ref, 2 * jj + 2)
        fold(sb_ref, 2 * jj + 1)
        return carry

    lax.fori_loop(0, i // 2, two_blocks, 0)

    @pl.when(i % 2 == 1)
    def _():
        logits_into(sb_ref, i)
        fold(sa_ref, i - 1)
        fold(sb_ref, i, diagonal=True)

    @pl.when(i % 2 == 0)
    def _():
        fold(sa_ref, i, diagonal=True)

    normed = []
    for hd in heads:
        acc = acc_ref[hd]
        o = acc[:HEAD_DIM] / acc[HEAD_DIM:HEAD_DIM + 1]
        ms = jnp.mean(o * o, axis=0, keepdims=True)
        normed.append(o * lax.rsqrt(ms + EPS))
    tiles = []
    for pair in range(N_FOX_HEADS // 2):
        both = jnp.concatenate(normed[2 * pair:2 * pair + 2], axis=0)
        tiles.append(both.T * nw_ref[pair:pair + 1, :])
    o_ref[...] = jnp.concatenate(tiles, axis=1).astype(o_ref.dtype)


def _fox_attn(qa, ka, vt, nw_pairs, batch, seq):
    t = qa.shape[1]
    nq = seq // SEQ_BLOCK
    return pl.pallas_call(
        _fox_attn_kernel,
        grid=(batch, nq),
        in_specs=[
            pl.BlockSpec((N_FOX_HEADS, SEQ_BLOCK, LANES), lambda b, i: (0, b * nq + i, 0)),
            pl.BlockSpec((N_FOX_HEADS, seq, LANES), lambda b, i: (0, b, 0)),
            pl.BlockSpec((N_FOX_HEADS, LANES, seq), lambda b, i: (0, 0, b)),
            pl.BlockSpec((SUBLANES, LANES), lambda b, i: (0, 0)),
        ],
        out_specs=pl.BlockSpec((SEQ_BLOCK, N_FOX_HEADS * HEAD_DIM), lambda b, i: (b * nq + i, 0)),
        out_shape=jax.ShapeDtypeStruct((t, N_FOX_HEADS * HEAD_DIM), BF16),
        scratch_shapes=[
            pltpu.VMEM((N_FOX_HEADS, 1, SEQ_BLOCK), F32),
            pltpu.VMEM((N_FOX_HEADS, LANES, SEQ_BLOCK), F32),
            pltpu.VMEM((N_FOX_HEADS, SEQ_BLOCK, SEQ_BLOCK), F32),
            pltpu.VMEM((N_FOX_HEADS, SEQ_BLOCK, SEQ_BLOCK), F32),
        ],
        compiler_params=_cparams("parallel", "arbitrary"),
        name="fox_attn",
    )(qa, ka, vt, nw_pairs)


def _chunk_views(refs, c):
    return [r.at[pl.ds(c * SCAN_CHUNK, SCAN_CHUNK)] for r in refs]


def _hgrn2_kernel(hq_ref, hf_ref, hi_ref, hg_ref, par_ref, tri_ref, seg_ref, o_ref, st_ref):
    @pl.when(pl.program_id(1) == 0)
    def _():
        st_ref[...] = jnp.zeros_like(st_ref)

    for c in range(SCAN_CHUNKS_PER_STEP):
        hq, hf, hi, hg, o = _chunk_views((hq_ref, hf_ref, hi_ref, hg_ref, o_ref), c)
        _hgrn2_chunk(hq, hf, hi, hg, par_ref, tri_ref, seg_ref, o, st_ref)


def _hgrn2_chunk(hq_ref, hf_ref, hi_ref, hg_ref, par_ref, tri_ref, seg_ref, o_ref, st_ref):
    rows, width = hq_ref.shape
    log_lb, log1m_lb, one_m_lb, nw = (par_ref[r:r + 1, :] for r in range(4))
    f_raw = hf_ref[...]
    q = _silu(hq_ref[...]) * (HEAD_DIM ** -0.5)
    b = log1m_lb + _log_sigmoid(f_raw)
    g = jnp.maximum(log_lb, b) + _log1p_exp(-jnp.abs(log_lb - b))
    k = one_m_lb * _sigmoid(-f_raw)
    v = hi_ref[...]
    cum = _exact_left(tri_ref[...], g)

    lane = lax.broadcasted_iota(jnp.int32, (1, width), 1)
    head_masks = [(lane >= hd * HEAD_DIM) & (lane < (hd + 1) * HEAD_DIM) for hd in range(N_HG_HEADS)]
    seg = seg_ref[...]
    v_b = v.astype(BF16)

    st = st_ref[...]
    o_state = _dot_nt((q * jnp.exp(cum)).astype(BF16), st.astype(BF16))

    n_sub = rows // SUB_BLOCK
    refs = [cum[i * SUB_BLOCK - 1:i * SUB_BLOCK] if i else jnp.zeros((1, width), F32) for i in range(n_sub)]
    local = [cum[i * SUB_BLOCK:(i + 1) * SUB_BLOCK] - refs[i] for i in range(n_sub)]

    def stack_heads(x):
        return jnp.concatenate([jnp.where(mk, x, 0.0) for mk in head_masks], axis=0).astype(BF16)

    def unstack_heads(base, p4):
        for hd, mk in enumerate(head_masks):
            base = base + jnp.where(mk, p4[hd * SUB_BLOCK:(hd + 1) * SUB_BLOCK], 0.0)
        return base

    def factored():
        scores = []
        for i in range(n_sub):
            hi = (i + 1) * SUB_BLOCK
            qs = q[hi - SUB_BLOCK:hi] * jnp.exp(local[i])
            ks = (k[:hi] * jnp.exp(refs[i] - cum[:hi])).astype(BF16)
            scores.append(_dot_nt(stack_heads(qs), ks))
        blocks = []
        for i in range(n_sub):
            hi = (i + 1) * SUB_BLOCK
            t_in = lax.broadcasted_iota(jnp.int32, (N_HG_HEADS * SUB_BLOCK, hi), 0) & (SUB_BLOCK - 1)
            s_in = lax.broadcasted_iota(jnp.int32, (N_HG_HEADS * SUB_BLOCK, hi), 1)
            sc = jnp.where(s_in <= t_in + (hi - SUB_BLOCK), scores[i], 0.0).astype(BF16)
            blocks.append(unstack_heads(o_state[hi - SUB_BLOCK:hi], _dot(sc, v_b[:hi])))
        return jnp.concatenate(blocks, axis=0)

    def pairwise():
        t_idx = lax.broadcasted_iota(jnp.int32, (SUB_BLOCK, 1), 0)
        blocks = []
        for i in range(n_sub):
            r0 = i * SUB_BLOCK
            q_i = q[r0:r0 + SUB_BLOCK]
            cum_i = cum[r0:r0 + SUB_BLOCK]
            o_i = o_state[r0:r0 + SUB_BLOCK]
            if i > 0:
                ks = (k[:r0] * jnp.exp(refs[i] - cum[:r0])).astype(BF16)
                sc = _dot_nt(stack_heads(q_i * jnp.exp(local[i])), ks)
                o_i = unstack_heads(o_i, _dot(sc.astype(BF16), v_b[:r0]))
            terms = []
            for s in range(SUB_BLOCK):
                keep = t_idx >= s
                e = _masked_exp(cum_i - cum[r0 + s:r0 + s + 1], keep)
                terms.append((q_i * e * k[r0 + s:r0 + s + 1]).astype(BF16))
            sums = _dot(jnp.concatenate(terms, axis=0), seg)
            for s in range(SUB_BLOCK):
                o_i = o_i + sums[s * SUB_BLOCK:(s + 1) * SUB_BLOCK] * v[r0 + s:r0 + s + 1]
            blocks.append(o_i)
        return jnp.concatenate(blocks, axis=0)

    worst = functools.reduce(jnp.minimum, local)
    o = lax.cond(jnp.min(worst) >= -HG_SAFE_DECAY, factored, pairwise)

    last = cum[rows - 1:rows]
    kd = (k * jnp.exp(last - cum)).astype(BF16)
    upd = _dot(v.T.astype(BF16), kd)
    st_ref[...] = st * jnp.exp(last) + jnp.where(seg > 0, upd, 0.0)

    sq = o * o
    s1 = sq.astype(BF16)
    s2 = (sq - s1.astype(F32)).astype(BF16)
    ms = (_dot(s1, seg) + _dot(s2, seg)) * (1.0 / HEAD_DIM)
    o_ref[...] = (o * lax.rsqrt(ms + EPS) * nw * _silu(hg_ref[...])).astype(o_ref.dtype)


def _hgrn2(hg, params, tri, seg, batch, seq):
    t = hg.shape[0]
    width = N_HG_HEADS * HEAD_DIM
    step_rows = SCAN_CHUNK * SCAN_CHUNKS_PER_STEP
    nc = seq // step_rows
    col = lambda c: pl.BlockSpec((step_rows, width), lambda b, j, c=c: (b * nc + j, c))
    return pl.pallas_call(
        _hgrn2_kernel,
        grid=(batch, nc),
        in_specs=[
            col(0), col(1), col(2), col(3),
            pl.BlockSpec(params.shape, lambda b, j: (0, 0)),
            pl.BlockSpec(tri.shape, lambda b, j: (0, 0)),
            pl.BlockSpec(seg.shape, lambda b, j: (0, 0)),
        ],
        out_specs=pl.BlockSpec((step_rows, width), lambda b, j: (b * nc + j, 0)),
        out_shape=jax.ShapeDtypeStruct((t, width), BF16),
        scratch_shapes=[pltpu.VMEM((width, width), F32)],
        compiler_params=_cparams("parallel", "arbitrary"),
        name="hgrn2",
    )(hg, hg, hg, hg, params, tri, seg)


def _ssd_kernel(z_ref, xbc_ref, small_ref, cw_ref, cb_ref, hp_ref, sp_ref, nw_ref, tri_ref, exp_ref, shift_ref,
                o_ref, buf_ref, st_ref):
    @pl.when(pl.program_id(1) == 0)
    def _():
        buf_ref[...] = jnp.zeros_like(buf_ref)
        st_ref[...] = jnp.zeros_like(st_ref)

    for c in range(SCAN_CHUNKS_PER_STEP):
        z, xbc, small, o = _chunk_views((z_ref, xbc_ref, small_ref, o_ref), c)
        _ssd_chunk(z, xbc, small, cw_ref, cb_ref, hp_ref, sp_ref, nw_ref, tri_ref, exp_ref, shift_ref, o,
                   buf_ref, st_ref)


def _ssd_chunk(z_ref, xbc_ref, small_ref, cw_ref, cb_ref, hp_ref, sp_ref, nw_ref, tri_ref, exp_ref, shift_ref,
               o_ref, buf_ref, st_ref):
    rows = z_ref.shape[0]
    width = z_ref.shape[1]

    x_in = xbc_ref[...]
    buf_ref[rows:2 * rows, :] = x_in.astype(BF16)
    shifted = _dot(shift_ref[...], buf_ref[...])
    buf_ref[0:rows, :] = buf_ref[rows:2 * rows, :]
    conv = cb_ref[...] + cw_ref[M2_CONV - 1:M2_CONV, :] * x_in
    for k in range(1, M2_CONV):
        tap = M2_CONV - 1 - k
        conv = conv + cw_ref[tap:tap + 1, :] * shifted[(k - 1) * rows:k * rows]
    xc = _silu(conv)
    xs = xc[:, :width]
    gn = M2_GROUPS * M2_STATE
    bm = xc[:, width:width + gn]
    cm = xc[:, width + gn:width + 2 * gn]

    d_x = hp_ref[0:1, :]
    dt_bias_c, a_log_c = (sp_ref[r:r + 1, :] for r in range(2))
    small = small_ref[...]
    tri = tri_ref[...]
    dt_c = _softplus(small + dt_bias_c)
    a_cum_c = _exact_left(tri, dt_c * -jnp.exp(a_log_c))
    a_cum_t = a_cum_c.T
    expand = exp_ref[...]
    dt_x = _exact_right(dt_c, expand)
    a_cum_x = _exact_right(a_cum_c, expand)

    row = lax.broadcasted_iota(jnp.int32, (rows, rows), 0)
    col = lax.broadcasted_iota(jnp.int32, (rows, rows), 1)
    causal = col <= row
    lane = lax.broadcasted_iota(jnp.int32, (1, LANES), 1)
    low = lane < HEAD_DIM

    xdt = xs * dt_x
    a_last = a_cum_x[rows - 1:rows]
    xd = (xdt * jnp.exp(a_last - a_cum_x)).astype(BF16)
    decay_out = jnp.exp(a_cum_x)
    st = st_ref[...]
    heads_per_group = M2_HEADS // M2_GROUPS
    gw = heads_per_group * HEAD_DIM
    cbs, y_offs, new_states = [], [], []
    for g in range(M2_GROUPS):
        b_g = bm[:, g * M2_STATE:(g + 1) * M2_STATE]
        c_g = cm[:, g * M2_STATE:(g + 1) * M2_STATE].astype(BF16)
        cbs.append(jnp.where(causal, _dot_nt(c_g, b_g.astype(BF16)), 0.0))
        y_offs.append(_dot(c_g, st[:, g * gw:(g + 1) * gw].astype(BF16)) * decay_out[:, g * gw:(g + 1) * gw])
        new_states.append(_dot(b_g.T.astype(BF16), xd[:, g * gw:(g + 1) * gw]))
    st_ref[...] = st * jnp.exp(a_last) + jnp.concatenate(new_states, axis=1)

    y_tiles = []
    for tile in range(M2_HEADS // 2):
        g = (2 * tile) // heads_per_group
        pair = tile - g * (heads_per_group // 2)
        x_tile = xdt[:, tile * LANES:(tile + 1) * LANES]
        y_pair = y_offs[g][:, pair * LANES:(pair + 1) * LANES]
        for half in range(2):
            hd = 2 * tile + half
            seg = a_cum_c[:, SUBLANES + hd:SUBLANES + hd + 1] - a_cum_t[SUBLANES + hd:SUBLANES + hd + 1, :]
            m_h = (cbs[g] * jnp.exp(jnp.minimum(seg, 0.0))).astype(BF16)
            x_h = jnp.where(low if half == 0 else ~low, x_tile, 0.0).astype(BF16)
            y_pair = y_pair + _dot(m_h, x_h)
        y_tiles.append(y_pair)

    y = jnp.concatenate(y_tiles, axis=1) + d_x * xs
    y = y * _silu(z_ref[...])
    nw = nw_ref[...]
    outs = []
    for g in range(M2_GROUPS):
        outs.append(_rms(y[:, g * gw:(g + 1) * gw], nw[:, g * gw:(g + 1) * gw]))
    o_ref[...] = jnp.concatenate(outs, axis=1).astype(o_ref.dtype)


def _ssd(z, xbc, small, conv_w, conv_b, head_rows, small_rows, norm_w, tri, expand, shift, batch, seq):
    t, width = z.shape
    ch = xbc.shape[1]
    step_rows = SCAN_CHUNK * SCAN_CHUNKS_PER_STEP
    nc = seq // step_rows
    row = lambda b, j: (b * nc + j, 0)
    const = lambda a: pl.BlockSpec(a.shape, lambda b, j: (0, 0))
    return pl.pallas_call(
        _ssd_kernel,
        grid=(batch, nc),
        in_specs=[
            pl.BlockSpec((step_rows, width), row),
            pl.BlockSpec((step_rows, ch), row),
            pl.BlockSpec((step_rows, LANES), row),
            const(conv_w), const(conv_b), const(head_rows), const(small_rows), const(norm_w),
            const(tri), const(expand), const(shift),
        ],
        out_specs=pl.BlockSpec((step_rows, width), row),
        out_shape=jax.ShapeDtypeStruct((t, width), BF16),
        scratch_shapes=[
            pltpu.VMEM((2 * SCAN_CHUNK, ch), BF16),
            pltpu.VMEM((M2_STATE, width), F32),
        ],
        compiler_params=_cparams("parallel", "arbitrary"),
        name="ssd",
    )(z, xbc, small, conv_w, conv_b, head_rows, small_rows, norm_w, tri, expand, shift)


def _xattn_kernel(h_ref, a_ref, b_ref, c_ref, wout_ref, nw_ref, wq_ref, kv_ref, wo_ref, o_ref):
    wa = a_ref.shape[1]
    wb = b_ref.shape[1]
    h = h_ref[...] + _dot(a_ref[...], wout_ref[0:wa, :])
    h = h + _dot(b_ref[...], wout_ref[wa:wa + wb, :])
    h = h + _dot(c_ref[...], wout_ref[wa + wb:, :])
    d = h.shape[1]
    hd = d // XA_HEADS
    q = (_dot(_rms(h, nw_ref[...]).astype(BF16), wq_ref[...]) * (hd ** -0.5)).astype(BF16)
    logits = [_dot_nt(q[:, a * hd:(a + 1) * hd], kv_ref[:, a * hd:(a + 1) * hd]) for a in range(XA_HEADS)]
    outs = []
    for a, s in enumerate(logits):
        p = jnp.exp(s - jnp.max(s, axis=-1, keepdims=True))
        p = p / jnp.sum(p, axis=-1, keepdims=True)
        outs.append(_dot(p.astype(BF16), kv_ref[:, d + a * hd:d + (a + 1) * hd]).astype(BF16))
    o_ref[...] = h + _dot(jnp.concatenate(outs, axis=1), wo_ref[...])


def _xattn(h, o_fox, o_hg, o_m2, w_out, nw, wq, kv, wo, layer, batch, seq):
    t, d = h.shape
    n_mem = kv.shape[0] // batch
    nb = seq // ROW_BLOCK
    row = lambda x: pl.BlockSpec((ROW_BLOCK, x.shape[1]), lambda b, i: (b * nb + i, 0))
    return pl.pallas_call(
        _xattn_kernel,
        grid=(batch, nb),
        in_specs=[
            row(h), row(o_fox), row(o_hg), row(o_m2),
            _layer_resident(w_out, layer), _resident(nw), _layer_resident(wq, layer),
            pl.BlockSpec((n_mem, kv.shape[1]), lambda b, i: (b, 0)),
            _layer_resident(wo, layer),
        ],
        out_specs=row(h),
        out_shape=jax.ShapeDtypeStruct((t, d), F32),
        compiler_params=_cparams("parallel", "parallel"),
        name="xattn",
    )(h, o_fox, o_hg, o_m2, w_out, nw, wq, kv, wo)


def _swiglu(x, w1_ref, w2_ref, d_ff):
    a = (_silu(_dot(x, w1_ref[:, :d_ff])) * _dot(x, w1_ref[:, d_ff:])).astype(BF16)
    return _dot(a, w2_ref[...])


def _ffn_kernel(h_ref, nw_ref, w1_ref, w2_ref, o_ref):
    h = h_ref[...]
    o_ref[...] = h + _swiglu(_rms(h, nw_ref[...]).astype(BF16), w1_ref, w2_ref, w2_ref.shape[0])


def _ffn(h, nw, w1, w2, layer):
    t, d = h.shape
    return pl.pallas_call(
        _ffn_kernel,
        grid=(t // ROW_BLOCK,),
        in_specs=[pl.BlockSpec((ROW_BLOCK, d), lambda i: (i, 0)), _resident(nw),
                  _layer_resident(w1, layer), _layer_resident(w2, layer)],
        out_specs=pl.BlockSpec((ROW_BLOCK, d), lambda i: (i, 0)),
        out_shape=jax.ShapeDtypeStruct((t, d), F32),
        compiler_params=_cparams("parallel"),
        name="ffn",
    )(h, nw, w1, w2)


def _router_kernel(h_ref, nw_ref, wr_ref, tri_ref, xn_ref, idx_ref, gate_ref, rank_ref, cnt_ref, carry_ref):
    @pl.when(pl.program_id(0) == 0)
    def _():
        carry_ref[...] = jnp.zeros_like(carry_ref)

    xn = _rms(h_ref[...], nw_ref[...])
    xn_ref[...] = xn
    x1 = xn.astype(BF16)
    x2 = (xn - x1.astype(F32)).astype(BF16)
    w12 = wr_ref[:, :2 * LANES]
    big = _dot(x1, w12) + _dot(x2, w12)
    logits = big[:, :LANES] + big[:, LANES:] + _dot(x1, wr_ref[:, 2 * LANES:])
    lane = lax.broadcasted_iota(jnp.int32, logits.shape, 1)
    logits = jnp.where(lane < N_EXPERTS, logits, -jnp.inf)
    m1 = jnp.max(logits, axis=-1, keepdims=True)
    i1 = jnp.min(jnp.where(logits == m1, lane, LANES), axis=-1, keepdims=True)
    rest = jnp.where(lane == i1, -jnp.inf, logits)
    m2 = jnp.max(rest, axis=-1, keepdims=True)
    i2 = jnp.min(jnp.where(rest == m2, lane, LANES), axis=-1, keepdims=True)
    e2 = jnp.exp(m2 - m1)
    denom = 1.0 + e2
    idx_ref[...] = jnp.where(lane == 0, i1, jnp.where(lane == 1, i2, 0))
    gate_ref[...] = jnp.where(lane == 0, 1.0 / denom, jnp.where(lane == 1, e2 / denom, 0.0))

    hit1 = lane == i1
    hit2 = lane == i2
    member = jnp.where(hit1, 1.0, jnp.where(hit2, 1.0, 0.0))
    incl = _dot(tri_ref[...], member.astype(BF16)) + carry_ref[...]
    excl = incl - member
    r1 = jnp.sum(jnp.where(hit1, excl, 0.0), axis=-1, keepdims=True)
    r2 = jnp.sum(jnp.where(hit2, excl, 0.0), axis=-1, keepdims=True)
    rows = member.shape[0]
    carry_ref[...] = incl[rows - 1:rows, :]
    rank_ref[...] = jnp.where(lane == 0, r1, jnp.where(lane == 1, r2, 0.0)).astype(jnp.int32)
    cnt_ref[...] = jnp.broadcast_to(incl[rows - 1:rows, :], cnt_ref.shape)


def _router(h, nw, wr, tri):
    t, d = h.shape
    row = lambda w: pl.BlockSpec((ROW_BLOCK, w), lambda i: (i, 0))
    return pl.pallas_call(
        _router_kernel,
        grid=(t // ROW_BLOCK,),
        in_specs=[row(d), _resident(nw), _resident(wr), _resident(tri)],
        out_specs=[row(d), row(LANES), row(LANES), row(LANES), pl.BlockSpec((SUBLANES, LANES), lambda i: (0, 0))],
        out_shape=[
            jax.ShapeDtypeStruct((t, d), F32),
            jax.ShapeDtypeStruct((t, LANES), jnp.int32),
            jax.ShapeDtypeStruct((t, LANES), F32),
            jax.ShapeDtypeStruct((t, LANES), jnp.int32),
            jax.ShapeDtypeStruct((SUBLANES, LANES), F32),
        ],
        scratch_shapes=[pltpu.VMEM((1, LANES), F32)],
        compiler_params=_cparams("arbitrary"),
        name="router",
    )(h, nw, wr, tri)


def _slot_kernel(idx_ref, rank_ref, start_ref, o_ref):
    idx = idx_ref[...]
    lane = lax.broadcasted_iota(jnp.int32, idx.shape, 1)
    starts = start_ref[...]
    s1 = jnp.sum(jnp.where(lane == idx[:, 0:1], starts, 0.0), axis=-1, keepdims=True)
    s2 = jnp.sum(jnp.where(lane == idx[:, 1:2], starts, 0.0), axis=-1, keepdims=True)
    o_ref[...] = jnp.where(lane == 0, s1, jnp.where(lane == 1, s2, 0.0)).astype(jnp.int32) + rank_ref[...]


def _slots(idx, rank, start_row):
    t = idx.shape[0]
    rows = min(t, SLOT_ROWS)
    row = pl.BlockSpec((rows, LANES), lambda i: (i, 0))
    return pl.pallas_call(
        _slot_kernel,
        grid=(t // rows,),
        in_specs=[row, row, pl.BlockSpec((1, LANES), lambda i: (0, 0))],
        out_specs=row,
        out_shape=jax.ShapeDtypeStruct((t, LANES), jnp.int32),
        compiler_params=_cparams("parallel"),
        name="moe_slots",
    )(idx, rank, start_row)


def _dispatch_kernel(pend_ref, padded_ref, nused_ref, dest_ref, x_ref, xs_ref, zero_ref, sem):
    rows = x_ref.shape[0]
    n_blk = xs_ref.shape[0] // MOE_ROWS

    @pl.when(pl.program_id(0) == 0)
    def _():
        zero_ref[...] = jnp.zeros_like(zero_ref)

        def fill_block(start):
            fill = pltpu.make_async_copy(zero_ref, xs_ref.at[pl.ds(start, MOE_ROWS), :], sem)
            fill.start()
            fill.wait()

        for e in range(N_EXPERTS):
            @pl.when(padded_ref[e] > 0)
            def _():
                fill_block(pl.multiple_of(pend_ref[e] - MOE_ROWS, MOE_ROWS))

            @pl.when(nused_ref[0] + e < n_blk)
            def _():
                fill_block(pl.multiple_of((nused_ref[0] + e) * MOE_ROWS, MOE_ROWS))

    def issue(group, carry):
        base = pl.multiple_of(group * SUBLANES, SUBLANES)
        for s in range(SUBLANES):
            for k in range(2):
                slot = dest_ref[0, 0, k * rows + base + s]
                pltpu.make_async_copy(x_ref.at[pl.ds(base + s, 1), :], xs_ref.at[pl.ds(slot, 1), :], sem).start()
        return carry

    lax.fori_loop(0, rows // SUBLANES, issue, 0)
    for k in range(2):
        pltpu.make_async_copy(x_ref, xs_ref.at[pl.ds(0, rows), :], sem).wait()


def _dispatch(pends, padded, n_used, dest_t, xn, cap):
    t, d = xn.shape
    grid_spec = pltpu.PrefetchScalarGridSpec(
        num_scalar_prefetch=3,
        grid=(t // ROW_BLOCK,),
        in_specs=[
            pl.BlockSpec((1, 1, 2 * ROW_BLOCK), lambda i, pe, pa, nu: (i, 0, 0), memory_space=pltpu.SMEM),
            pl.BlockSpec((ROW_BLOCK, d), lambda i, pe, pa, nu: (i, 0)),
        ],
        out_specs=pl.BlockSpec(memory_space=pl.ANY),
        scratch_shapes=[pltpu.VMEM((MOE_ROWS, d), F32), pltpu.SemaphoreType.DMA(())],
    )
    return pl.pallas_call(
        _dispatch_kernel,
        grid_spec=grid_spec,
        out_shape=jax.ShapeDtypeStruct((cap, d), F32),
        compiler_params=_cparams("arbitrary"),
        name="moe_dispatch",
    )(pends, padded, n_used, dest_t, xn)


def _expert_kernel(be_ref, nused_ref, x_ref, w1_ref, w2_ref, o_ref):
    live = pl.program_id(0) < nused_ref[0]

    @pl.when(live)
    def _():
        o_ref[...] = _swiglu(x_ref[...].astype(BF16), w1_ref, w2_ref, w2_ref.shape[0])

    @pl.when(jnp.logical_not(live))
    def _():
        o_ref[...] = jnp.zeros_like(o_ref)


def _experts(blk_expert, n_used, xs, w1, w2, layer):
    cap, d = xs.shape
    blk = lambda i, be, nu: (i, 0)
    expert = lambda w: pl.BlockSpec((None, None) + w.shape[2:], lambda i, be, nu: (layer, be[i], 0, 0),
                                    pipeline_mode=pl.Buffered(1))
    grid_spec = pltpu.PrefetchScalarGridSpec(
        num_scalar_prefetch=2,
        grid=(cap // MOE_ROWS,),
        in_specs=[pl.BlockSpec((MOE_ROWS, d), blk), expert(w1), expert(w2)],
        out_specs=pl.BlockSpec((MOE_ROWS, d), blk),
    )
    return pl.pallas_call(
        _expert_kernel,
        grid_spec=grid_spec,
        out_shape=jax.ShapeDtypeStruct((cap, d), F32),
        compiler_params=_cparams("arbitrary"),
        name="experts",
    )(blk_expert, n_used, xs, w1, w2)


def _combine_kernel(dest_ref, h_ref, gate_ref, nw_ref, yb_ref, o_ref, buf_ref, sem, *, final_norm):
    rows = h_ref.shape[0]

    def issue(group, carry):
        base = pl.multiple_of(group * SUBLANES, SUBLANES)
        for s in range(SUBLANES):
            for k in range(2):
                slot = dest_ref[0, 0, k * rows + base + s]
                pltpu.make_async_copy(
                    yb_ref.at[pl.ds(slot, 1), :], buf_ref.at[k, pl.ds(base + s, 1), :], sem).start()
        return carry

    lax.fori_loop(0, rows // SUBLANES, issue, 0)
    for k in range(2):
        pltpu.make_async_copy(yb_ref.at[pl.ds(0, rows), :], buf_ref.at[k], sem).wait()
    gate = gate_ref[...]
    out = h_ref[...] + gate[:, 0:1] * buf_ref[0] + gate[:, 1:2] * buf_ref[1]
    if final_norm:
        out = _rms(out, nw_ref[...])
    o_ref[...] = out


def _combine(dest_t, h, gate, nw, yb, final_norm):
    t, d = h.shape
    return pl.pallas_call(
        functools.partial(_combine_kernel, final_norm=final_norm),
        grid=(t // ROW_BLOCK,),
        in_specs=[
            pl.BlockSpec((1, 1, 2 * ROW_BLOCK), lambda i: (i, 0, 0), memory_space=pltpu.SMEM),
            pl.BlockSpec((ROW_BLOCK, d), lambda i: (i, 0)),
            pl.BlockSpec((ROW_BLOCK, LANES), lambda i: (i, 0)),
            pl.BlockSpec((1, d), lambda i: (0, 0)),
            pl.BlockSpec(memory_space=pl.ANY),
        ],
        out_specs=pl.BlockSpec((ROW_BLOCK, d), lambda i: (i, 0)),
        out_shape=jax.ShapeDtypeStruct((t, d), F32),
        scratch_shapes=[pltpu.VMEM((2, ROW_BLOCK, d), F32), pltpu.SemaphoreType.DMA(())],
        compiler_params=_cparams("arbitrary"),
        name="moe_combine",
    )(dest_t, h, gate, nw, yb)


def _final_norm_kernel(h_ref, nw_ref, o_ref):
    o_ref[...] = _rms(h_ref[...], nw_ref[...])


def _final_norm(h, nw):
    t, d = h.shape
    return pl.pallas_call(
        _final_norm_kernel,
        grid=(t // ROW_BLOCK,),
        in_specs=[pl.BlockSpec((ROW_BLOCK, d), lambda i: (i, 0)), pl.BlockSpec((1, d), lambda i: (0, 0))],
        out_specs=pl.BlockSpec((ROW_BLOCK, d), lambda i: (i, 0)),
        out_shape=jax.ShapeDtypeStruct((t, d), F32),
        compiler_params=_cparams("parallel"),
        name="final_norm",
    )(h, nw)


def _moe(h, nw, router_w, w1, w2, layer, tri, out_nw, final_norm):
    t, d = h.shape
    assert MOE_ROWS == ROW_BLOCK
    r1, r2, r3 = _split3(jnp.pad(router_w, ((0, 0), (0, LANES - N_EXPERTS))))
    xn, idx, gate, rank, cnt = _router(h, nw, jnp.concatenate([r1, r2, r3], axis=1), tri)
    counts = cnt[0, :N_EXPERTS].astype(jnp.int32)
    padded = (counts + MOE_ROWS - 1) // MOE_ROWS * MOE_ROWS
    pends = jnp.cumsum(padded)
    starts = pends - padded
    cap = (2 * t // MOE_ROWS + N_EXPERTS) * MOE_ROWS
    n_blk = cap // MOE_ROWS
    blk_start = jnp.arange(n_blk, dtype=jnp.int32) * MOE_ROWS
    blk_expert = jnp.minimum(jnp.sum(blk_start[:, None] >= pends[None, :], axis=1), N_EXPERTS - 1).astype(jnp.int32)
    n_used = (pends[-1:] // MOE_ROWS).astype(jnp.int32)
    start_row = jnp.zeros((1, LANES), F32).at[0, :N_EXPERTS].set(starts.astype(F32))
    dest = _slots(idx, rank, start_row)
    dest_t = dest[:, :2].reshape(t // ROW_BLOCK, ROW_BLOCK, 2).transpose(0, 2, 1).reshape(
        t // ROW_BLOCK, 1, 2 * ROW_BLOCK)
    xs = _dispatch(pends.astype(jnp.int32), padded.astype(jnp.int32), n_used, dest_t, xn, cap)
    yb = _experts(blk_expert, n_used, xs, w1, w2, layer)
    return _combine(dest_t, h, gate, out_nw, yb, final_norm)


def _tri(n):
    return jnp.tril(jnp.ones((n, n), F32)).astype(BF16)


def kernel(x, mem, mix_norm_w, w_in, fox_f_bias, fox_norm_w, hg_lb_raw, hg_norm_w, m2_conv_w, m2_conv_b, m2_dt_bias, m2_a_log, m2_d, m2_norm_w, w_out, xa_norm_w, xa_mem_norm_w, xa_w_q, xa_w_kv, xa_w_o, ffn_norm_w, ffn_w1, ffn_w2, moe_router, moe_w1, moe_w2, final_norm_w):
    batch, seq, d = x.shape
    depth = w_in.shape[0]
    t = batch * seq
    fox_w = N_FOX_HEADS * HEAD_DIM
    hg_w = N_HG_HEADS * HEAD_DIM
    m2_w = M2_HEADS * HEAD_DIM
    conv_ch = m2_w + 2 * M2_GROUPS * M2_STATE
    in_splits = (fox_w, fox_w, fox_w, N_FOX_HEADS, hg_w, hg_w, hg_w, hg_w, m2_w, conv_ch, M2_HEADS)
    offs = [0]
    for s in in_splits:
        offs.append(offs[-1] + s)

    lb_p = jax.nn.softmax(hg_lb_raw.astype(F32), axis=0)
    hg_lb = jnp.cumsum(lb_p, axis=0) - lb_p[0]

    tri_seq = _tri(SEQ_BLOCK)
    tri_chunk = _tri(SCAN_CHUNK)
    head_of_lane = jnp.arange(hg_w) // HEAD_DIM
    seg_ones = (head_of_lane[:, None] == head_of_lane[None, :]).astype(BF16)
    expand = (jnp.arange(LANES)[:, None] - SUBLANES == jnp.arange(m2_w)[None, :] // HEAD_DIM).astype(BF16)
    shift_row = jnp.arange((M2_CONV - 1) * SCAN_CHUNK)
    shift_src = SCAN_CHUNK + shift_row % SCAN_CHUNK - (shift_row // SCAN_CHUNK + 1)
    conv_shift = (jnp.arange(2 * SCAN_CHUNK)[None, :] == shift_src[:, None]).astype(BF16)

    def pad_lanes(v, offset=0, width=LANES):
        return jnp.zeros((1, width), F32).at[0, offset:offset + v.shape[0]].set(v)

    small_w = jnp.zeros((depth, d, LANES), F32)
    small_w = small_w.at[:, :, 0:N_FOX_HEADS].set(w_in[:, :, offs[3]:offs[4]])
    small_w = small_w.at[:, :, SUBLANES:SUBLANES + M2_HEADS].set(w_in[:, :, offs[10]:offs[11]])
    in_weights = tuple(w.astype(BF16) for w in (
        w_in[:, :, offs[0]:offs[3]], small_w, w_in[:, :, offs[4]:offs[8]], w_in[:, :, offs[8]:offs[9]],
        w_in[:, :, offs[9]:offs[10]]))
    in_dtypes = (BF16, F32, F32, F32, F32)
    w_out_b, xa_wq_b, xa_wkv_b, xa_wo_b, ffn_w1_b, ffn_w2_b, moe_w1_b, moe_w2_b = (
        w.astype(BF16) for w in (w_out, xa_w_q, xa_w_kv, xa_w_o, ffn_w1, ffn_w2, moe_w1, moe_w2))

    h = x.reshape(t, d)
    mem2 = mem.reshape(batch * mem.shape[1], d)
    for layer in range(depth):
        qkv, small, hg, z, xbc = _in_proj(h, mix_norm_w[layer][None, :], in_weights, layer, in_dtypes)

        qa, ka, va = _fox_prep(qkv, small, pad_lanes(fox_f_bias[layer]), tri_seq, batch, seq)
        fox_nw = jnp.zeros((SUBLANES, LANES), F32).at[:fox_w // LANES].set(
            fox_norm_w[layer].reshape(fox_w // LANES, LANES))
        o_fox = _fox_attn(qa, ka, va, fox_nw, batch, seq)

        lb = hg_lb[layer]
        hg_params = jnp.zeros((SUBLANES, hg_w), F32)
        hg_params = hg_params.at[0].set(jnp.log(jnp.maximum(lb, LB_FLOOR)))
        hg_params = hg_params.at[1].set(jnp.log1p(-lb))
        hg_params = hg_params.at[2].set(1.0 - lb)
        hg_params = hg_params.at[3].set(hg_norm_w[layer])
        o_hg = _hgrn2(hg, hg_params, tri_chunk, seg_ones, batch, seq)

        conv_w = jnp.zeros((SUBLANES, conv_ch), F32).at[:M2_CONV].set(m2_conv_w[layer])
        head_rows = jnp.zeros((SUBLANES, m2_w), F32).at[0].set(jnp.repeat(m2_d[layer], HEAD_DIM))
        small_rows = jnp.concatenate(
            [pad_lanes(m2_dt_bias[layer], SUBLANES), pad_lanes(m2_a_log[layer], SUBLANES),
             jnp.zeros((SUBLANES - 2, LANES), F32)], axis=0)
        o_m2 = _ssd(z, xbc, small, conv_w, m2_conv_b[layer][None, :], head_rows, small_rows,
                    m2_norm_w[layer][None, :], tri_chunk, expand, conv_shift, batch, seq)

        kv = _norm_proj(mem2, xa_mem_norm_w[layer][None, :], xa_wkv_b, layer, BF16, mem.shape[1])
        h = _xattn(h, o_fox, o_hg, o_m2, w_out_b, xa_norm_w[layer][None, :], xa_wq_b, kv, xa_wo_b, layer,
                   batch, seq)

        nw = ffn_norm_w[layer][None, :]
        if layer % 2 == 0:
            h = _ffn(h, nw, ffn_w1_b, ffn_w2_b, layer // 2)
        else:
            last = layer == depth - 1
            h = _moe(h, nw, moe_router[layer // 2], moe_w1_b, moe_w2_b, layer // 2, tri_seq,
                     final_norm_w[None, :], last)
    if depth % 2:
        h = _final_norm(h, final_norm_w[None, :])
    return h.reshape(batch, seq, d)
```

```python
import functools

import jax
import jax.numpy as jnp
from jax import lax
from jax.experimental import pallas as pl
from jax.experimental.pallas import tpu as pltpu

F32 = jnp.float32
BF16 = jnp.bfloat16

EPS = 1e-6
MASK_VALUE = -1e9
LB_FLOOR = 1e-30
HEAD_DIM = 64
N_FOX_HEADS = 4
N_HG_HEADS = 4
M2_HEADS = 8
M2_GROUPS = 2
M2_STATE = 128
M2_CONV = 4
XA_HEADS = 4
N_EXPERTS = 8

LANES = 128
SUBLANES = 8
VMEM_LIMIT_BYTES = 56 * 1024 * 1024

ROW_BLOCK = 512
SEQ_BLOCK = 512
SCAN_CHUNK = 128
SCAN_CHUNKS_PER_STEP = 2
SUB_BLOCK = 16
HG_SAFE_DECAY = 60.0
MOE_ROWS = 512
SLOT_ROWS = 4096


def _cparams(*sem):
    return pltpu.CompilerParams(dimension_semantics=sem, vmem_limit_bytes=VMEM_LIMIT_BYTES)


def _split3(x):
    x1 = x.astype(BF16)
    r1 = x - x1.astype(F32)
    x2 = r1.astype(BF16)
    x3 = (r1 - x2.astype(F32)).astype(BF16)
    return x1, x2, x3


def _dot(a, b):
    return jnp.dot(a, b, preferred_element_type=F32)


def _dot_nt(a, b):
    return lax.dot_general(a, b, (((1,), (1,)), ((), ())), preferred_element_type=F32)


def _exact_left(mat01, x):
    x1, x2, x3 = _split3(x)
    return _dot(mat01, x1) + _dot(mat01, x2) + _dot(mat01, x3)


def _exact_right(x, mat01):
    x1, x2, x3 = _split3(x)
    return _dot(x1, mat01) + _dot(x2, mat01) + _dot(x3, mat01)


def _rms(x, w):
    return x * lax.rsqrt(jnp.mean(x * x, axis=-1, keepdims=True) + EPS) * w


def _sigmoid(x):
    return 1.0 / (1.0 + jnp.exp(-x))


def _silu(x):
    return x * _sigmoid(x)


def _log1p_exp(x):
    return jnp.log(1.0 + jnp.exp(x))


def _log_sigmoid(x):
    return jnp.minimum(x, 0.0) - _log1p_exp(-jnp.abs(x))


def _softplus(x):
    return jnp.maximum(x, 0.0) + _log1p_exp(-jnp.abs(x))


def _masked_exp(x, mask):
    return jnp.where(mask, jnp.exp(jnp.where(mask, x, 0.0)), 0.0)


def _resident(a):
    zeros = (0,) * a.ndim
    return pl.BlockSpec(a.shape, lambda *_: zeros, pipeline_mode=pl.Buffered(1))


def _layer_resident(stacked, layer):
    index = (layer,) + (0,) * (stacked.ndim - 1)
    return pl.BlockSpec((None,) + stacked.shape[1:], lambda *_: index, pipeline_mode=pl.Buffered(1))


def _in_proj_kernel(x_ref, nw_ref, *refs):
    n = len(refs) // 2
    xn = _rms(x_ref[...], nw_ref[...]).astype(BF16)
    for w_ref, o_ref in zip(refs[:n], refs[n:]):
        o_ref[...] = _dot(xn, w_ref[...]).astype(o_ref.dtype)


def _in_proj(h, nw, weights, layer, out_dtypes):
    t, d = h.shape
    widths = [w.shape[-1] for w in weights]
    return pl.pallas_call(
        _in_proj_kernel,
        grid=(t // ROW_BLOCK,),
        in_specs=[pl.BlockSpec((ROW_BLOCK, d), lambda i: (i, 0)), _resident(nw)]
        + [_layer_resident(w, layer) for w in weights],
        out_specs=[pl.BlockSpec((ROW_BLOCK, w), lambda i: (i, 0)) for w in widths],
        out_shape=[jax.ShapeDtypeStruct((t, w), dt) for w, dt in zip(widths, out_dtypes)],
        compiler_params=_cparams("parallel"),
        name="in_proj",
    )(h, nw, *weights)


def _norm_proj_kernel(x_ref, nw_ref, w_ref, o_ref):
    xn = _rms(x_ref[...], nw_ref[...]).astype(BF16)
    o_ref[...] = _dot(xn, w_ref[...]).astype(o_ref.dtype)


def _norm_proj(x, nw, w, layer, out_dtype, rows):
    t, d = x.shape
    n = w.shape[-1]
    return pl.pallas_call(
        _norm_proj_kernel,
        grid=(t // rows,),
        in_specs=[pl.BlockSpec((rows, d), lambda i: (i, 0)), _resident(nw), _layer_resident(w, layer)],
        out_specs=pl.BlockSpec((rows, n), lambda i: (i, 0)),
        out_shape=jax.ShapeDtypeStruct((t, n), out_dtype),
        compiler_params=_cparams("parallel"),
        name="norm_proj",
    )(x, nw, w)


def _fox_prep_kernel(qkv_ref, small_ref, bias_ref, tri_ref, q_ref, k_ref, v_ref, carry_ref):
    @pl.when(pl.program_id(1) == 0)
    def _():
        carry_ref[...] = jnp.zeros_like(carry_ref)

    rows = qkv_ref.shape[0]
    log_f = _log_sigmoid(small_ref[...] + bias_ref[...])
    c = _exact_left(tri_ref[...], log_f) + carry_ref[...]
    carry_ref[...] = c[rows - 1:rows, :]

    lane = lax.broadcasted_iota(jnp.int32, (rows, LANES), 1)
    width = N_FOX_HEADS * HEAD_DIM
    scale = HEAD_DIM ** -0.5
    for hd in range(N_FOX_HEADS):
        tile = (hd * HEAD_DIM) // LANES
        ch = c[:, hd:hd + 1]
        c1 = ch.astype(BF16).astype(F32)
        r1 = ch - c1
        c2 = r1.astype(BF16).astype(F32)
        c3 = r1 - c2

        def head_tile(base):
            x = qkv_ref[:, base + tile * LANES: base + (tile + 1) * LANES].astype(F32)
            if (hd * HEAD_DIM) % LANES:
                x = pltpu.roll(x, LANES - (hd * HEAD_DIM) % LANES, axis=1)
            return x

        def augment(x, first, second):
            out = jnp.where(lane < HEAD_DIM, x, 0.0)
            for j, val in enumerate(first + second):
                out = jnp.where(lane == HEAD_DIM + j, val, out)
            return out.astype(BF16)

        ones = (1.0, 1.0, 1.0)
        q_ref[hd] = augment(head_tile(0) * scale, ones, (c1, c2, c3))
        k_ref[hd] = augment(head_tile(width), (-c1, -c2, -c3), ones)
        v_aug = jnp.where(lane < HEAD_DIM, head_tile(2 * width), jnp.where(lane == HEAD_DIM, 1.0, 0.0))
        v_ref[hd] = v_aug.T.astype(BF16)


def _fox_prep(qkv, small, bias_row, tri, batch, seq):
    t = qkv.shape[0]
    nb = seq // SEQ_BLOCK
    row = lambda b, c: (b * nb + c, 0)
    head_spec = pl.BlockSpec((N_FOX_HEADS, SEQ_BLOCK, LANES), lambda b, c: (0, b * nb + c, 0))
    head_shape = jax.ShapeDtypeStruct((N_FOX_HEADS, t, LANES), BF16)
    vt_spec = pl.BlockSpec((N_FOX_HEADS, LANES, SEQ_BLOCK), lambda b, c: (0, 0, b * nb + c))
    vt_shape = jax.ShapeDtypeStruct((N_FOX_HEADS, LANES, t), BF16)
    return pl.pallas_call(
        _fox_prep_kernel,
        grid=(batch, nb),
        in_specs=[
            pl.BlockSpec((SEQ_BLOCK, qkv.shape[1]), row),
            pl.BlockSpec((SEQ_BLOCK, LANES), row),
            pl.BlockSpec((1, LANES), lambda b, c: (0, 0)),
            pl.BlockSpec((SEQ_BLOCK, SEQ_BLOCK), lambda b, c: (0, 0)),
        ],
        out_specs=[head_spec, head_spec, vt_spec],
        out_shape=[head_shape, head_shape, vt_shape],
        scratch_shapes=[pltpu.VMEM((1, LANES), F32)],
        compiler_params=_cparams("parallel", "arbitrary"),
        name="fox_prep",
    )(qkv, small, bias_row, tri)


def _fox_attn_kernel(q_ref, k_ref, vt_ref, nw_ref, o_ref, m_ref, acc_ref, sa_ref, sb_ref):
    i = pl.program_id(1)
    tq = q_ref.shape[1]
    kv_idx = lax.broadcasted_iota(jnp.int32, (tq, tq), 0)
    q_idx = lax.broadcasted_iota(jnp.int32, (tq, tq), 1)
    heads = range(N_FOX_HEADS)
    m_ref[...] = jnp.full(m_ref.shape, -jnp.inf, F32)
    acc_ref[...] = jnp.zeros_like(acc_ref)

    def logits_into(dst_ref, j):
        start = pl.multiple_of(j * tq, tq)
        for hd in heads:
            dst_ref[hd] = _dot_nt(k_ref[hd, pl.ds(start, tq), :], q_ref[hd])

    def fold(src_ref, j, diagonal=False):
        start = pl.multiple_of(j * tq, tq)
        for hd in heads:
            s = src_ref[hd]
            if diagonal:
                s = jnp.where(kv_idx <= q_idx, s, MASK_VALUE)
            m_old = m_ref[hd]
            m_new = jnp.maximum(m_old, jnp.max(s, axis=0, keepdims=True))
            p = jnp.exp(s - m_new).astype(BF16)
            acc_ref[hd] = jnp.exp(m_old - m_new) * acc_ref[hd] + _dot(vt_ref[hd, :, pl.ds(start, tq)], p)
            m_ref[hd] = m_new

    logits_into(sa_ref, 0)

    def two_blocks(jj, carry):
        logits_into(sb_ref, 2 * jj + 1)
        fold(sa_ref, 2 * jj)
        logits_into(sa_ref, 2 * jj + 2)
        fold(sb_ref, 2 * jj + 1)
        return carry

    lax.fori_loop(0, i // 2, two_blocks, 0)

    @pl.when(i % 2 == 1)
    def _():
        logits_into(sb_ref, i)
        fold(sa_ref, i - 1)
        fold(sb_ref, i, diagonal=True)

    @pl.when(i % 2 == 0)
    def _():
        fold(sa_ref, i, diagonal=True)

    normed = []
    for hd in heads:
        acc = acc_ref[hd]
        o = acc[:HEAD_DIM] / acc[HEAD_DIM:HEAD_DIM + 1]
        ms = jnp.mean(o * o, axis=0, keepdims=True)
        normed.append(o * lax.rsqrt(ms + EPS))
    tiles = []
    for pair in range(N_FOX_HEADS // 2):
        both = jnp.concatenate(normed[2 * pair:2 * pair + 2], axis=0)
        tiles.append(both.T * nw_ref[pair:pair + 1, :])
    o_ref[...] = jnp.concatenate(tiles, axis=1).astype(o_ref.dtype)


def _fox_attn(qa, ka, vt, nw_pairs, batch, seq):
    t = qa.shape[1]
    nq = seq // SEQ_BLOCK
    return pl.pallas_call(
        _fox_attn_kernel,
        grid=(batch, nq),
        in_specs=[
            pl.BlockSpec((N_FOX_HEADS, SEQ_BLOCK, LANES), lambda b, i: (0, b * nq + i, 0)),
            pl.BlockSpec((N_FOX_HEADS, seq, LANES), lambda b, i: (0, b, 0)),
            pl.BlockSpec((N_FOX_HEADS, LANES, seq), lambda b, i: (0, 0, b)),
            pl.BlockSpec((SUBLANES, LANES), lambda b, i: (0, 0)),
        ],
        out_specs=pl.BlockSpec((SEQ_BLOCK, N_FOX_HEADS * HEAD_DIM), lambda b, i: (b * nq + i, 0)),
        out_shape=jax.ShapeDtypeStruct((t, N_FOX_HEADS * HEAD_DIM), BF16),
        scratch_shapes=[
            pltpu.VMEM((N_FOX_HEADS, 1, SEQ_BLOCK), F32),
            pltpu.VMEM((N_FOX_HEADS, LANES, SEQ_BLOCK), F32),
            pltpu.VMEM((N_FOX_HEADS, SEQ_BLOCK, SEQ_BLOCK), F32),
            pltpu.VMEM((N_FOX_HEADS, SEQ_BLOCK, SEQ_BLOCK), F32),
        ],
        compiler_params=_cparams("parallel", "arbitrary"),
        name="fox_attn",
    )(qa, ka, vt, nw_pairs)


def _chunk_views(refs, c):
    return [r.at[pl.ds(c * SCAN_CHUNK, SCAN_CHUNK)] for r in refs]


def _hgrn2_kernel(hq_ref, hf_ref, hi_ref, hg_ref, par_ref, tri_ref, seg_ref, o_ref, st_ref):
    @pl.when(pl.program_id(1) == 0)
    def _():
        st_ref[...] = jnp.zeros_like(st_ref)

    for c in range(SCAN_CHUNKS_PER_STEP):
        hq, hf, hi, hg, o = _chunk_views((hq_ref, hf_ref, hi_ref, hg_ref, o_ref), c)
        _hgrn2_chunk(hq, hf, hi, hg, par_ref, tri_ref, seg_ref, o, st_ref)


def _hgrn2_chunk(hq_ref, hf_ref, hi_ref, hg_ref, par_ref, tri_ref, seg_ref, o_ref, st_ref):
    rows, width = hq_ref.shape
    log_lb, log1m_lb, one_m_lb, nw = (par_ref[r:r + 1, :] for r in range(4))
    f_raw = hf_ref[...]
    q = _silu(hq_ref[...]) * (HEAD_DIM ** -0.5)
    b = log1m_lb + _log_sigmoid(f_raw)
    g = jnp.maximum(log_lb, b) + _log1p_exp(-jnp.abs(log_lb - b))
    k = one_m_lb * _sigmoid(-f_raw)
    v = hi_ref[...]
    cum = _exact_left(tri_ref[...], g)

    lane = lax.broadcasted_iota(jnp.int32, (1, width), 1)
    head_masks = [(lane >= hd * HEAD_DIM) & (lane < (hd + 1) * HEAD_DIM) for hd in range(N_HG_HEADS)]
    seg = seg_ref[...]
    v_b = v.astype(BF16)

    st = st_ref[...]
    o_state = _dot_nt((q * jnp.exp(cum)).astype(BF16), st.astype(BF16))

    n_sub = rows // SUB_BLOCK
    refs = [cum[i * SUB_BLOCK - 1:i * SUB_BLOCK] if i else jnp.zeros((1, width), F32) for i in range(n_sub)]
    local = [cum[i * SUB_BLOCK:(i + 1) * SUB_BLOCK] - refs[i] for i in range(n_sub)]

    def stack_heads(x):
        return jnp.concatenate([jnp.where(mk, x, 0.0) for mk in head_masks], axis=0).astype(BF16)

    def unstack_heads(base, p4):
        for hd, mk in enumerate(head_masks):
            base = base + jnp.where(mk, p4[hd * SUB_BLOCK:(hd + 1) * SUB_BLOCK], 0.0)
        return base

    def factored():
        scores = []
        for i in range(n_sub):
            hi = (i + 1) * SUB_BLOCK
            qs = q[hi - SUB_BLOCK:hi] * jnp.exp(local[i])
            ks = (k[:hi] * jnp.exp(refs[i] - cum[:hi])).astype(BF16)
            scores.append(_dot_nt(stack_heads(qs), ks))
        blocks = []
        for i in range(n_sub):
            hi = (i + 1) * SUB_BLOCK
            t_in = lax.broadcasted_iota(jnp.int32, (N_HG_HEADS * SUB_BLOCK, hi), 0) & (SUB_BLOCK - 1)
            s_in = lax.broadcasted_iota(jnp.int32, (N_HG_HEADS * SUB_BLOCK, hi), 1)
            sc = jnp.where(s_in <= t_in + (hi - SUB_BLOCK), scores[i], 0.0).astype(BF16)
            blocks.append(unstack_heads(o_state[hi - SUB_BLOCK:hi], _dot(sc, v_b[:hi])))
        return jnp.concatenate(blocks, axis=0)

    def pairwise():
        t_idx = lax.broadcasted_iota(jnp.int32, (SUB_BLOCK, 1), 0)
        blocks = []
        for i in range(n_sub):
            r0 = i * SUB_BLOCK
            q_i = q[r0:r0 + SUB_BLOCK]
            cum_i = cum[r0:r0 + SUB_BLOCK]
            o_i = o_state[r0:r0 + SUB_BLOCK]
            if i > 0:
                ks = (k[:r0] * jnp.exp(refs[i] - cum[:r0])).astype(BF16)
                sc = _dot_nt(stack_heads(q_i * jnp.exp(local[i])), ks)
                o_i = unstack_heads(o_i, _dot(sc.astype(BF16), v_b[:r0]))
            terms = []
            for s in range(SUB_BLOCK):
                keep = t_idx >= s
                e = _masked_exp(cum_i - cum[r0 + s:r0 + s + 1], keep)
                terms.append((q_i * e * k[r0 + s:r0 + s + 1]).astype(BF16))
            sums = _dot(jnp.concatenate(terms, axis=0), seg)
            for s in range(SUB_BLOCK):
                o_i = o_i + sums[s * SUB_BLOCK:(s + 1) * SUB_BLOCK] * v[r0 + s:r0 + s + 1]
            blocks.append(o_i)
        return jnp.concatenate(blocks, axis=0)

    worst = functools.reduce(jnp.minimum, local)
    o = lax.cond(jnp.min(worst) >= -HG_SAFE_DECAY, factored, pairwise)

    last = cum[rows - 1:rows]
    kd = (k * jnp.exp(last - cum)).astype(BF16)
    upd = _dot(v.T.astype(BF16), kd)
    st_ref[...] = st * jnp.exp(last) + jnp.where(seg > 0, upd, 0.0)

    sq = o * o
    s1 = sq.astype(BF16)
    s2 = (sq - s1.astype(F32)).astype(BF16)
    ms = (_dot(s1, seg) + _dot(s2, seg)) * (1.0 / HEAD_DIM)
    o_ref[...] = (o * lax.rsqrt(ms + EPS) * nw * _silu(hg_ref[...])).astype(o_ref.dtype)


def _hgrn2(hg, params, tri, seg, batch, seq):
    t = hg.shape[0]
    width = N_HG_HEADS * HEAD_DIM
    step_rows = SCAN_CHUNK * SCAN_CHUNKS_PER_STEP
    nc = seq // step_rows
    col = lambda c: pl.BlockSpec((step_rows, width), lambda b, j, c=c: (b * nc + j, c))
    return pl.pallas_call(
        _hgrn2_kernel,
        grid=(batch, nc),
        in_specs=[
            col(0), col(1), col(2), col(3),
            pl.BlockSpec(params.shape, lambda b, j: (0, 0)),
            pl.BlockSpec(tri.shape, lambda b, j: (0, 0)),
            pl.BlockSpec(seg.shape, lambda b, j: (0, 0)),
        ],
        out_specs=pl.BlockSpec((step_rows, width), lambda b, j: (b * nc + j, 0)),
        out_shape=jax.ShapeDtypeStruct((t, width), BF16),
        scratch_shapes=[pltpu.VMEM((width, width), F32)],
        compiler_params=_cparams("parallel", "arbitrary"),
        name="hgrn2",
    )(hg, hg, hg, hg, params, tri, seg)


def _ssd_kernel(z_ref, xbc_ref, small_ref, cw_ref, cb_ref, hp_ref, sp_ref, nw_ref, tri_ref, exp_ref, shift_ref,
                o_ref, buf_ref, st_ref):
    @pl.when(pl.program_id(1) == 0)
    def _():
        buf_ref[...] = jnp.zeros_like(buf_ref)
        st_ref[...] = jnp.zeros_like(st_ref)

    for c in range(SCAN_CHUNKS_PER_STEP):
        z, xbc, small, o = _chunk_views((z_ref, xbc_ref, small_ref, o_ref), c)
        _ssd_chunk(z, xbc, small, cw_ref, cb_ref, hp_ref, sp_ref, nw_ref, tri_ref, exp_ref, shift_ref, o,
                   buf_ref, st_ref)


def _ssd_chunk(z_ref, xbc_ref, small_ref, cw_ref, cb_ref, hp_ref, sp_ref, nw_ref, tri_ref, exp_ref, shift_ref,
               o_ref, buf_ref, st_ref):
    rows = z_ref.shape[0]
    width = z_ref.shape[1]

    x_in = xbc_ref[...]
    buf_ref[rows:2 * rows, :] = x_in.astype(BF16)
    shifted = _dot(shift_ref[...], buf_ref[...])
    buf_ref[0:rows, :] = buf_ref[rows:2 * rows, :]
    conv = cb_ref[...] + cw_ref[M2_CONV - 1:M2_CONV, :] * x_in
    for k in range(1, M2_CONV):
        tap = M2_CONV - 1 - k
        conv = conv + cw_ref[tap:tap + 1, :] * shifted[(k - 1) * rows:k * rows]
    xc = _silu(conv)
    xs = xc[:, :width]
    gn = M2_GROUPS * M2_STATE
    bm = xc[:, width:width + gn]
    cm = xc[:, width + gn:width + 2 * gn]

    d_x = hp_ref[0:1, :]
    dt_bias_c, a_log_c = (sp_ref[r:r + 1, :] for r in range(2))
    small = small_ref[...]
    tri = tri_ref[...]
    dt_c = _softplus(small + dt_bias_c)
    a_cum_c = _exact_left(tri, dt_c * -jnp.exp(a_log_c))
    a_cum_t = a_cum_c.T
    expand = exp_ref[...]
    dt_x = _exact_right(dt_c, expand)
    a_cum_x = _exact_right(a_cum_c, expand)

    row = lax.broadcasted_iota(jnp.int32, (rows, rows), 0)
    col = lax.broadcasted_iota(jnp.int32, (rows, rows), 1)
    causal = col <= row
    lane = lax.broadcasted_iota(jnp.int32, (1, LANES), 1)
    low = lane < HEAD_DIM

    xdt = xs * dt_x
    a_last = a_cum_x[rows - 1:rows]
    xd = (xdt * jnp.exp(a_last - a_cum_x)).astype(BF16)
    decay_out = jnp.exp(a_cum_x)
    st = st_ref[...]
    heads_per_group = M2_HEADS // M2_GROUPS
    gw = heads_per_group * HEAD_DIM
    cbs, y_offs, new_states = [], [], []
    for g in range(M2_GROUPS):
        b_g = bm[:, g * M2_STATE:(g + 1) * M2_STATE]
        c_g = cm[:, g * M2_STATE:(g + 1) * M2_STATE].astype(BF16)
        cbs.append(jnp.where(causal, _dot_nt(c_g, b_g.astype(BF16)), 0.0))
        y_offs.append(_dot(c_g, st[:, g * gw:(g + 1) * gw].astype(BF16)) * decay_out[:, g * gw:(g + 1) * gw])
        new_states.append(_dot(b_g.T.astype(BF16), xd[:, g * gw:(g + 1) * gw]))
    st_ref[...] = st * jnp.exp(a_last) + jnp.concatenate(new_states, axis=1)

    y_tiles = []
    for tile in range(M2_HEADS // 2):
        g = (2 * tile) // heads_per_group
        pair = tile - g * (heads_per_group // 2)
        x_tile = xdt[:, tile * LANES:(tile + 1) * LANES]
        y_pair = y_offs[g][:, pair * LANES:(pair + 1) * LANES]
        for half in range(2):
            hd = 2 * tile + half
            seg = a_cum_c[:, SUBLANES + hd:SUBLANES + hd + 1] - a_cum_t[SUBLANES + hd:SUBLANES + hd + 1, :]
            m_h = (cbs[g] * jnp.exp(jnp.minimum(seg, 0.0))).astype(BF16)
            x_h = jnp.where(low if half == 0 else ~low, x_tile, 0.0).astype(BF16)
            y_pair = y_pair + _dot(m_h, x_h)
        y_tiles.append(y_pair)

    y = jnp.concatenate(y_tiles, axis=1) + d_x * xs
    y = y * _silu(z_ref[...])
    nw = nw_ref[...]
    outs = []
    for g in range(M2_GROUPS):
        outs.append(_rms(y[:, g * gw:(g + 1) * gw], nw[:, g * gw:(g + 1) * gw]))
    o_ref[...] = jnp.concatenate(outs, axis=1).astype(o_ref.dtype)


def _ssd(z, xbc, small, conv_w, conv_b, head_rows, small_rows, norm_w, tri, expand, shift, batch, seq):
    t, width = z.shape
    ch = xbc.shape[1]
    step_rows = SCAN_CHUNK * SCAN_CHUNKS_PER_STEP
    nc = seq // step_rows
    row = lambda b, j: (b * nc + j, 0)
    const = lambda a: pl.BlockSpec(a.shape, lambda b, j: (0, 0))
    return pl.pallas_call(
        _ssd_kernel,
        grid=(batch, nc),
        in_specs=[
            pl.BlockSpec((step_rows, width), row),
            pl.BlockSpec((step_rows, ch), row),
            pl.BlockSpec((step_rows, LANES), row),
            const(conv_w), const(conv_b), const(head_rows), const(small_rows), const(norm_w),
            const(tri), const(expand), const(shift),
        ],
        out_specs=pl.BlockSpec((step_rows, width), row),
        out_shape=jax.ShapeDtypeStruct((t, width), BF16),
        scratch_shapes=[
            pltpu.VMEM((2 * SCAN_CHUNK, ch), BF16),
            pltpu.VMEM((M2_STATE, width), F32),
        ],
        compiler_params=_cparams("parallel", "arbitrary"),
        name="ssd",
    )(z, xbc, small, conv_w, conv_b, head_rows, small_rows, norm_w, tri, expand, shift)


def _xattn_kernel(h_ref, a_ref, b_ref, c_ref, wout_ref, nw_ref, wq_ref, kv_ref, wo_ref, o_ref):
    wa = a_ref.shape[1]
    wb = b_ref.shape[1]
    h = h_ref[...] + _dot(a_ref[...], wout_ref[0:wa, :])
    h = h + _dot(b_ref[...], wout_ref[wa:wa + wb, :])
    h = h + _dot(c_ref[...], wout_ref[wa + wb:, :])
    d = h.shape[1]
    hd = d // XA_HEADS
    q = (_dot(_rms(h, nw_ref[...]).astype(BF16), wq_ref[...]) * (hd ** -0.5)).astype(BF16)
    logits = [_dot_nt(q[:, a * hd:(a + 1) * hd], kv_ref[:, a * hd:(a + 1) * hd]) for a in range(XA_HEADS)]
    outs = []
    for a, s in enumerate(logits):
        p = jnp.exp(s - jnp.max(s, axis=-1, keepdims=True))
        p = p / jnp.sum(p, axis=-1, keepdims=True)
        outs.append(_dot(p.astype(BF16), kv_ref[:, d + a * hd:d + (a + 1) * hd]).astype(BF16))
    o_ref[...] = h + _dot(jnp.concatenate(outs, axis=1), wo_ref[...])


def _xattn(h, o_fox, o_hg, o_m2, w_out, nw, wq, kv, wo, layer, batch, seq):
    t, d = h.shape
    n_mem = kv.shape[0] // batch
    nb = seq // ROW_BLOCK
    row = lambda x: pl.BlockSpec((ROW_BLOCK, x.shape[1]), lambda b, i: (b * nb + i, 0))
    return pl.pallas_call(
        _xattn_kernel,
        grid=(batch, nb),
        in_specs=[
            row(h), row(o_fox), row(o_hg), row(o_m2),
            _layer_resident(w_out, layer), _resident(nw), _layer_resident(wq, layer),
            pl.BlockSpec((n_mem, kv.shape[1]), lambda b, i: (b, 0)),
            _layer_resident(wo, layer),
        ],
        out_specs=row(h),
        out_shape=jax.ShapeDtypeStruct((t, d), F32),
        compiler_params=_cparams("parallel", "parallel"),
        name="xattn",
    )(h, o_fox, o_hg, o_m2, w_out, nw, wq, kv, wo)


def _swiglu(x, w1_ref, w2_ref, d_ff):
    a = (_silu(_dot(x, w1_ref[:, :d_ff])) * _dot(x, w1_ref[:, d_ff:])).astype(BF16)
    return _dot(a, w2_ref[...])


def _ffn_kernel(h_ref, nw_ref, w1_ref, w2_ref, o_ref):
    h = h_ref[...]
    o_ref[...] = h + _swiglu(_rms(h, nw_ref[...]).astype(BF16), w1_ref, w2_ref, w2_ref.shape[0])


def _ffn(h, nw, w1, w2, layer):
    t, d = h.shape
    return pl.pallas_call(
        _ffn_kernel,
        grid=(t // ROW_BLOCK,),
        in_specs=[pl.BlockSpec((ROW_BLOCK, d), lambda i: (i, 0)), _resident(nw),
                  _layer_resident(w1, layer), _layer_resident(w2, layer)],
        out_specs=pl.BlockSpec((ROW_BLOCK, d), lambda i: (i, 0)),
        out_shape=jax.ShapeDtypeStruct((t, d), F32),
        compiler_params=_cparams("parallel"),
        name="ffn",
    )(h, nw, w1, w2)


def _router_kernel(h_ref, nw_ref, wr_ref, tri_ref, idx_ref, gate_ref, rank_ref, cnt_ref, carry_ref):
    @pl.when(pl.program_id(0) == 0)
    def _():
        carry_ref[...] = jnp.zeros_like(carry_ref)

    xn = _rms(h_ref[...], nw_ref[...])
    x1 = xn.astype(BF16)
    x2 = (xn - x1.astype(F32)).astype(BF16)
    w12 = wr_ref[:, :2 * LANES]
    big = _dot(x1, w12) + _dot(x2, w12)
    logits = big[:, :LANES] + big[:, LANES:] + _dot(x1, wr_ref[:, 2 * LANES:])
    lane = lax.broadcasted_iota(jnp.int32, logits.shape, 1)
    logits = jnp.where(lane < N_EXPERTS, logits, -jnp.inf)
    m1 = jnp.max(logits, axis=-1, keepdims=True)
    i1 = jnp.min(jnp.where(logits == m1, lane, LANES), axis=-1, keepdims=True)
    rest = jnp.where(lane == i1, -jnp.inf, logits)
    m2 = jnp.max(rest, axis=-1, keepdims=True)
    i2 = jnp.min(jnp.where(rest == m2, lane, LANES), axis=-1, keepdims=True)
    e2 = jnp.exp(m2 - m1)
    denom = 1.0 + e2
    idx_ref[...] = jnp.where(lane == 0, i1, jnp.where(lane == 1, i2, 0))
    gate_ref[...] = jnp.where(lane == 0, 1.0 / denom, jnp.where(lane == 1, e2 / denom, 0.0))

    hit1 = lane == i1
    hit2 = lane == i2
    member = jnp.where(hit1, 1.0, jnp.where(hit2, 1.0, 0.0))
    incl = _dot(tri_ref[...], member.astype(BF16)) + carry_ref[...]
    excl = incl - member
    r1 = jnp.sum(jnp.where(hit1, excl, 0.0), axis=-1, keepdims=True)
    r2 = jnp.sum(jnp.where(hit2, excl, 0.0), axis=-1, keepdims=True)
    rows = member.shape[0]
    carry_ref[...] = incl[rows - 1:rows, :]
    rank_ref[...] = jnp.where(lane == 0, r1, jnp.where(lane == 1, r2, 0.0)).astype(jnp.int32)
    cnt_ref[...] = jnp.broadcast_to(incl[rows - 1:rows, :], cnt_ref.shape)


def _router(h, nw, wr, tri):
    t, d = h.shape
    row = lambda w: pl.BlockSpec((ROW_BLOCK, w), lambda i: (i, 0))
    return pl.pallas_call(
        _router_kernel,
        grid=(t // ROW_BLOCK,),
        in_specs=[row(d), _resident(nw), _resident(wr), _resident(tri)],
        out_specs=[row(LANES), row(LANES), row(LANES), pl.BlockSpec((SUBLANES, LANES), lambda i: (0, 0))],
        out_shape=[
            jax.ShapeDtypeStruct((t, LANES), jnp.int32),
            jax.ShapeDtypeStruct((t, LANES), F32),
            jax.ShapeDtypeStruct((t, LANES), jnp.int32),
            jax.ShapeDtypeStruct((SUBLANES, LANES), F32),
        ],
        scratch_shapes=[pltpu.VMEM((1, LANES), F32)],
        compiler_params=_cparams("arbitrary"),
        name="router",
    )(h, nw, wr, tri)


def _slot_kernel(idx_ref, rank_ref, start_ref, o_ref):
    idx = idx_ref[...]
    lane = lax.broadcasted_iota(jnp.int32, idx.shape, 1)
    starts = start_ref[...]
    s1 = jnp.sum(jnp.where(lane == idx[:, 0:1], starts, 0.0), axis=-1, keepdims=True)
    s2 = jnp.sum(jnp.where(lane == idx[:, 1:2], starts, 0.0), axis=-1, keepdims=True)
    o_ref[...] = jnp.where(lane == 0, s1, jnp.where(lane == 1, s2, 0.0)).astype(jnp.int32) + rank_ref[...]


def _slots(idx, rank, start_row):
    t = idx.shape[0]
    rows = min(t, SLOT_ROWS)
    row = pl.BlockSpec((rows, LANES), lambda i: (i, 0))
    return pl.pallas_call(
        _slot_kernel,
        grid=(t // rows,),
        in_specs=[row, row, pl.BlockSpec((1, LANES), lambda i: (0, 0))],
        out_specs=row,
        out_shape=jax.ShapeDtypeStruct((t, LANES), jnp.int32),
        compiler_params=_cparams("parallel"),
        name="moe_slots",
    )(idx, rank, start_row)


def _dispatch_kernel(pend_ref, padded_ref, nused_ref, dest_ref, h_ref, nw_ref, xs_ref, x_ref, zero_ref, sem):
    rows = x_ref.shape[0]
    n_blk = xs_ref.shape[0] // MOE_ROWS
    x_ref[...] = _rms(h_ref[...], nw_ref[...])

    @pl.when(pl.program_id(0) == 0)
    def _():
        zero_ref[...] = jnp.zeros_like(zero_ref)

        def fill_block(start):
            fill = pltpu.make_async_copy(zero_ref, xs_ref.at[pl.ds(start, MOE_ROWS), :], sem)
            fill.start()
            fill.wait()

        for e in range(N_EXPERTS):
            @pl.when(padded_ref[e] > 0)
            def _():
                fill_block(pl.multiple_of(pend_ref[e] - MOE_ROWS, MOE_ROWS))

            @pl.when(nused_ref[0] + e < n_blk)
            def _():
                fill_block(pl.multiple_of((nused_ref[0] + e) * MOE_ROWS, MOE_ROWS))

    def issue(group, carry):
        base = pl.multiple_of(group * SUBLANES, SUBLANES)
        for s in range(SUBLANES):
            for k in range(2):
                slot = dest_ref[0, 0, k * rows + base + s]
                pltpu.make_async_copy(
                    x_ref.at[pl.ds(base + s, 1), :], xs_ref.at[pl.ds(slot, 1), :], sem).start(priority=k)
        return carry

    lax.fori_loop(0, rows // SUBLANES, issue, 0)
    for k in range(2):
        pltpu.make_async_copy(x_ref, xs_ref.at[pl.ds(0, rows), :], sem).wait()


def _dispatch(pends, padded, n_used, dest_t, h, nw, cap):
    t, d = h.shape
    grid_spec = pltpu.PrefetchScalarGridSpec(
        num_scalar_prefetch=3,
        grid=(t // ROW_BLOCK,),
        in_specs=[
            pl.BlockSpec((1, 1, 2 * ROW_BLOCK), lambda i, pe, pa, nu: (i, 0, 0), memory_space=pltpu.SMEM),
            pl.BlockSpec((ROW_BLOCK, d), lambda i, pe, pa, nu: (i, 0)),
            pl.BlockSpec((1, d), lambda i, pe, pa, nu: (0, 0)),
        ],
        out_specs=pl.BlockSpec(memory_space=pl.ANY),
        scratch_shapes=[pltpu.VMEM((ROW_BLOCK, d), F32), pltpu.VMEM((MOE_ROWS, d), F32),
                        pltpu.SemaphoreType.DMA(())],
    )
    return pl.pallas_call(
        _dispatch_kernel,
        grid_spec=grid_spec,
        out_shape=jax.ShapeDtypeStruct((cap, d), F32),
        compiler_params=_cparams("arbitrary"),
        name="moe_dispatch",
    )(pends, padded, n_used, dest_t, h, nw)


def _expert_kernel(be_ref, nused_ref, x_ref, w1_ref, w2_ref, o_ref):
    live = pl.program_id(0) < nused_ref[0]

    @pl.when(live)
    def _():
        o_ref[...] = _swiglu(x_ref[...].astype(BF16), w1_ref, w2_ref, w2_ref.shape[0])

    @pl.when(jnp.logical_not(live))
    def _():
        o_ref[...] = jnp.zeros_like(o_ref)


def _experts(blk_expert, n_used, xs, w1, w2, layer):
    cap, d = xs.shape
    blk = lambda i, be, nu: (i, 0)
    expert = lambda w: pl.BlockSpec((None, None) + w.shape[2:], lambda i, be, nu: (layer, be[i], 0, 0),
                                    pipeline_mode=pl.Buffered(1))
    grid_spec = pltpu.PrefetchScalarGridSpec(
        num_scalar_prefetch=2,
        grid=(cap // MOE_ROWS,),
        in_specs=[pl.BlockSpec((MOE_ROWS, d), blk), expert(w1), expert(w2)],
        out_specs=pl.BlockSpec((MOE_ROWS, d), blk),
    )
    return pl.pallas_call(
        _expert_kernel,
        grid_spec=grid_spec,
        out_shape=jax.ShapeDtypeStruct((cap, d), F32),
        compiler_params=_cparams("arbitrary"),
        name="experts",
    )(blk_expert, n_used, xs, w1, w2)


def _combine_kernel(dest_ref, h_ref, gate_ref, nw_ref, yb_ref, o_ref, buf_ref, sem, *, final_norm):
    rows = h_ref.shape[0]

    def issue(group, carry):
        base = pl.multiple_of(group * SUBLANES, SUBLANES)
        for s in range(SUBLANES):
            for k in range(2):
                slot = dest_ref[0, 0, k * rows + base + s]
                pltpu.make_async_copy(
                    yb_ref.at[pl.ds(slot, 1), :], buf_ref.at[k, pl.ds(base + s, 1), :], sem).start(priority=k)
        return carry

    lax.fori_loop(0, rows // SUBLANES, issue, 0)
    for k in range(2):
        pltpu.make_async_copy(yb_ref.at[pl.ds(0, rows), :], buf_ref.at[k], sem).wait()
    gate = gate_ref[...]
    out = h_ref[...] + gate[:, 0:1] * buf_ref[0] + gate[:, 1:2] * buf_ref[1]
    if final_norm:
        out = _rms(out, nw_ref[...])
    o_ref[...] = out


def _combine(dest_t, h, gate, nw, yb, final_norm):
    t, d = h.shape
    return pl.pallas_call(
        functools.partial(_combine_kernel, final_norm=final_norm),
        grid=(t // ROW_BLOCK,),
        in_specs=[
            pl.BlockSpec((1, 1, 2 * ROW_BLOCK), lambda i: (i, 0, 0), memory_space=pltpu.SMEM),
            pl.BlockSpec((ROW_BLOCK, d), lambda i: (i, 0)),
            pl.BlockSpec((ROW_BLOCK, LANES), lambda i: (i, 0)),
            pl.BlockSpec((1, d), lambda i: (0, 0)),
            pl.BlockSpec(memory_space=pl.ANY),
        ],
        out_specs=pl.BlockSpec((ROW_BLOCK, d), lambda i: (i, 0)),
        out_shape=jax.ShapeDtypeStruct((t, d), F32),
        scratch_shapes=[pltpu.VMEM((2, ROW_BLOCK, d), F32), pltpu.SemaphoreType.DMA(())],
        compiler_params=_cparams("arbitrary"),
        name="moe_combine",
    )(dest_t, h, gate, nw, yb)


def _final_norm_kernel(h_ref, nw_ref, o_ref):
    o_ref[...] = _rms(h_ref[...], nw_ref[...])


def _final_norm(h, nw):
    t, d = h.shape
    return pl.pallas_call(
        _final_norm_kernel,
        grid=(t // ROW_BLOCK,),
        in_specs=[pl.BlockSpec((ROW_BLOCK, d), lambda i: (i, 0)), pl.BlockSpec((1, d), lambda i: (0, 0))],
        out_specs=pl.BlockSpec((ROW_BLOCK, d), lambda i: (i, 0)),
        out_shape=jax.ShapeDtypeStruct((t, d), F32),
        compiler_params=_cparams("parallel"),
        name="final_norm",
    )(h, nw)


def _moe(h, nw, router_w, w1, w2, layer, tri, out_nw, final_norm):
    t, d = h.shape
    assert MOE_ROWS == ROW_BLOCK
    r1, r2, r3 = _split3(jnp.pad(router_w, ((0, 0), (0, LANES - N_EXPERTS))))
    idx, gate, rank, cnt = _router(h, nw, jnp.concatenate([r1, r2, r3], axis=1), tri)
    counts = cnt[0, :N_EXPERTS].astype(jnp.int32)
    padded = (counts + MOE_ROWS - 1) // MOE_ROWS * MOE_ROWS
    pends = jnp.cumsum(padded)
    starts = pends - padded
    cap = (2 * t // MOE_ROWS + N_EXPERTS) * MOE_ROWS
    n_blk = cap // MOE_ROWS
    blk_start = jnp.arange(n_blk, dtype=jnp.int32) * MOE_ROWS
    blk_expert = jnp.minimum(jnp.sum(blk_start[:, None] >= pends[None, :], axis=1), N_EXPERTS - 1).astype(jnp.int32)
    n_used = (pends[-1:] // MOE_ROWS).astype(jnp.int32)
    start_row = jnp.zeros((1, LANES), F32).at[0, :N_EXPERTS].set(starts.astype(F32))
    dest = _slots(idx, rank, start_row)
    dest_t = dest[:, :2].reshape(t // ROW_BLOCK, ROW_BLOCK, 2).transpose(0, 2, 1).reshape(
        t // ROW_BLOCK, 1, 2 * ROW_BLOCK)
    xs = _dispatch(pends.astype(jnp.int32), padded.astype(jnp.int32), n_used, dest_t, h, nw, cap)
    yb = _experts(blk_expert, n_used, xs, w1, w2, layer)
    return _combine(dest_t, h, gate, out_nw, yb, final_norm)


def _tri(n):
    return jnp.tril(jnp.ones((n, n), F32)).astype(BF16)


def kernel(x, mem, mix_norm_w, w_in, fox_f_bias, fox_norm_w, hg_lb_raw, hg_norm_w, m2_conv_w, m2_conv_b, m2_dt_bias, m2_a_log, m2_d, m2_norm_w, w_out, xa_norm_w, xa_mem_norm_w, xa_w_q, xa_w_kv, xa_w_o, ffn_norm_w, ffn_w1, ffn_w2, moe_router, moe_w1, moe_w2, final_norm_w):
    batch, seq, d = x.shape
    depth = w_in.shape[0]
    t = batch * seq
    fox_w = N_FOX_HEADS * HEAD_DIM
    hg_w = N_HG_HEADS * HEAD_DIM
    m2_w = M2_HEADS * HEAD_DIM
    conv_ch = m2_w + 2 * M2_GROUPS * M2_STATE
    in_splits = (fox_w, fox_w, fox_w, N_FOX_HEADS, hg_w, hg_w, hg_w, hg_w, m2_w, conv_ch, M2_HEADS)
    offs = [0]
    for s in in_splits:
        offs.append(offs[-1] + s)

    lb_p = jax.nn.softmax(hg_lb_raw.astype(F32), axis=0)
    hg_lb = jnp.cumsum(lb_p, axis=0) - lb_p[0]

    tri_seq = _tri(SEQ_BLOCK)
    tri_chunk = _tri(SCAN_CHUNK)
    head_of_lane = jnp.arange(hg_w) // HEAD_DIM
    seg_ones = (head_of_lane[:, None] == head_of_lane[None, :]).astype(BF16)
    expand = (jnp.arange(LANES)[:, None] - SUBLANES == jnp.arange(m2_w)[None, :] // HEAD_DIM).astype(BF16)
    shift_row = jnp.arange((M2_CONV - 1) * SCAN_CHUNK)
    shift_src = SCAN_CHUNK + shift_row % SCAN_CHUNK - (shift_row // SCAN_CHUNK + 1)
    conv_shift = (jnp.arange(2 * SCAN_CHUNK)[None, :] == shift_src[:, None]).astype(BF16)

    def pad_lanes(v, offset=0, width=LANES):
        return jnp.zeros((1, width), F32).at[0, offset:offset + v.shape[0]].set(v)

    small_w = jnp.zeros((depth, d, LANES), F32)
    small_w = small_w.at[:, :, 0:N_FOX_HEADS].set(w_in[:, :, offs[3]:offs[4]])
    small_w = small_w.at[:, :, SUBLANES:SUBLANES + M2_HEADS].set(w_in[:, :, offs[10]:offs[11]])
    in_weights = tuple(w.astype(BF16) for w in (
        w_in[:, :, offs[0]:offs[3]], small_w, w_in[:, :, offs[4]:offs[8]], w_in[:, :, offs[8]:offs[9]],
        w_in[:, :, offs[9]:offs[10]]))
    in_dtypes = (BF16, F32, F32, F32, F32)
    w_out_b, xa_wq_b, xa_wkv_b, xa_wo_b, ffn_w1_b, ffn_w2_b, moe_w1_b, moe_w2_b = (
        w.astype(BF16) for w in (w_out, xa_w_q, xa_w_kv, xa_w_o, ffn_w1, ffn_w2, moe_w1, moe_w2))

    h = x.reshape(t, d)
    mem2 = mem.reshape(batch * mem.shape[1], d)
    for layer in range(depth):
        qkv, small, hg, z, xbc = _in_proj(h, mix_norm_w[layer][None, :], in_weights, layer, in_dtypes)

        qa, ka, va = _fox_prep(qkv, small, pad_lanes(fox_f_bias[layer]), tri_seq, batch, seq)
        fox_nw = jnp.zeros((SUBLANES, LANES), F32).at[:fox_w // LANES].set(
            fox_norm_w[layer].reshape(fox_w // LANES, LANES))
        o_fox = _fox_attn(qa, ka, va, fox_nw, batch, seq)

        lb = hg_lb[layer]
        hg_params = jnp.zeros((SUBLANES, hg_w), F32)
        hg_params = hg_params.at[0].set(jnp.log(jnp.maximum(lb, LB_FLOOR)))
        hg_params = hg_params.at[1].set(jnp.log1p(-lb))
        hg_params = hg_params.at[2].set(1.0 - lb)
        hg_params = hg_params.at[3].set(hg_norm_w[layer])
        o_hg = _hgrn2(hg, hg_params, tri_chunk, seg_ones, batch, seq)

        conv_w = jnp.zeros((SUBLANES, conv_ch), F32).at[:M2_CONV].set(m2_conv_w[layer])
        head_rows = jnp.zeros((SUBLANES, m2_w), F32).at[0].set(jnp.repeat(m2_d[layer], HEAD_DIM))
        small_rows = jnp.concatenate(
            [pad_lanes(m2_dt_bias[layer], SUBLANES), pad_lanes(m2_a_log[layer], SUBLANES),
             jnp.zeros((SUBLANES - 2, LANES), F32)], axis=0)
        o_m2 = _ssd(z, xbc, small, conv_w, m2_conv_b[layer][None, :], head_rows, small_rows,
                    m2_norm_w[layer][None, :], tri_chunk, expand, conv_shift, batch, seq)

        kv = _norm_proj(mem2, xa_mem_norm_w[layer][None, :], xa_wkv_b, layer, BF16, mem.shape[1])
        h = _xattn(h, o_fox, o_hg, o_m2, w_out_b, xa_norm_w[layer][None, :], xa_wq_b, kv, xa_wo_b, layer,
                   batch, seq)

        nw = ffn_norm_w[layer][None, :]
        if layer % 2 == 0:
            h = _ffn(h, nw, ffn_w1_b, ffn_w2_b, layer // 2)
        else:
            last = layer == depth - 1
            h = _moe(h, nw, moe_router[layer // 2], moe_w1_b, moe_w2_b, layer // 2, tri_seq,
                     final_norm_w[None, :], last)
    if depth % 2:
        h = _final_norm(h, final_norm_w[None, :])
    return h.reshape(batch, seq, d)
```

```python
import functools

import jax
import jax.numpy as jnp
from jax import lax
from jax.experimental import pallas as pl
from jax.experimental.pallas import tpu as pltpu

F32 = jnp.float32
BF16 = jnp.bfloat16

EPS = 1e-6
MASK_VALUE = -1e9
LB_FLOOR = 1e-30
HEAD_DIM = 64
N_FOX_HEADS = 4
N_HG_HEADS = 4
M2_HEADS = 8
M2_GROUPS = 2
M2_STATE = 128
M2_CONV = 4
XA_HEADS = 4
N_EXPERTS = 8

LANES = 128
SUBLANES = 8
VMEM_LIMIT_BYTES = 56 * 1024 * 1024

ROW_BLOCK = 512
SEQ_BLOCK = 512
SCAN_CHUNK = 128
SCAN_CHUNKS_PER_STEP = 2
SUB_BLOCK = 16
HG_SAFE_DECAY = 60.0
MOE_ROWS = 512
SLOT_ROWS = 4096


def _cparams(*sem):
    return pltpu.CompilerParams(dimension_semantics=sem, vmem_limit_bytes=VMEM_LIMIT_BYTES)


def _split3(x):
    x1 = x.astype(BF16)
    r1 = x - x1.astype(F32)
    x2 = r1.astype(BF16)
    x3 = (r1 - x2.astype(F32)).astype(BF16)
    return x1, x2, x3


def _dot(a, b):
    return jnp.dot(a, b, preferred_element_type=F32)


def _dot_nt(a, b):
    return lax.dot_general(a, b, (((1,), (1,)), ((), ())), preferred_element_type=F32)


def _exact_left(mat01, x):
    x1, x2, x3 = _split3(x)
    return _dot(mat01, x1) + _dot(mat01, x2) + _dot(mat01, x3)


def _exact_right(x, mat01):
    x1, x2, x3 = _split3(x)
    return _dot(x1, mat01) + _dot(x2, mat01) + _dot(x3, mat01)


def _rms(x, w):
    return x * lax.rsqrt(jnp.mean(x * x, axis=-1, keepdims=True) + EPS) * w


def _sigmoid(x):
    return 1.0 / (1.0 + jnp.exp(-x))


def _silu(x):
    return x * _sigmoid(x)


def _log1p_exp(x):
    return jnp.log(1.0 + jnp.exp(x))


def _log_sigmoid(x):
    return jnp.minimum(x, 0.0) - _log1p_exp(-jnp.abs(x))


def _softplus(x):
    return jnp.maximum(x, 0.0) + _log1p_exp(-jnp.abs(x))


def _masked_exp(x, mask):
    return jnp.where(mask, jnp.exp(jnp.where(mask, x, 0.0)), 0.0)


def _resident(a):
    zeros = (0,) * a.ndim
    return pl.BlockSpec(a.shape, lambda *_: zeros, pipeline_mode=pl.Buffered(1))


def _layer_resident(stacked, layer):
    index = (layer,) + (0,) * (stacked.ndim - 1)
    return pl.BlockSpec((None,) + stacked.shape[1:], lambda *_: index, pipeline_mode=pl.Buffered(1))


def _in_proj_kernel(x_ref, nw_ref, *refs):
    n = len(refs) // 2
    xn = _rms(x_ref[...], nw_ref[...]).astype(BF16)
    for w_ref, o_ref in zip(refs[:n], refs[n:]):
        o_ref[...] = _dot(xn, w_ref[...]).astype(o_ref.dtype)


def _in_proj(h, nw, weights, layer, out_dtypes):
    t, d = h.shape
    widths = [w.shape[-1] for w in weights]
    return pl.pallas_call(
        _in_proj_kernel,
        grid=(t // ROW_BLOCK,),
        in_specs=[pl.BlockSpec((ROW_BLOCK, d), lambda i: (i, 0)), _resident(nw)]
        + [_layer_resident(w, layer) for w in weights],
        out_specs=[pl.BlockSpec((ROW_BLOCK, w), lambda i: (i, 0)) for w in widths],
        out_shape=[jax.ShapeDtypeStruct((t, w), dt) for w, dt in zip(widths, out_dtypes)],
        compiler_params=_cparams("parallel"),
        name="in_proj",
    )(h, nw, *weights)


def _norm_proj_kernel(x_ref, nw_ref, w_ref, o_ref):
    xn = _rms(x_ref[...], nw_ref[...]).astype(BF16)
    o_ref[...] = _dot(xn, w_ref[...]).astype(o_ref.dtype)


def _norm_proj(x, nw, w, layer, out_dtype, rows):
    t, d = x.shape
    n = w.shape[-1]
    return pl.pallas_call(
        _norm_proj_kernel,
        grid=(t // rows,),
        in_specs=[pl.BlockSpec((rows, d), lambda i: (i, 0)), _resident(nw), _layer_resident(w, layer)],
        out_specs=pl.BlockSpec((rows, n), lambda i: (i, 0)),
        out_shape=jax.ShapeDtypeStruct((t, n), out_dtype),
        compiler_params=_cparams("parallel"),
        name="norm_proj",
    )(x, nw, w)


def _fox_prep_kernel(qkv_ref, small_ref, bias_ref, tri_ref, q_ref, k_ref, v_ref, carry_ref):
    @pl.when(pl.program_id(1) == 0)
    def _():
        carry_ref[...] = jnp.zeros_like(carry_ref)

    rows = qkv_ref.shape[0]
    log_f = _log_sigmoid(small_ref[...] + bias_ref[...])
    c = _exact_left(tri_ref[...], log_f) + carry_ref[...]
    carry_ref[...] = c[rows - 1:rows, :]

    lane = lax.broadcasted_iota(jnp.int32, (rows, LANES), 1)
    width = N_FOX_HEADS * HEAD_DIM
    scale = HEAD_DIM ** -0.5
    for hd in range(N_FOX_HEADS):
        tile = (hd * HEAD_DIM) // LANES
        ch = c[:, hd:hd + 1]
        c1 = ch.astype(BF16).astype(F32)
        r1 = ch - c1
        c2 = r1.astype(BF16).astype(F32)
        c3 = r1 - c2

        def head_tile(base):
            x = qkv_ref[:, base + tile * LANES: base + (tile + 1) * LANES].astype(F32)
            if (hd * HEAD_DIM) % LANES:
                x = pltpu.roll(x, LANES - (hd * HEAD_DIM) % LANES, axis=1)
            return x

        def augment(x, first, second):
            out = jnp.where(lane < HEAD_DIM, x, 0.0)
            for j, val in enumerate(first + second):
                out = jnp.where(lane == HEAD_DIM + j, val, out)
            return out.astype(BF16)

        ones = (1.0, 1.0, 1.0)
        q_ref[hd] = augment(head_tile(0) * scale, ones, (c1, c2, c3))
        k_ref[hd] = augment(head_tile(width), (-c1, -c2, -c3), ones)
        v_aug = jnp.where(lane < HEAD_DIM, head_tile(2 * width), jnp.where(lane == HEAD_DIM, 1.0, 0.0))
        v_ref[hd] = v_aug.T.astype(BF16)


def _fox_prep(qkv, small, bias_row, tri, batch, seq):
    t = qkv.shape[0]
    nb = seq // SEQ_BLOCK
    row = lambda b, c: (b * nb + c, 0)
    head_spec = pl.BlockSpec((N_FOX_HEADS, SEQ_BLOCK, LANES), lambda b, c: (0, b * nb + c, 0))
    head_shape = jax.ShapeDtypeStruct((N_FOX_HEADS, t, LANES), BF16)
    vt_spec = pl.BlockSpec((N_FOX_HEADS, LANES, SEQ_BLOCK), lambda b, c: (0, 0, b * nb + c))
    vt_shape = jax.ShapeDtypeStruct((N_FOX_HEADS, LANES, t), BF16)
    return pl.pallas_call(
        _fox_prep_kernel,
        grid=(batch, nb),
        in_specs=[
            pl.BlockSpec((SEQ_BLOCK, qkv.shape[1]), row),
            pl.BlockSpec((SEQ_BLOCK, LANES), row),
            pl.BlockSpec((1, LANES), lambda b, c: (0, 0)),
            pl.BlockSpec((SEQ_BLOCK, SEQ_BLOCK), lambda b, c: (0, 0)),
        ],
        out_specs=[head_spec, head_spec, vt_spec],
        out_shape=[head_shape, head_shape, vt_shape],
        scratch_shapes=[pltpu.VMEM((1, LANES), F32)],
        compiler_params=_cparams("parallel", "arbitrary"),
        name="fox_prep",
    )(qkv, small, bias_row, tri)


def _fox_attn_kernel(q_ref, k_ref, vt_ref, nw_ref, o_ref, m_ref, acc_ref, sa_ref, sb_ref):
    i = pl.program_id(1)
    tq = q_ref.shape[1]
    kv_idx = lax.broadcasted_iota(jnp.int32, (tq, tq), 0)
    q_idx = lax.broadcasted_iota(jnp.int32, (tq, tq), 1)
    heads = range(N_FOX_HEADS)
    m_ref[...] = jnp.full(m_ref.shape, -jnp.inf, F32)
    acc_ref[...] = jnp.zeros_like(acc_ref)

    def logits_into(dst_ref, j):
        start = pl.multiple_of(j * tq, tq)
        for hd in heads:
            dst_ref[hd] = _dot_nt(k_ref[hd, pl.ds(start, tq), :], q_ref[hd])

    def fold(src_ref, j, diagonal=False):
        start = pl.multiple_of(j * tq, tq)
        for hd in heads:
            s = src_ref[hd]
            if diagonal:
                s = jnp.where(kv_idx <= q_idx, s, MASK_VALUE)
            m_old = m_ref[hd]
            m_new = jnp.maximum(m_old, jnp.max(s, axis=0, keepdims=True))
            p = jnp.exp(s - m_new).astype(BF16)
            acc_ref[hd] = jnp.exp(m_old - m_new) * acc_ref[hd] + _dot(vt_ref[hd, :, pl.ds(start, tq)], p)
            m_ref[hd] = m_new

    logits_into(sa_ref, 0)

    def two_blocks(jj, carry):
        logits_into(sb_ref, 2 * jj + 1)
        fold(sa_ref, 2 * jj)
        logits_into(sa_ref, 2 * jj + 2)
        fold(sb_ref, 2 * jj + 1)
        return carry

    lax.fori_loop(0, i // 2, two_blocks, 0)

    @pl.when(i % 2 == 1)
    def _():
        logits_into(sb_ref, i)
        fold(sa_ref, i - 1)
        fold(sb_ref, i, diagonal=True)

    @pl.when(i % 2 == 0)
    def _():
        fold(sa_ref, i, diagonal=True)

    normed = []
    for hd in heads:
        acc = acc_ref[hd]
        o = acc[:HEAD_DIM] / acc[HEAD_DIM:HEAD_DIM + 1]
        ms = jnp.mean(o * o, axis=0, keepdims=True)
        normed.append(o * lax.rsqrt(ms + EPS))
    tiles = []
    for pair in range(N_FOX_HEADS // 2):
        both = jnp.concatenate(normed[2 * pair:2 * pair + 2], axis=0)
        tiles.append(both.T * nw_ref[pair:pair + 1, :])
    o_ref[...] = jnp.concatenate(tiles, axis=1).astype(o_ref.dtype)


def _fox_attn(qa, ka, vt, nw_pairs, batch, seq):
    t = qa.shape[1]
    nq = seq // SEQ_BLOCK
    return pl.pallas_call(
        _fox_attn_kernel,
        grid=(batch, nq),
        in_specs=[
            pl.BlockSpec((N_FOX_HEADS, SEQ_BLOCK, LANES), lambda b, i: (0, b * nq + i, 0)),
            pl.BlockSpec((N_FOX_HEADS, seq, LANES), lambda b, i: (0, b, 0)),
            pl.BlockSpec((N_FOX_HEADS, LANES, seq), lambda b, i: (0, 0, b)),
            pl.BlockSpec((SUBLANES, LANES), lambda b, i: (0, 0)),
        ],
        out_specs=pl.BlockSpec((SEQ_BLOCK, N_FOX_HEADS * HEAD_DIM), lambda b, i: (b * nq + i, 0)),
        out_shape=jax.ShapeDtypeStruct((t, N_FOX_HEADS * HEAD_DIM), BF16),
        scratch_shapes=[
            pltpu.VMEM((N_FOX_HEADS, 1, SEQ_BLOCK), F32),
            pltpu.VMEM((N_FOX_HEADS, LANES, SEQ_BLOCK), F32),
            pltpu.VMEM((N_FOX_HEADS, SEQ_BLOCK, SEQ_BLOCK), F32),
            pltpu.VMEM((N_FOX_HEADS, SEQ_BLOCK, SEQ_BLOCK), F32),
        ],
        compiler_params=_cparams("parallel", "arbitrary"),
        name="fox_attn",
    )(qa, ka, vt, nw_pairs)


def _chunk_views(refs, c):
    return [r.at[pl.ds(c * SCAN_CHUNK, SCAN_CHUNK)] for r in refs]


def _hgrn2_kernel(hq_ref, hf_ref, hi_ref, hg_ref, par_ref, tri_ref, seg_ref, o_ref, st_ref):
    @pl.when(pl.program_id(1) == 0)
    def _():
        st_ref[...] = jnp.zeros_like(st_ref)

    for c in range(SCAN_CHUNKS_PER_STEP):
        hq, hf, hi, hg, o = _chunk_views((hq_ref, hf_ref, hi_ref, hg_ref, o_ref), c)
        _hgrn2_chunk(hq, hf, hi, hg, par_ref, tri_ref, seg_ref, o, st_ref)


def _hgrn2_chunk(hq_ref, hf_ref, hi_ref, hg_ref, par_ref, tri_ref, seg_ref, o_ref, st_ref):
    rows, width = hq_ref.shape
    log_lb, log1m_lb, one_m_lb, nw = (par_ref[r:r + 1, :] for r in range(4))
    f_raw = hf_ref[...]
    q = _silu(hq_ref[...]) * (HEAD_DIM ** -0.5)
    b = log1m_lb + _log_sigmoid(f_raw)
    g = jnp.maximum(log_lb, b) + _log1p_exp(-jnp.abs(log_lb - b))
    k = one_m_lb * _sigmoid(-f_raw)
    v = hi_ref[...]
    cum = _exact_left(tri_ref[...], g)

    lane = lax.broadcasted_iota(jnp.int32, (1, width), 1)
    head_masks = [(lane >= hd * HEAD_DIM) & (lane < (hd + 1) * HEAD_DIM) for hd in range(N_HG_HEADS)]
    seg = seg_ref[...]
    v_b = v.astype(BF16)

    st = st_ref[...]
    o_state = _dot_nt((q * jnp.exp(cum)).astype(BF16), st.astype(BF16))

    n_sub = rows // SUB_BLOCK
    refs = [cum[i * SUB_BLOCK - 1:i * SUB_BLOCK] if i else jnp.zeros((1, width), F32) for i in range(n_sub)]
    local = [cum[i * SUB_BLOCK:(i + 1) * SUB_BLOCK] - refs[i] for i in range(n_sub)]

    def stack_heads(x):
        return jnp.concatenate([jnp.where(mk, x, 0.0) for mk in head_masks], axis=0).astype(BF16)

    def unstack_heads(base, p4):
        for hd, mk in enumerate(head_masks):
            base = base + jnp.where(mk, p4[hd * SUB_BLOCK:(hd + 1) * SUB_BLOCK], 0.0)
        return base

    def factored():
        scores = []
        for i in range(n_sub):
            hi = (i + 1) * SUB_BLOCK
            qs = q[hi - SUB_BLOCK:hi] * jnp.exp(local[i])
            ks = (k[:hi] * jnp.exp(refs[i] - cum[:hi])).astype(BF16)
            scores.append(_dot_nt(stack_heads(qs), ks))
        blocks = []
        for i in range(n_sub):
            hi = (i + 1) * SUB_BLOCK
            t_in = lax.broadcasted_iota(jnp.int32, (N_HG_HEADS * SUB_BLOCK, hi), 0) & (SUB_BLOCK - 1)
            s_in = lax.broadcasted_iota(jnp.int32, (N_HG_HEADS * SUB_BLOCK, hi), 1)
            sc = jnp.where(s_in <= t_in + (hi - SUB_BLOCK), scores[i], 0.0).astype(BF16)
            blocks.append(unstack_heads(o_state[hi - SUB_BLOCK:hi], _dot(sc, v_b[:hi])))
        return jnp.concatenate(blocks, axis=0)

    def pairwise():
        t_idx = lax.broadcasted_iota(jnp.int32, (SUB_BLOCK, 1), 0)
        blocks = []
        for i in range(n_sub):
            r0 = i * SUB_BLOCK
            q_i = q[r0:r0 + SUB_BLOCK]
            cum_i = cum[r0:r0 + SUB_BLOCK]
            o_i = o_state[r0:r0 + SUB_BLOCK]
            if i > 0:
                ks = (k[:r0] * jnp.exp(refs[i] - cum[:r0])).astype(BF16)
                sc = _dot_nt(stack_heads(q_i * jnp.exp(local[i])), ks)
                o_i = unstack_heads(o_i, _dot(sc.astype(BF16), v_b[:r0]))
            terms = []
            for s in range(SUB_BLOCK):
                keep = t_idx >= s
                e = _masked_exp(cum_i - cum[r0 + s:r0 + s + 1], keep)
                terms.append((q_i * e * k[r0 + s:r0 + s + 1]).astype(BF16))
            sums = _dot(jnp.concatenate(terms, axis=0), seg)
            for s in range(SUB_BLOCK):
                o_i = o_i + sums[s * SUB_BLOCK:(s + 1) * SUB_BLOCK] * v[r0 + s:r0 + s + 1]
            blocks.append(o_i)
        return jnp.concatenate(blocks, axis=0)

    worst = functools.reduce(jnp.minimum, local)
    o = lax.cond(jnp.min(worst) >= -HG_SAFE_DECAY, factored, pairwise)

    last = cum[rows - 1:rows]
    kd = (k * jnp.exp(last - cum)).astype(BF16)
    upd = _dot(v.T.astype(BF16), kd)
    st_ref[...] = st * jnp.exp(last) + jnp.where(seg > 0, upd, 0.0)

    sq = o * o
    s1 = sq.astype(BF16)
    s2 = (sq - s1.astype(F32)).astype(BF16)
    ms = (_dot(s1, seg) + _dot(s2, seg)) * (1.0 / HEAD_DIM)
    o_ref[...] = (o * lax.rsqrt(ms + EPS) * nw * _silu(hg_ref[...])).astype(o_ref.dtype)


def _hgrn2(hg, params, tri, seg, batch, seq):
    t = hg.shape[0]
    width = N_HG_HEADS * HEAD_DIM
    step_rows = SCAN_CHUNK * SCAN_CHUNKS_PER_STEP
    nc = seq // step_rows
    col = lambda c: pl.BlockSpec((step_rows, width), lambda b, j, c=c: (b * nc + j, c))
    return pl.pallas_call(
        _hgrn2_kernel,
        grid=(batch, nc),
        in_specs=[
            col(0), col(1), col(2), col(3),
            pl.BlockSpec(params.shape, lambda b, j: (0, 0)),
            pl.BlockSpec(tri.shape, lambda b, j: (0, 0)),
            pl.BlockSpec(seg.shape, lambda b, j: (0, 0)),
        ],
        out_specs=pl.BlockSpec((step_rows, width), lambda b, j: (b * nc + j, 0)),
        out_shape=jax.ShapeDtypeStruct((t, width), BF16),
        scratch_shapes=[pltpu.VMEM((width, width), F32)],
        compiler_params=_cparams("parallel", "arbitrary"),
        name="hgrn2",
    )(hg, hg, hg, hg, params, tri, seg)


def _ssd_kernel(z_ref, xbc_ref, small_ref, cw_ref, cb_ref, hp_ref, sp_ref, nw_ref, tri_ref, exp_ref, shift_ref,
                o_ref, buf_ref, st_ref):
    @pl.when(pl.program_id(1) == 0)
    def _():
        buf_ref[...] = jnp.zeros_like(buf_ref)
        st_ref[...] = jnp.zeros_like(st_ref)

    for c in range(SCAN_CHUNKS_PER_STEP):
        z, xbc, small, o = _chunk_views((z_ref, xbc_ref, small_ref, o_ref), c)
        _ssd_chunk(z, xbc, small, cw_ref, cb_ref, hp_ref, sp_ref, nw_ref, tri_ref, exp_ref, shift_ref, o,
                   buf_ref, st_ref)


def _ssd_chunk(z_ref, xbc_ref, small_ref, cw_ref, cb_ref, hp_ref, sp_ref, nw_ref, tri_ref, exp_ref, shift_ref,
               o_ref, buf_ref, st_ref):
    rows = z_ref.shape[0]
    width = z_ref.shape[1]

    x_in = xbc_ref[...]
    buf_ref[rows:2 * rows, :] = x_in.astype(BF16)
    shifted = _dot(shift_ref[...], buf_ref[...])
    buf_ref[0:rows, :] = buf_ref[rows:2 * rows, :]
    conv = cb_ref[...] + cw_ref[M2_CONV - 1:M2_CONV, :] * x_in
    for k in range(1, M2_CONV):
        tap = M2_CONV - 1 - k
        conv = conv + cw_ref[tap:tap + 1, :] * shifted[(k - 1) * rows:k * rows]
    xc = _silu(conv)
    xs = xc[:, :width]
    gn = M2_GROUPS * M2_STATE
    bm = xc[:, width:width + gn]
    cm = xc[:, width + gn:width + 2 * gn]

    d_x = hp_ref[0:1, :]
    dt_bias_c, a_log_c = (sp_ref[r:r + 1, :] for r in range(2))
    small = small_ref[...]
    tri = tri_ref[...]
    dt_c = _softplus(small + dt_bias_c)
    a_cum_c = _exact_left(tri, dt_c * -jnp.exp(a_log_c))
    a_cum_t = a_cum_c.T
    expand = exp_ref[...]
    dt_x = _exact_right(dt_c, expand)
    a_cum_x = _exact_right(a_cum_c, expand)

    row = lax.broadcasted_iota(jnp.int32, (rows, rows), 0)
    col = lax.broadcasted_iota(jnp.int32, (rows, rows), 1)
    causal = col <= row
    lane = lax.broadcasted_iota(jnp.int32, (1, LANES), 1)
    low = lane < HEAD_DIM

    xdt = xs * dt_x
    a_last = a_cum_x[rows - 1:rows]
    xd = (xdt * jnp.exp(a_last - a_cum_x)).astype(BF16)
    decay_out = jnp.exp(a_cum_x)
    st = st_ref[...]
    heads_per_group = M2_HEADS // M2_GROUPS
    gw = heads_per_group * HEAD_DIM
    cbs, y_offs, new_states = [], [], []
    for g in range(M2_GROUPS):
        b_g = bm[:, g * M2_STATE:(g + 1) * M2_STATE]
        c_g = cm[:, g * M2_STATE:(g + 1) * M2_STATE].astype(BF16)
        cbs.append(jnp.where(causal, _dot_nt(c_g, b_g.astype(BF16)), 0.0))
        y_offs.append(_dot(c_g, st[:, g * gw:(g + 1) * gw].astype(BF16)) * decay_out[:, g * gw:(g + 1) * gw])
        new_states.append(_dot(b_g.T.astype(BF16), xd[:, g * gw:(g + 1) * gw]))
    st_ref[...] = st * jnp.exp(a_last) + jnp.concatenate(new_states, axis=1)

    y_tiles = []
    for tile in range(M2_HEADS // 2):
        g = (2 * tile) // heads_per_group
        pair = tile - g * (heads_per_group // 2)
        x_tile = xdt[:, tile * LANES:(tile + 1) * LANES]
        y_pair = y_offs[g][:, pair * LANES:(pair + 1) * LANES]
        for half in range(2):
            hd = 2 * tile + half
            seg = a_cum_c[:, SUBLANES + hd:SUBLANES + hd + 1] - a_cum_t[SUBLANES + hd:SUBLANES + hd + 1, :]
            m_h = (cbs[g] * jnp.exp(jnp.minimum(seg, 0.0))).astype(BF16)
            x_h = jnp.where(low if half == 0 else ~low, x_tile, 0.0).astype(BF16)
            y_pair = y_pair + _dot(m_h, x_h)
        y_tiles.append(y_pair)

    y = jnp.concatenate(y_tiles, axis=1) + d_x * xs
    y = y * _silu(z_ref[...])
    nw = nw_ref[...]
    outs = []
    for g in range(M2_GROUPS):
        outs.append(_rms(y[:, g * gw:(g + 1) * gw], nw[:, g * gw:(g + 1) * gw]))
    o_ref[...] = jnp.concatenate(outs, axis=1).astype(o_ref.dtype)


def _ssd(z, xbc, small, conv_w, conv_b, head_rows, small_rows, norm_w, tri, expand, shift, batch, seq):
    t, width = z.shape
    ch = xbc.shape[1]
    step_rows = SCAN_CHUNK * SCAN_CHUNKS_PER_STEP
    nc = seq // step_rows
    row = lambda b, j: (b * nc + j, 0)
    const = lambda a: pl.BlockSpec(a.shape, lambda b, j: (0, 0))
    return pl.pallas_call(
        _ssd_kernel,
        grid=(batch, nc),
        in_specs=[
            pl.BlockSpec((step_rows, width), row),
            pl.BlockSpec((step_rows, ch), row),
            pl.BlockSpec((step_rows, LANES), row),
            const(conv_w), const(conv_b), const(head_rows), const(small_rows), const(norm_w),
            const(tri), const(expand), const(shift),
        ],
        out_specs=pl.BlockSpec((step_rows, width), row),
        out_shape=jax.ShapeDtypeStruct((t, width), BF16),
        scratch_shapes=[
            pltpu.VMEM((2 * SCAN_CHUNK, ch), BF16),
            pltpu.VMEM((M2_STATE, width), F32),
        ],
        compiler_params=_cparams("parallel", "arbitrary"),
        name="ssd",
    )(z, xbc, small, conv_w, conv_b, head_rows, small_rows, norm_w, tri, expand, shift)


def _xattn_kernel(h_ref, a_ref, b_ref, c_ref, wout_ref, nw_ref, wq_ref, kv_ref, wo_ref, o_ref):
    wa = a_ref.shape[1]
    wb = b_ref.shape[1]
    h = h_ref[...] + _dot(a_ref[...], wout_ref[0:wa, :])
    h = h + _dot(b_ref[...], wout_ref[wa:wa + wb, :])
    h = h + _dot(c_ref[...], wout_ref[wa + wb:, :])
    d = h.shape[1]
    hd = d // XA_HEADS
    q = (_dot(_rms(h, nw_ref[...]).astype(BF16), wq_ref[...]) * (hd ** -0.5)).astype(BF16)
    logits = [_dot_nt(q[:, a * hd:(a + 1) * hd], kv_ref[:, a * hd:(a + 1) * hd]) for a in range(XA_HEADS)]
    outs = []
    for a, s in enumerate(logits):
        p = jnp.exp(s - jnp.max(s, axis=-1, keepdims=True))
        p = p / jnp.sum(p, axis=-1, keepdims=True)
        outs.append(_dot(p.astype(BF16), kv_ref[:, d + a * hd:d + (a + 1) * hd]).astype(BF16))
    o_ref[...] = h + _dot(jnp.concatenate(outs, axis=1), wo_ref[...])


def _xattn(h, o_fox, o_hg, o_m2, w_out, nw, wq, kv, wo, layer, batch, seq):
    t, d = h.shape
    n_mem = kv.shape[0] // batch
    nb = seq // ROW_BLOCK
    row = lambda x: pl.BlockSpec((ROW_BLOCK, x.shape[1]), lambda b, i: (b * nb + i, 0))
    return pl.pallas_call(
        _xattn_kernel,
        grid=(batch, nb),
        in_specs=[
            row(h), row(o_fox), row(o_hg), row(o_m2),
            _layer_resident(w_out, layer), _resident(nw), _layer_resident(wq, layer),
            pl.BlockSpec((n_mem, kv.shape[1]), lambda b, i: (b, 0)),
            _layer_resident(wo, layer),
        ],
        out_specs=row(h),
        out_shape=jax.ShapeDtypeStruct((t, d), F32),
        compiler_params=_cparams("parallel", "parallel"),
        name="xattn",
    )(h, o_fox, o_hg, o_m2, w_out, nw, wq, kv, wo)


def _swiglu(x, w1_ref, w2_ref, d_ff):
    a = (_silu(_dot(x, w1_ref[:, :d_ff])) * _dot(x, w1_ref[:, d_ff:])).astype(BF16)
    return _dot(a, w2_ref[...])


def _ffn_kernel(h_ref, nw_ref, w1_ref, w2_ref, o_ref):
    h = h_ref[...]
    o_ref[...] = h + _swiglu(_rms(h, nw_ref[...]).astype(BF16), w1_ref, w2_ref, w2_ref.shape[0])


def _ffn(h, nw, w1, w2, layer):
    t, d = h.shape
    return pl.pallas_call(
        _ffn_kernel,
        grid=(t // ROW_BLOCK,),
        in_specs=[pl.BlockSpec((ROW_BLOCK, d), lambda i: (i, 0)), _resident(nw),
                  _layer_resident(w1, layer), _layer_resident(w2, layer)],
        out_specs=pl.BlockSpec((ROW_BLOCK, d), lambda i: (i, 0)),
        out_shape=jax.ShapeDtypeStruct((t, d), F32),
        compiler_params=_cparams("parallel"),
        name="ffn",
    )(h, nw, w1, w2)


def _router_kernel(h_ref, nw_ref, wr_ref, tri_ref, idx_ref, gate_ref, rank_ref, cnt_ref, carry_ref):
    @pl.when(pl.program_id(0) == 0)
    def _():
        carry_ref[...] = jnp.zeros_like(carry_ref)

    xn = _rms(h_ref[...], nw_ref[...])
    x1 = xn.astype(BF16)
    x2 = (xn - x1.astype(F32)).astype(BF16)
    w12 = wr_ref[:, :2 * LANES]
    big = _dot(x1, w12) + _dot(x2, w12)
    logits = big[:, :LANES] + big[:, LANES:] + _dot(x1, wr_ref[:, 2 * LANES:])
    lane = lax.broadcasted_iota(jnp.int32, logits.shape, 1)
    logits = jnp.where(lane < N_EXPERTS, logits, -jnp.inf)
    m1 = jnp.max(logits, axis=-1, keepdims=True)
    i1 = jnp.min(jnp.where(logits == m1, lane, LANES), axis=-1, keepdims=True)
    rest = jnp.where(lane == i1, -jnp.inf, logits)
    m2 = jnp.max(rest, axis=-1, keepdims=True)
    i2 = jnp.min(jnp.where(rest == m2, lane, LANES), axis=-1, keepdims=True)
    e2 = jnp.exp(m2 - m1)
    denom = 1.0 + e2
    idx_ref[...] = jnp.where(lane == 0, i1, jnp.where(lane == 1, i2, 0))
    gate_ref[...] = jnp.where(lane == 0, 1.0 / denom, jnp.where(lane == 1, e2 / denom, 0.0))

    hit1 = lane == i1
    hit2 = lane == i2
    member = jnp.where(hit1, 1.0, jnp.where(hit2, 1.0, 0.0))
    incl = _dot(tri_ref[...], member.astype(BF16)) + carry_ref[...]
    excl = incl - member
    r1 = jnp.sum(jnp.where(hit1, excl, 0.0), axis=-1, keepdims=True)
    r2 = jnp.sum(jnp.where(hit2, excl, 0.0), axis=-1, keepdims=True)
    rows = member.shape[0]
    carry_ref[...] = incl[rows - 1:rows, :]
    rank_ref[...] = jnp.where(lane == 0, r1, jnp.where(lane == 1, r2, 0.0)).astype(jnp.int32)
    cnt_ref[...] = jnp.broadcast_to(incl[rows - 1:rows, :], cnt_ref.shape)


def _router(h, nw, wr, tri):
    t, d = h.shape
    row = lambda w: pl.BlockSpec((ROW_BLOCK, w), lambda i: (i, 0))
    return pl.pallas_call(
        _router_kernel,
        grid=(t // ROW_BLOCK,),
        in_specs=[row(d), _resident(nw), _resident(wr), _resident(tri)],
        out_specs=[row(LANES), row(LANES), row(LANES), pl.BlockSpec((SUBLANES, LANES), lambda i: (0, 0))],
        out_shape=[
            jax.ShapeDtypeStruct((t, LANES), jnp.int32),
            jax.ShapeDtypeStruct((t, LANES), F32),
            jax.ShapeDtypeStruct((t, LANES), jnp.int32),
            jax.ShapeDtypeStruct((SUBLANES, LANES), F32),
        ],
        scratch_shapes=[pltpu.VMEM((1, LANES), F32)],
        compiler_params=_cparams("arbitrary"),
        name="router",
    )(h, nw, wr, tri)


def _slot_kernel(idx_ref, rank_ref, start_ref, o_ref):
    idx = idx_ref[...]
    lane = lax.broadcasted_iota(jnp.int32, idx.shape, 1)
    starts = start_ref[...]
    s1 = jnp.sum(jnp.where(lane == idx[:, 0:1], starts, 0.0), axis=-1, keepdims=True)
    s2 = jnp.sum(jnp.where(lane == idx[:, 1:2], starts, 0.0), axis=-1, keepdims=True)
    o_ref[...] = jnp.where(lane == 0, s1, jnp.where(lane == 1, s2, 0.0)).astype(jnp.int32) + rank_ref[...]


def _slots(idx, rank, start_row):
    t = idx.shape[0]
    rows = min(t, SLOT_ROWS)
    row = pl.BlockSpec((rows, LANES), lambda i: (i, 0))
    return pl.pallas_call(
        _slot_kernel,
        grid=(t // rows,),
        in_specs=[row, row, pl.BlockSpec((1, LANES), lambda i: (0, 0))],
        out_specs=row,
        out_shape=jax.ShapeDtypeStruct((t, LANES), jnp.int32),
        compiler_params=_cparams("parallel"),
        name="moe_slots",
    )(idx, rank, start_row)


def _dispatch_kernel(pend_ref, padded_ref, nused_ref, dest_ref, h_ref, nw_ref, xs_ref, x_ref, zero_ref, sem):
    rows = h_ref.shape[0]
    n_blk = xs_ref.shape[0] // MOE_ROWS
    x_ref[...] = _rms(h_ref[...], nw_ref[...]).reshape(x_ref.shape)

    @pl.when(pl.program_id(0) == 0)
    def _():
        zero_ref[...] = jnp.zeros_like(zero_ref)

        def fill_block(start):
            fill = pltpu.make_async_copy(zero_ref, xs_ref.at[pl.ds(start, MOE_ROWS), :], sem)
            fill.start()
            fill.wait()

        for e in range(N_EXPERTS):
            @pl.when(padded_ref[e] > 0)
            def _():
                fill_block(pl.multiple_of(pend_ref[e] - MOE_ROWS, MOE_ROWS))

            @pl.when(nused_ref[0] + e < n_blk)
            def _():
                fill_block(pl.multiple_of((nused_ref[0] + e) * MOE_ROWS, MOE_ROWS))

    def issue(group, carry):
        base = pl.multiple_of(group * SUBLANES, SUBLANES)
        for s in range(SUBLANES):
            for k in range(2):
                slot = dest_ref[0, 0, k * rows + base + s]
                pltpu.make_async_copy(
                    x_ref.at[group, pl.ds(s, 1), :], xs_ref.at[pl.ds(slot, 1), :], sem).start(priority=k)
        return carry

    lax.fori_loop(0, rows // SUBLANES, issue, 0)
    for k in range(2):
        pltpu.make_async_copy(zero_ref, xs_ref.at[pl.ds(0, rows), :], sem).wait()


def _dispatch(pends, padded, n_used, dest_t, h, nw, cap):
    t, d = h.shape
    grid_spec = pltpu.PrefetchScalarGridSpec(
        num_scalar_prefetch=3,
        grid=(t // ROW_BLOCK,),
        in_specs=[
            pl.BlockSpec((1, 1, 2 * ROW_BLOCK), lambda i, pe, pa, nu: (i, 0, 0), memory_space=pltpu.SMEM),
            pl.BlockSpec((ROW_BLOCK, d), lambda i, pe, pa, nu: (i, 0)),
            pl.BlockSpec((1, d), lambda i, pe, pa, nu: (0, 0)),
        ],
        out_specs=pl.BlockSpec(memory_space=pl.ANY),
        scratch_shapes=[pltpu.VMEM((ROW_BLOCK // SUBLANES, SUBLANES, d), F32), pltpu.VMEM((MOE_ROWS, d), F32),
                        pltpu.SemaphoreType.DMA(())],
    )
    return pl.pallas_call(
        _dispatch_kernel,
        grid_spec=grid_spec,
        out_shape=jax.ShapeDtypeStruct((cap, d), F32),
        compiler_params=_cparams("arbitrary"),
        name="moe_dispatch",
    )(pends, padded, n_used, dest_t, h, nw)


def _expert_kernel(be_ref, nused_ref, x_ref, w1_ref, w2_ref, o_ref):
    live = pl.program_id(0) < nused_ref[0]

    @pl.when(live)
    def _():
        o_ref[...] = _swiglu(x_ref[...].astype(BF16), w1_ref, w2_ref, w2_ref.shape[0])

    @pl.when(jnp.logical_not(live))
    def _():
        o_ref[...] = jnp.zeros_like(o_ref)


def _experts(blk_expert, n_used, xs, w1, w2, layer):
    cap, d = xs.shape
    blk = lambda i, be, nu: (i, 0)
    expert = lambda w: pl.BlockSpec((None, None) + w.shape[2:], lambda i, be, nu: (layer, be[i], 0, 0),
                                    pipeline_mode=pl.Buffered(1))
    grid_spec = pltpu.PrefetchScalarGridSpec(
        num_scalar_prefetch=2,
        grid=(cap // MOE_ROWS,),
        in_specs=[pl.BlockSpec((MOE_ROWS, d), blk), expert(w1), expert(w2)],
        out_specs=pl.BlockSpec((MOE_ROWS, d), blk),
    )
    return pl.pallas_call(
        _expert_kernel,
        grid_spec=grid_spec,
        out_shape=jax.ShapeDtypeStruct((cap, d), F32),
        compiler_params=_cparams("arbitrary"),
        name="experts",
    )(blk_expert, n_used, xs, w1, w2)


def _combine_kernel(dest_ref, h_ref, gate_ref, nw_ref, yb_ref, o_ref, buf_ref, sem, *, final_norm):
    rows = h_ref.shape[0]

    def issue(group, carry):
        base = pl.multiple_of(group * SUBLANES, SUBLANES)
        for s in range(SUBLANES):
            for k in range(2):
                slot = dest_ref[0, 0, k * rows + base + s]
                pltpu.make_async_copy(
                    yb_ref.at[pl.ds(slot, 1), :], buf_ref.at[k, group, pl.ds(s, 1), :], sem).start(priority=k)
        return carry

    lax.fori_loop(0, rows // SUBLANES, issue, 0)
    for k in range(2):
        pltpu.make_async_copy(buf_ref.at[k], buf_ref.at[k], sem).wait()
    gate = gate_ref[...]
    picked = [buf_ref[k].reshape(rows, h_ref.shape[1]) for k in range(2)]
    out = h_ref[...] + gate[:, 0:1] * picked[0] + gate[:, 1:2] * picked[1]
    if final_norm:
        out = _rms(out, nw_ref[...])
    o_ref[...] = out


def _combine(dest_t, h, gate, nw, yb, final_norm):
    t, d = h.shape
    return pl.pallas_call(
        functools.partial(_combine_kernel, final_norm=final_norm),
        grid=(t // ROW_BLOCK,),
        in_specs=[
            pl.BlockSpec((1, 1, 2 * ROW_BLOCK), lambda i: (i, 0, 0), memory_space=pltpu.SMEM),
            pl.BlockSpec((ROW_BLOCK, d), lambda i: (i, 0)),
            pl.BlockSpec((ROW_BLOCK, LANES), lambda i: (i, 0)),
            pl.BlockSpec((1, d), lambda i: (0, 0)),
            pl.BlockSpec(memory_space=pl.ANY),
        ],
        out_specs=pl.BlockSpec((ROW_BLOCK, d), lambda i: (i, 0)),
        out_shape=jax.ShapeDtypeStruct((t, d), F32),
        scratch_shapes=[pltpu.VMEM((2, ROW_BLOCK // SUBLANES, SUBLANES, d), F32), pltpu.SemaphoreType.DMA(())],
        compiler_params=_cparams("arbitrary"),
        name="moe_combine",
    )(dest_t, h, gate, nw, yb)


def _final_norm_kernel(h_ref, nw_ref, o_ref):
    o_ref[...] = _rms(h_ref[...], nw_ref[...])


def _final_norm(h, nw):
    t, d = h.shape
    return pl.pallas_call(
        _final_norm_kernel,
        grid=(t // ROW_BLOCK,),
        in_specs=[pl.BlockSpec((ROW_BLOCK, d), lambda i: (i, 0)), pl.BlockSpec((1, d), lambda i: (0, 0))],
        out_specs=pl.BlockSpec((ROW_BLOCK, d), lambda i: (i, 0)),
        out_shape=jax.ShapeDtypeStruct((t, d), F32),
        compiler_params=_cparams("parallel"),
        name="final_norm",
    )(h, nw)


def _moe(h, nw, router_w, w1, w2, layer, tri, out_nw, final_norm):
    t, d = h.shape
    assert MOE_ROWS == ROW_BLOCK
    r1, r2, r3 = _split3(jnp.pad(router_w, ((0, 0), (0, LANES - N_EXPERTS))))
    idx, gate, rank, cnt = _router(h, nw, jnp.concatenate([r1, r2, r3], axis=1), tri)
    counts = cnt[0, :N_EXPERTS].astype(jnp.int32)
    padded = (counts + MOE_ROWS - 1) // MOE_ROWS * MOE_ROWS
    pends = jnp.cumsum(padded)
    starts = pends - padded
    cap = (2 * t // MOE_ROWS + N_EXPERTS) * MOE_ROWS
    n_blk = cap // MOE_ROWS
    blk_start = jnp.arange(n_blk, dtype=jnp.int32) * MOE_ROWS
    blk_expert = jnp.minimum(jnp.sum(blk_start[:, None] >= pends[None, :], axis=1), N_EXPERTS - 1).astype(jnp.int32)
    n_used = (pends[-1:] // MOE_ROWS).astype(jnp.int32)
    start_row = jnp.zeros((1, LANES), F32).at[0, :N_EXPERTS].set(starts.astype(F32))
    dest = _slots(idx, rank, start_row)
    dest_t = dest[:, :2].reshape(t // ROW_BLOCK, ROW_BLOCK, 2).transpose(0, 2, 1).reshape(
        t // ROW_BLOCK, 1, 2 * ROW_BLOCK)
    xs = _dispatch(pends.astype(jnp.int32), padded.astype(jnp.int32), n_used, dest_t, h, nw, cap)
    yb = _experts(blk_expert, n_used, xs, w1, w2, layer)
    return _combine(dest_t, h, gate, out_nw, yb, final_norm)


def _tri(n):
    return jnp.tril(jnp.ones((n, n), F32)).astype(BF16)


def kernel(x, mem, mix_norm_w, w_in, fox_f_bias, fox_norm_w, hg_lb_raw, hg_norm_w, m2_conv_w, m2_conv_b, m2_dt_bias, m2_a_log, m2_d, m2_norm_w, w_out, xa_norm_w, xa_mem_norm_w, xa_w_q, xa_w_kv, xa_w_o, ffn_norm_w, ffn_w1, ffn_w2, moe_router, moe_w1, moe_w2, final_norm_w):
    batch, seq, d = x.shape
    depth = w_in.shape[0]
    t = batch * seq
    fox_w = N_FOX_HEADS * HEAD_DIM
    hg_w = N_HG_HEADS * HEAD_DIM
    m2_w = M2_HEADS * HEAD_DIM
    conv_ch = m2_w + 2 * M2_GROUPS * M2_STATE
    in_splits = (fox_w, fox_w, fox_w, N_FOX_HEADS, hg_w, hg_w, hg_w, hg_w, m2_w, conv_ch, M2_HEADS)
    offs = [0]
    for s in in_splits:
        offs.append(offs[-1] + s)

    lb_p = jax.nn.softmax(hg_lb_raw.astype(F32), axis=0)
    hg_lb = jnp.cumsum(lb_p, axis=0) - lb_p[0]

    tri_seq = _tri(SEQ_BLOCK)
    tri_chunk = _tri(SCAN_CHUNK)
    head_of_lane = jnp.arange(hg_w) // HEAD_DIM
    seg_ones = (head_of_lane[:, None] == head_of_lane[None, :]).astype(BF16)
    expand = (jnp.arange(LANES)[:, None] - SUBLANES == jnp.arange(m2_w)[None, :] // HEAD_DIM).astype(BF16)
    shift_row = jnp.arange((M2_CONV - 1) * SCAN_CHUNK)
    shift_src = SCAN_CHUNK + shift_row % SCAN_CHUNK - (shift_row // SCAN_CHUNK + 1)
    conv_shift = (jnp.arange(2 * SCAN_CHUNK)[None, :] == shift_src[:, None]).astype(BF16)

    def pad_lanes(v, offset=0, width=LANES):
        return jnp.zeros((1, width), F32).at[0, offset:offset + v.shape[0]].set(v)

    small_w = jnp.zeros((depth, d, LANES), F32)
    small_w = small_w.at[:, :, 0:N_FOX_HEADS].set(w_in[:, :, offs[3]:offs[4]])
    small_w = small_w.at[:, :, SUBLANES:SUBLANES + M2_HEADS].set(w_in[:, :, offs[10]:offs[11]])
    in_weights = tuple(w.astype(BF16) for w in (
        w_in[:, :, offs[0]:offs[3]], small_w, w_in[:, :, offs[4]:offs[8]], w_in[:, :, offs[8]:offs[9]],
        w_in[:, :, offs[9]:offs[10]]))
    in_dtypes = (BF16, F32, F32, F32, F32)
    w_out_b, xa_wq_b, xa_wkv_b, xa_wo_b, ffn_w1_b, ffn_w2_b, moe_w1_b, moe_w2_b = (
        w.astype(BF16) for w in (w_out, xa_w_q, xa_w_kv, xa_w_o, ffn_w1, ffn_w2, moe_w1, moe_w2))

    h = x.reshape(t, d)
    mem2 = mem.reshape(batch * mem.shape[1], d)
    for layer in range(depth):
        qkv, small, hg, z, xbc = _in_proj(h, mix_norm_w[layer][None, :], in_weights, layer, in_dtypes)

        qa, ka, va = _fox_prep(qkv, small, pad_lanes(fox_f_bias[layer]), tri_seq, batch, seq)
        fox_nw = jnp.zeros((SUBLANES, LANES), F32).at[:fox_w // LANES].set(
            fox_norm_w[layer].reshape(fox_w // LANES, LANES))
        o_fox = _fox_attn(qa, ka, va, fox_nw, batch, seq)

        lb = hg_lb[layer]
        hg_params = jnp.zeros((SUBLANES, hg_w), F32)
        hg_params = hg_params.at[0].set(jnp.log(jnp.maximum(lb, LB_FLOOR)))
        hg_params = hg_params.at[1].set(jnp.log1p(-lb))
        hg_params = hg_params.at[2].set(1.0 - lb)
        hg_params = hg_params.at[3].set(hg_norm_w[layer])
        o_hg = _hgrn2(hg, hg_params, tri_chunk, seg_ones, batch, seq)

        conv_w = jnp.zeros((SUBLANES, conv_ch), F32).at[:M2_CONV].set(m2_conv_w[layer])
        head_rows = jnp.zeros((SUBLANES, m2_w), F32).at[0].set(jnp.repeat(m2_d[layer], HEAD_DIM))
        small_rows = jnp.concatenate(
            [pad_lanes(m2_dt_bias[layer], SUBLANES), pad_lanes(m2_a_log[layer], SUBLANES),
             jnp.zeros((SUBLANES - 2, LANES), F32)], axis=0)
        o_m2 = _ssd(z, xbc, small, conv_w, m2_conv_b[layer][None, :], head_rows, small_rows,
                    m2_norm_w[layer][None, :], tri_chunk, expand, conv_shift, batch, seq)

        kv = _norm_proj(mem2, xa_mem_norm_w[layer][None, :], xa_wkv_b, layer, BF16, mem.shape[1])
        h = _xattn(h, o_fox, o_hg, o_m2, w_out_b, xa_norm_w[layer][None, :], xa_wq_b, kv, xa_wo_b, layer,
                   batch, seq)

        nw = ffn_norm_w[layer][None, :]
        if layer % 2 == 0:
            h = _ffn(h, nw, ffn_w1_b, ffn_w2_b, layer // 2)
        else:
            last = layer == depth - 1
            h = _moe(h, nw, moe_router[layer // 2], moe_w1_b, moe_w2_b, layer // 2, tri_seq,
                     final_norm_w[None, :], last)
    if depth % 2:
        h = _final_norm(h, final_norm_w[None, :])
    return h.reshape(batch, seq, d)
```

```python
import functools

import jax
import jax.numpy as jnp
from jax import lax
from jax.experimental import pallas as pl
from jax.experimental.pallas import tpu as pltpu

F32 = jnp.float32
BF16 = jnp.bfloat16

EPS = 1e-6
MASK_VALUE = -1e9
LB_FLOOR = 1e-30
HEAD_DIM = 64
N_FOX_HEADS = 4
N_HG_HEADS = 4
M2_HEADS = 8
M2_GROUPS = 2
M2_STATE = 128
M2_CONV = 4
XA_HEADS = 4
N_EXPERTS = 8

LANES = 128
SUBLANES = 8
VMEM_LIMIT_BYTES = 56 * 1024 * 1024

ROW_BLOCK = 512
SEQ_BLOCK = 512
SCAN_CHUNK = 128
SCAN_CHUNKS_PER_STEP = 8
SUB_BLOCK = 16
HG_SAFE_DECAY = 60.0
MOE_ROWS = 512
SLOT_ROWS = 4096


def _cparams(*sem):
    return pltpu.CompilerParams(dimension_semantics=sem, vmem_limit_bytes=VMEM_LIMIT_BYTES)


def _split3(x):
    x1 = x.astype(BF16)
    r1 = x - x1.astype(F32)
    x2 = r1.astype(BF16)
    x3 = (r1 - x2.astype(F32)).astype(BF16)
    return x1, x2, x3


def _dot(a, b):
    return jnp.dot(a, b, preferred_element_type=F32)


def _dot_nt(a, b):
    return lax.dot_general(a, b, (((1,), (1,)), ((), ())), preferred_element_type=F32)


def _exact_left(mat01, x):
    x1, x2, x3 = _split3(x)
    return _dot(mat01, x1) + _dot(mat01, x2) + _dot(mat01, x3)


def _exact_right(x, mat01):
    x1, x2, x3 = _split3(x)
    return _dot(x1, mat01) + _dot(x2, mat01) + _dot(x3, mat01)


def _rms(x, w):
    return x * lax.rsqrt(jnp.mean(x * x, axis=-1, keepdims=True) + EPS) * w


def _sigmoid(x):
    return 1.0 / (1.0 + jnp.exp(-x))


def _silu(x):
    return x * _sigmoid(x)


def _log1p_exp(x):
    return jnp.log(1.0 + jnp.exp(x))


def _log_sigmoid(x):
    return jnp.minimum(x, 0.0) - _log1p_exp(-jnp.abs(x))


def _softplus(x):
    return jnp.maximum(x, 0.0) + _log1p_exp(-jnp.abs(x))


def _masked_exp(x, mask):
    return jnp.where(mask, jnp.exp(jnp.where(mask, x, 0.0)), 0.0)


def _resident(a):
    zeros = (0,) * a.ndim
    return pl.BlockSpec(a.shape, lambda *_: zeros, pipeline_mode=pl.Buffered(1))


def _layer_resident(stacked, layer):
    index = (layer,) + (0,) * (stacked.ndim - 1)
    return pl.BlockSpec((None,) + stacked.shape[1:], lambda *_: index, pipeline_mode=pl.Buffered(1))


def _in_proj_kernel(x_ref, nw_ref, *refs):
    n = len(refs) // 2
    xn = _rms(x_ref[...], nw_ref[...]).astype(BF16)
    for w_ref, o_ref in zip(refs[:n], refs[n:]):
        o_ref[...] = _dot(xn, w_ref[...]).astype(o_ref.dtype)


def _in_proj(h, nw, weights, layer, out_dtypes):
    t, d = h.shape
    widths = [w.shape[-1] for w in weights]
    return pl.pallas_call(
        _in_proj_kernel,
        grid=(t // ROW_BLOCK,),
        in_specs=[pl.BlockSpec((ROW_BLOCK, d), lambda i: (i, 0)), _resident(nw)]
        + [_layer_resident(w, layer) for w in weights],
        out_specs=[pl.BlockSpec((ROW_BLOCK, w), lambda i: (i, 0)) for w in widths],
        out_shape=[jax.ShapeDtypeStruct((t, w), dt) for w, dt in zip(widths, out_dtypes)],
        compiler_params=_cparams("parallel"),
        name="in_proj",
    )(h, nw, *weights)


def _norm_proj_kernel(x_ref, nw_ref, w_ref, o_ref):
    xn = _rms(x_ref[...], nw_ref[...]).astype(BF16)
    o_ref[...] = _dot(xn, w_ref[...]).astype(o_ref.dtype)


def _norm_proj(x, nw, w, layer, out_dtype, rows):
    t, d = x.shape
    n = w.shape[-1]
    return pl.pallas_call(
        _norm_proj_kernel,
        grid=(t // rows,),
        in_specs=[pl.BlockSpec((rows, d), lambda i: (i, 0)), _resident(nw), _layer_resident(w, layer)],
        out_specs=pl.BlockSpec((rows, n), lambda i: (i, 0)),
        out_shape=jax.ShapeDtypeStruct((t, n), out_dtype),
        compiler_params=_cparams("parallel"),
        name="norm_proj",
    )(x, nw, w)


def _fox_prep_kernel(qkv_ref, small_ref, bias_ref, tri_ref, q_ref, k_ref, v_ref, carry_ref):
    @pl.when(pl.program_id(1) == 0)
    def _():
        carry_ref[...] = jnp.zeros_like(carry_ref)

    rows = qkv_ref.shape[0]
    log_f = _log_sigmoid(small_ref[...] + bias_ref[...])
    c = _exact_left(tri_ref[...], log_f) + carry_ref[...]
    carry_ref[...] = c[rows - 1:rows, :]

    lane = lax.broadcasted_iota(jnp.int32, (rows, LANES), 1)
    width = N_FOX_HEADS * HEAD_DIM
    scale = HEAD_DIM ** -0.5
    for hd in range(N_FOX_HEADS):
        tile = (hd * HEAD_DIM) // LANES
        ch = c[:, hd:hd + 1]
        c1 = ch.astype(BF16).astype(F32)
        r1 = ch - c1
        c2 = r1.astype(BF16).astype(F32)
        c3 = r1 - c2

        def head_tile(base):
            x = qkv_ref[:, base + tile * LANES: base + (tile + 1) * LANES].astype(F32)
            if (hd * HEAD_DIM) % LANES:
                x = pltpu.roll(x, LANES - (hd * HEAD_DIM) % LANES, axis=1)
            return x

        def augment(x, first, second):
            out = jnp.where(lane < HEAD_DIM, x, 0.0)
            for j, val in enumerate(first + second):
                out = jnp.where(lane == HEAD_DIM + j, val, out)
            return out.astype(BF16)

        ones = (1.0, 1.0, 1.0)
        q_ref[hd] = augment(head_tile(0) * scale, ones, (c1, c2, c3))
        k_ref[hd] = augment(head_tile(width), (-c1, -c2, -c3), ones)
        v_aug = jnp.where(lane < HEAD_DIM, head_tile(2 * width), jnp.where(lane == HEAD_DIM, 1.0, 0.0))
        v_ref[hd] = v_aug.T.astype(BF16)


def _fox_prep(qkv, small, bias_row, tri, batch, seq):
    t = qkv.shape[0]
    nb = seq // SEQ_BLOCK
    row = lambda b, c: (b * nb + c, 0)
    head_spec = pl.BlockSpec((N_FOX_HEADS, SEQ_BLOCK, LANES), lambda b, c: (0, b * nb + c, 0))
    head_shape = jax.ShapeDtypeStruct((N_FOX_HEADS, t, LANES), BF16)
    vt_spec = pl.BlockSpec((N_FOX_HEADS, LANES, SEQ_BLOCK), lambda b, c: (0, 0, b * nb + c))
    vt_shape = jax.ShapeDtypeStruct((N_FOX_HEADS, LANES, t), BF16)
    return pl.pallas_call(
        _fox_prep_kernel,
        grid=(batch, nb),
        in_specs=[
            pl.BlockSpec((SEQ_BLOCK, qkv.shape[1]), row),
            pl.BlockSpec((SEQ_BLOCK, LANES), row),
            pl.BlockSpec((1, LANES), lambda b, c: (0, 0)),
            pl.BlockSpec((SEQ_BLOCK, SEQ_BLOCK), lambda b, c: (0, 0)),
        ],
        out_specs=[head_spec, head_spec, vt_spec],
        out_shape=[head_shape, head_shape, vt_shape],
        scratch_shapes=[pltpu.VMEM((1, LANES), F32)],
        compiler_params=_cparams("parallel", "arbitrary"),
        name="fox_prep",
    )(qkv, small, bias_row, tri)


def _fox_attn_kernel(q_ref, k_ref, vt_ref, nw_ref, o_ref, m_ref, acc_ref, sa_ref, sb_ref):
    i = pl.program_id(1)
    tq = q_ref.shape[1]
    kv_idx = lax.broadcasted_iota(jnp.int32, (tq, tq), 0)
    q_idx = lax.broadcasted_iota(jnp.int32, (tq, tq), 1)
    heads = range(N_FOX_HEADS)
    m_ref[...] = jnp.full(m_ref.shape, -jnp.inf, F32)
    acc_ref[...] = jnp.zeros_like(acc_ref)

    def logits_into(dst_ref, j):
        start = pl.multiple_of(j * tq, tq)
        for hd in heads:
            dst_ref[hd] = _dot_nt(k_ref[hd, pl.ds(start, tq), :], q_ref[hd])

    def fold(src_ref, j, diagonal=False):
        start = pl.multiple_of(j * tq, tq)
        for hd in heads:
            s = src_ref[hd]
            if diagonal:
                s = jnp.where(kv_idx <= q_idx, s, MASK_VALUE)
            m_old = m_ref[hd]
            m_new = jnp.maximum(m_old, jnp.max(s, axis=0, keepdims=True))
            p = jnp.exp(s - m_new).astype(BF16)
            acc_ref[hd] = jnp.exp(m_old - m_new) * acc_ref[hd] + _dot(vt_ref[hd, :, pl.ds(start, tq)], p)
            m_ref[hd] = m_new

    logits_into(sa_ref, 0)

    def two_blocks(jj, carry):
        logits_into(sb_ref, 2 * jj + 1)
        fold(sa_ref, 2 * jj)
        logits_into(sa_ref, 2 * jj + 2)
        fold(sb_ref, 2 * jj + 1)
        return carry

    lax.fori_loop(0, i // 2, two_blocks, 0)

    @pl.when(i % 2 == 1)
    def _():
        logits_into(sb_ref, i)
        fold(sa_ref, i - 1)
        fold(sb_ref, i, diagonal=True)

    @pl.when(i % 2 == 0)
    def _():
        fold(sa_ref, i, diagonal=True)

    normed = []
    for hd in heads:
        acc = acc_ref[hd]
        o = acc[:HEAD_DIM] / acc[HEAD_DIM:HEAD_DIM + 1]
        ms = jnp.mean(o * o, axis=0, keepdims=True)
        normed.append(o * lax.rsqrt(ms + EPS))
    tiles = []
    for pair in range(N_FOX_HEADS // 2):
        both = jnp.concatenate(normed[2 * pair:2 * pair + 2], axis=0)
        tiles.append(both.T * nw_ref[pair:pair + 1, :])
    o_ref[...] = jnp.concatenate(tiles, axis=1).astype(o_ref.dtype)


def _fox_attn(qa, ka, vt, nw_pairs, batch, seq):
    t = qa.shape[1]
    nq = seq // SEQ_BLOCK
    return pl.pallas_call(
        _fox_attn_kernel,
        grid=(batch, nq),
        in_specs=[
            pl.BlockSpec((N_FOX_HEADS, SEQ_BLOCK, LANES), lambda b, i: (0, b * nq + i, 0)),
            pl.BlockSpec((N_FOX_HEADS, seq, LANES), lambda b, i: (0, b, 0)),
            pl.BlockSpec((N_FOX_HEADS, LANES, seq), lambda b, i: (0, 0, b)),
            pl.BlockSpec((SUBLANES, LANES), lambda b, i: (0, 0)),
        ],
        out_specs=pl.BlockSpec((SEQ_BLOCK, N_FOX_HEADS * HEAD_DIM), lambda b, i: (b * nq + i, 0)),
        out_shape=jax.ShapeDtypeStruct((t, N_FOX_HEADS * HEAD_DIM), BF16),
        scratch_shapes=[
            pltpu.VMEM((N_FOX_HEADS, 1, SEQ_BLOCK), F32),
            pltpu.VMEM((N_FOX_HEADS, LANES, SEQ_BLOCK), F32),
            pltpu.VMEM((N_FOX_HEADS, SEQ_BLOCK, SEQ_BLOCK), F32),
            pltpu.VMEM((N_FOX_HEADS, SEQ_BLOCK, SEQ_BLOCK), F32),
        ],
        compiler_params=_cparams("parallel", "arbitrary"),
        name="fox_attn",
    )(qa, ka, vt, nw_pairs)


def _chunk_views(refs, c):
    return [r.at[pl.ds(c * SCAN_CHUNK, SCAN_CHUNK)] for r in refs]


def _hgrn2_kernel(hq_ref, hf_ref, hi_ref, hg_ref, par_ref, tri_ref, seg_ref, o_ref, st_ref):
    @pl.when(pl.program_id(1) == 0)
    def _():
        st_ref[...] = jnp.zeros_like(st_ref)

    for c in range(SCAN_CHUNKS_PER_STEP):
        hq, hf, hi, hg, o = _chunk_views((hq_ref, hf_ref, hi_ref, hg_ref, o_ref), c)
        _hgrn2_chunk(hq, hf, hi, hg, par_ref, tri_ref, seg_ref, o, st_ref)


def _hgrn2_chunk(hq_ref, hf_ref, hi_ref, hg_ref, par_ref, tri_ref, seg_ref, o_ref, st_ref):
    rows, width = hq_ref.shape
    log_lb, log1m_lb, one_m_lb, nw = (par_ref[r:r + 1, :] for r in range(4))
    f_raw = hf_ref[...]
    q = _silu(hq_ref[...]) * (HEAD_DIM ** -0.5)
    b = log1m_lb + _log_sigmoid(f_raw)
    g = jnp.maximum(log_lb, b) + _log1p_exp(-jnp.abs(log_lb - b))
    k = one_m_lb * _sigmoid(-f_raw)
    v = hi_ref[...]
    cum = _exact_left(tri_ref[...], g)

    lane = lax.broadcasted_iota(jnp.int32, (1, width), 1)
    head_masks = [(lane >= hd * HEAD_DIM) & (lane < (hd + 1) * HEAD_DIM) for hd in range(N_HG_HEADS)]
    seg = seg_ref[...]
    v_b = v.astype(BF16)

    st = st_ref[...]
    o_state = _dot_nt((q * jnp.exp(cum)).astype(BF16), st.astype(BF16))

    n_sub = rows // SUB_BLOCK
    refs = [cum[i * SUB_BLOCK - 1:i * SUB_BLOCK] if i else jnp.zeros((1, width), F32) for i in range(n_sub)]
    local = [cum[i * SUB_BLOCK:(i + 1) * SUB_BLOCK] - refs[i] for i in range(n_sub)]

    def stack_heads(x):
        return jnp.concatenate([jnp.where(mk, x, 0.0) for mk in head_masks], axis=0).astype(BF16)

    def unstack_heads(base, p4):
        for hd, mk in enumerate(head_masks):
            base = base + jnp.where(mk, p4[hd * SUB_BLOCK:(hd + 1) * SUB_BLOCK], 0.0)
        return base

    def factored():
        scores = []
        for i in range(n_sub):
            hi = (i + 1) * SUB_BLOCK
            qs = q[hi - SUB_BLOCK:hi] * jnp.exp(local[i])
            ks = (k[:hi] * jnp.exp(refs[i] - cum[:hi])).astype(BF16)
            scores.append(_dot_nt(stack_heads(qs), ks))
        blocks = []
        for i in range(n_sub):
            hi = (i + 1) * SUB_BLOCK
            t_in = lax.broadcasted_iota(jnp.int32, (N_HG_HEADS * SUB_BLOCK, hi), 0) & (SUB_BLOCK - 1)
            s_in = lax.broadcasted_iota(jnp.int32, (N_HG_HEADS * SUB_BLOCK, hi), 1)
            sc = jnp.where(s_in <= t_in + (hi - SUB_BLOCK), scores[i], 0.0).astype(BF16)
            blocks.append(unstack_heads(o_state[hi - SUB_BLOCK:hi], _dot(sc, v_b[:hi])))
        return jnp.concatenate(blocks, axis=0)

    def pairwise():
        t_idx = lax.broadcasted_iota(jnp.int32, (SUB_BLOCK, 1), 0)
        blocks = []
        for i in range(n_sub):
            r0 = i * SUB_BLOCK
            q_i = q[r0:r0 + SUB_BLOCK]
            cum_i = cum[r0:r0 + SUB_BLOCK]
            o_i = o_state[r0:r0 + SUB_BLOCK]
            if i > 0:
                ks = (k[:r0] * jnp.exp(refs[i] - cum[:r0])).astype(BF16)
                sc = _dot_nt(stack_heads(q_i * jnp.exp(local[i])), ks)
                o_i = unstack_heads(o_i, _dot(sc.astype(BF16), v_b[:r0]))
            terms = []
            for s in range(SUB_BLOCK):
                keep = t_idx >= s
                e = _masked_exp(cum_i - cum[r0 + s:r0 + s + 1], keep)
                terms.append((q_i * e * k[r0 + s:r0 + s + 1]).astype(BF16))
            sums = _dot(jnp.concatenate(terms, axis=0), seg)
            for s in range(SUB_BLOCK):
                o_i = o_i + sums[s * SUB_BLOCK:(s + 1) * SUB_BLOCK] * v[r0 + s:r0 + s + 1]
            blocks.append(o_i)
        return jnp.concatenate(blocks, axis=0)

    worst = functools.reduce(jnp.minimum, local)
    o = lax.cond(jnp.min(worst) >= -HG_SAFE_DECAY, factored, pairwise)

    last = cum[rows - 1:rows]
    kd = (k * jnp.exp(last - cum)).astype(BF16)
    upd = _dot(v.T.astype(BF16), kd)
    st_ref[...] = st * jnp.exp(last) + jnp.where(seg > 0, upd, 0.0)

    sq = o * o
    s1 = sq.astype(BF16)
    s2 = (sq - s1.astype(F32)).astype(BF16)
    ms = (_dot(s1, seg) + _dot(s2, seg)) * (1.0 / HEAD_DIM)
    o_ref[...] = (o * lax.rsqrt(ms + EPS) * nw * _silu(hg_ref[...])).astype(o_ref.dtype)


def _hgrn2(hg, params, tri, seg, batch, seq):
    t = hg.shape[0]
    width = N_HG_HEADS * HEAD_DIM
    step_rows = SCAN_CHUNK * SCAN_CHUNKS_PER_STEP
    nc = seq // step_rows
    col = lambda c: pl.BlockSpec((step_rows, width), lambda b, j, c=c: (b * nc + j, c))
    return pl.pallas_call(
        _hgrn2_kernel,
        grid=(batch, nc),
        in_specs=[
            col(0), col(1), col(2), col(3),
            pl.BlockSpec(params.shape, lambda b, j: (0, 0)),
            pl.BlockSpec(tri.shape, lambda b, j: (0, 0)),
            pl.BlockSpec(seg.shape, lambda b, j: (0, 0)),
        ],
        out_specs=pl.BlockSpec((step_rows, width), lambda b, j: (b * nc + j, 0)),
        out_shape=jax.ShapeDtypeStruct((t, width), BF16),
        scratch_shapes=[pltpu.VMEM((width, width), F32)],
        compiler_params=_cparams("parallel", "arbitrary"),
        name="hgrn2",
    )(hg, hg, hg, hg, params, tri, seg)


def _ssd_kernel(z_ref, xbc_ref, small_ref, cw_ref, cb_ref, hp_ref, sp_ref, nw_ref, tri_ref, exp_ref, shift_ref,
                o_ref, buf_ref, st_ref):
    @pl.when(pl.program_id(1) == 0)
    def _():
        buf_ref[...] = jnp.zeros_like(buf_ref)
        st_ref[...] = jnp.zeros_like(st_ref)

    for c in range(SCAN_CHUNKS_PER_STEP):
        z, xbc, small, o = _chunk_views((z_ref, xbc_ref, small_ref, o_ref), c)
        _ssd_chunk(z, xbc, small, cw_ref, cb_ref, hp_ref, sp_ref, nw_ref, tri_ref, exp_ref, shift_ref, o,
                   buf_ref, st_ref)


def _ssd_chunk(z_ref, xbc_ref, small_ref, cw_ref, cb_ref, hp_ref, sp_ref, nw_ref, tri_ref, exp_ref, shift_ref,
               o_ref, buf_ref, st_ref):
    rows = z_ref.shape[0]
    width = z_ref.shape[1]

    x_in = xbc_ref[...]
    buf_ref[rows:2 * rows, :] = x_in.astype(BF16)
    shifted = _dot(shift_ref[...], buf_ref[...])
    buf_ref[0:rows, :] = buf_ref[rows:2 * rows, :]
    conv = cb_ref[...] + cw_ref[M2_CONV - 1:M2_CONV, :] * x_in
    for k in range(1, M2_CONV):
        tap = M2_CONV - 1 - k
        conv = conv + cw_ref[tap:tap + 1, :] * shifted[(k - 1) * rows:k * rows]
    xc = _silu(conv)
    xs = xc[:, :width]
    gn = M2_GROUPS * M2_STATE
    bm = xc[:, width:width + gn]
    cm = xc[:, width + gn:width + 2 * gn]

    d_x = hp_ref[0:1, :]
    dt_bias_c, a_log_c = (sp_ref[r:r + 1, :] for r in range(2))
    small = small_ref[...]
    tri = tri_ref[...]
    dt_c = _softplus(small + dt_bias_c)
    a_cum_c = _exact_left(tri, dt_c * -jnp.exp(a_log_c))
    a_cum_t = a_cum_c.T
    expand = exp_ref[...]
    dt_x = _exact_right(dt_c, expand)
    a_cum_x = _exact_right(a_cum_c, expand)

    row = lax.broadcasted_iota(jnp.int32, (rows, rows), 0)
    col = lax.broadcasted_iota(jnp.int32, (rows, rows), 1)
    causal = col <= row
    lane = lax.broadcasted_iota(jnp.int32, (1, LANES), 1)
    low = lane < HEAD_DIM

    xdt = xs * dt_x
    a_last = a_cum_x[rows - 1:rows]
    xd = (xdt * jnp.exp(a_last - a_cum_x)).astype(BF16)
    decay_out = jnp.exp(a_cum_x)
    st = st_ref[...]
    heads_per_group = M2_HEADS // M2_GROUPS
    gw = heads_per_group * HEAD_DIM
    cbs, y_offs, new_states = [], [], []
    for g in range(M2_GROUPS):
        b_g = bm[:, g * M2_STATE:(g + 1) * M2_STATE]
        c_g = cm[:, g * M2_STATE:(g + 1) * M2_STATE].astype(BF16)
        cbs.append(jnp.where(causal, _dot_nt(c_g, b_g.astype(BF16)), 0.0))
        y_offs.append(_dot(c_g, st[:, g * gw:(g + 1) * gw].astype(BF16)) * decay_out[:, g * gw:(g + 1) * gw])
        new_states.append(_dot(b_g.T.astype(BF16), xd[:, g * gw:(g + 1) * gw]))
    st_ref[...] = st * jnp.exp(a_last) + jnp.concatenate(new_states, axis=1)

    y_tiles = []
    for tile in range(M2_HEADS // 2):
        g = (2 * tile) // heads_per_group
        pair = tile - g * (heads_per_group // 2)
        x_tile = xdt[:, tile * LANES:(tile + 1) * LANES]
        y_pair = y_offs[g][:, pair * LANES:(pair + 1) * LANES]
        for half in range(2):
            hd = 2 * tile + half
            seg = a_cum_c[:, SUBLANES + hd:SUBLANES + hd + 1] - a_cum_t[SUBLANES + hd:SUBLANES + hd + 1, :]
            m_h = (cbs[g] * jnp.exp(jnp.minimum(seg, 0.0))).astype(BF16)
            x_h = jnp.where(low if half == 0 else ~low, x_tile, 0.0).astype(BF16)
            y_pair = y_pair + _dot(m_h, x_h)
        y_tiles.append(y_pair)

    y = jnp.concatenate(y_tiles, axis=1) + d_x * xs
    y = y * _silu(z_ref[...])
    nw = nw_ref[...]
    outs = []
    for g in range(M2_GROUPS):
        outs.append(_rms(y[:, g * gw:(g + 1) * gw], nw[:, g * gw:(g + 1) * gw]))
    o_ref[...] = jnp.concatenate(outs, axis=1).astype(o_ref.dtype)


def _ssd(z, xbc, small, conv_w, conv_b, head_rows, small_rows, norm_w, tri, expand, shift, batch, seq):
    t, width = z.shape
    ch = xbc.shape[1]
    step_rows = SCAN_CHUNK * SCAN_CHUNKS_PER_STEP
    nc = seq // step_rows
    row = lambda b, j: (b * nc + j, 0)
    const = lambda a: pl.BlockSpec(a.shape, lambda b, j: (0, 0))
    return pl.pallas_call(
        _ssd_kernel,
        grid=(batch, nc),
        in_specs=[
            pl.BlockSpec((step_rows, width), row),
            pl.BlockSpec((step_rows, ch), row),
            pl.BlockSpec((step_rows, LANES), row),
            const(conv_w), const(conv_b), const(head_rows), const(small_rows), const(norm_w),
            const(tri), const(expand), const(shift),
        ],
        out_specs=pl.BlockSpec((step_rows, width), row),
        out_shape=jax.ShapeDtypeStruct((t, width), BF16),
        scratch_shapes=[
            pltpu.VMEM((2 * SCAN_CHUNK, ch), BF16),
            pltpu.VMEM((M2_STATE, width), F32),
        ],
        compiler_params=_cparams("parallel", "arbitrary"),
        name="ssd",
    )(z, xbc, small, conv_w, conv_b, head_rows, small_rows, norm_w, tri, expand, shift)


def _xattn_kernel(h_ref, a_ref, b_ref, c_ref, wout_ref, nw_ref, wq_ref, kv_ref, wo_ref, o_ref):
    wa = a_ref.shape[1]
    wb = b_ref.shape[1]
    h = h_ref[...] + _dot(a_ref[...], wout_ref[0:wa, :])
    h = h + _dot(b_ref[...], wout_ref[wa:wa + wb, :])
    h = h + _dot(c_ref[...], wout_ref[wa + wb:, :])
    d = h.shape[1]
    hd = d // XA_HEADS
    q = (_dot(_rms(h, nw_ref[...]).astype(BF16), wq_ref[...]) * (hd ** -0.5)).astype(BF16)
    logits = [_dot_nt(q[:, a * hd:(a + 1) * hd], kv_ref[:, a * hd:(a + 1) * hd]) for a in range(XA_HEADS)]
    outs = []
    for a, s in enumerate(logits):
        p = jnp.exp(s - jnp.max(s, axis=-1, keepdims=True))
        p = p / jnp.sum(p, axis=-1, keepdims=True)
        outs.append(_dot(p.astype(BF16), kv_ref[:, d + a * hd:d + (a + 1) * hd]).astype(BF16))
    o_ref[...] = h + _dot(jnp.concatenate(outs, axis=1), wo_ref[...])


def _xattn(h, o_fox, o_hg, o_m2, w_out, nw, wq, kv, wo, layer, batch, seq):
    t, d = h.shape
    n_mem = kv.shape[0] // batch
    nb = seq // ROW_BLOCK
    row = lambda x: pl.BlockSpec((ROW_BLOCK, x.shape[1]), lambda b, i: (b * nb + i, 0))
    return pl.pallas_call(
        _xattn_kernel,
        grid=(batch, nb),
        in_specs=[
            row(h), row(o_fox), row(o_hg), row(o_m2),
            _layer_resident(w_out, layer), _resident(nw), _layer_resident(wq, layer),
            pl.BlockSpec((n_mem, kv.shape[1]), lambda b, i: (b, 0)),
            _layer_resident(wo, layer),
        ],
        out_specs=row(h),
        out_shape=jax.ShapeDtypeStruct((t, d), F32),
        compiler_params=_cparams("parallel", "parallel"),
        name="xattn",
    )(h, o_fox, o_hg, o_m2, w_out, nw, wq, kv, wo)


def _swiglu(x, w1_ref, w2_ref, d_ff):
    a = (_silu(_dot(x, w1_ref[:, :d_ff])) * _dot(x, w1_ref[:, d_ff:])).astype(BF16)
    return _dot(a, w2_ref[...])


def _ffn_kernel(h_ref, nw_ref, w1_ref, w2_ref, o_ref):
    h = h_ref[...]
    o_ref[...] = h + _swiglu(_rms(h, nw_ref[...]).astype(BF16), w1_ref, w2_ref, w2_ref.shape[0])


def _ffn(h, nw, w1, w2, layer):
    t, d = h.shape
    return pl.pallas_call(
        _ffn_kernel,
        grid=(t // ROW_BLOCK,),
        in_specs=[pl.BlockSpec((ROW_BLOCK, d), lambda i: (i, 0)), _resident(nw),
                  _layer_resident(w1, layer), _layer_resident(w2, layer)],
        out_specs=pl.BlockSpec((ROW_BLOCK, d), lambda i: (i, 0)),
        out_shape=jax.ShapeDtypeStruct((t, d), F32),
        compiler_params=_cparams("parallel"),
        name="ffn",
    )(h, nw, w1, w2)


def _router_kernel(h_ref, nw_ref, wr_ref, tri_ref, idx_ref, gate_ref, rank_ref, cnt_ref, carry_ref):
    @pl.when(pl.program_id(0) == 0)
    def _():
        carry_ref[...] = jnp.zeros_like(carry_ref)

    xn = _rms(h_ref[...], nw_ref[...])
    x1 = xn.astype(BF16)
    x2 = (xn - x1.astype(F32)).astype(BF16)
    w12 = wr_ref[:, :2 * LANES]
    big = _dot(x1, w12) + _dot(x2, w12)
    logits = big[:, :LANES] + big[:, LANES:] + _dot(x1, wr_ref[:, 2 * LANES:])
    lane = lax.broadcasted_iota(jnp.int32, logits.shape, 1)
    logits = jnp.where(lane < N_EXPERTS, logits, -jnp.inf)
    m1 = jnp.max(logits, axis=-1, keepdims=True)
    i1 = jnp.min(jnp.where(logits == m1, lane, LANES), axis=-1, keepdims=True)
    rest = jnp.where(lane == i1, -jnp.inf, logits)
    m2 = jnp.max(rest, axis=-1, keepdims=True)
    i2 = jnp.min(jnp.where(rest == m2, lane, LANES), axis=-1, keepdims=True)
    e2 = jnp.exp(m2 - m1)
    denom = 1.0 + e2
    idx_ref[...] = jnp.where(lane == 0, i1, jnp.where(lane == 1, i2, 0))
    gate_ref[...] = jnp.where(lane == 0, 1.0 / denom, jnp.where(lane == 1, e2 / denom, 0.0))

    hit1 = lane == i1
    hit2 = lane == i2
    member = jnp.where(hit1, 1.0, jnp.where(hit2, 1.0, 0.0))
    incl = _dot(tri_ref[...], member.astype(BF16)) + carry_ref[...]
    excl = incl - member
    r1 = jnp.sum(jnp.where(hit1, excl, 0.0), axis=-1, keepdims=True)
    r2 = jnp.sum(jnp.where(hit2, excl, 0.0), axis=-1, keepdims=True)
    rows = member.shape[0]
    carry_ref[...] = incl[rows - 1:rows, :]
    rank_ref[...] = jnp.where(lane == 0, r1, jnp.where(lane == 1, r2, 0.0)).astype(jnp.int32)
    cnt_ref[...] = jnp.broadcast_to(incl[rows - 1:rows, :], cnt_ref.shape)


def _router(h, nw, wr, tri):
    t, d = h.shape
    row = lambda w: pl.BlockSpec((ROW_BLOCK, w), lambda i: (i, 0))
    return pl.pallas_call(
        _router_kernel,
        grid=(t // ROW_BLOCK,),
        in_specs=[row(d), _resident(nw), _resident(wr), _resident(tri)],
        out_specs=[row(LANES), row(LANES), row(LANES), pl.BlockSpec((SUBLANES, LANES), lambda i: (0, 0))],
        out_shape=[
            jax.ShapeDtypeStruct((t, LANES), jnp.int32),
            jax.ShapeDtypeStruct((t, LANES), F32),
            jax.ShapeDtypeStruct((t, LANES), jnp.int32),
            jax.ShapeDtypeStruct((SUBLANES, LANES), F32),
        ],
        scratch_shapes=[pltpu.VMEM((1, LANES), F32)],
        compiler_params=_cparams("arbitrary"),
        name="router",
    )(h, nw, wr, tri)


def _slot_kernel(idx_ref, rank_ref, start_ref, o_ref):
    idx = idx_ref[...]
    lane = lax.broadcasted_iota(jnp.int32, idx.shape, 1)
    starts = start_ref[...]
    s1 = jnp.sum(jnp.where(lane == idx[:, 0:1], starts, 0.0), axis=-1, keepdims=True)
    s2 = jnp.sum(jnp.where(lane == idx[:, 1:2], starts, 0.0), axis=-1, keepdims=True)
    o_ref[...] = jnp.where(lane == 0, s1, jnp.where(lane == 1, s2, 0.0)).astype(jnp.int32) + rank_ref[...]


def _slots(idx, rank, start_row):
    t = idx.shape[0]
    rows = min(t, SLOT_ROWS)
    row = pl.BlockSpec((rows, LANES), lambda i: (i, 0))
    return pl.pallas_call(
        _slot_kernel,
        grid=(t // rows,),
        in_specs=[row, row, pl.BlockSpec((1, LANES), lambda i: (0, 0))],
        out_specs=row,
        out_shape=jax.ShapeDtypeStruct((t, LANES), jnp.int32),
        compiler_params=_cparams("parallel"),
        name="moe_slots",
    )(idx, rank, start_row)


def _dispatch_kernel(pend_ref, padded_ref, nused_ref, dest_ref, h_ref, nw_ref, xs_ref, x_ref, zero_ref, sem):
    rows = h_ref.shape[0]
    n_blk = xs_ref.shape[0] // MOE_ROWS
    x_ref[...] = _rms(h_ref[...], nw_ref[...]).reshape(x_ref.shape)

    @pl.when(pl.program_id(0) == 0)
    def _():
        zero_ref[...] = jnp.zeros_like(zero_ref)

        def fill_block(start):
            fill = pltpu.make_async_copy(zero_ref, xs_ref.at[pl.ds(start, MOE_ROWS), :], sem)
            fill.start()
            fill.wait()

        for e in range(N_EXPERTS):
            @pl.when(padded_ref[e] > 0)
            def _():
                fill_block(pl.multiple_of(pend_ref[e] - MOE_ROWS, MOE_ROWS))

            @pl.when(nused_ref[0] + e < n_blk)
            def _():
                fill_block(pl.multiple_of((nused_ref[0] + e) * MOE_ROWS, MOE_ROWS))

    def issue(group, carry):
        base = pl.multiple_of(group * SUBLANES, SUBLANES)
        for s in range(SUBLANES):
            for k in range(2):
                slot = dest_ref[0, 0, k * rows + base + s]
                pltpu.make_async_copy(
                    x_ref.at[group, pl.ds(s, 1), :], xs_ref.at[pl.ds(slot, 1), :], sem).start(priority=k)
        return carry

    lax.fori_loop(0, rows // SUBLANES, issue, 0)
    for k in range(2):
        pltpu.make_async_copy(zero_ref, xs_ref.at[pl.ds(0, rows), :], sem).wait()


def _dispatch(pends, padded, n_used, dest_t, h, nw, cap):
    t, d = h.shape
    grid_spec = pltpu.PrefetchScalarGridSpec(
        num_scalar_prefetch=3,
        grid=(t // ROW_BLOCK,),
        in_specs=[
            pl.BlockSpec((1, 1, 2 * ROW_BLOCK), lambda i, pe, pa, nu: (i, 0, 0), memory_space=pltpu.SMEM),
            pl.BlockSpec((ROW_BLOCK, d), lambda i, pe, pa, nu: (i, 0)),
            pl.BlockSpec((1, d), lambda i, pe, pa, nu: (0, 0)),
        ],
        out_specs=pl.BlockSpec(memory_space=pl.ANY),
        scratch_shapes=[pltpu.VMEM((ROW_BLOCK // SUBLANES, SUBLANES, d), F32), pltpu.VMEM((MOE_ROWS, d), F32),
                        pltpu.SemaphoreType.DMA(())],
    )
    return pl.pallas_call(
        _dispatch_kernel,
        grid_spec=grid_spec,
        out_shape=jax.ShapeDtypeStruct((cap, d), F32),
        compiler_params=_cparams("arbitrary"),
        name="moe_dispatch",
    )(pends, padded, n_used, dest_t, h, nw)


def _expert_kernel(be_ref, nused_ref, x_ref, w1_ref, w2_ref, o_ref):
    live = pl.program_id(0) < nused_ref[0]

    @pl.when(live)
    def _():
        o_ref[...] = _swiglu(x_ref[...].astype(BF16), w1_ref, w2_ref, w2_ref.shape[0])

    @pl.when(jnp.logical_not(live))
    def _():
        o_ref[...] = jnp.zeros_like(o_ref)


def _experts(blk_expert, n_used, xs, w1, w2, layer):
    cap, d = xs.shape
    blk = lambda i, be, nu: (i, 0)
    expert = lambda w: pl.BlockSpec((None, None) + w.shape[2:], lambda i, be, nu: (layer, be[i], 0, 0),
                                    pipeline_mode=pl.Buffered(1))
    grid_spec = pltpu.PrefetchScalarGridSpec(
        num_scalar_prefetch=2,
        grid=(cap // MOE_ROWS,),
        in_specs=[pl.BlockSpec((MOE_ROWS, d), blk), expert(w1), expert(w2)],
        out_specs=pl.BlockSpec((MOE_ROWS, d), blk),
    )
    return pl.pallas_call(
        _expert_kernel,
        grid_spec=grid_spec,
        out_shape=jax.ShapeDtypeStruct((cap, d), F32),
        compiler_params=_cparams("arbitrary"),
        name="experts",
    )(blk_expert, n_used, xs, w1, w2)


def _combine_kernel(dest_ref, h_ref, gate_ref, nw_ref, yb_ref, o_ref, buf_ref, sem, *, final_norm):
    rows = h_ref.shape[0]

    def issue(group, carry):
        base = pl.multiple_of(group * SUBLANES, SUBLANES)
        for s in range(SUBLANES):
            for k in range(2):
                slot = dest_ref[0, 0, k * rows + base + s]
                pltpu.make_async_copy(
                    yb_ref.at[pl.ds(slot, 1), :], buf_ref.at[k, group, pl.ds(s, 1), :], sem).start(priority=k)
        return carry

    lax.fori_loop(0, rows // SUBLANES, issue, 0)
    for k in range(2):
        pltpu.make_async_copy(buf_ref.at[k], buf_ref.at[k], sem).wait()
    gate = gate_ref[...]
    picked = [buf_ref[k].reshape(rows, h_ref.shape[1]) for k in range(2)]
    out = h_ref[...] + gate[:, 0:1] * picked[0] + gate[:, 1:2] * picked[1]
    if final_norm:
        out = _rms(out, nw_ref[...])
    o_ref[...] = out


def _combine(dest_t, h, gate, nw, yb, final_norm):
    t, d = h.shape
    return pl.pallas_call(
        functools.partial(_combine_kernel, final_norm=final_norm),
        grid=(t // ROW_BLOCK,),
        in_specs=[
            pl.BlockSpec((1, 1, 2 * ROW_BLOCK), lambda i: (i, 0, 0), memory_space=pltpu.SMEM),
            pl.BlockSpec((ROW_BLOCK, d), lambda i: (i, 0)),
            pl.BlockSpec((ROW_BLOCK, LANES), lambda i: (i, 0)),
            pl.BlockSpec((1, d), lambda i: (0, 0)),
            pl.BlockSpec(memory_space=pl.ANY),
        ],
        out_specs=pl.BlockSpec((ROW_BLOCK, d), lambda i: (i, 0)),
        out_shape=jax.ShapeDtypeStruct((t, d), F32),
        scratch_shapes=[pltpu.VMEM((2, ROW_BLOCK // SUBLANES, SUBLANES, d), F32), pltpu.SemaphoreType.DMA(())],
        compiler_params=_cparams("arbitrary"),
        name="moe_combine",
    )(dest_t, h, gate, nw, yb)


def _final_norm_kernel(h_ref, nw_ref, o_ref):
    o_ref[...] = _rms(h_ref[...], nw_ref[...])


def _final_norm(h, nw):
    t, d = h.shape
    return pl.pallas_call(
        _final_norm_kernel,
        grid=(t // ROW_BLOCK,),
        in_specs=[pl.BlockSpec((ROW_BLOCK, d), lambda i: (i, 0)), pl.BlockSpec((1, d), lambda i: (0, 0))],
        out_specs=pl.BlockSpec((ROW_BLOCK, d), lambda i: (i, 0)),
        out_shape=jax.ShapeDtypeStruct((t, d), F32),
        compiler_params=_cparams("parallel"),
        name="final_norm",
    )(h, nw)


def _moe(h, nw, router_w, w1, w2, layer, tri, out_nw, final_norm):
    t, d = h.shape
    assert MOE_ROWS == ROW_BLOCK
    r1, r2, r3 = _split3(jnp.pad(router_w, ((0, 0), (0, LANES - N_EXPERTS))))
    idx, gate, rank, cnt = _router(h, nw, jnp.concatenate([r1, r2, r3], axis=1), tri)
    counts = cnt[0, :N_EXPERTS].astype(jnp.int32)
    padded = (counts + MOE_ROWS - 1) // MOE_ROWS * MOE_ROWS
    pends = jnp.cumsum(padded)
    starts = pends - padded
    cap = (2 * t // MOE_ROWS + N_EXPERTS) * MOE_ROWS
    n_blk = cap // MOE_ROWS
    blk_start = jnp.arange(n_blk, dtype=jnp.int32) * MOE_ROWS
    blk_expert = jnp.minimum(jnp.sum(blk_start[:, None] >= pends[None, :], axis=1), N_EXPERTS - 1).astype(jnp.int32)
    n_used = (pends[-1:] // MOE_ROWS).astype(jnp.int32)
    start_row = jnp.zeros((1, LANES), F32).at[0, :N_EXPERTS].set(starts.astype(F32))
    dest = _slots(idx, rank, start_row)
    dest_t = dest[:, :2].reshape(t // ROW_BLOCK, ROW_BLOCK, 2).transpose(0, 2, 1).reshape(
        t // ROW_BLOCK, 1, 2 * ROW_BLOCK)
    xs = _dispatch(pends.astype(jnp.int32), padded.astype(jnp.int32), n_used, dest_t, h, nw, cap)
    yb = _experts(blk_expert, n_used, xs, w1, w2, layer)
    return _combine(dest_t, h, gate, out_nw, yb, final_norm)


def _tri(n):
    return jnp.tril(jnp.ones((n, n), F32)).astype(BF16)


def kernel(x, mem, mix_norm_w, w_in, fox_f_bias, fox_norm_w, hg_lb_raw, hg_norm_w, m2_conv_w, m2_conv_b, m2_dt_bias, m2_a_log, m2_d, m2_norm_w, w_out, xa_norm_w, xa_mem_norm_w, xa_w_q, xa_w_kv, xa_w_o, ffn_norm_w, ffn_w1, ffn_w2, moe_router, moe_w1, moe_w2, final_norm_w):
    batch, seq, d = x.shape
    depth = w_in.shape[0]
    t = batch * seq
    fox_w = N_FOX_HEADS * HEAD_DIM
    hg_w = N_HG_HEADS * HEAD_DIM
    m2_w = M2_HEADS * HEAD_DIM
    conv_ch = m2_w + 2 * M2_GROUPS * M2_STATE
    in_splits = (fox_w, fox_w, fox_w, N_FOX_HEADS, hg_w, hg_w, hg_w, hg_w, m2_w, conv_ch, M2_HEADS)
    offs = [0]
    for s in in_splits:
        offs.append(offs[-1] + s)

    lb_p = jax.nn.softmax(hg_lb_raw.astype(F32), axis=0)
    hg_lb = jnp.cumsum(lb_p, axis=0) - lb_p[0]

    tri_seq = _tri(SEQ_BLOCK)
    tri_chunk = _tri(SCAN_CHUNK)
    head_of_lane = jnp.arange(hg_w) // HEAD_DIM
    seg_ones = (head_of_lane[:, None] == head_of_lane[None, :]).astype(BF16)
    expand = (jnp.arange(LANES)[:, None] - SUBLANES == jnp.arange(m2_w)[None, :] // HEAD_DIM).astype(BF16)
    shift_row = jnp.arange((M2_CONV - 1) * SCAN_CHUNK)
    shift_src = SCAN_CHUNK + shift_row % SCAN_CHUNK - (shift_row // SCAN_CHUNK + 1)
    conv_shift = (jnp.arange(2 * SCAN_CHUNK)[None, :] == shift_src[:, None]).astype(BF16)

    def pad_lanes(v, offset=0, width=LANES):
        return jnp.zeros((1, width), F32).at[0, offset:offset + v.shape[0]].set(v)

    small_w = jnp.zeros((depth, d, LANES), F32)
    small_w = small_w.at[:, :, 0:N_FOX_HEADS].set(w_in[:, :, offs[3]:offs[4]])
    small_w = small_w.at[:, :, SUBLANES:SUBLANES + M2_HEADS].set(w_in[:, :, offs[10]:offs[11]])
    in_weights = tuple(w.astype(BF16) for w in (
        w_in[:, :, offs[0]:offs[3]], small_w, w_in[:, :, offs[4]:offs[8]], w_in[:, :, offs[8]:offs[9]],
        w_in[:, :, offs[9]:offs[10]]))
    in_dtypes = (BF16, F32, F32, F32, F32)
    w_out_b, xa_wq_b, xa_wkv_b, xa_wo_b, ffn_w1_b, ffn_w2_b, moe_w1_b, moe_w2_b = (
        w.astype(BF16) for w in (w_out, xa_w_q, xa_w_kv, xa_w_o, ffn_w1, ffn_w2, moe_w1, moe_w2))

    h = x.reshape(t, d)
    mem2 = mem.reshape(batch * mem.shape[1], d)
    for layer in range(depth):
        qkv, small, hg, z, xbc = _in_proj(h, mix_norm_w[layer][None, :], in_weights, layer, in_dtypes)

        qa, ka, va = _fox_prep(qkv, small, pad_lanes(fox_f_bias[layer]), tri_seq, batch, seq)
        fox_nw = jnp.zeros((SUBLANES, LANES), F32).at[:fox_w // LANES].set(
            fox_norm_w[layer].reshape(fox_w // LANES, LANES))
        o_fox = _fox_attn(qa, ka, va, fox_nw, batch, seq)

        lb = hg_lb[layer]
        hg_params = jnp.zeros((SUBLANES, hg_w), F32)
        hg_params = hg_params.at[0].set(jnp.log(jnp.maximum(lb, LB_FLOOR)))
        hg_params = hg_params.at[1].set(jnp.log1p(-lb))
        hg_params = hg_params.at[2].set(1.0 - lb)
        hg_params = hg_params.at[3].set(hg_norm_w[layer])
        o_hg = _hgrn2(hg, hg_params, tri_chunk, seg_ones, batch, seq)

        conv_w = jnp.zeros((SUBLANES, conv_ch), F32).at[:M2_CONV].set(m2_conv_w[layer])
        head_rows = jnp.zeros((SUBLANES, m2_w), F32).at[0].set(jnp.repeat(m2_d[layer], HEAD_DIM))
        small_rows = jnp.concatenate(
            [pad_lanes(m2_dt_bias[layer], SUBLANES), pad_lanes(m2_a_log[layer], SUBLANES),
             jnp.zeros((SUBLANES - 2, LANES), F32)], axis=0)
        o_m2 = _ssd(z, xbc, small, conv_w, m2_conv_b[layer][None, :], head_rows, small_rows,
                    m2_norm_w[layer][None, :], tri_chunk, expand, conv_shift, batch, seq)

        kv = _norm_proj(mem2, xa_mem_norm_w[layer][None, :], xa_wkv_b, layer, BF16, mem.shape[1])
        h = _xattn(h, o_fox, o_hg, o_m2, w_out_b, xa_norm_w[layer][None, :], xa_wq_b, kv, xa_wo_b, layer,
                   batch, seq)

        nw = ffn_norm_w[layer][None, :]
        if layer % 2 == 0:
            h = _ffn(h, nw, ffn_w1_b, ffn_w2_b, layer // 2)
        else:
            last = layer == depth - 1
            h = _moe(h, nw, moe_router[layer // 2], moe_w1_b, moe_w2_b, layer // 2, tri_seq,
                     final_norm_w[None, :], last)
    if depth % 2:
        h = _final_norm(h, final_norm_w[None, :])
    return h.reshape(batch, seq, d)
```

```python
import functools

import jax
import jax.numpy as jnp
from jax import lax
from jax.experimental import pallas as pl
from jax.experimental.pallas import tpu as pltpu

F32 = jnp.float32
BF16 = jnp.bfloat16

EPS = 1e-6
MASK_VALUE = -1e9
LB_FLOOR = 1e-30
HEAD_DIM = 64
N_FOX_HEADS = 4
N_HG_HEADS = 4
M2_HEADS = 8
M2_GROUPS = 2
M2_STATE = 128
M2_CONV = 4
XA_HEADS = 4
N_EXPERTS = 8

LANES = 128
SUBLANES = 8
VMEM_LIMIT_BYTES = 56 * 1024 * 1024

ROW_BLOCK = 512
SEQ_BLOCK = 512
SCAN_CHUNK = 128
SCAN_CHUNKS_PER_STEP = 8
SUB_BLOCK = 16
HG_SAFE_DECAY = 60.0
MOE_ROWS = 512
SLOT_ROWS = 4096


def _cparams(*sem):
    return pltpu.CompilerParams(dimension_semantics=sem, vmem_limit_bytes=VMEM_LIMIT_BYTES)


def _split3(x):
    x1 = x.astype(BF16)
    r1 = x - x1.astype(F32)
    x2 = r1.astype(BF16)
    x3 = (r1 - x2.astype(F32)).astype(BF16)
    return x1, x2, x3


def _dot(a, b):
    return jnp.dot(a, b, preferred_element_type=F32)


def _dot_nt(a, b):
    return lax.dot_general(a, b, (((1,), (1,)), ((), ())), preferred_element_type=F32)


def _exact_left(mat01, x):
    x1, x2, x3 = _split3(x)
    return _dot(mat01, x1) + _dot(mat01, x2) + _dot(mat01, x3)


def _exact_right(x, mat01):
    x1, x2, x3 = _split3(x)
    return _dot(x1, mat01) + _dot(x2, mat01) + _dot(x3, mat01)


def _rms(x, w):
    return x * lax.rsqrt(jnp.mean(x * x, axis=-1, keepdims=True) + EPS) * w


def _sigmoid(x):
    return 1.0 / (1.0 + jnp.exp(-x))


def _silu(x):
    return x * _sigmoid(x)


def _log1p_exp(x):
    return jnp.log(1.0 + jnp.exp(x))


def _log_sigmoid(x):
    return jnp.minimum(x, 0.0) - _log1p_exp(-jnp.abs(x))


def _softplus(x):
    return jnp.maximum(x, 0.0) + _log1p_exp(-jnp.abs(x))


def _masked_exp(x, mask):
    return jnp.where(mask, jnp.exp(jnp.where(mask, x, 0.0)), 0.0)


def _resident(a):
    zeros = (0,) * a.ndim
    return pl.BlockSpec(a.shape, lambda *_: zeros, pipeline_mode=pl.Buffered(1))


def _layer_resident(stacked, layer):
    index = (layer,) + (0,) * (stacked.ndim - 1)
    return pl.BlockSpec((None,) + stacked.shape[1:], lambda *_: index, pipeline_mode=pl.Buffered(1))


def _in_proj_kernel(x_ref, nw_ref, *refs):
    n = len(refs) // 2
    xn = _rms(x_ref[...], nw_ref[...]).astype(BF16)
    for w_ref, o_ref in zip(refs[:n], refs[n:]):
        o_ref[...] = _dot(xn, w_ref[...]).astype(o_ref.dtype)


def _in_proj(h, nw, weights, layer, out_dtypes):
    t, d = h.shape
    widths = [w.shape[-1] for w in weights]
    return pl.pallas_call(
        _in_proj_kernel,
        grid=(t // ROW_BLOCK,),
        in_specs=[pl.BlockSpec((ROW_BLOCK, d), lambda i: (i, 0)), _resident(nw)]
        + [_layer_resident(w, layer) for w in weights],
        out_specs=[pl.BlockSpec((ROW_BLOCK, w), lambda i: (i, 0)) for w in widths],
        out_shape=[jax.ShapeDtypeStruct((t, w), dt) for w, dt in zip(widths, out_dtypes)],
        compiler_params=_cparams("parallel"),
        name="in_proj",
    )(h, nw, *weights)


def _norm_proj_kernel(x_ref, nw_ref, w_ref, o_ref):
    xn = _rms(x_ref[...], nw_ref[...]).astype(BF16)
    o_ref[...] = _dot(xn, w_ref[...]).astype(o_ref.dtype)


def _norm_proj(x, nw, w, layer, out_dtype, rows):
    t, d = x.shape
    n = w.shape[-1]
    return pl.pallas_call(
        _norm_proj_kernel,
        grid=(t // rows,),
        in_specs=[pl.BlockSpec((rows, d), lambda i: (i, 0)), _resident(nw), _layer_resident(w, layer)],
        out_specs=pl.BlockSpec((rows, n), lambda i: (i, 0)),
        out_shape=jax.ShapeDtypeStruct((t, n), out_dtype),
        compiler_params=_cparams("parallel"),
        name="norm_proj",
    )(x, nw, w)


def _fox_prep_kernel(qkv_ref, small_ref, bias_ref, tri_ref, q_ref, k_ref, v_ref, carry_ref):
    @pl.when(pl.program_id(1) == 0)
    def _():
        carry_ref[...] = jnp.zeros_like(carry_ref)

    rows = qkv_ref.shape[0]
    log_f = _log_sigmoid(small_ref[...] + bias_ref[...])
    c = _exact_left(tri_ref[...], log_f) + carry_ref[...]
    carry_ref[...] = c[rows - 1:rows, :]

    lane = lax.broadcasted_iota(jnp.int32, (rows, LANES), 1)
    width = N_FOX_HEADS * HEAD_DIM
    scale = HEAD_DIM ** -0.5
    for hd in range(N_FOX_HEADS):
        tile = (hd * HEAD_DIM) // LANES
        ch = c[:, hd:hd + 1]
        c1 = ch.astype(BF16).astype(F32)
        r1 = ch - c1
        c2 = r1.astype(BF16).astype(F32)
        c3 = r1 - c2

        def head_tile(base):
            x = qkv_ref[:, base + tile * LANES: base + (tile + 1) * LANES].astype(F32)
            if (hd * HEAD_DIM) % LANES:
                x = pltpu.roll(x, LANES - (hd * HEAD_DIM) % LANES, axis=1)
            return x

        def augment(x, first, second):
            out = jnp.where(lane < HEAD_DIM, x, 0.0)
            for j, val in enumerate(first + second):
                out = jnp.where(lane == HEAD_DIM + j, val, out)
            return out.astype(BF16)

        ones = (1.0, 1.0, 1.0)
        q_ref[hd] = augment(head_tile(0) * scale, ones, (c1, c2, c3))
        k_ref[hd] = augment(head_tile(width), (-c1, -c2, -c3), ones)
        v_aug = jnp.where(lane < HEAD_DIM, head_tile(2 * width), jnp.where(lane == HEAD_DIM, 1.0, 0.0))
        v_ref[hd] = v_aug.T.astype(BF16)


def _fox_prep(qkv, small, bias_row, tri, batch, seq):
    t = qkv.shape[0]
    nb = seq // SEQ_BLOCK
    row = lambda b, c: (b * nb + c, 0)
    head_spec = pl.BlockSpec((N_FOX_HEADS, SEQ_BLOCK, LANES), lambda b, c: (0, b * nb + c, 0))
    head_shape = jax.ShapeDtypeStruct((N_FOX_HEADS, t, LANES), BF16)
    vt_spec = pl.BlockSpec((N_FOX_HEADS, LANES, SEQ_BLOCK), lambda b, c: (0, 0, b * nb + c))
    vt_shape = jax.ShapeDtypeStruct((N_FOX_HEADS, LANES, t), BF16)
    return pl.pallas_call(
        _fox_prep_kernel,
        grid=(batch, nb),
        in_specs=[
            pl.BlockSpec((SEQ_BLOCK, qkv.shape[1]), row),
            pl.BlockSpec((SEQ_BLOCK, LANES), row),
            pl.BlockSpec((1, LANES), lambda b, c: (0, 0)),
            pl.BlockSpec((SEQ_BLOCK, SEQ_BLOCK), lambda b, c: (0, 0)),
        ],
        out_specs=[head_spec, head_spec, vt_spec],
        out_shape=[head_shape, head_shape, vt_shape],
        scratch_shapes=[pltpu.VMEM((1, LANES), F32)],
        compiler_params=_cparams("parallel", "arbitrary"),
        name="fox_prep",
    )(qkv, small, bias_row, tri)


def _fox_attn_kernel(q_ref, k_ref, vt_ref, nw_ref, o_ref, m_ref, acc_ref, sa_ref, sb_ref):
    i = pl.program_id(1)
    tq = q_ref.shape[1]
    kv_idx = lax.broadcasted_iota(jnp.int32, (tq, tq), 0)
    q_idx = lax.broadcasted_iota(jnp.int32, (tq, tq), 1)
    heads = range(N_FOX_HEADS)
    m_ref[...] = jnp.full(m_ref.shape, -jnp.inf, F32)
    acc_ref[...] = jnp.zeros_like(acc_ref)

    def logits_into(dst_ref, j):
        start = pl.multiple_of(j * tq, tq)
        for hd in heads:
            dst_ref[hd] = _dot_nt(k_ref[hd, pl.ds(start, tq), :], q_ref[hd])

    def fold(src_ref, j, diagonal=False):
        start = pl.multiple_of(j * tq, tq)
        for hd in heads:
            s = src_ref[hd]
            if diagonal:
                s = jnp.where(kv_idx <= q_idx, s, MASK_VALUE)
            m_old = m_ref[hd]
            m_new = jnp.maximum(m_old, jnp.max(s, axis=0, keepdims=True))
            p = jnp.exp(s - m_new).astype(BF16)
            acc_ref[hd] = jnp.exp(m_old - m_new) * acc_ref[hd] + _dot(vt_ref[hd, :, pl.ds(start, tq)], p)
            m_ref[hd] = m_new

    logits_into(sa_ref, 0)

    def two_blocks(jj, carry):
        logits_into(sb_ref, 2 * jj + 1)
        fold(sa_ref, 2 * jj)
        logits_into(sa_ref, 2 * jj + 2)
        fold(sb_ref, 2 * jj + 1)
        return carry

    lax.fori_loop(0, i // 2, two_blocks, 0)

    @pl.when(i % 2 == 1)
    def _():
        logits_into(sb_ref, i)
        fold(sa_ref, i - 1)
        fold(sb_ref, i, diagonal=True)

    @pl.when(i % 2 == 0)
    def _():
        fold(sa_ref, i, diagonal=True)

    normed = []
    for hd in heads:
        acc = acc_ref[hd]
        o = acc[:HEAD_DIM] / acc[HEAD_DIM:HEAD_DIM + 1]
        ms = jnp.mean(o * o, axis=0, keepdims=True)
        normed.append(o * lax.rsqrt(ms + EPS))
    tiles = []
    for pair in range(N_FOX_HEADS // 2):
        both = jnp.concatenate(normed[2 * pair:2 * pair + 2], axis=0)
        tiles.append(both.T * nw_ref[pair:pair + 1, :])
    o_ref[...] = jnp.concatenate(tiles, axis=1).astype(o_ref.dtype)


def _fox_attn(qa, ka, vt, nw_pairs, batch, seq):
    t = qa.shape[1]
    nq = seq // SEQ_BLOCK
    return pl.pallas_call(
        _fox_attn_kernel,
        grid=(batch, nq),
        in_specs=[
            pl.BlockSpec((N_FOX_HEADS, SEQ_BLOCK, LANES), lambda b, i: (0, b * nq + i, 0)),
            pl.BlockSpec((N_FOX_HEADS, seq, LANES), lambda b, i: (0, b, 0)),
            pl.BlockSpec((N_FOX_HEADS, LANES, seq), lambda b, i: (0, 0, b)),
            pl.BlockSpec((SUBLANES, LANES), lambda b, i: (0, 0)),
        ],
        out_specs=pl.BlockSpec((SEQ_BLOCK, N_FOX_HEADS * HEAD_DIM), lambda b, i: (b * nq + i, 0)),
        out_shape=jax.ShapeDtypeStruct((t, N_FOX_HEADS * HEAD_DIM), BF16),
        scratch_shapes=[
            pltpu.VMEM((N_FOX_HEADS, 1, SEQ_BLOCK), F32),
            pltpu.VMEM((N_FOX_HEADS, LANES, SEQ_BLOCK), F32),
            pltpu.VMEM((N_FOX_HEADS, SEQ_BLOCK, SEQ_BLOCK), F32),
            pltpu.VMEM((N_FOX_HEADS, SEQ_BLOCK, SEQ_BLOCK), F32),
        ],
        compiler_params=_cparams("parallel", "arbitrary"),
        name="fox_attn",
    )(qa, ka, vt, nw_pairs)


def _chunk_views(refs, c):
    return [r.at[pl.ds(c * SCAN_CHUNK, SCAN_CHUNK)] for r in refs]


def _hgrn2_kernel(hq_ref, hf_ref, hi_ref, hg_ref, par_ref, tri_ref, seg_ref, o_ref, st_ref):
    @pl.when(pl.program_id(1) == 0)
    def _():
        st_ref[...] = jnp.zeros_like(st_ref)

    for c in range(SCAN_CHUNKS_PER_STEP):
        hq, hf, hi, hg, o = _chunk_views((hq_ref, hf_ref, hi_ref, hg_ref, o_ref), c)
        _hgrn2_chunk(hq, hf, hi, hg, par_ref, tri_ref, seg_ref, o, st_ref)


def _hgrn2_chunk(hq_ref, hf_ref, hi_ref, hg_ref, par_ref, tri_ref, seg_ref, o_ref, st_ref):
    rows, width = hq_ref.shape
    log_lb, log1m_lb, one_m_lb, nw = (par_ref[r:r + 1, :] for r in range(4))
    f_raw = hf_ref[...]
    q = _silu(hq_ref[...]) * (HEAD_DIM ** -0.5)
    b = log1m_lb + _log_sigmoid(f_raw)
    g = jnp.maximum(log_lb, b) + _log1p_exp(-jnp.abs(log_lb - b))
    k = one_m_lb * _sigmoid(-f_raw)
    v = hi_ref[...]
    cum = _exact_left(tri_ref[...], g)

    lane = lax.broadcasted_iota(jnp.int32, (1, width), 1)
    head_masks = [(lane >= hd * HEAD_DIM) & (lane < (hd + 1) * HEAD_DIM) for hd in range(N_HG_HEADS)]
    seg = seg_ref[...]
    v_b = v.astype(BF16)

    st = st_ref[...]
    o_state = _dot_nt((q * jnp.exp(cum)).astype(BF16), st.astype(BF16))

    n_sub = rows // SUB_BLOCK
    refs = [cum[i * SUB_BLOCK - 1:i * SUB_BLOCK] if i else jnp.zeros((1, width), F32) for i in range(n_sub)]
    local = [cum[i * SUB_BLOCK:(i + 1) * SUB_BLOCK] - refs[i] for i in range(n_sub)]

    def stack_heads(x):
        return jnp.concatenate([jnp.where(mk, x, 0.0) for mk in head_masks], axis=0).astype(BF16)

    def unstack_heads(base, p4):
        for hd, mk in enumerate(head_masks):
            base = base + jnp.where(mk, p4[hd * SUB_BLOCK:(hd + 1) * SUB_BLOCK], 0.0)
        return base

    def factored():
        scores = []
        for i in range(n_sub):
            hi = (i + 1) * SUB_BLOCK
            qs = q[hi - SUB_BLOCK:hi] * jnp.exp(local[i])
            ks = (k[:hi] * jnp.exp(refs[i] - cum[:hi])).astype(BF16)
            scores.append(_dot_nt(stack_heads(qs), ks))
        blocks = []
        for i in range(n_sub):
            hi = (i + 1) * SUB_BLOCK
            t_in = lax.broadcasted_iota(jnp.int32, (N_HG_HEADS * SUB_BLOCK, hi), 0) & (SUB_BLOCK - 1)
            s_in = lax.broadcasted_iota(jnp.int32, (N_HG_HEADS * SUB_BLOCK, hi), 1)
            sc = jnp.where(s_in <= t_in + (hi - SUB_BLOCK), scores[i], 0.0).astype(BF16)
            blocks.append(unstack_heads(o_state[hi - SUB_BLOCK:hi], _dot(sc, v_b[:hi])))
        return jnp.concatenate(blocks, axis=0)

    def pairwise():
        t_idx = lax.broadcasted_iota(jnp.int32, (SUB_BLOCK, 1), 0)
        blocks = []
        for i in range(n_sub):
            r0 = i * SUB_BLOCK
            q_i = q[r0:r0 + SUB_BLOCK]
            cum_i = cum[r0:r0 + SUB_BLOCK]
            o_i = o_state[r0:r0 + SUB_BLOCK]
            if i > 0:
                ks = (k[:r0] * jnp.exp(refs[i] - cum[:r0])).astype(BF16)
                sc = _dot_nt(stack_heads(q_i * jnp.exp(local[i])), ks)
                o_i = unstack_heads(o_i, _dot(sc.astype(BF16), v_b[:r0]))
            terms = []
            for s in range(SUB_BLOCK):
                keep = t_idx >= s
                e = _masked_exp(cum_i - cum[r0 + s:r0 + s + 1], keep)
                terms.append((q_i * e * k[r0 + s:r0 + s + 1]).astype(BF16))
            sums = _dot(jnp.concatenate(terms, axis=0), seg)
            for s in range(SUB_BLOCK):
                o_i = o_i + sums[s * SUB_BLOCK:(s + 1) * SUB_BLOCK] * v[r0 + s:r0 + s + 1]
            blocks.append(o_i)
        return jnp.concatenate(blocks, axis=0)

    worst = functools.reduce(jnp.minimum, local)
    o = lax.cond(jnp.min(worst) >= -HG_SAFE_DECAY, factored, pairwise)

    last = cum[rows - 1:rows]
    kd = (k * jnp.exp(last - cum)).astype(BF16)
    upd = _dot(v.T.astype(BF16), kd)
    st_ref[...] = st * jnp.exp(last) + jnp.where(seg > 0, upd, 0.0)

    sq = o * o
    s1 = sq.astype(BF16)
    s2 = (sq - s1.astype(F32)).astype(BF16)
    ms = (_dot(s1, seg) + _dot(s2, seg)) * (1.0 / HEAD_DIM)
    o_ref[...] = (o * lax.rsqrt(ms + EPS) * nw * _silu(hg_ref[...])).astype(o_ref.dtype)


def _hgrn2(hg, params, tri, seg, batch, seq):
    t = hg.shape[0]
    width = N_HG_HEADS * HEAD_DIM
    step_rows = SCAN_CHUNK * SCAN_CHUNKS_PER_STEP
    nc = seq // step_rows
    col = lambda c: pl.BlockSpec((step_rows, width), lambda b, j, c=c: (b * nc + j, c))
    return pl.pallas_call(
        _hgrn2_kernel,
        grid=(batch, nc),
        in_specs=[
            col(0), col(1), col(2), col(3),
            pl.BlockSpec(params.shape, lambda b, j: (0, 0)),
            pl.BlockSpec(tri.shape, lambda b, j: (0, 0)),
            pl.BlockSpec(seg.shape, lambda b, j: (0, 0)),
        ],
        out_specs=pl.BlockSpec((step_rows, width), lambda b, j: (b * nc + j, 0)),
        out_shape=jax.ShapeDtypeStruct((t, width), BF16),
        scratch_shapes=[pltpu.VMEM((width, width), F32)],
        compiler_params=_cparams("parallel", "arbitrary"),
        name="hgrn2",
    )(hg, hg, hg, hg, params, tri, seg)


def _ssd_kernel(z_ref, xbc_ref, small_ref, cw_ref, cb_ref, hp_ref, sp_ref, nw_ref, tri_ref, exp_ref, shift_ref,
                o_ref, buf_ref, st_ref):
    @pl.when(pl.program_id(1) == 0)
    def _():
        buf_ref[...] = jnp.zeros_like(buf_ref)
        st_ref[...] = jnp.zeros_like(st_ref)

    for c in range(SCAN_CHUNKS_PER_STEP):
        z, xbc, small, o = _chunk_views((z_ref, xbc_ref, small_ref, o_ref), c)
        _ssd_chunk(z, xbc, small, cw_ref, cb_ref, hp_ref, sp_ref, nw_ref, tri_ref, exp_ref, shift_ref, o,
                   buf_ref, st_ref)


def _ssd_chunk(z_ref, xbc_ref, small_ref, cw_ref, cb_ref, hp_ref, sp_ref, nw_ref, tri_ref, exp_ref, shift_ref,
               o_ref, buf_ref, st_ref):
    rows = z_ref.shape[0]
    width = z_ref.shape[1]

    x_in = xbc_ref[...]
    buf_ref[rows:2 * rows, :] = x_in.astype(BF16)
    shifted = _dot(shift_ref[...], buf_ref[...])
    buf_ref[0:rows, :] = buf_ref[rows:2 * rows, :]
    conv = cb_ref[...] + cw_ref[M2_CONV - 1:M2_CONV, :] * x_in
    for k in range(1, M2_CONV):
        tap = M2_CONV - 1 - k
        conv = conv + cw_ref[tap:tap + 1, :] * shifted[(k - 1) * rows:k * rows]
    xc = _silu(conv)
    xs = xc[:, :width]
    gn = M2_GROUPS * M2_STATE
    bm = xc[:, width:width + gn]
    cm = xc[:, width + gn:width + 2 * gn]

    d_x = hp_ref[0:1, :]
    dt_bias_c, a_log_c = (sp_ref[r:r + 1, :] for r in range(2))
    small = small_ref[...]
    tri = tri_ref[...]
    dt_c = _softplus(small + dt_bias_c)
    a_cum_c = _exact_left(tri, dt_c * -jnp.exp(a_log_c))
    a_cum_t = a_cum_c.T
    expand = exp_ref[...]
    dt_x = _exact_right(dt_c, expand)
    a_cum_x = _exact_right(a_cum_c, expand)

    row = lax.broadcasted_iota(jnp.int32, (rows, rows), 0)
    col = lax.broadcasted_iota(jnp.int32, (rows, rows), 1)
    causal = col <= row
    lane = lax.broadcasted_iota(jnp.int32, (1, LANES), 1)
    low = lane < HEAD_DIM

    xdt = xs * dt_x
    a_last = a_cum_x[rows - 1:rows]
    xd = (xdt * jnp.exp(a_last - a_cum_x)).astype(BF16)
    decay_out = jnp.exp(a_cum_x)
    st = st_ref[...]
    heads_per_group = M2_HEADS // M2_GROUPS
    gw = heads_per_group * HEAD_DIM
    cbs, y_offs, new_states = [], [], []
    for g in range(M2_GROUPS):
        b_g = bm[:, g * M2_STATE:(g + 1) * M2_STATE]
        c_g = cm[:, g * M2_STATE:(g + 1) * M2_STATE].astype(BF16)
        cbs.append(jnp.where(causal, _dot_nt(c_g, b_g.astype(BF16)), 0.0))
        y_offs.append(_dot(c_g, st[:, g * gw:(g + 1) * gw].astype(BF16)) * decay_out[:, g * gw:(g + 1) * gw])
        new_states.append(_dot(b_g.T.astype(BF16), xd[:, g * gw:(g + 1) * gw]))
    st_ref[...] = st * jnp.exp(a_last) + jnp.concatenate(new_states, axis=1)

    y_tiles = []
    for tile in range(M2_HEADS // 2):
        g = (2 * tile) // heads_per_group
        pair = tile - g * (heads_per_group // 2)
        x_tile = xdt[:, tile * LANES:(tile + 1) * LANES]
        y_pair = y_offs[g][:, pair * LANES:(pair + 1) * LANES]
        for half in range(2):
            hd = 2 * tile + half
            seg = a_cum_c[:, SUBLANES + hd:SUBLANES + hd + 1] - a_cum_t[SUBLANES + hd:SUBLANES + hd + 1, :]
            m_h = (cbs[g] * jnp.exp(jnp.minimum(seg, 0.0))).astype(BF16)
            x_h = jnp.where(low if half == 0 else ~low, x_tile, 0.0).astype(BF16)
            y_pair = y_pair + _dot(m_h, x_h)
        y_tiles.append(y_pair)

    y = jnp.concatenate(y_tiles, axis=1) + d_x * xs
    y = y * _silu(z_ref[...])
    nw = nw_ref[...]
    outs = []
    for g in range(M2_GROUPS):
        outs.append(_rms(y[:, g * gw:(g + 1) * gw], nw[:, g * gw:(g + 1) * gw]))
    o_ref[...] = jnp.concatenate(outs, axis=1).astype(o_ref.dtype)


def _ssd(z, xbc, small, conv_w, conv_b, head_rows, small_rows, norm_w, tri, expand, shift, batch, seq):
    t, width = z.shape
    ch = xbc.shape[1]
    step_rows = SCAN_CHUNK * SCAN_CHUNKS_PER_STEP
    nc = seq // step_rows
    row = lambda b, j: (b * nc + j, 0)
    const = lambda a: pl.BlockSpec(a.shape, lambda b, j: (0, 0))
    return pl.pallas_call(
        _ssd_kernel,
        grid=(batch, nc),
        in_specs=[
            pl.BlockSpec((step_rows, width), row),
            pl.BlockSpec((step_rows, ch), row),
            pl.BlockSpec((step_rows, LANES), row),
            const(conv_w), const(conv_b), const(head_rows), const(small_rows), const(norm_w),
            const(tri), const(expand), const(shift),
        ],
        out_specs=pl.BlockSpec((step_rows, width), row),
        out_shape=jax.ShapeDtypeStruct((t, width), BF16),
        scratch_shapes=[
            pltpu.VMEM((2 * SCAN_CHUNK, ch), BF16),
            pltpu.VMEM((M2_STATE, width), F32),
        ],
        compiler_params=_cparams("parallel", "arbitrary"),
        name="ssd",
    )(z, xbc, small, conv_w, conv_b, head_rows, small_rows, norm_w, tri, expand, shift)


def _xattn_kernel(h_ref, a_ref, b_ref, c_ref, wout_ref, nw_ref, wq_ref, kv_ref, wo_ref, *rest):
    o_ref = rest[-1]
    wa = a_ref.shape[1]
    wb = b_ref.shape[1]
    h = h_ref[...] + _dot(a_ref[...], wout_ref[0:wa, :])
    h = h + _dot(b_ref[...], wout_ref[wa:wa + wb, :])
    h = h + _dot(c_ref[...], wout_ref[wa + wb:, :])
    d = h.shape[1]
    hd = d // XA_HEADS
    q = (_dot(_rms(h, nw_ref[...]).astype(BF16), wq_ref[...]) * (hd ** -0.5)).astype(BF16)
    logits = [_dot_nt(q[:, a * hd:(a + 1) * hd], kv_ref[:, a * hd:(a + 1) * hd]) for a in range(XA_HEADS)]
    outs = []
    for a, s in enumerate(logits):
        p = jnp.exp(s - jnp.max(s, axis=-1, keepdims=True))
        p = p / jnp.sum(p, axis=-1, keepdims=True)
        outs.append(_dot(p.astype(BF16), kv_ref[:, d + a * hd:d + (a + 1) * hd]).astype(BF16))
    h = h + _dot(jnp.concatenate(outs, axis=1), wo_ref[...])
    if len(rest) > 1:
        fnw_ref, w1_ref, w2_ref = rest[:3]
        h = h + _swiglu(_rms(h, fnw_ref[...]).astype(BF16), w1_ref, w2_ref, w2_ref.shape[0])
    o_ref[...] = h


def _xattn(h, o_fox, o_hg, o_m2, w_out, nw, wq, kv, wo, layer, batch, seq, ffn=None):
    t, d = h.shape
    n_mem = kv.shape[0] // batch
    nb = seq // ROW_BLOCK
    row = lambda x: pl.BlockSpec((ROW_BLOCK, x.shape[1]), lambda b, i: (b * nb + i, 0))
    extra_specs, extra_args = [], []
    if ffn is not None:
        fnw, w1, w2, ffn_layer = ffn
        extra_specs = [_resident(fnw), _layer_resident(w1, ffn_layer), _layer_resident(w2, ffn_layer)]
        extra_args = [fnw, w1, w2]
    return pl.pallas_call(
        _xattn_kernel,
        grid=(batch, nb),
        in_specs=[
            row(h), row(o_fox), row(o_hg), row(o_m2),
            _layer_resident(w_out, layer), _resident(nw), _layer_resident(wq, layer),
            pl.BlockSpec((n_mem, kv.shape[1]), lambda b, i: (b, 0)),
            _layer_resident(wo, layer),
        ] + extra_specs,
        out_specs=row(h),
        out_shape=jax.ShapeDtypeStruct((t, d), F32),
        compiler_params=_cparams("parallel", "parallel"),
        name="xattn_ffn" if ffn is not None else "xattn",
    )(h, o_fox, o_hg, o_m2, w_out, nw, wq, kv, wo, *extra_args)


def _swiglu(x, w1_ref, w2_ref, d_ff):
    a = (_silu(_dot(x, w1_ref[:, :d_ff])) * _dot(x, w1_ref[:, d_ff:])).astype(BF16)
    return _dot(a, w2_ref[...])


def _ffn_kernel(h_ref, nw_ref, w1_ref, w2_ref, o_ref):
    h = h_ref[...]
    o_ref[...] = h + _swiglu(_rms(h, nw_ref[...]).astype(BF16), w1_ref, w2_ref, w2_ref.shape[0])


def _ffn(h, nw, w1, w2, layer):
    t, d = h.shape
    return pl.pallas_call(
        _ffn_kernel,
        grid=(t // ROW_BLOCK,),
        in_specs=[pl.BlockSpec((ROW_BLOCK, d), lambda i: (i, 0)), _resident(nw),
                  _layer_resident(w1, layer), _layer_resident(w2, layer)],
        out_specs=pl.BlockSpec((ROW_BLOCK, d), lambda i: (i, 0)),
        out_shape=jax.ShapeDtypeStruct((t, d), F32),
        compiler_params=_cparams("parallel"),
        name="ffn",
    )(h, nw, w1, w2)


def _router_kernel(h_ref, nw_ref, wr_ref, tri_ref, idx_ref, gate_ref, rank_ref, cnt_ref, carry_ref):
    @pl.when(pl.program_id(0) == 0)
    def _():
        carry_ref[...] = jnp.zeros_like(carry_ref)

    xn = _rms(h_ref[...], nw_ref[...])
    x1 = xn.astype(BF16)
    x2 = (xn - x1.astype(F32)).astype(BF16)
    w12 = wr_ref[:, :2 * LANES]
    big = _dot(x1, w12) + _dot(x2, w12)
    logits = big[:, :LANES] + big[:, LANES:] + _dot(x1, wr_ref[:, 2 * LANES:])
    lane = lax.broadcasted_iota(jnp.int32, logits.shape, 1)
    logits = jnp.where(lane < N_EXPERTS, logits, -jnp.inf)
    m1 = jnp.max(logits, axis=-1, keepdims=True)
    i1 = jnp.min(jnp.where(logits == m1, lane, LANES), axis=-1, keepdims=True)
    rest = jnp.where(lane == i1, -jnp.inf, logits)
    m2 = jnp.max(rest, axis=-1, keepdims=True)
    i2 = jnp.min(jnp.where(rest == m2, lane, LANES), axis=-1, keepdims=True)
    e2 = jnp.exp(m2 - m1)
    denom = 1.0 + e2
    idx_ref[...] = jnp.where(lane == 0, i1, jnp.where(lane == 1, i2, 0))
    gate_ref[...] = jnp.where(lane == 0, 1.0 / denom, jnp.where(lane == 1, e2 / denom, 0.0))

    hit1 = lane == i1
    hit2 = lane == i2
    member = jnp.where(hit1, 1.0, jnp.where(hit2, 1.0, 0.0))
    incl = _dot(tri_ref[...], member.astype(BF16)) + carry_ref[...]
    excl = incl - member
    r1 = jnp.sum(jnp.where(hit1, excl, 0.0), axis=-1, keepdims=True)
    r2 = jnp.sum(jnp.where(hit2, excl, 0.0), axis=-1, keepdims=True)
    rows = member.shape[0]
    carry_ref[...] = incl[rows - 1:rows, :]
    rank_ref[...] = jnp.where(lane == 0, r1, jnp.where(lane == 1, r2, 0.0)).astype(jnp.int32)
    cnt_ref[...] = jnp.broadcast_to(incl[rows - 1:rows, :], cnt_ref.shape)


def _router(h, nw, wr, tri):
    t, d = h.shape
    row = lambda w: pl.BlockSpec((ROW_BLOCK, w), lambda i: (i, 0))
    return pl.pallas_call(
        _router_kernel,
        grid=(t // ROW_BLOCK,),
        in_specs=[row(d), _resident(nw), _resident(wr), _resident(tri)],
        out_specs=[row(LANES), row(LANES), row(LANES), pl.BlockSpec((SUBLANES, LANES), lambda i: (0, 0))],
        out_shape=[
            jax.ShapeDtypeStruct((t, LANES), jnp.int32),
            jax.ShapeDtypeStruct((t, LANES), F32),
            jax.ShapeDtypeStruct((t, LANES), jnp.int32),
            jax.ShapeDtypeStruct((SUBLANES, LANES), F32),
        ],
        scratch_shapes=[pltpu.VMEM((1, LANES), F32)],
        compiler_params=_cparams("arbitrary"),
        name="router",
    )(h, nw, wr, tri)


def _slot_kernel(idx_ref, rank_ref, start_ref, o_ref):
    idx = idx_ref[...]
    lane = lax.broadcasted_iota(jnp.int32, idx.shape, 1)
    starts = start_ref[...]
    s1 = jnp.sum(jnp.where(lane == idx[:, 0:1], starts, 0.0), axis=-1, keepdims=True)
    s2 = jnp.sum(jnp.where(lane == idx[:, 1:2], starts, 0.0), axis=-1, keepdims=True)
    o_ref[...] = jnp.where(lane == 0, s1, jnp.where(lane == 1, s2, 0.0)).astype(jnp.int32) + rank_ref[...]


def _slots(idx, rank, start_row):
    t = idx.shape[0]
    rows = min(t, SLOT_ROWS)
    row = pl.BlockSpec((rows, LANES), lambda i: (i, 0))
    return pl.pallas_call(
        _slot_kernel,
        grid=(t // rows,),
        in_specs=[row, row, pl.BlockSpec((1, LANES), lambda i: (0, 0))],
        out_specs=row,
        out_shape=jax.ShapeDtypeStruct((t, LANES), jnp.int32),
        compiler_params=_cparams("parallel"),
        name="moe_slots",
    )(idx, rank, start_row)


def _dispatch_kernel(pend_ref, padded_ref, nused_ref, dest_ref, h_ref, nw_ref, xs_ref, x_ref, zero_ref, sem):
    rows = h_ref.shape[0]
    n_blk = xs_ref.shape[0] // MOE_ROWS
    x_ref[...] = _rms(h_ref[...], nw_ref[...]).reshape(x_ref.shape)

    @pl.when(pl.program_id(0) == 0)
    def _():
        zero_ref[...] = jnp.zeros_like(zero_ref)

        def fill_block(start):
            fill = pltpu.make_async_copy(zero_ref, xs_ref.at[pl.ds(start, MOE_ROWS), :], sem)
            fill.start()
            fill.wait()

        for e in range(N_EXPERTS):
            @pl.when(padded_ref[e] > 0)
            def _():
                fill_block(pl.multiple_of(pend_ref[e] - MOE_ROWS, MOE_ROWS))

            @pl.when(nused_ref[0] + e < n_blk)
            def _():
                fill_block(pl.multiple_of((nused_ref[0] + e) * MOE_ROWS, MOE_ROWS))

    def issue(group, carry):
        base = pl.multiple_of(group * SUBLANES, SUBLANES)
        for s in range(SUBLANES):
            for k in range(2):
                slot = dest_ref[0, 0, k * rows + base + s]
                pltpu.make_async_copy(
                    x_ref.at[group, pl.ds(s, 1), :], xs_ref.at[pl.ds(slot, 1), :], sem).start(priority=k)
        return carry

    lax.fori_loop(0, rows // SUBLANES, issue, 0)
    for k in range(2):
        pltpu.make_async_copy(zero_ref, xs_ref.at[pl.ds(0, rows), :], sem).wait()


def _dispatch(pends, padded, n_used, dest_t, h, nw, cap):
    t, d = h.shape
    grid_spec = pltpu.PrefetchScalarGridSpec(
        num_scalar_prefetch=3,
        grid=(t // ROW_BLOCK,),
        in_specs=[
            pl.BlockSpec((1, 1, 2 * ROW_BLOCK), lambda i, pe, pa, nu: (i, 0, 0), memory_space=pltpu.SMEM),
            pl.BlockSpec((ROW_BLOCK, d), lambda i, pe, pa, nu: (i, 0)),
            pl.BlockSpec((1, d), lambda i, pe, pa, nu: (0, 0)),
        ],
        out_specs=pl.BlockSpec(memory_space=pl.ANY),
        scratch_shapes=[pltpu.VMEM((ROW_BLOCK // SUBLANES, SUBLANES, d), F32), pltpu.VMEM((MOE_ROWS, d), F32),
                        pltpu.SemaphoreType.DMA(())],
    )
    return pl.pallas_call(
        _dispatch_kernel,
        grid_spec=grid_spec,
        out_shape=jax.ShapeDtypeStruct((cap, d), F32),
        compiler_params=_cparams("arbitrary"),
        name="moe_dispatch",
    )(pends, padded, n_used, dest_t, h, nw)


def _expert_kernel(be_ref, nused_ref, x_ref, w1_ref, w2_ref, o_ref):
    live = pl.program_id(0) < nused_ref[0]

    @pl.when(live)
    def _():
        o_ref[...] = _swiglu(x_ref[...].astype(BF16), w1_ref, w2_ref, w2_ref.shape[0])

    @pl.when(jnp.logical_not(live))
    def _():
        o_ref[...] = jnp.zeros_like(o_ref)


def _experts(blk_expert, n_used, xs, w1, w2, layer):
    cap, d = xs.shape
    blk = lambda i, be, nu: (i, 0)
    expert = lambda w: pl.BlockSpec((None, None) + w.shape[2:], lambda i, be, nu: (layer, be[i], 0, 0),
                                    pipeline_mode=pl.Buffered(1))
    grid_spec = pltpu.PrefetchScalarGridSpec(
        num_scalar_prefetch=2,
        grid=(cap // MOE_ROWS,),
        in_specs=[pl.BlockSpec((MOE_ROWS, d), blk), expert(w1), expert(w2)],
        out_specs=pl.BlockSpec((MOE_ROWS, d), blk),
    )
    return pl.pallas_call(
        _expert_kernel,
        grid_spec=grid_spec,
        out_shape=jax.ShapeDtypeStruct((cap, d), F32),
        compiler_params=_cparams("arbitrary"),
        name="experts",
    )(blk_expert, n_used, xs, w1, w2)


def _combine_kernel(dest_ref, h_ref, gate_ref, nw_ref, yb_ref, o_ref, buf_ref, sem, *, final_norm):
    rows = h_ref.shape[0]

    def issue(group, carry):
        base = pl.multiple_of(group * SUBLANES, SUBLANES)
        for s in range(SUBLANES):
            for k in range(2):
                slot = dest_ref[0, 0, k * rows + base + s]
                pltpu.make_async_copy(
                    yb_ref.at[pl.ds(slot, 1), :], buf_ref.at[k, group, pl.ds(s, 1), :], sem).start(priority=k)
        return carry

    lax.fori_loop(0, rows // SUBLANES, issue, 0)
    for k in range(2):
        pltpu.make_async_copy(buf_ref.at[k], buf_ref.at[k], sem).wait()
    gate = gate_ref[...]
    picked = [buf_ref[k].reshape(rows, h_ref.shape[1]) for k in range(2)]
    out = h_ref[...] + gate[:, 0:1] * picked[0] + gate[:, 1:2] * picked[1]
    if final_norm:
        out = _rms(out, nw_ref[...])
    o_ref[...] = out


def _combine(dest_t, h, gate, nw, yb, final_norm):
    t, d = h.shape
    return pl.pallas_call(
        functools.partial(_combine_kernel, final_norm=final_norm),
        grid=(t // ROW_BLOCK,),
        in_specs=[
            pl.BlockSpec((1, 1, 2 * ROW_BLOCK), lambda i: (i, 0, 0), memory_space=pltpu.SMEM),
            pl.BlockSpec((ROW_BLOCK, d), lambda i: (i, 0)),
            pl.BlockSpec((ROW_BLOCK, LANES), lambda i: (i, 0)),
            pl.BlockSpec((1, d), lambda i: (0, 0)),
            pl.BlockSpec(memory_space=pl.ANY),
        ],
        out_specs=pl.BlockSpec((ROW_BLOCK, d), lambda i: (i, 0)),
        out_shape=jax.ShapeDtypeStruct((t, d), F32),
        scratch_shapes=[pltpu.VMEM((2, ROW_BLOCK // SUBLANES, SUBLANES, d), F32), pltpu.SemaphoreType.DMA(())],
        compiler_params=_cparams("arbitrary"),
        name="moe_combine",
    )(dest_t, h, gate, nw, yb)


def _final_norm_kernel(h_ref, nw_ref, o_ref):
    o_ref[...] = _rms(h_ref[...], nw_ref[...])


def _final_norm(h, nw):
    t, d = h.shape
    return pl.pallas_call(
        _final_norm_kernel,
        grid=(t // ROW_BLOCK,),
        in_specs=[pl.BlockSpec((ROW_BLOCK, d), lambda i: (i, 0)), pl.BlockSpec((1, d), lambda i: (0, 0))],
        out_specs=pl.BlockSpec((ROW_BLOCK, d), lambda i: (i, 0)),
        out_shape=jax.ShapeDtypeStruct((t, d), F32),
        compiler_params=_cparams("parallel"),
        name="final_norm",
    )(h, nw)


def _moe(h, nw, router_w, w1, w2, layer, tri, out_nw, final_norm):
    t, d = h.shape
    assert MOE_ROWS == ROW_BLOCK
    r1, r2, r3 = _split3(jnp.pad(router_w, ((0, 0), (0, LANES - N_EXPERTS))))
    idx, gate, rank, cnt = _router(h, nw, jnp.concatenate([r1, r2, r3], axis=1), tri)
    counts = cnt[0, :N_EXPERTS].astype(jnp.int32)
    padded = (counts + MOE_ROWS - 1) // MOE_ROWS * MOE_ROWS
    pends = jnp.cumsum(padded)
    starts = pends - padded
    cap = (2 * t // MOE_ROWS + N_EXPERTS) * MOE_ROWS
    n_blk = cap // MOE_ROWS
    blk_start = jnp.arange(n_blk, dtype=jnp.int32) * MOE_ROWS
    blk_expert = jnp.minimum(jnp.sum(blk_start[:, None] >= pends[None, :], axis=1), N_EXPERTS - 1).astype(jnp.int32)
    n_used = (pends[-1:] // MOE_ROWS).astype(jnp.int32)
    start_row = jnp.zeros((1, LANES), F32).at[0, :N_EXPERTS].set(starts.astype(F32))
    dest = _slots(idx, rank, start_row)
    dest_t = dest[:, :2].reshape(t // ROW_BLOCK, ROW_BLOCK, 2).transpose(0, 2, 1).reshape(
        t // ROW_BLOCK, 1, 2 * ROW_BLOCK)
    xs = _dispatch(pends.astype(jnp.int32), padded.astype(jnp.int32), n_used, dest_t, h, nw, cap)
    yb = _experts(blk_expert, n_used, xs, w1, w2, layer)
    return _combine(dest_t, h, gate, out_nw, yb, final_norm)


def _tri(n):
    return jnp.tril(jnp.ones((n, n), F32)).astype(BF16)


def kernel(x, mem, mix_norm_w, w_in, fox_f_bias, fox_norm_w, hg_lb_raw, hg_norm_w, m2_conv_w, m2_conv_b, m2_dt_bias, m2_a_log, m2_d, m2_norm_w, w_out, xa_norm_w, xa_mem_norm_w, xa_w_q, xa_w_kv, xa_w_o, ffn_norm_w, ffn_w1, ffn_w2, moe_router, moe_w1, moe_w2, final_norm_w):
    batch, seq, d = x.shape
    depth = w_in.shape[0]
    t = batch * seq
    fox_w = N_FOX_HEADS * HEAD_DIM
    hg_w = N_HG_HEADS * HEAD_DIM
    m2_w = M2_HEADS * HEAD_DIM
    conv_ch = m2_w + 2 * M2_GROUPS * M2_STATE
    in_splits = (fox_w, fox_w, fox_w, N_FOX_HEADS, hg_w, hg_w, hg_w, hg_w, m2_w, conv_ch, M2_HEADS)
    offs = [0]
    for s in in_splits:
        offs.append(offs[-1] + s)

    lb_p = jax.nn.softmax(hg_lb_raw.astype(F32), axis=0)
    hg_lb = jnp.cumsum(lb_p, axis=0) - lb_p[0]

    tri_seq = _tri(SEQ_BLOCK)
    tri_chunk = _tri(SCAN_CHUNK)
    head_of_lane = jnp.arange(hg_w) // HEAD_DIM
    seg_ones = (head_of_lane[:, None] == head_of_lane[None, :]).astype(BF16)
    expand = (jnp.arange(LANES)[:, None] - SUBLANES == jnp.arange(m2_w)[None, :] // HEAD_DIM).astype(BF16)
    shift_row = jnp.arange((M2_CONV - 1) * SCAN_CHUNK)
    shift_src = SCAN_CHUNK + shift_row % SCAN_CHUNK - (shift_row // SCAN_CHUNK + 1)
    conv_shift = (jnp.arange(2 * SCAN_CHUNK)[None, :] == shift_src[:, None]).astype(BF16)

    def pad_lanes(v, offset=0, width=LANES):
        return jnp.zeros((1, width), F32).at[0, offset:offset + v.shape[0]].set(v)

    small_w = jnp.zeros((depth, d, LANES), F32)
    small_w = small_w.at[:, :, 0:N_FOX_HEADS].set(w_in[:, :, offs[3]:offs[4]])
    small_w = small_w.at[:, :, SUBLANES:SUBLANES + M2_HEADS].set(w_in[:, :, offs[10]:offs[11]])
    in_weights = tuple(w.astype(BF16) for w in (
        w_in[:, :, offs[0]:offs[3]], small_w, w_in[:, :, offs[4]:offs[8]], w_in[:, :, offs[8]:offs[9]],
        w_in[:, :, offs[9]:offs[10]]))
    in_dtypes = (BF16, F32, F32, F32, F32)
    w_out_b, xa_wq_b, xa_wkv_b, xa_wo_b, ffn_w1_b, ffn_w2_b, moe_w1_b, moe_w2_b = (
        w.astype(BF16) for w in (w_out, xa_w_q, xa_w_kv, xa_w_o, ffn_w1, ffn_w2, moe_w1, moe_w2))

    h = x.reshape(t, d)
    mem2 = mem.reshape(batch * mem.shape[1], d)
    for layer in range(depth):
        qkv, small, hg, z, xbc = _in_proj(h, mix_norm_w[layer][None, :], in_weights, layer, in_dtypes)

        qa, ka, va = _fox_prep(qkv, small, pad_lanes(fox_f_bias[layer]), tri_seq, batch, seq)
        fox_nw = jnp.zeros((SUBLANES, LANES), F32).at[:fox_w // LANES].set(
            fox_norm_w[layer].reshape(fox_w // LANES, LANES))
        o_fox = _fox_attn(qa, ka, va, fox_nw, batch, seq)

        lb = hg_lb[layer]
        hg_params = jnp.zeros((SUBLANES, hg_w), F32)
        hg_params = hg_params.at[0].set(jnp.log(jnp.maximum(lb, LB_FLOOR)))
        hg_params = hg_params.at[1].set(jnp.log1p(-lb))
        hg_params = hg_params.at[2].set(1.0 - lb)
        hg_params = hg_params.at[3].set(hg_norm_w[layer])
        o_hg = _hgrn2(hg, hg_params, tri_chunk, seg_ones, batch, seq)

        conv_w = jnp.zeros((SUBLANES, conv_ch), F32).at[:M2_CONV].set(m2_conv_w[layer])
        head_rows = jnp.zeros((SUBLANES, m2_w), F32).at[0].set(jnp.repeat(m2_d[layer], HEAD_DIM))
        small_rows = jnp.concatenate(
            [pad_lanes(m2_dt_bias[layer], SUBLANES), pad_lanes(m2_a_log[layer], SUBLANES),
             jnp.zeros((SUBLANES - 2, LANES), F32)], axis=0)
        o_m2 = _ssd(z, xbc, small, conv_w, m2_conv_b[layer][None, :], head_rows, small_rows,
                    m2_norm_w[layer][None, :], tri_chunk, expand, conv_shift, batch, seq)

        kv = _norm_proj(mem2, xa_mem_norm_w[layer][None, :], xa_wkv_b, layer, BF16, mem.shape[1])
        nw = ffn_norm_w[layer][None, :]
        dense = (nw, ffn_w1_b, ffn_w2_b, layer // 2) if layer % 2 == 0 else None
        h = _xattn(h, o_fox, o_hg, o_m2, w_out_b, xa_norm_w[layer][None, :], xa_wq_b, kv, xa_wo_b, layer,
                   batch, seq, ffn=dense)

        if layer % 2:
            last = layer == depth - 1
            h = _moe(h, nw, moe_router[layer // 2], moe_w1_b, moe_w2_b, layer // 2, tri_seq,
                     final_norm_w[None, :], last)
    if depth % 2:
        h = _final_norm(h, final_norm_w[None, :])
    return h.reshape(batch, seq, d)
```

```python
import functools

import jax
import jax.numpy as jnp
from jax import lax
from jax.experimental import pallas as pl
from jax.experimental.pallas import tpu as pltpu

F32 = jnp.float32
BF16 = jnp.bfloat16

EPS = 1e-6
MASK_VALUE = -1e9
LB_FLOOR = 1e-30
HEAD_DIM = 64
N_FOX_HEADS = 4
N_HG_HEADS = 4
M2_HEADS = 8
M2_GROUPS = 2
M2_STATE = 128
M2_CONV = 4
XA_HEADS = 4
N_EXPERTS = 8

LANES = 128
SUBLANES = 8
VMEM_LIMIT_BYTES = 56 * 1024 * 1024

ROW_BLOCK = 512
SEQ_BLOCK = 512
SCAN_CHUNK = 128
SCAN_CHUNKS_PER_STEP = 8
SUB_BLOCK = 16
HG_SAFE_DECAY = 60.0
MOE_ROWS = 512
SLOT_ROWS = 4096


def _cparams(*sem):
    return pltpu.CompilerParams(dimension_semantics=sem, vmem_limit_bytes=VMEM_LIMIT_BYTES)


def _split3(x):
    x1 = x.astype(BF16)
    r1 = x - x1.astype(F32)
    x2 = r1.astype(BF16)
    x3 = (r1 - x2.astype(F32)).astype(BF16)
    return x1, x2, x3


def _dot(a, b):
    return jnp.dot(a, b, preferred_element_type=F32)


def _dot_nt(a, b):
    return lax.dot_general(a, b, (((1,), (1,)), ((), ())), preferred_element_type=F32)


def _exact_left(mat01, x):
    x1, x2, x3 = _split3(x)
    return _dot(mat01, x1) + _dot(mat01, x2) + _dot(mat01, x3)


def _exact_right(x, mat01):
    x1, x2, x3 = _split3(x)
    return _dot(x1, mat01) + _dot(x2, mat01) + _dot(x3, mat01)


def _rms(x, w):
    return x * lax.rsqrt(jnp.mean(x * x, axis=-1, keepdims=True) + EPS) * w


def _sigmoid(x):
    return 1.0 / (1.0 + jnp.exp(-x))


def _silu(x):
    return x * _sigmoid(x)


def _log1p_exp(x):
    return jnp.log(1.0 + jnp.exp(x))


def _log_sigmoid(x):
    return jnp.minimum(x, 0.0) - _log1p_exp(-jnp.abs(x))


def _softplus(x):
    return jnp.maximum(x, 0.0) + _log1p_exp(-jnp.abs(x))


def _masked_exp(x, mask):
    return jnp.where(mask, jnp.exp(jnp.where(mask, x, 0.0)), 0.0)


def _resident(a):
    zeros = (0,) * a.ndim
    return pl.BlockSpec(a.shape, lambda *_: zeros, pipeline_mode=pl.Buffered(1))


def _layer_resident(stacked, layer):
    index = (layer,) + (0,) * (stacked.ndim - 1)
    return pl.BlockSpec((None,) + stacked.shape[1:], lambda *_: index, pipeline_mode=pl.Buffered(1))


def _in_proj_kernel(x_ref, nw_ref, *refs):
    n = len(refs) // 2
    xn = _rms(x_ref[...], nw_ref[...]).astype(BF16)
    for w_ref, o_ref in zip(refs[:n], refs[n:]):
        o_ref[...] = _dot(xn, w_ref[...]).astype(o_ref.dtype)


def _in_proj(h, nw, weights, layer, out_dtypes):
    t, d = h.shape
    widths = [w.shape[-1] for w in weights]
    return pl.pallas_call(
        _in_proj_kernel,
        grid=(t // ROW_BLOCK,),
        in_specs=[pl.BlockSpec((ROW_BLOCK, d), lambda i: (i, 0)), _resident(nw)]
        + [_layer_resident(w, layer) for w in weights],
        out_specs=[pl.BlockSpec((ROW_BLOCK, w), lambda i: (i, 0)) for w in widths],
        out_shape=[jax.ShapeDtypeStruct((t, w), dt) for w, dt in zip(widths, out_dtypes)],
        compiler_params=_cparams("parallel"),
        name="in_proj",
    )(h, nw, *weights)


def _norm_proj_kernel(x_ref, nw_ref, w_ref, o_ref):
    xn = _rms(x_ref[...], nw_ref[...]).astype(BF16)
    o_ref[...] = _dot(xn, w_ref[...]).astype(o_ref.dtype)


def _norm_proj(x, nw, w, layer, out_dtype, rows):
    t, d = x.shape
    n = w.shape[-1]
    return pl.pallas_call(
        _norm_proj_kernel,
        grid=(t // rows,),
        in_specs=[pl.BlockSpec((rows, d), lambda i: (i, 0)), _resident(nw), _layer_resident(w, layer)],
        out_specs=pl.BlockSpec((rows, n), lambda i: (i, 0)),
        out_shape=jax.ShapeDtypeStruct((t, n), out_dtype),
        compiler_params=_cparams("parallel"),
        name="norm_proj",
    )(x, nw, w)


def _fox_prep_kernel(qkv_ref, small_ref, bias_ref, tri_ref, q_ref, k_ref, v_ref, carry_ref):
    @pl.when(pl.program_id(1) == 0)
    def _():
        carry_ref[...] = jnp.zeros_like(carry_ref)

    rows = qkv_ref.shape[0]
    log_f = _log_sigmoid(small_ref[...] + bias_ref[...])
    c = _exact_left(tri_ref[...], log_f) + carry_ref[...]
    carry_ref[...] = c[rows - 1:rows, :]

    lane = lax.broadcasted_iota(jnp.int32, (rows, LANES), 1)
    width = N_FOX_HEADS * HEAD_DIM
    scale = HEAD_DIM ** -0.5
    for hd in range(N_FOX_HEADS):
        tile = (hd * HEAD_DIM) // LANES
        ch = c[:, hd:hd + 1]
        c1 = ch.astype(BF16).astype(F32)
        r1 = ch - c1
        c2 = r1.astype(BF16).astype(F32)
        c3 = r1 - c2

        def head_tile(base):
            x = qkv_ref[:, base + tile * LANES: base + (tile + 1) * LANES].astype(F32)
            if (hd * HEAD_DIM) % LANES:
                x = pltpu.roll(x, LANES - (hd * HEAD_DIM) % LANES, axis=1)
            return x

        def augment(x, first, second):
            out = jnp.where(lane < HEAD_DIM, x, 0.0)
            for j, val in enumerate(first + second):
                out = jnp.where(lane == HEAD_DIM + j, val, out)
            return out.astype(BF16)

        ones = (1.0, 1.0, 1.0)
        q_ref[hd] = augment(head_tile(0) * scale, ones, (c1, c2, c3))
        k_ref[hd] = augment(head_tile(width), (-c1, -c2, -c3), ones)
        v_aug = jnp.where(lane < HEAD_DIM, head_tile(2 * width), jnp.where(lane == HEAD_DIM, 1.0, 0.0))
        v_ref[hd] = v_aug.T.astype(BF16)


def _fox_prep(qkv, small, bias_row, tri, batch, seq):
    t = qkv.shape[0]
    nb = seq // SEQ_BLOCK
    row = lambda b, c: (b * nb + c, 0)
    head_spec = pl.BlockSpec((N_FOX_HEADS, SEQ_BLOCK, LANES), lambda b, c: (0, b * nb + c, 0))
    head_shape = jax.ShapeDtypeStruct((N_FOX_HEADS, t, LANES), BF16)
    vt_spec = pl.BlockSpec((N_FOX_HEADS, LANES, SEQ_BLOCK), lambda b, c: (0, 0, b * nb + c))
    vt_shape = jax.ShapeDtypeStruct((N_FOX_HEADS, LANES, t), BF16)
    return pl.pallas_call(
        _fox_prep_kernel,
        grid=(batch, nb),
        in_specs=[
            pl.BlockSpec((SEQ_BLOCK, qkv.shape[1]), row),
            pl.BlockSpec((SEQ_BLOCK, LANES), row),
            pl.BlockSpec((1, LANES), lambda b, c: (0, 0)),
            pl.BlockSpec((SEQ_BLOCK, SEQ_BLOCK), lambda b, c: (0, 0)),
        ],
        out_specs=[head_spec, head_spec, vt_spec],
        out_shape=[head_shape, head_shape, vt_shape],
        scratch_shapes=[pltpu.VMEM((1, LANES), F32)],
        compiler_params=_cparams("parallel", "arbitrary"),
        name="fox_prep",
    )(qkv, small, bias_row, tri)


def _fox_attn_kernel(q_ref, k_ref, vt_ref, nw_ref, o_ref, m_ref, acc_ref, sa_ref, sb_ref):
    i = pl.program_id(1)
    tq = q_ref.shape[1]
    kv_idx = lax.broadcasted_iota(jnp.int32, (tq, tq), 0)
    q_idx = lax.broadcasted_iota(jnp.int32, (tq, tq), 1)
    heads = range(N_FOX_HEADS)
    m_ref[...] = jnp.full(m_ref.shape, -jnp.inf, F32)
    acc_ref[...] = jnp.zeros_like(acc_ref)

    def logits_into(dst_ref, j):
        start = pl.multiple_of(j * tq, tq)
        for hd in heads:
            dst_ref[hd] = _dot_nt(k_ref[hd, pl.ds(start, tq), :], q_ref[hd])

    def fold(src_ref, j, diagonal=False):
        start = pl.multiple_of(j * tq, tq)
        for hd in heads:
            s = src_ref[hd]
            if diagonal:
                s = jnp.where(kv_idx <= q_idx, s, MASK_VALUE)
            m_old = m_ref[hd]
            m_new = jnp.maximum(m_old, jnp.max(s, axis=0, keepdims=True))
            p = jnp.exp(s - m_new).astype(BF16)
            acc_ref[hd] = jnp.exp(m_old - m_new) * acc_ref[hd] + _dot(vt_ref[hd, :, pl.ds(start, tq)], p)
            m_ref[hd] = m_new

    logits_into(sa_ref, 0)

    def two_blocks(jj, carry):
        logits_into(sb_ref, 2 * jj + 1)
        fold(sa_ref, 2 * jj)
        logits_into(sa_ref, 2 * jj + 2)
        fold(sb_ref, 2 * jj + 1)
        return carry

    lax.fori_loop(0, i // 2, two_blocks, 0)

    @pl.when(i % 2 == 1)
    def _():
        logits_into(sb_ref, i)
        fold(sa_ref, i - 1)
        fold(sb_ref, i, diagonal=True)

    @pl.when(i % 2 == 0)
    def _():
        fold(sa_ref, i, diagonal=True)

    normed = []
    for hd in heads:
        acc = acc_ref[hd]
        o = acc[:HEAD_DIM] / acc[HEAD_DIM:HEAD_DIM + 1]
        ms = jnp.mean(o * o, axis=0, keepdims=True)
        normed.append(o * lax.rsqrt(ms + EPS))
    tiles = []
    for pair in range(N_FOX_HEADS // 2):
        both = jnp.concatenate(normed[2 * pair:2 * pair + 2], axis=0)
        tiles.append(both.T * nw_ref[pair:pair + 1, :])
    o_ref[...] = jnp.concatenate(tiles, axis=1).astype(o_ref.dtype)


def _fox_attn(qa, ka, vt, nw_pairs, batch, seq):
    t = qa.shape[1]
    nq = seq // SEQ_BLOCK
    return pl.pallas_call(
        _fox_attn_kernel,
        grid=(batch, nq),
        in_specs=[
            pl.BlockSpec((N_FOX_HEADS, SEQ_BLOCK, LANES), lambda b, i: (0, b * nq + i, 0)),
            pl.BlockSpec((N_FOX_HEADS, seq, LANES), lambda b, i: (0, b, 0)),
            pl.BlockSpec((N_FOX_HEADS, LANES, seq), lambda b, i: (0, 0, b)),
            pl.BlockSpec((SUBLANES, LANES), lambda b, i: (0, 0)),
        ],
        out_specs=pl.BlockSpec((SEQ_BLOCK, N_FOX_HEADS * HEAD_DIM), lambda b, i: (b * nq + i, 0)),
        out_shape=jax.ShapeDtypeStruct((t, N_FOX_HEADS * HEAD_DIM), BF16),
        scratch_shapes=[
            pltpu.VMEM((N_FOX_HEADS, 1, SEQ_BLOCK), F32),
            pltpu.VMEM((N_FOX_HEADS, LANES, SEQ_BLOCK), F32),
            pltpu.VMEM((N_FOX_HEADS, SEQ_BLOCK, SEQ_BLOCK), F32),
            pltpu.VMEM((N_FOX_HEADS, SEQ_BLOCK, SEQ_BLOCK), F32),
        ],
        compiler_params=_cparams("parallel", "arbitrary"),
        name="fox_attn",
    )(qa, ka, vt, nw_pairs)


def _chunk_views(refs, c):
    return [r.at[pl.ds(c * SCAN_CHUNK, SCAN_CHUNK)] for r in refs]


def _hgrn2_kernel(hq_ref, hf_ref, hi_ref, hg_ref, par_ref, tri_ref, seg_ref, o_ref, st_ref):
    @pl.when(pl.program_id(1) == 0)
    def _():
        st_ref[...] = jnp.zeros_like(st_ref)

    for c in range(SCAN_CHUNKS_PER_STEP):
        hq, hf, hi, hg, o = _chunk_views((hq_ref, hf_ref, hi_ref, hg_ref, o_ref), c)
        _hgrn2_chunk(hq, hf, hi, hg, par_ref, tri_ref, seg_ref, o, st_ref)


def _hgrn2_chunk(hq_ref, hf_ref, hi_ref, hg_ref, par_ref, tri_ref, seg_ref, o_ref, st_ref):
    rows, width = hq_ref.shape
    log_lb, log1m_lb, one_m_lb, nw = (par_ref[r:r + 1, :] for r in range(4))
    f_raw = hf_ref[...]
    q = _silu(hq_ref[...]) * (HEAD_DIM ** -0.5)
    e_f = jnp.exp(-jnp.abs(f_raw))
    den = 1.0 + e_f
    b = log1m_lb + (jnp.minimum(f_raw, 0.0) - jnp.log(den))
    g = jnp.maximum(log_lb, b) + _log1p_exp(-jnp.abs(log_lb - b))
    k = one_m_lb * (jnp.where(f_raw > 0.0, e_f, 1.0) / den)
    v = hi_ref[...]
    cum = _exact_left(tri_ref[...], g)

    lane = lax.broadcasted_iota(jnp.int32, (1, width), 1)
    head_masks = [(lane >= hd * HEAD_DIM) & (lane < (hd + 1) * HEAD_DIM) for hd in range(N_HG_HEADS)]
    seg = seg_ref[...]
    v_b = v.astype(BF16)

    st = st_ref[...]
    o_state = _dot_nt((q * jnp.exp(cum)).astype(BF16), st.astype(BF16))

    n_sub = rows // SUB_BLOCK
    refs = [cum[i * SUB_BLOCK - 1:i * SUB_BLOCK] if i else jnp.zeros((1, width), F32) for i in range(n_sub)]
    local = [cum[i * SUB_BLOCK:(i + 1) * SUB_BLOCK] - refs[i] for i in range(n_sub)]

    def stack_heads(x):
        return jnp.concatenate([jnp.where(mk, x, 0.0) for mk in head_masks], axis=0).astype(BF16)

    def unstack_heads(base, p4):
        for hd, mk in enumerate(head_masks):
            base = base + jnp.where(mk, p4[hd * SUB_BLOCK:(hd + 1) * SUB_BLOCK], 0.0)
        return base

    def factored():
        scores = []
        for i in range(n_sub):
            hi = (i + 1) * SUB_BLOCK
            qs = q[hi - SUB_BLOCK:hi] * jnp.exp(local[i])
            ks = (k[:hi] * jnp.exp(refs[i] - cum[:hi])).astype(BF16)
            scores.append(_dot_nt(stack_heads(qs), ks))
        blocks = []
        for i in range(n_sub):
            hi = (i + 1) * SUB_BLOCK
            t_in = lax.broadcasted_iota(jnp.int32, (N_HG_HEADS * SUB_BLOCK, hi), 0) & (SUB_BLOCK - 1)
            s_in = lax.broadcasted_iota(jnp.int32, (N_HG_HEADS * SUB_BLOCK, hi), 1)
            sc = jnp.where(s_in <= t_in + (hi - SUB_BLOCK), scores[i], 0.0).astype(BF16)
            blocks.append(unstack_heads(o_state[hi - SUB_BLOCK:hi], _dot(sc, v_b[:hi])))
        return jnp.concatenate(blocks, axis=0)

    def pairwise():
        t_idx = lax.broadcasted_iota(jnp.int32, (SUB_BLOCK, 1), 0)
        blocks = []
        for i in range(n_sub):
            r0 = i * SUB_BLOCK
            q_i = q[r0:r0 + SUB_BLOCK]
            cum_i = cum[r0:r0 + SUB_BLOCK]
            o_i = o_state[r0:r0 + SUB_BLOCK]
            if i > 0:
                ks = (k[:r0] * jnp.exp(refs[i] - cum[:r0])).astype(BF16)
                sc = _dot_nt(stack_heads(q_i * jnp.exp(local[i])), ks)
                o_i = unstack_heads(o_i, _dot(sc.astype(BF16), v_b[:r0]))
            terms = []
            for s in range(SUB_BLOCK):
                keep = t_idx >= s
                e = _masked_exp(cum_i - cum[r0 + s:r0 + s + 1], keep)
                terms.append((q_i * e * k[r0 + s:r0 + s + 1]).astype(BF16))
            sums = _dot(jnp.concatenate(terms, axis=0), seg)
            for s in range(SUB_BLOCK):
                o_i = o_i + sums[s * SUB_BLOCK:(s + 1) * SUB_BLOCK] * v[r0 + s:r0 + s + 1]
            blocks.append(o_i)
        return jnp.concatenate(blocks, axis=0)

    worst = functools.reduce(jnp.minimum, local)
    o = lax.cond(jnp.min(worst) >= -HG_SAFE_DECAY, factored, pairwise)

    last = cum[rows - 1:rows]
    kd = (k * jnp.exp(last - cum)).astype(BF16)
    upd = _dot(v.T.astype(BF16), kd)
    st_ref[...] = st * jnp.exp(last) + jnp.where(seg > 0, upd, 0.0)

    sq = o * o
    s1 = sq.astype(BF16)
    s2 = (sq - s1.astype(F32)).astype(BF16)
    ms = (_dot(s1, seg) + _dot(s2, seg)) * (1.0 / HEAD_DIM)
    o_ref[...] = (o * lax.rsqrt(ms + EPS) * nw * _silu(hg_ref[...])).astype(o_ref.dtype)


def _hgrn2(hg, params, tri, seg, batch, seq):
    t = hg.shape[0]
    width = N_HG_HEADS * HEAD_DIM
    step_rows = SCAN_CHUNK * SCAN_CHUNKS_PER_STEP
    nc = seq // step_rows
    col = lambda c: pl.BlockSpec((step_rows, width), lambda b, j, c=c: (b * nc + j, c))
    return pl.pallas_call(
        _hgrn2_kernel,
        grid=(batch, nc),
        in_specs=[
            col(0), col(1), col(2), col(3),
            pl.BlockSpec(params.shape, lambda b, j: (0, 0)),
            pl.BlockSpec(tri.shape, lambda b, j: (0, 0)),
            pl.BlockSpec(seg.shape, lambda b, j: (0, 0)),
        ],
        out_specs=pl.BlockSpec((step_rows, width), lambda b, j: (b * nc + j, 0)),
        out_shape=jax.ShapeDtypeStruct((t, width), BF16),
        scratch_shapes=[pltpu.VMEM((width, width), F32)],
        compiler_params=_cparams("parallel", "arbitrary"),
        name="hgrn2",
    )(hg, hg, hg, hg, params, tri, seg)


def _ssd_kernel(z_ref, xbc_ref, small_ref, cw_ref, cb_ref, hp_ref, sp_ref, nw_ref, tri_ref, exp_ref, shift_ref,
                o_ref, buf_ref, st_ref):
    @pl.when(pl.program_id(1) == 0)
    def _():
        buf_ref[...] = jnp.zeros_like(buf_ref)
        st_ref[...] = jnp.zeros_like(st_ref)

    for c in range(SCAN_CHUNKS_PER_STEP):
        z, xbc, small, o = _chunk_views((z_ref, xbc_ref, small_ref, o_ref), c)
        _ssd_chunk(z, xbc, small, cw_ref, cb_ref, hp_ref, sp_ref, nw_ref, tri_ref, exp_ref, shift_ref, o,
                   buf_ref, st_ref)


def _ssd_chunk(z_ref, xbc_ref, small_ref, cw_ref, cb_ref, hp_ref, sp_ref, nw_ref, tri_ref, exp_ref, shift_ref,
               o_ref, buf_ref, st_ref):
    rows = z_ref.shape[0]
    width = z_ref.shape[1]

    x_in = xbc_ref[...]
    buf_ref[rows:2 * rows, :] = x_in.astype(BF16)
    shifted = _dot(shift_ref[...], buf_ref[...])
    buf_ref[0:rows, :] = buf_ref[rows:2 * rows, :]
    conv = cb_ref[...] + cw_ref[M2_CONV - 1:M2_CONV, :] * x_in
    for k in range(1, M2_CONV):
        tap = M2_CONV - 1 - k
        conv = conv + cw_ref[tap:tap + 1, :] * shifted[(k - 1) * rows:k * rows]
    xc = _silu(conv)
    xs = xc[:, :width]
    gn = M2_GROUPS * M2_STATE
    bm = xc[:, width:width + gn]
    cm = xc[:, width + gn:width + 2 * gn]

    d_x = hp_ref[0:1, :]
    dt_bias_c, a_log_c = (sp_ref[r:r + 1, :] for r in range(2))
    small = small_ref[...]
    tri = tri_ref[...]
    dt_c = _softplus(small + dt_bias_c)
    a_cum_c = _exact_left(tri, dt_c * -jnp.exp(a_log_c))
    a_cum_t = a_cum_c.T
    expand = exp_ref[...]
    dt_x = _exact_right(dt_c, expand)
    a_cum_x = _exact_right(a_cum_c, expand)

    row = lax.broadcasted_iota(jnp.int32, (rows, rows), 0)
    col = lax.broadcasted_iota(jnp.int32, (rows, rows), 1)
    causal = col <= row
    lane = lax.broadcasted_iota(jnp.int32, (1, LANES), 1)
    low = lane < HEAD_DIM

    xdt = xs * dt_x
    a_last = a_cum_x[rows - 1:rows]
    xd = (xdt * jnp.exp(a_last - a_cum_x)).astype(BF16)
    decay_out = jnp.exp(a_cum_x)
    st = st_ref[...]
    heads_per_group = M2_HEADS // M2_GROUPS
    gw = heads_per_group * HEAD_DIM
    cbs, y_offs, new_states = [], [], []
    for g in range(M2_GROUPS):
        b_g = bm[:, g * M2_STATE:(g + 1) * M2_STATE]
        c_g = cm[:, g * M2_STATE:(g + 1) * M2_STATE].astype(BF16)
        cbs.append(jnp.where(causal, _dot_nt(c_g, b_g.astype(BF16)), 0.0))
        y_offs.append(_dot(c_g, st[:, g * gw:(g + 1) * gw].astype(BF16)) * decay_out[:, g * gw:(g + 1) * gw])
        new_states.append(_dot(b_g.T.astype(BF16), xd[:, g * gw:(g + 1) * gw]))
    st_ref[...] = st * jnp.exp(a_last) + jnp.concatenate(new_states, axis=1)

    y_tiles = []
    for tile in range(M2_HEADS // 2):
        g = (2 * tile) // heads_per_group
        pair = tile - g * (heads_per_group // 2)
        x_tile = xdt[:, tile * LANES:(tile + 1) * LANES]
        y_pair = y_offs[g][:, pair * LANES:(pair + 1) * LANES]
        for half in range(2):
            hd = 2 * tile + half
            seg = a_cum_c[:, SUBLANES + hd:SUBLANES + hd + 1] - a_cum_t[SUBLANES + hd:SUBLANES + hd + 1, :]
            m_h = (cbs[g] * jnp.exp(jnp.minimum(seg, 0.0))).astype(BF16)
            x_h = jnp.where(low if half == 0 else ~low, x_tile, 0.0).astype(BF16)
            y_pair = y_pair + _dot(m_h, x_h)
        y_tiles.append(y_pair)

    y = jnp.concatenate(y_tiles, axis=1) + d_x * xs
    y = y * _silu(z_ref[...])
    nw = nw_ref[...]
    outs = []
    for g in range(M2_GROUPS):
        outs.append(_rms(y[:, g * gw:(g + 1) * gw], nw[:, g * gw:(g + 1) * gw]))
    o_ref[...] = jnp.concatenate(outs, axis=1).astype(o_ref.dtype)


def _ssd(z, xbc, small, conv_w, conv_b, head_rows, small_rows, norm_w, tri, expand, shift, batch, seq):
    t, width = z.shape
    ch = xbc.shape[1]
    step_rows = SCAN_CHUNK * SCAN_CHUNKS_PER_STEP
    nc = seq // step_rows
    row = lambda b, j: (b * nc + j, 0)
    const = lambda a: pl.BlockSpec(a.shape, lambda b, j: (0, 0))
    return pl.pallas_call(
        _ssd_kernel,
        grid=(batch, nc),
        in_specs=[
            pl.BlockSpec((step_rows, width), row),
            pl.BlockSpec((step_rows, ch), row),
            pl.BlockSpec((step_rows, LANES), row),
            const(conv_w), const(conv_b), const(head_rows), const(small_rows), const(norm_w),
            const(tri), const(expand), const(shift),
        ],
        out_specs=pl.BlockSpec((step_rows, width), row),
        out_shape=jax.ShapeDtypeStruct((t, width), BF16),
        scratch_shapes=[
            pltpu.VMEM((2 * SCAN_CHUNK, ch), BF16),
            pltpu.VMEM((M2_STATE, width), F32),
        ],
        compiler_params=_cparams("parallel", "arbitrary"),
        name="ssd",
    )(z, xbc, small, conv_w, conv_b, head_rows, small_rows, norm_w, tri, expand, shift)


def _xattn_kernel(h_ref, a_ref, b_ref, c_ref, wout_ref, nw_ref, wq_ref, kv_ref, wo_ref, *rest):
    o_ref = rest[-1]
    wa = a_ref.shape[1]
    wb = b_ref.shape[1]
    h = h_ref[...] + _dot(a_ref[...], wout_ref[0:wa, :])
    h = h + _dot(b_ref[...], wout_ref[wa:wa + wb, :])
    h = h + _dot(c_ref[...], wout_ref[wa + wb:, :])
    d = h.shape[1]
    hd = d // XA_HEADS
    q = (_dot(_rms(h, nw_ref[...]).astype(BF16), wq_ref[...]) * (hd ** -0.5)).astype(BF16)
    logits = [_dot_nt(q[:, a * hd:(a + 1) * hd], kv_ref[:, a * hd:(a + 1) * hd]) for a in range(XA_HEADS)]
    outs = []
    for a, s in enumerate(logits):
        p = jnp.exp(s - jnp.max(s, axis=-1, keepdims=True))
        p = p / jnp.sum(p, axis=-1, keepdims=True)
        outs.append(_dot(p.astype(BF16), kv_ref[:, d + a * hd:d + (a + 1) * hd]).astype(BF16))
    h = h + _dot(jnp.concatenate(outs, axis=1), wo_ref[...])
    if len(rest) > 1:
        fnw_ref, w1_ref, w2_ref = rest[:3]
        h = h + _swiglu(_rms(h, fnw_ref[...]).astype(BF16), w1_ref, w2_ref, w2_ref.shape[0])
    o_ref[...] = h


def _xattn(h, o_fox, o_hg, o_m2, w_out, nw, wq, kv, wo, layer, batch, seq, ffn=None):
    t, d = h.shape
    n_mem = kv.shape[0] // batch
    nb = seq // ROW_BLOCK
    row = lambda x: pl.BlockSpec((ROW_BLOCK, x.shape[1]), lambda b, i: (b * nb + i, 0))
    extra_specs, extra_args = [], []
    if ffn is not None:
        fnw, w1, w2, ffn_layer = ffn
        extra_specs = [_resident(fnw), _layer_resident(w1, ffn_layer), _layer_resident(w2, ffn_layer)]
        extra_args = [fnw, w1, w2]
    return pl.pallas_call(
        _xattn_kernel,
        grid=(batch, nb),
        in_specs=[
            row(h), row(o_fox), row(o_hg), row(o_m2),
            _layer_resident(w_out, layer), _resident(nw), _layer_resident(wq, layer),
            pl.BlockSpec((n_mem, kv.shape[1]), lambda b, i: (b, 0)),
            _layer_resident(wo, layer),
        ] + extra_specs,
        out_specs=row(h),
        out_shape=jax.ShapeDtypeStruct((t, d), F32),
        compiler_params=_cparams("parallel", "parallel"),
        name="xattn_ffn" if ffn is not None else "xattn",
    )(h, o_fox, o_hg, o_m2, w_out, nw, wq, kv, wo, *extra_args)


def _swiglu(x, w1_ref, w2_ref, d_ff):
    a = (_silu(_dot(x, w1_ref[:, :d_ff])) * _dot(x, w1_ref[:, d_ff:])).astype(BF16)
    return _dot(a, w2_ref[...])


def _router_kernel(h_ref, nw_ref, wr_ref, tri_ref, idx_ref, gate_ref, rank_ref, cnt_ref, carry_ref):
    @pl.when(pl.program_id(0) == 0)
    def _():
        carry_ref[...] = jnp.zeros_like(carry_ref)

    xn = _rms(h_ref[...], nw_ref[...])
    x1 = xn.astype(BF16)
    x2 = (xn - x1.astype(F32)).astype(BF16)
    w12 = wr_ref[:, :2 * LANES]
    big = _dot(x1, w12) + _dot(x2, w12)
    logits = big[:, :LANES] + big[:, LANES:] + _dot(x1, wr_ref[:, 2 * LANES:])
    lane = lax.broadcasted_iota(jnp.int32, logits.shape, 1)
    logits = jnp.where(lane < N_EXPERTS, logits, -jnp.inf)
    m1 = jnp.max(logits, axis=-1, keepdims=True)
    i1 = jnp.min(jnp.where(logits == m1, lane, LANES), axis=-1, keepdims=True)
    rest = jnp.where(lane == i1, -jnp.inf, logits)
    m2 = jnp.max(rest, axis=-1, keepdims=True)
    i2 = jnp.min(jnp.where(rest == m2, lane, LANES), axis=-1, keepdims=True)
    e2 = jnp.exp(m2 - m1)
    denom = 1.0 + e2
    idx_ref[...] = jnp.where(lane == 0, i1, jnp.where(lane == 1, i2, 0))
    gate_ref[...] = jnp.where(lane == 0, 1.0 / denom, jnp.where(lane == 1, e2 / denom, 0.0))

    hit1 = lane == i1
    hit2 = lane == i2
    member = jnp.where(hit1, 1.0, jnp.where(hit2, 1.0, 0.0))
    incl = _dot(tri_ref[...], member.astype(BF16)) + carry_ref[...]
    excl = incl - member
    r1 = jnp.sum(jnp.where(hit1, excl, 0.0), axis=-1, keepdims=True)
    r2 = jnp.sum(jnp.where(hit2, excl, 0.0), axis=-1, keepdims=True)
    rows = member.shape[0]
    carry_ref[...] = incl[rows - 1:rows, :]
    rank_ref[...] = jnp.where(lane == 0, r1, jnp.where(lane == 1, r2, 0.0)).astype(jnp.int32)
    cnt_ref[...] = jnp.broadcast_to(incl[rows - 1:rows, :], cnt_ref.shape)


def _router(h, nw, wr, tri):
    t, d = h.shape
    row = lambda w: pl.BlockSpec((ROW_BLOCK, w), lambda i: (i, 0))
    return pl.pallas_call(
        _router_kernel,
        grid=(t // ROW_BLOCK,),
        in_specs=[row(d), _resident(nw), _resident(wr), _resident(tri)],
        out_specs=[row(LANES), row(LANES), row(LANES), pl.BlockSpec((SUBLANES, LANES), lambda i: (0, 0))],
        out_shape=[
            jax.ShapeDtypeStruct((t, LANES), jnp.int32),
            jax.ShapeDtypeStruct((t, LANES), F32),
            jax.ShapeDtypeStruct((t, LANES), jnp.int32),
            jax.ShapeDtypeStruct((SUBLANES, LANES), F32),
        ],
        scratch_shapes=[pltpu.VMEM((1, LANES), F32)],
        compiler_params=_cparams("arbitrary"),
        name="router",
    )(h, nw, wr, tri)


def _slot_kernel(idx_ref, rank_ref, start_ref, o_ref):
    idx = idx_ref[...]
    lane = lax.broadcasted_iota(jnp.int32, idx.shape, 1)
    starts = start_ref[...]
    s1 = jnp.sum(jnp.where(lane == idx[:, 0:1], starts, 0.0), axis=-1, keepdims=True)
    s2 = jnp.sum(jnp.where(lane == idx[:, 1:2], starts, 0.0), axis=-1, keepdims=True)
    o_ref[...] = jnp.where(lane == 0, s1, jnp.where(lane == 1, s2, 0.0)).astype(jnp.int32) + rank_ref[...]


def _slots(idx, rank, start_row):
    t = idx.shape[0]
    rows = min(t, SLOT_ROWS)
    row = pl.BlockSpec((rows, LANES), lambda i: (i, 0))
    return pl.pallas_call(
        _slot_kernel,
        grid=(t // rows,),
        in_specs=[row, row, pl.BlockSpec((1, LANES), lambda i: (0, 0))],
        out_specs=row,
        out_shape=jax.ShapeDtypeStruct((t, LANES), jnp.int32),
        compiler_params=_cparams("parallel"),
        name="moe_slots",
    )(idx, rank, start_row)


def _dispatch_kernel(pend_ref, padded_ref, nused_ref, dest_ref, h_ref, nw_ref, xs_ref, x_ref, zero_ref, sem):
    rows = h_ref.shape[0]
    n_blk = xs_ref.shape[0] // MOE_ROWS
    x_ref[...] = _rms(h_ref[...], nw_ref[...]).reshape(x_ref.shape)

    @pl.when(pl.program_id(0) == 0)
    def _():
        zero_ref[...] = jnp.zeros_like(zero_ref)

        def fill_block(start):
            fill = pltpu.make_async_copy(zero_ref, xs_ref.at[pl.ds(start, MOE_ROWS), :], sem)
            fill.start()
            fill.wait()

        for e in range(N_EXPERTS):
            @pl.when(padded_ref[e] > 0)
            def _():
                fill_block(pl.multiple_of(pend_ref[e] - MOE_ROWS, MOE_ROWS))

            @pl.when(nused_ref[0] + e < n_blk)
            def _():
                fill_block(pl.multiple_of((nused_ref[0] + e) * MOE_ROWS, MOE_ROWS))

    def issue(group, carry):
        base = pl.multiple_of(group * SUBLANES, SUBLANES)
        for s in range(SUBLANES):
            for k in range(2):
                slot = dest_ref[0, 0, k * rows + base + s]
                pltpu.make_async_copy(
                    x_ref.at[group, pl.ds(s, 1), :], xs_ref.at[pl.ds(slot, 1), :], sem).start(priority=k)
        return carry

    lax.fori_loop(0, rows // SUBLANES, issue, 0)
    for k in range(2):
        pltpu.make_async_copy(zero_ref, xs_ref.at[pl.ds(0, rows), :], sem).wait()


def _dispatch(pends, padded, n_used, dest_t, h, nw, cap):
    t, d = h.shape
    grid_spec = pltpu.PrefetchScalarGridSpec(
        num_scalar_prefetch=3,
        grid=(t // ROW_BLOCK,),
        in_specs=[
            pl.BlockSpec((1, 1, 2 * ROW_BLOCK), lambda i, pe, pa, nu: (i, 0, 0), memory_space=pltpu.SMEM),
            pl.BlockSpec((ROW_BLOCK, d), lambda i, pe, pa, nu: (i, 0)),
            pl.BlockSpec((1, d), lambda i, pe, pa, nu: (0, 0)),
        ],
        out_specs=pl.BlockSpec(memory_space=pl.ANY),
        scratch_shapes=[pltpu.VMEM((ROW_BLOCK // SUBLANES, SUBLANES, d), F32), pltpu.VMEM((MOE_ROWS, d), F32),
                        pltpu.SemaphoreType.DMA(())],
    )
    return pl.pallas_call(
        _dispatch_kernel,
        grid_spec=grid_spec,
        out_shape=jax.ShapeDtypeStruct((cap, d), F32),
        compiler_params=_cparams("arbitrary"),
        name="moe_dispatch",
    )(pends, padded, n_used, dest_t, h, nw)


def _expert_kernel(be_ref, nused_ref, x_ref, w1_ref, w2_ref, o_ref):
    live = pl.program_id(0) < nused_ref[0]

    @pl.when(live)
    def _():
        o_ref[...] = _swiglu(x_ref[...].astype(BF16), w1_ref, w2_ref, w2_ref.shape[0])

    @pl.when(jnp.logical_not(live))
    def _():
        o_ref[...] = jnp.zeros_like(o_ref)


def _experts(blk_expert, n_used, xs, w1, w2, layer):
    cap, d = xs.shape
    blk = lambda i, be, nu: (i, 0)
    expert = lambda w: pl.BlockSpec((None, None) + w.shape[2:], lambda i, be, nu: (layer, be[i], 0, 0),
                                    pipeline_mode=pl.Buffered(1))
    grid_spec = pltpu.PrefetchScalarGridSpec(
        num_scalar_prefetch=2,
        grid=(cap // MOE_ROWS,),
        in_specs=[pl.BlockSpec((MOE_ROWS, d), blk), expert(w1), expert(w2)],
        out_specs=pl.BlockSpec((MOE_ROWS, d), blk),
    )
    return pl.pallas_call(
        _expert_kernel,
        grid_spec=grid_spec,
        out_shape=jax.ShapeDtypeStruct((cap, d), F32),
        compiler_params=_cparams("arbitrary"),
        name="experts",
    )(blk_expert, n_used, xs, w1, w2)


def _combine_kernel(dest_ref, h_ref, gate_ref, nw_ref, yb_ref, o_ref, buf_ref, sem, *, final_norm):
    rows = h_ref.shape[0]

    def issue(group, carry):
        base = pl.multiple_of(group * SUBLANES, SUBLANES)
        for s in range(SUBLANES):
            for k in range(2):
                slot = dest_ref[0, 0, k * rows + base + s]
                pltpu.make_async_copy(
                    yb_ref.at[pl.ds(slot, 1), :], buf_ref.at[k, group, pl.ds(s, 1), :], sem).start(priority=k)
        return carry

    lax.fori_loop(0, rows // SUBLANES, issue, 0)
    for k in range(2):
        pltpu.make_async_copy(buf_ref.at[k], buf_ref.at[k], sem).wait()
    gate = gate_ref[...]
    picked = [buf_ref[k].reshape(rows, h_ref.shape[1]) for k in range(2)]
    out = h_ref[...] + gate[:, 0:1] * picked[0] + gate[:, 1:2] * picked[1]
    if final_norm:
        out = _rms(out, nw_ref[...])
    o_ref[...] = out


def _combine(dest_t, h, gate, nw, yb, final_norm):
    t, d = h.shape
    return pl.pallas_call(
        functools.partial(_combine_kernel, final_norm=final_norm),
        grid=(t // ROW_BLOCK,),
        in_specs=[
            pl.BlockSpec((1, 1, 2 * ROW_BLOCK), lambda i: (i, 0, 0), memory_space=pltpu.SMEM),
            pl.BlockSpec((ROW_BLOCK, d), lambda i: (i, 0)),
            pl.BlockSpec((ROW_BLOCK, LANES), lambda i: (i, 0)),
            pl.BlockSpec((1, d), lambda i: (0, 0)),
            pl.BlockSpec(memory_space=pl.ANY),
        ],
        out_specs=pl.BlockSpec((ROW_BLOCK, d), lambda i: (i, 0)),
        out_shape=jax.ShapeDtypeStruct((t, d), F32),
        scratch_shapes=[pltpu.VMEM((2, ROW_BLOCK // SUBLANES, SUBLANES, d), F32), pltpu.SemaphoreType.DMA(())],
        compiler_params=_cparams("arbitrary"),
        name="moe_combine",
    )(dest_t, h, gate, nw, yb)


def _final_norm_kernel(h_ref, nw_ref, o_ref):
    o_ref[...] = _rms(h_ref[...], nw_ref[...])


def _final_norm(h, nw):
    t, d = h.shape
    return pl.pallas_call(
        _final_norm_kernel,
        grid=(t // ROW_BLOCK,),
        in_specs=[pl.BlockSpec((ROW_BLOCK, d), lambda i: (i, 0)), pl.BlockSpec((1, d), lambda i: (0, 0))],
        out_specs=pl.BlockSpec((ROW_BLOCK, d), lambda i: (i, 0)),
        out_shape=jax.ShapeDtypeStruct((t, d), F32),
        compiler_params=_cparams("parallel"),
        name="final_norm",
    )(h, nw)


def _moe(h, nw, router_w, w1, w2, layer, tri, out_nw, final_norm):
    t, d = h.shape
    assert MOE_ROWS == ROW_BLOCK
    r1, r2, r3 = _split3(jnp.pad(router_w, ((0, 0), (0, LANES - N_EXPERTS))))
    idx, gate, rank, cnt = _router(h, nw, jnp.concatenate([r1, r2, r3], axis=1), tri)
    counts = cnt[0, :N_EXPERTS].astype(jnp.int32)
    padded = (counts + MOE_ROWS - 1) // MOE_ROWS * MOE_ROWS
    pends = jnp.cumsum(padded)
    starts = pends - padded
    cap = (2 * t // MOE_ROWS + N_EXPERTS) * MOE_ROWS
    n_blk = cap // MOE_ROWS
    blk_start = jnp.arange(n_blk, dtype=jnp.int32) * MOE_ROWS
    blk_expert = jnp.minimum(jnp.sum(blk_start[:, None] >= pends[None, :], axis=1), N_EXPERTS - 1).astype(jnp.int32)
    n_used = (pends[-1:] // MOE_ROWS).astype(jnp.int32)
    start_row = jnp.zeros((1, LANES), F32).at[0, :N_EXPERTS].set(starts.astype(F32))
    dest = _slots(idx, rank, start_row)
    dest_t = dest[:, :2].reshape(t // ROW_BLOCK, ROW_BLOCK, 2).transpose(0, 2, 1).reshape(
        t // ROW_BLOCK, 1, 2 * ROW_BLOCK)
    xs = _dispatch(pends.astype(jnp.int32), padded.astype(jnp.int32), n_used, dest_t, h, nw, cap)
    yb = _experts(blk_expert, n_used, xs, w1, w2, layer)
    return _combine(dest_t, h, gate, out_nw, yb, final_norm)


def _tri(n):
    return jnp.tril(jnp.ones((n, n), F32)).astype(BF16)


def kernel(x, mem, mix_norm_w, w_in, fox_f_bias, fox_norm_w, hg_lb_raw, hg_norm_w, m2_conv_w, m2_conv_b, m2_dt_bias, m2_a_log, m2_d, m2_norm_w, w_out, xa_norm_w, xa_mem_norm_w, xa_w_q, xa_w_kv, xa_w_o, ffn_norm_w, ffn_w1, ffn_w2, moe_router, moe_w1, moe_w2, final_norm_w):
    batch, seq, d = x.shape
    depth = w_in.shape[0]
    t = batch * seq
    fox_w = N_FOX_HEADS * HEAD_DIM
    hg_w = N_HG_HEADS * HEAD_DIM
    m2_w = M2_HEADS * HEAD_DIM
    conv_ch = m2_w + 2 * M2_GROUPS * M2_STATE
    in_splits = (fox_w, fox_w, fox_w, N_FOX_HEADS, hg_w, hg_w, hg_w, hg_w, m2_w, conv_ch, M2_HEADS)
    offs = [0]
    for s in in_splits:
        offs.append(offs[-1] + s)

    lb_p = jax.nn.softmax(hg_lb_raw.astype(F32), axis=0)
    hg_lb = jnp.cumsum(lb_p, axis=0) - lb_p[0]

    tri_seq = _tri(SEQ_BLOCK)
    tri_chunk = _tri(SCAN_CHUNK)
    head_of_lane = jnp.arange(hg_w) // HEAD_DIM
    seg_ones = (head_of_lane[:, None] == head_of_lane[None, :]).astype(BF16)
    expand = (jnp.arange(LANES)[:, None] - SUBLANES == jnp.arange(m2_w)[None, :] // HEAD_DIM).astype(BF16)
    shift_row = jnp.arange((M2_CONV - 1) * SCAN_CHUNK)
    shift_src = SCAN_CHUNK + shift_row % SCAN_CHUNK - (shift_row // SCAN_CHUNK + 1)
    conv_shift = (jnp.arange(2 * SCAN_CHUNK)[None, :] == shift_src[:, None]).astype(BF16)

    def pad_lanes(v, offset=0, width=LANES):
        return jnp.zeros((1, width), F32).at[0, offset:offset + v.shape[0]].set(v)

    small_w = jnp.zeros((depth, d, LANES), F32)
    small_w = small_w.at[:, :, 0:N_FOX_HEADS].set(w_in[:, :, offs[3]:offs[4]])
    small_w = small_w.at[:, :, SUBLANES:SUBLANES + M2_HEADS].set(w_in[:, :, offs[10]:offs[11]])
    in_weights = tuple(w.astype(BF16) for w in (
        w_in[:, :, offs[0]:offs[3]], small_w, w_in[:, :, offs[4]:offs[8]], w_in[:, :, offs[8]:offs[9]],
        w_in[:, :, offs[9]:offs[10]]))
    in_dtypes = (BF16, F32, F32, F32, F32)
    w_out_b, xa_wq_b, xa_wkv_b, xa_wo_b, ffn_w1_b, ffn_w2_b, moe_w1_b, moe_w2_b = (
        w.astype(BF16) for w in (w_out, xa_w_q, xa_w_kv, xa_w_o, ffn_w1, ffn_w2, moe_w1, moe_w2))

    h = x.reshape(t, d)
    mem2 = mem.reshape(batch * mem.shape[1], d)
    for layer in range(depth):
        qkv, small, hg, z, xbc = _in_proj(h, mix_norm_w[layer][None, :], in_weights, layer, in_dtypes)

        qa, ka, va = _fox_prep(qkv, small, pad_lanes(fox_f_bias[layer]), tri_seq, batch, seq)
        fox_nw = jnp.zeros((SUBLANES, LANES), F32).at[:fox_w // LANES].set(
            fox_norm_w[layer].reshape(fox_w // LANES, LANES))
        o_fox = _fox_attn(qa, ka, va, fox_nw, batch, seq)

        lb = hg_lb[layer]
        hg_params = jnp.zeros((SUBLANES, hg_w), F32)
        hg_params = hg_params.at[0].set(jnp.log(jnp.maximum(lb, LB_FLOOR)))
        hg_params = hg_params.at[1].set(jnp.log1p(-lb))
        hg_params = hg_params.at[2].set(1.0 - lb)
        hg_params = hg_params.at[3].set(hg_norm_w[layer])
        o_hg = _hgrn2(hg, hg_params, tri_chunk, seg_ones, batch, seq)

        conv_w = jnp.zeros((SUBLANES, conv_ch), F32).at[:M2_CONV].set(m2_conv_w[layer])
        head_rows = jnp.zeros((SUBLANES, m2_w), F32).at[0].set(jnp.repeat(m2_d[layer], HEAD_DIM))
        small_rows = jnp.concatenate(
            [pad_lanes(m2_dt_bias[layer], SUBLANES), pad_lanes(m2_a_log[layer], SUBLANES),
             jnp.zeros((SUBLANES - 2, LANES), F32)], axis=0)
        o_m2 = _ssd(z, xbc, small, conv_w, m2_conv_b[layer][None, :], head_rows, small_rows,
                    m2_norm_w[layer][None, :], tri_chunk, expand, conv_shift, batch, seq)

        kv = _norm_proj(mem2, xa_mem_norm_w[layer][None, :], xa_wkv_b, layer, BF16, mem.shape[1])
        nw = ffn_norm_w[layer][None, :]
        dense = (nw, ffn_w1_b, ffn_w2_b, layer // 2) if layer % 2 == 0 else None
        h = _xattn(h, o_fox, o_hg, o_m2, w_out_b, xa_norm_w[layer][None, :], xa_wq_b, kv, xa_wo_b, layer,
                   batch, seq, ffn=dense)

        if layer % 2:
            last = layer == depth - 1
            h = _moe(h, nw, moe_router[layer // 2], moe_w1_b, moe_w2_b, layer // 2, tri_seq,
                     final_norm_w[None, :], last)
    if depth % 2:
        h = _final_norm(h, final_norm_w[None, :])
    return h.reshape(batch, seq, d)
```

```python
import functools

import jax
import jax.numpy as jnp
from jax import lax
from jax.experimental import pallas as pl
from jax.experimental.pallas import tpu as pltpu

F32 = jnp.float32
BF16 = jnp.bfloat16

EPS = 1e-6
MASK_VALUE = -1e9
LB_FLOOR = 1e-30
HEAD_DIM = 64
N_FOX_HEADS = 4
N_HG_HEADS = 4
M2_HEADS = 8
M2_GROUPS = 2
M2_STATE = 128
M2_CONV = 4
XA_HEADS = 4
N_EXPERTS = 8

LANES = 128
SUBLANES = 8
VMEM_LIMIT_BYTES = 56 * 1024 * 1024

ROW_BLOCK = 512
SEQ_BLOCK = 512
SCAN_CHUNK = 128
SCAN_CHUNKS_PER_STEP = 8
SUB_BLOCK = 16
HG_SAFE_DECAY = 60.0
MOE_ROWS = 512
SLOT_ROWS = 4096


def _cparams(*sem):
    return pltpu.CompilerParams(dimension_semantics=sem, vmem_limit_bytes=VMEM_LIMIT_BYTES)


def _split3(x):
    x1 = x.astype(BF16)
    r1 = x - x1.astype(F32)
    x2 = r1.astype(BF16)
    x3 = (r1 - x2.astype(F32)).astype(BF16)
    return x1, x2, x3


def _dot(a, b):
    return jnp.dot(a, b, preferred_element_type=F32)


def _dot_nt(a, b):
    return lax.dot_general(a, b, (((1,), (1,)), ((), ())), preferred_element_type=F32)


def _exact_left(mat01, x):
    x1, x2, x3 = _split3(x)
    return _dot(mat01, x1) + _dot(mat01, x2) + _dot(mat01, x3)


def _exact_right(x, mat01):
    x1, x2, x3 = _split3(x)
    return _dot(x1, mat01) + _dot(x2, mat01) + _dot(x3, mat01)


def _rms(x, w):
    return x * lax.rsqrt(jnp.mean(x * x, axis=-1, keepdims=True) + EPS) * w


def _sigmoid(x):
    return 1.0 / (1.0 + jnp.exp(-x))


def _silu(x):
    return x * _sigmoid(x)


def _log1p_exp(x):
    return jnp.log(1.0 + jnp.exp(x))


def _log_sigmoid(x):
    return jnp.minimum(x, 0.0) - _log1p_exp(-jnp.abs(x))


def _softplus(x):
    return jnp.maximum(x, 0.0) + _log1p_exp(-jnp.abs(x))


def _masked_exp(x, mask):
    return jnp.where(mask, jnp.exp(jnp.where(mask, x, 0.0)), 0.0)


def _resident(a):
    zeros = (0,) * a.ndim
    return pl.BlockSpec(a.shape, lambda *_: zeros, pipeline_mode=pl.Buffered(1))


def _layer_resident(stacked, layer):
    index = (layer,) + (0,) * (stacked.ndim - 1)
    return pl.BlockSpec((None,) + stacked.shape[1:], lambda *_: index, pipeline_mode=pl.Buffered(1))


def _in_proj_kernel(x_ref, nw_ref, *refs):
    n = len(refs) // 2
    xn = _rms(x_ref[...], nw_ref[...]).astype(BF16)
    for w_ref, o_ref in zip(refs[:n], refs[n:]):
        o_ref[...] = _dot(xn, w_ref[...]).astype(o_ref.dtype)


def _in_proj(h, nw, weights, layer, out_dtypes):
    t, d = h.shape
    widths = [w.shape[-1] for w in weights]
    return pl.pallas_call(
        _in_proj_kernel,
        grid=(t // ROW_BLOCK,),
        in_specs=[pl.BlockSpec((ROW_BLOCK, d), lambda i: (i, 0)), _resident(nw)]
        + [_layer_resident(w, layer) for w in weights],
        out_specs=[pl.BlockSpec((ROW_BLOCK, w), lambda i: (i, 0)) for w in widths],
        out_shape=[jax.ShapeDtypeStruct((t, w), dt) for w, dt in zip(widths, out_dtypes)],
        compiler_params=_cparams("parallel"),
        name="in_proj",
    )(h, nw, *weights)


def _norm_proj_kernel(x_ref, nw_ref, w_ref, o_ref):
    xn = _rms(x_ref[...], nw_ref[...]).astype(BF16)
    o_ref[...] = _dot(xn, w_ref[...]).astype(o_ref.dtype)


def _norm_proj(x, nw, w, layer, out_dtype, rows):
    t, d = x.shape
    n = w.shape[-1]
    return pl.pallas_call(
        _norm_proj_kernel,
        grid=(t // rows,),
        in_specs=[pl.BlockSpec((rows, d), lambda i: (i, 0)), _resident(nw), _layer_resident(w, layer)],
        out_specs=pl.BlockSpec((rows, n), lambda i: (i, 0)),
        out_shape=jax.ShapeDtypeStruct((t, n), out_dtype),
        compiler_params=_cparams("parallel"),
        name="norm_proj",
    )(x, nw, w)


def _fox_prep_kernel(qkv_ref, small_ref, bias_ref, tri_ref, q_ref, k_ref, v_ref, carry_ref):
    @pl.when(pl.program_id(1) == 0)
    def _():
        carry_ref[...] = jnp.zeros_like(carry_ref)

    rows = qkv_ref.shape[0]
    log_f = _log_sigmoid(small_ref[...] + bias_ref[...])
    c = _exact_left(tri_ref[...], log_f) + carry_ref[...]
    carry_ref[...] = c[rows - 1:rows, :]

    lane = lax.broadcasted_iota(jnp.int32, (rows, LANES), 1)
    width = N_FOX_HEADS * HEAD_DIM
    scale = HEAD_DIM ** -0.5
    for hd in range(N_FOX_HEADS):
        tile = (hd * HEAD_DIM) // LANES
        ch = c[:, hd:hd + 1]
        c1 = ch.astype(BF16).astype(F32)
        r1 = ch - c1
        c2 = r1.astype(BF16).astype(F32)
        c3 = r1 - c2

        def head_tile(base):
            x = qkv_ref[:, base + tile * LANES: base + (tile + 1) * LANES].astype(F32)
            if (hd * HEAD_DIM) % LANES:
                x = pltpu.roll(x, LANES - (hd * HEAD_DIM) % LANES, axis=1)
            return x

        def augment(x, first, second):
            out = jnp.where(lane < HEAD_DIM, x, 0.0)
            for j, val in enumerate(first + second):
                out = jnp.where(lane == HEAD_DIM + j, val, out)
            return out.astype(BF16)

        ones = (1.0, 1.0, 1.0)
        q_ref[hd] = augment(head_tile(0) * scale, ones, (c1, c2, c3))
        k_ref[hd] = augment(head_tile(width), (-c1, -c2, -c3), ones)
        v_aug = jnp.where(lane < HEAD_DIM, head_tile(2 * width), jnp.where(lane == HEAD_DIM, 1.0, 0.0))
        v_ref[hd] = v_aug.T.astype(BF16)


def _fox_prep(qkv, small, bias_row, tri, batch, seq):
    t = qkv.shape[0]
    nb = seq // SEQ_BLOCK
    row = lambda b, c: (b * nb + c, 0)
    head_spec = pl.BlockSpec((N_FOX_HEADS, SEQ_BLOCK, LANES), lambda b, c: (0, b * nb + c, 0))
    head_shape = jax.ShapeDtypeStruct((N_FOX_HEADS, t, LANES), BF16)
    vt_spec = pl.BlockSpec((N_FOX_HEADS, LANES, SEQ_BLOCK), lambda b, c: (0, 0, b * nb + c))
    vt_shape = jax.ShapeDtypeStruct((N_FOX_HEADS, LANES, t), BF16)
    return pl.pallas_call(
        _fox_prep_kernel,
        grid=(batch, nb),
        in_specs=[
            pl.BlockSpec((SEQ_BLOCK, qkv.shape[1]), row),
            pl.BlockSpec((SEQ_BLOCK, LANES), row),
            pl.BlockSpec((1, LANES), lambda b, c: (0, 0)),
            pl.BlockSpec((SEQ_BLOCK, SEQ_BLOCK), lambda b, c: (0, 0)),
        ],
        out_specs=[head_spec, head_spec, vt_spec],
        out_shape=[head_shape, head_shape, vt_shape],
        scratch_shapes=[pltpu.VMEM((1, LANES), F32)],
        compiler_params=_cparams("parallel", "arbitrary"),
        name="fox_prep",
    )(qkv, small, bias_row, tri)


def _fox_attn_kernel(q_ref, k_ref, vt_ref, nw_ref, o_ref, m_ref, acc_ref, sa_ref, sb_ref):
    i = pl.program_id(1)
    tq = q_ref.shape[1]
    kv_idx = lax.broadcasted_iota(jnp.int32, (tq, tq), 0)
    q_idx = lax.broadcasted_iota(jnp.int32, (tq, tq), 1)
    heads = range(N_FOX_HEADS)
    m_ref[...] = jnp.full(m_ref.shape, -jnp.inf, F32)
    acc_ref[...] = jnp.zeros_like(acc_ref)

    def logits_into(dst_ref, j):
        start = pl.multiple_of(j * tq, tq)
        for hd in heads:
            dst_ref[hd] = _dot_nt(k_ref[hd, pl.ds(start, tq), :], q_ref[hd])

    def fold(src_ref, j, diagonal=False):
        start = pl.multiple_of(j * tq, tq)
        for hd in heads:
            s = src_ref[hd]
            if diagonal:
                s = jnp.where(kv_idx <= q_idx, s, MASK_VALUE)
            m_old = m_ref[hd]
            m_new = jnp.maximum(m_old, jnp.max(s, axis=0, keepdims=True))
            p = jnp.exp(s - m_new).astype(BF16)
            acc_ref[hd] = jnp.exp(m_old - m_new) * acc_ref[hd] + _dot(vt_ref[hd, :, pl.ds(start, tq)], p)
            m_ref[hd] = m_new

    logits_into(sa_ref, 0)

    def two_blocks(jj, carry):
        logits_into(sb_ref, 2 * jj + 1)
        fold(sa_ref, 2 * jj)
        logits_into(sa_ref, 2 * jj + 2)
        fold(sb_ref, 2 * jj + 1)
        return carry

    lax.fori_loop(0, i // 2, two_blocks, 0)

    @pl.when(i % 2 == 1)
    def _():
        logits_into(sb_ref, i)
        fold(sa_ref, i - 1)
        fold(sb_ref, i, diagonal=True)

    @pl.when(i % 2 == 0)
    def _():
        fold(sa_ref, i, diagonal=True)

    normed = []
    for hd in heads:
        acc = acc_ref[hd]
        o = acc[:HEAD_DIM] / acc[HEAD_DIM:HEAD_DIM + 1]
        ms = jnp.mean(o * o, axis=0, keepdims=True)
        normed.append(o * lax.rsqrt(ms + EPS))
    tiles = []
    for pair in range(N_FOX_HEADS // 2):
        both = jnp.concatenate(normed[2 * pair:2 * pair + 2], axis=0)
        tiles.append(both.T * nw_ref[pair:pair + 1, :])
    o_ref[...] = jnp.concatenate(tiles, axis=1).astype(o_ref.dtype)


def _fox_attn(qa, ka, vt, nw_pairs, batch, seq):
    t = qa.shape[1]
    nq = seq // SEQ_BLOCK
    return pl.pallas_call(
        _fox_attn_kernel,
        grid=(batch, nq),
        in_specs=[
            pl.BlockSpec((N_FOX_HEADS, SEQ_BLOCK, LANES), lambda b, i: (0, b * nq + i, 0)),
            pl.BlockSpec((N_FOX_HEADS, seq, LANES), lambda b, i: (0, b, 0)),
            pl.BlockSpec((N_FOX_HEADS, LANES, seq), lambda b, i: (0, 0, b)),
            pl.BlockSpec((SUBLANES, LANES), lambda b, i: (0, 0)),
        ],
        out_specs=pl.BlockSpec((SEQ_BLOCK, N_FOX_HEADS * HEAD_DIM), lambda b, i: (b * nq + i, 0)),
        out_shape=jax.ShapeDtypeStruct((t, N_FOX_HEADS * HEAD_DIM), BF16),
        scratch_shapes=[
            pltpu.VMEM((N_FOX_HEADS, 1, SEQ_BLOCK), F32),
            pltpu.VMEM((N_FOX_HEADS, LANES, SEQ_BLOCK), F32),
            pltpu.VMEM((N_FOX_HEADS, SEQ_BLOCK, SEQ_BLOCK), F32),
            pltpu.VMEM((N_FOX_HEADS, SEQ_BLOCK, SEQ_BLOCK), F32),
        ],
        compiler_params=_cparams("parallel", "arbitrary"),
        name="fox_attn",
    )(qa, ka, vt, nw_pairs)


def _chunk_views(refs, c):
    return [r.at[pl.ds(c * SCAN_CHUNK, SCAN_CHUNK)] for r in refs]


def _hgrn2_kernel(hq_ref, hf_ref, hi_ref, hg_ref, par_ref, tri_ref, seg_ref, o_ref, st_ref):
    @pl.when(pl.program_id(1) == 0)
    def _():
        st_ref[...] = jnp.zeros_like(st_ref)

    for c in range(SCAN_CHUNKS_PER_STEP):
        hq, hf, hi, hg, o = _chunk_views((hq_ref, hf_ref, hi_ref, hg_ref, o_ref), c)
        _hgrn2_chunk(hq, hf, hi, hg, par_ref, tri_ref, seg_ref, o, st_ref)


def _hgrn2_chunk(hq_ref, hf_ref, hi_ref, hg_ref, par_ref, tri_ref, seg_ref, o_ref, st_ref):
    rows, width = hq_ref.shape
    log_lb, log1m_lb, one_m_lb, nw = (par_ref[r:r + 1, :] for r in range(4))
    f_raw = hf_ref[...]
    q = _silu(hq_ref[...]) * (HEAD_DIM ** -0.5)
    e_f = jnp.exp(-jnp.abs(f_raw))
    den = 1.0 + e_f
    b = log1m_lb + (jnp.minimum(f_raw, 0.0) - jnp.log(den))
    g = jnp.maximum(log_lb, b) + _log1p_exp(-jnp.abs(log_lb - b))
    k = one_m_lb * (jnp.where(f_raw > 0.0, e_f, 1.0) / den)
    v = hi_ref[...]
    cum = _exact_left(tri_ref[...], g)

    lane = lax.broadcasted_iota(jnp.int32, (1, width), 1)
    head_masks = [(lane >= hd * HEAD_DIM) & (lane < (hd + 1) * HEAD_DIM) for hd in range(N_HG_HEADS)]
    seg = seg_ref[...]
    v_b = v.astype(BF16)

    st = st_ref[...]
    o_state = _dot_nt((q * jnp.exp(cum)).astype(BF16), st.astype(BF16))

    n_sub = rows // SUB_BLOCK
    refs = [cum[i * SUB_BLOCK - 1:i * SUB_BLOCK] if i else jnp.zeros((1, width), F32) for i in range(n_sub)]
    local = [cum[i * SUB_BLOCK:(i + 1) * SUB_BLOCK] - refs[i] for i in range(n_sub)]

    def stack_heads(x):
        return jnp.concatenate([jnp.where(mk, x, 0.0) for mk in head_masks], axis=0).astype(BF16)

    def unstack_heads(base, p4):
        for hd, mk in enumerate(head_masks):
            base = base + jnp.where(mk, p4[hd * SUB_BLOCK:(hd + 1) * SUB_BLOCK], 0.0)
        return base

    def factored():
        scores = []
        for i in range(n_sub):
            hi = (i + 1) * SUB_BLOCK
            qs = q[hi - SUB_BLOCK:hi] * jnp.exp(local[i])
            ks = (k[:hi] * jnp.exp(refs[i] - cum[:hi])).astype(BF16)
            scores.append(_dot_nt(stack_heads(qs), ks))
        blocks = []
        for i in range(n_sub):
            hi = (i + 1) * SUB_BLOCK
            t_in = lax.broadcasted_iota(jnp.int32, (N_HG_HEADS * SUB_BLOCK, hi), 0) & (SUB_BLOCK - 1)
            s_in = lax.broadcasted_iota(jnp.int32, (N_HG_HEADS * SUB_BLOCK, hi), 1)
            sc = jnp.where(s_in <= t_in + (hi - SUB_BLOCK), scores[i], 0.0).astype(BF16)
            blocks.append(unstack_heads(o_state[hi - SUB_BLOCK:hi], _dot(sc, v_b[:hi])))
        return jnp.concatenate(blocks, axis=0)

    def pairwise():
        t_idx = lax.broadcasted_iota(jnp.int32, (SUB_BLOCK, 1), 0)
        blocks = []
        for i in range(n_sub):
            r0 = i * SUB_BLOCK
            q_i = q[r0:r0 + SUB_BLOCK]
            cum_i = cum[r0:r0 + SUB_BLOCK]
            o_i = o_state[r0:r0 + SUB_BLOCK]
            if i > 0:
                ks = (k[:r0] * jnp.exp(refs[i] - cum[:r0])).astype(BF16)
                sc = _dot_nt(stack_heads(q_i * jnp.exp(local[i])), ks)
                o_i = unstack_heads(o_i, _dot(sc.astype(BF16), v_b[:r0]))
            terms = []
            for s in range(SUB_BLOCK):
                keep = t_idx >= s
                e = _masked_exp(cum_i - cum[r0 + s:r0 + s + 1], keep)
                terms.append((q_i * e * k[r0 + s:r0 + s + 1]).astype(BF16))
            sums = _dot(jnp.concatenate(terms, axis=0), seg)
            for s in range(SUB_BLOCK):
                o_i = o_i + sums[s * SUB_BLOCK:(s + 1) * SUB_BLOCK] * v[r0 + s:r0 + s + 1]
            blocks.append(o_i)
        return jnp.concatenate(blocks, axis=0)

    worst = functools.reduce(jnp.minimum, local)
    o = lax.cond(jnp.min(worst) >= -HG_SAFE_DECAY, factored, pairwise)

    last = cum[rows - 1:rows]
    kd = (k * jnp.exp(last - cum)).astype(BF16)
    upd = _dot(v.T.astype(BF16), kd)
    st_ref[...] = st * jnp.exp(last) + jnp.where(seg > 0, upd, 0.0)

    sq = o * o
    s1 = sq.astype(BF16)
    s2 = (sq - s1.astype(F32)).astype(BF16)
    ms = (_dot(s1, seg) + _dot(s2, seg)) * (1.0 / HEAD_DIM)
    o_ref[...] = (o * lax.rsqrt(ms + EPS) * nw * _silu(hg_ref[...])).astype(o_ref.dtype)


def _hgrn2(hg, params, tri, seg, batch, seq):
    t = hg.shape[0]
    width = N_HG_HEADS * HEAD_DIM
    step_rows = SCAN_CHUNK * SCAN_CHUNKS_PER_STEP
    nc = seq // step_rows
    col = lambda c: pl.BlockSpec((step_rows, width), lambda b, j, c=c: (b * nc + j, c))
    return pl.pallas_call(
        _hgrn2_kernel,
        grid=(batch, nc),
        in_specs=[
            col(0), col(1), col(2), col(3),
            pl.BlockSpec(params.shape, lambda b, j: (0, 0)),
            pl.BlockSpec(tri.shape, lambda b, j: (0, 0)),
            pl.BlockSpec(seg.shape, lambda b, j: (0, 0)),
        ],
        out_specs=pl.BlockSpec((step_rows, width), lambda b, j: (b * nc + j, 0)),
        out_shape=jax.ShapeDtypeStruct((t, width), BF16),
        scratch_shapes=[pltpu.VMEM((width, width), F32)],
        compiler_params=_cparams("parallel", "arbitrary"),
        name="hgrn2",
    )(hg, hg, hg, hg, params, tri, seg)


def _ssd_kernel(z_ref, xbc_ref, small_ref, cw_ref, cb_ref, hp_ref, sp_ref, nw_ref, tri_ref, exp_ref, shift_ref,
                o_ref, buf_ref, st_ref):
    @pl.when(pl.program_id(1) == 0)
    def _():
        buf_ref[...] = jnp.zeros_like(buf_ref)
        st_ref[...] = jnp.zeros_like(st_ref)

    for c in range(SCAN_CHUNKS_PER_STEP):
        z, xbc, small, o = _chunk_views((z_ref, xbc_ref, small_ref, o_ref), c)
        _ssd_chunk(z, xbc, small, cw_ref, cb_ref, hp_ref, sp_ref, nw_ref, tri_ref, exp_ref, shift_ref, o,
                   buf_ref, st_ref)


def _ssd_chunk(z_ref, xbc_ref, small_ref, cw_ref, cb_ref, hp_ref, sp_ref, nw_ref, tri_ref, exp_ref, shift_ref,
               o_ref, buf_ref, st_ref):
    rows = z_ref.shape[0]
    width = z_ref.shape[1]

    x_in = xbc_ref[...]
    buf_ref[rows:2 * rows, :] = x_in.astype(BF16)
    shifted = _dot(shift_ref[...], buf_ref[...])
    buf_ref[0:rows, :] = buf_ref[rows:2 * rows, :]
    conv = cb_ref[...] + cw_ref[M2_CONV - 1:M2_CONV, :] * x_in
    for k in range(1, M2_CONV):
        tap = M2_CONV - 1 - k
        conv = conv + cw_ref[tap:tap + 1, :] * shifted[(k - 1) * rows:k * rows]
    xc = _silu(conv)
    xs = xc[:, :width]
    gn = M2_GROUPS * M2_STATE
    bm = xc[:, width:width + gn]
    cm = xc[:, width + gn:width + 2 * gn]

    d_x = hp_ref[0:1, :]
    dt_bias_c, a_log_c = (sp_ref[r:r + 1, :] for r in range(2))
    small = small_ref[...]
    tri = tri_ref[...]
    dt_c = _softplus(small + dt_bias_c)
    a_cum_c = _exact_left(tri, dt_c * -jnp.exp(a_log_c))
    a_cum_t = a_cum_c.T
    expand = exp_ref[...]
    dt_x = _exact_right(dt_c, expand)
    a_cum_x = _exact_right(a_cum_c, expand)

    row = lax.broadcasted_iota(jnp.int32, (rows, rows), 0)
    col = lax.broadcasted_iota(jnp.int32, (rows, rows), 1)
    causal = col <= row
    lane = lax.broadcasted_iota(jnp.int32, (1, LANES), 1)
    low = lane < HEAD_DIM

    xdt = xs * dt_x
    a_last = a_cum_x[rows - 1:rows]
    xd = (xdt * jnp.exp(a_last - a_cum_x)).astype(BF16)
    decay_out = jnp.exp(a_cum_x)
    st = st_ref[...]
    heads_per_group = M2_HEADS // M2_GROUPS
    gw = heads_per_group * HEAD_DIM
    cbs, y_offs, new_states = [], [], []
    for g in range(M2_GROUPS):
        b_g = bm[:, g * M2_STATE:(g + 1) * M2_STATE]
        c_g = cm[:, g * M2_STATE:(g + 1) * M2_STATE].astype(BF16)
        cbs.append(jnp.where(causal, _dot_nt(c_g, b_g.astype(BF16)), 0.0))
        y_offs.append(_dot(c_g, st[:, g * gw:(g + 1) * gw].astype(BF16)) * decay_out[:, g * gw:(g + 1) * gw])
        new_states.append(_dot(b_g.T.astype(BF16), xd[:, g * gw:(g + 1) * gw]))
    st_ref[...] = st * jnp.exp(a_last) + jnp.concatenate(new_states, axis=1)

    y_tiles = []
    for tile in range(M2_HEADS // 2):
        g = (2 * tile) // heads_per_group
        pair = tile - g * (heads_per_group // 2)
        x_tile = xdt[:, tile * LANES:(tile + 1) * LANES]
        y_pair = y_offs[g][:, pair * LANES:(pair + 1) * LANES]
        for half in range(2):
            hd = 2 * tile + half
            seg = a_cum_c[:, SUBLANES + hd:SUBLANES + hd + 1] - a_cum_t[SUBLANES + hd:SUBLANES + hd + 1, :]
            m_h = (cbs[g] * jnp.exp(jnp.minimum(seg, 0.0))).astype(BF16)
            x_h = jnp.where(low if half == 0 else ~low, x_tile, 0.0).astype(BF16)
            y_pair = y_pair + _dot(m_h, x_h)
        y_tiles.append(y_pair)

    y = jnp.concatenate(y_tiles, axis=1) + d_x * xs
    y = y * _silu(z_ref[...])
    nw = nw_ref[...]
    outs = []
    for g in range(M2_GROUPS):
        outs.append(_rms(y[:, g * gw:(g + 1) * gw], nw[:, g * gw:(g + 1) * gw]))
    o_ref[...] = jnp.concatenate(outs, axis=1).astype(o_ref.dtype)


def _ssd(z, xbc, small, conv_w, conv_b, head_rows, small_rows, norm_w, tri, expand, shift, batch, seq):
    t, width = z.shape
    ch = xbc.shape[1]
    step_rows = SCAN_CHUNK * SCAN_CHUNKS_PER_STEP
    nc = seq // step_rows
    row = lambda b, j: (b * nc + j, 0)
    const = lambda a: pl.BlockSpec(a.shape, lambda b, j: (0, 0))
    return pl.pallas_call(
        _ssd_kernel,
        grid=(batch, nc),
        in_specs=[
            pl.BlockSpec((step_rows, width), row),
            pl.BlockSpec((step_rows, ch), row),
            pl.BlockSpec((step_rows, LANES), row),
            const(conv_w), const(conv_b), const(head_rows), const(small_rows), const(norm_w),
            const(tri), const(expand), const(shift),
        ],
        out_specs=pl.BlockSpec((step_rows, width), row),
        out_shape=jax.ShapeDtypeStruct((t, width), BF16),
        scratch_shapes=[
            pltpu.VMEM((2 * SCAN_CHUNK, ch), BF16),
            pltpu.VMEM((M2_STATE, width), F32),
        ],
        compiler_params=_cparams("parallel", "arbitrary"),
        name="ssd",
    )(z, xbc, small, conv_w, conv_b, head_rows, small_rows, norm_w, tri, expand, shift)


def _xattn_kernel(h_ref, a_ref, b_ref, c_ref, wout_ref, nw_ref, wq_ref, kv_ref, wo_ref, *rest, tail):
    o_ref = rest[3]
    wa = a_ref.shape[1]
    wb = b_ref.shape[1]
    h = h_ref[...] + _dot(a_ref[...], wout_ref[0:wa, :])
    h = h + _dot(b_ref[...], wout_ref[wa:wa + wb, :])
    h = h + _dot(c_ref[...], wout_ref[wa + wb:, :])
    d = h.shape[1]
    hd = d // XA_HEADS
    q = (_dot(_rms(h, nw_ref[...]).astype(BF16), wq_ref[...]) * (hd ** -0.5)).astype(BF16)
    logits = [_dot_nt(q[:, a * hd:(a + 1) * hd], kv_ref[:, a * hd:(a + 1) * hd]) for a in range(XA_HEADS)]
    outs = []
    for a, s in enumerate(logits):
        p = jnp.exp(s - jnp.max(s, axis=-1, keepdims=True))
        p = p / jnp.sum(p, axis=-1, keepdims=True)
        outs.append(_dot(p.astype(BF16), kv_ref[:, d + a * hd:d + (a + 1) * hd]).astype(BF16))
    h = h + _dot(jnp.concatenate(outs, axis=1), wo_ref[...])
    if tail == "ffn":
        fnw_ref, w1_ref, w2_ref = rest[:3]
        h = h + _swiglu(_rms(h, fnw_ref[...]).astype(BF16), w1_ref, w2_ref, w2_ref.shape[0])
    else:
        fnw_ref, wr_ref, tri_ref = rest[:3]
        first_step = (pl.program_id(0) == 0) & (pl.program_id(1) == 0)
        _route(_rms(h, fnw_ref[...]), first_step, wr_ref, tri_ref, *rest[4:])
    o_ref[...] = h


def _xattn(h, o_fox, o_hg, o_m2, w_out, nw, wq, kv, wo, layer, batch, seq, ffn=None, router=None):
    t, d = h.shape
    n_mem = kv.shape[0] // batch
    nb = seq // ROW_BLOCK
    row = lambda width: pl.BlockSpec((ROW_BLOCK, width), lambda b, i: (b * nb + i, 0))
    out_specs, out_shape, scratch = [row(d)], [jax.ShapeDtypeStruct((t, d), F32)], []
    if ffn is not None:
        fnw, w1, w2, ffn_layer = ffn
        extra_specs = [_resident(fnw), _layer_resident(w1, ffn_layer), _layer_resident(w2, ffn_layer)]
        extra_args = [fnw, w1, w2]
    else:
        extra_args = list(router)
        extra_specs = [_resident(a) for a in extra_args]
        out_specs += [row(LANES), row(LANES), row(LANES), pl.BlockSpec((SUBLANES, LANES), lambda b, i: (0, 0))]
        out_shape += [jax.ShapeDtypeStruct((t, LANES), jnp.int32), jax.ShapeDtypeStruct((t, LANES), F32),
                      jax.ShapeDtypeStruct((t, LANES), jnp.int32), jax.ShapeDtypeStruct((SUBLANES, LANES), F32)]
        scratch = [pltpu.VMEM((1, LANES), F32)]
    outs = pl.pallas_call(
        functools.partial(_xattn_kernel, tail="ffn" if ffn is not None else "router"),
        grid=(batch, nb),
        in_specs=[
            row(d), row(o_fox.shape[1]), row(o_hg.shape[1]), row(o_m2.shape[1]),
            _layer_resident(w_out, layer), _resident(nw), _layer_resident(wq, layer),
            pl.BlockSpec((n_mem, kv.shape[1]), lambda b, i: (b, 0)),
            _layer_resident(wo, layer),
        ] + extra_specs,
        out_specs=out_specs,
        out_shape=out_shape,
        scratch_shapes=scratch,
        compiler_params=_cparams("arbitrary", "arbitrary"),
        name="xattn_ffn" if ffn is not None else "xattn_router",
    )(h, o_fox, o_hg, o_m2, w_out, nw, wq, kv, wo, *extra_args)
    return outs[0] if ffn is not None else outs


def _swiglu(x, w1_ref, w2_ref, d_ff):
    a = (_silu(_dot(x, w1_ref[:, :d_ff])) * _dot(x, w1_ref[:, d_ff:])).astype(BF16)
    return _dot(a, w2_ref[...])


def _route(xn, first_step, wr_ref, tri_ref, idx_ref, gate_ref, rank_ref, cnt_ref, carry_ref):
    @pl.when(first_step)
    def _():
        carry_ref[...] = jnp.zeros_like(carry_ref)

    x1 = xn.astype(BF16)
    x2 = (xn - x1.astype(F32)).astype(BF16)
    w12 = wr_ref[:, :2 * LANES]
    big = _dot(x1, w12) + _dot(x2, w12)
    logits = big[:, :LANES] + big[:, LANES:] + _dot(x1, wr_ref[:, 2 * LANES:])
    lane = lax.broadcasted_iota(jnp.int32, logits.shape, 1)
    logits = jnp.where(lane < N_EXPERTS, logits, -jnp.inf)
    m1 = jnp.max(logits, axis=-1, keepdims=True)
    i1 = jnp.min(jnp.where(logits == m1, lane, LANES), axis=-1, keepdims=True)
    rest = jnp.where(lane == i1, -jnp.inf, logits)
    m2 = jnp.max(rest, axis=-1, keepdims=True)
    i2 = jnp.min(jnp.where(rest == m2, lane, LANES), axis=-1, keepdims=True)
    e2 = jnp.exp(m2 - m1)
    denom = 1.0 + e2
    idx_ref[...] = jnp.where(lane == 0, i1, jnp.where(lane == 1, i2, 0))
    gate_ref[...] = jnp.where(lane == 0, 1.0 / denom, jnp.where(lane == 1, e2 / denom, 0.0))

    hit1 = lane == i1
    hit2 = lane == i2
    member = jnp.where(hit1, 1.0, jnp.where(hit2, 1.0, 0.0))
    incl = _dot(tri_ref[...], member.astype(BF16)) + carry_ref[...]
    excl = incl - member
    r1 = jnp.sum(jnp.where(hit1, excl, 0.0), axis=-1, keepdims=True)
    r2 = jnp.sum(jnp.where(hit2, excl, 0.0), axis=-1, keepdims=True)
    rows = member.shape[0]
    carry_ref[...] = incl[rows - 1:rows, :]
    rank_ref[...] = jnp.where(lane == 0, r1, jnp.where(lane == 1, r2, 0.0)).astype(jnp.int32)
    cnt_ref[...] = jnp.broadcast_to(incl[rows - 1:rows, :], cnt_ref.shape)


def _slot_kernel(idx_ref, rank_ref, start_ref, o_ref):
    idx = idx_ref[...]
    lane = lax.broadcasted_iota(jnp.int32, idx.shape, 1)
    starts = start_ref[...]
    s1 = jnp.sum(jnp.where(lane == idx[:, 0:1], starts, 0.0), axis=-1, keepdims=True)
    s2 = jnp.sum(jnp.where(lane == idx[:, 1:2], starts, 0.0), axis=-1, keepdims=True)
    o_ref[...] = jnp.where(lane == 0, s1, jnp.where(lane == 1, s2, 0.0)).astype(jnp.int32) + rank_ref[...]


def _slots(idx, rank, start_row):
    t = idx.shape[0]
    rows = min(t, SLOT_ROWS)
    row = pl.BlockSpec((rows, LANES), lambda i: (i, 0))
    return pl.pallas_call(
        _slot_kernel,
        grid=(t // rows,),
        in_specs=[row, row, pl.BlockSpec((1, LANES), lambda i: (0, 0))],
        out_specs=row,
        out_shape=jax.ShapeDtypeStruct((t, LANES), jnp.int32),
        compiler_params=_cparams("parallel"),
        name="moe_slots",
    )(idx, rank, start_row)


def _dispatch_kernel(pend_ref, padded_ref, nused_ref, dest_ref, h_ref, nw_ref, xs_ref, x_ref, zero_ref, sem):
    rows = h_ref.shape[0]
    n_blk = xs_ref.shape[0] // MOE_ROWS
    x_ref[...] = _rms(h_ref[...], nw_ref[...]).reshape(x_ref.shape)

    @pl.when(pl.program_id(0) == 0)
    def _():
        zero_ref[...] = jnp.zeros_like(zero_ref)

        def fill_block(start):
            fill = pltpu.make_async_copy(zero_ref, xs_ref.at[pl.ds(start, MOE_ROWS), :], sem)
            fill.start()
            fill.wait()

        for e in range(N_EXPERTS):
            @pl.when(padded_ref[e] > 0)
            def _():
                fill_block(pl.multiple_of(pend_ref[e] - MOE_ROWS, MOE_ROWS))

            @pl.when(nused_ref[0] + e < n_blk)
            def _():
                fill_block(pl.multiple_of((nused_ref[0] + e) * MOE_ROWS, MOE_ROWS))

    def issue(group, carry):
        base = pl.multiple_of(group * SUBLANES, SUBLANES)
        for s in range(SUBLANES):
            for k in range(2):
                slot = dest_ref[0, 0, k * rows + base + s]
                pltpu.make_async_copy(
                    x_ref.at[group, pl.ds(s, 1), :], xs_ref.at[pl.ds(slot, 1), :], sem).start(priority=k)
        return carry

    lax.fori_loop(0, rows // SUBLANES, issue, 0)
    for k in range(2):
        pltpu.make_async_copy(zero_ref, xs_ref.at[pl.ds(0, rows), :], sem).wait()


def _dispatch(pends, padded, n_used, dest_t, h, nw, cap):
    t, d = h.shape
    grid_spec = pltpu.PrefetchScalarGridSpec(
        num_scalar_prefetch=3,
        grid=(t // ROW_BLOCK,),
        in_specs=[
            pl.BlockSpec((1, 1, 2 * ROW_BLOCK), lambda i, pe, pa, nu: (i, 0, 0), memory_space=pltpu.SMEM),
            pl.BlockSpec((ROW_BLOCK, d), lambda i, pe, pa, nu: (i, 0)),
            pl.BlockSpec((1, d), lambda i, pe, pa, nu: (0, 0)),
        ],
        out_specs=pl.BlockSpec(memory_space=pl.ANY),
        scratch_shapes=[pltpu.VMEM((ROW_BLOCK // SUBLANES, SUBLANES, d), F32), pltpu.VMEM((MOE_ROWS, d), F32),
                        pltpu.SemaphoreType.DMA(())],
    )
    return pl.pallas_call(
        _dispatch_kernel,
        grid_spec=grid_spec,
        out_shape=jax.ShapeDtypeStruct((cap, d), F32),
        compiler_params=_cparams("arbitrary"),
        name="moe_dispatch",
    )(pends, padded, n_used, dest_t, h, nw)


def _expert_kernel(be_ref, nused_ref, x_ref, w1_ref, w2_ref, o_ref):
    live = pl.program_id(0) < nused_ref[0]

    @pl.when(live)
    def _():
        o_ref[...] = _swiglu(x_ref[...].astype(BF16), w1_ref, w2_ref, w2_ref.shape[0])

    @pl.when(jnp.logical_not(live))
    def _():
        o_ref[...] = jnp.zeros_like(o_ref)


def _experts(blk_expert, n_used, xs, w1, w2, layer):
    cap, d = xs.shape
    blk = lambda i, be, nu: (i, 0)
    expert = lambda w: pl.BlockSpec((None, None) + w.shape[2:], lambda i, be, nu: (layer, be[i], 0, 0),
                                    pipeline_mode=pl.Buffered(1))
    grid_spec = pltpu.PrefetchScalarGridSpec(
        num_scalar_prefetch=2,
        grid=(cap // MOE_ROWS,),
        in_specs=[pl.BlockSpec((MOE_ROWS, d), blk), expert(w1), expert(w2)],
        out_specs=pl.BlockSpec((MOE_ROWS, d), blk),
    )
    return pl.pallas_call(
        _expert_kernel,
        grid_spec=grid_spec,
        out_shape=jax.ShapeDtypeStruct((cap, d), F32),
        compiler_params=_cparams("arbitrary"),
        name="experts",
    )(blk_expert, n_used, xs, w1, w2)


def _combine_kernel(dest_ref, h_ref, gate_ref, nw_ref, yb_ref, o_ref, buf_ref, sem, *, final_norm):
    rows = h_ref.shape[0]

    def issue(group, carry):
        base = pl.multiple_of(group * SUBLANES, SUBLANES)
        for s in range(SUBLANES):
            for k in range(2):
                slot = dest_ref[0, 0, k * rows + base + s]
                pltpu.make_async_copy(
                    yb_ref.at[pl.ds(slot, 1), :], buf_ref.at[k, group, pl.ds(s, 1), :], sem).start(priority=k)
        return carry

    lax.fori_loop(0, rows // SUBLANES, issue, 0)
    for k in range(2):
        pltpu.make_async_copy(buf_ref.at[k], buf_ref.at[k], sem).wait()
    gate = gate_ref[...]
    picked = [buf_ref[k].reshape(rows, h_ref.shape[1]) for k in range(2)]
    out = h_ref[...] + gate[:, 0:1] * picked[0] + gate[:, 1:2] * picked[1]
    if final_norm:
        out = _rms(out, nw_ref[...])
    o_ref[...] = out


def _combine(dest_t, h, gate, nw, yb, final_norm):
    t, d = h.shape
    return pl.pallas_call(
        functools.partial(_combine_kernel, final_norm=final_norm),
        grid=(t // ROW_BLOCK,),
        in_specs=[
            pl.BlockSpec((1, 1, 2 * ROW_BLOCK), lambda i: (i, 0, 0), memory_space=pltpu.SMEM),
            pl.BlockSpec((ROW_BLOCK, d), lambda i: (i, 0)),
            pl.BlockSpec((ROW_BLOCK, LANES), lambda i: (i, 0)),
            pl.BlockSpec((1, d), lambda i: (0, 0)),
            pl.BlockSpec(memory_space=pl.ANY),
        ],
        out_specs=pl.BlockSpec((ROW_BLOCK, d), lambda i: (i, 0)),
        out_shape=jax.ShapeDtypeStruct((t, d), F32),
        scratch_shapes=[pltpu.VMEM((2, ROW_BLOCK // SUBLANES, SUBLANES, d), F32), pltpu.SemaphoreType.DMA(())],
        compiler_params=_cparams("arbitrary"),
        name="moe_combine",
    )(dest_t, h, gate, nw, yb)


def _final_norm_kernel(h_ref, nw_ref, o_ref):
    o_ref[...] = _rms(h_ref[...], nw_ref[...])


def _final_norm(h, nw):
    t, d = h.shape
    return pl.pallas_call(
        _final_norm_kernel,
        grid=(t // ROW_BLOCK,),
        in_specs=[pl.BlockSpec((ROW_BLOCK, d), lambda i: (i, 0)), pl.BlockSpec((1, d), lambda i: (0, 0))],
        out_specs=pl.BlockSpec((ROW_BLOCK, d), lambda i: (i, 0)),
        out_shape=jax.ShapeDtypeStruct((t, d), F32),
        compiler_params=_cparams("parallel"),
        name="final_norm",
    )(h, nw)


def _router_weights(router_w):
    r1, r2, r3 = _split3(jnp.pad(router_w, ((0, 0), (0, LANES - N_EXPERTS))))
    return jnp.concatenate([r1, r2, r3], axis=1)


def _moe(h, routing, nw, w1, w2, layer, out_nw, final_norm):
    t, d = h.shape
    assert MOE_ROWS == ROW_BLOCK
    idx, gate, rank, cnt = routing
    counts = cnt[0, :N_EXPERTS].astype(jnp.int32)
    padded = (counts + MOE_ROWS - 1) // MOE_ROWS * MOE_ROWS
    pends = jnp.cumsum(padded)
    starts = pends - padded
    cap = (2 * t // MOE_ROWS + N_EXPERTS) * MOE_ROWS
    n_blk = cap // MOE_ROWS
    blk_start = jnp.arange(n_blk, dtype=jnp.int32) * MOE_ROWS
    blk_expert = jnp.minimum(jnp.sum(blk_start[:, None] >= pends[None, :], axis=1), N_EXPERTS - 1).astype(jnp.int32)
    n_used = (pends[-1:] // MOE_ROWS).astype(jnp.int32)
    start_row = jnp.zeros((1, LANES), F32).at[0, :N_EXPERTS].set(starts.astype(F32))
    dest = _slots(idx, rank, start_row)
    dest_t = dest[:, :2].reshape(t // ROW_BLOCK, ROW_BLOCK, 2).transpose(0, 2, 1).reshape(
        t // ROW_BLOCK, 1, 2 * ROW_BLOCK)
    xs = _dispatch(pends.astype(jnp.int32), padded.astype(jnp.int32), n_used, dest_t, h, nw, cap)
    yb = _experts(blk_expert, n_used, xs, w1, w2, layer)
    return _combine(dest_t, h, gate, out_nw, yb, final_norm)


def _tri(n):
    return jnp.tril(jnp.ones((n, n), F32)).astype(BF16)


def kernel(x, mem, mix_norm_w, w_in, fox_f_bias, fox_norm_w, hg_lb_raw, hg_norm_w, m2_conv_w, m2_conv_b, m2_dt_bias, m2_a_log, m2_d, m2_norm_w, w_out, xa_norm_w, xa_mem_norm_w, xa_w_q, xa_w_kv, xa_w_o, ffn_norm_w, ffn_w1, ffn_w2, moe_router, moe_w1, moe_w2, final_norm_w):
    batch, seq, d = x.shape
    depth = w_in.shape[0]
    t = batch * seq
    fox_w = N_FOX_HEADS * HEAD_DIM
    hg_w = N_HG_HEADS * HEAD_DIM
    m2_w = M2_HEADS * HEAD_DIM
    conv_ch = m2_w + 2 * M2_GROUPS * M2_STATE
    in_splits = (fox_w, fox_w, fox_w, N_FOX_HEADS, hg_w, hg_w, hg_w, hg_w, m2_w, conv_ch, M2_HEADS)
    offs = [0]
    for s in in_splits:
        offs.append(offs[-1] + s)

    lb_p = jax.nn.softmax(hg_lb_raw.astype(F32), axis=0)
    hg_lb = jnp.cumsum(lb_p, axis=0) - lb_p[0]

    tri_seq = _tri(SEQ_BLOCK)
    tri_chunk = _tri(SCAN_CHUNK)
    head_of_lane = jnp.arange(hg_w) // HEAD_DIM
    seg_ones = (head_of_lane[:, None] == head_of_lane[None, :]).astype(BF16)
    expand = (jnp.arange(LANES)[:, None] - SUBLANES == jnp.arange(m2_w)[None, :] // HEAD_DIM).astype(BF16)
    shift_row = jnp.arange((M2_CONV - 1) * SCAN_CHUNK)
    shift_src = SCAN_CHUNK + shift_row % SCAN_CHUNK - (shift_row // SCAN_CHUNK + 1)
    conv_shift = (jnp.arange(2 * SCAN_CHUNK)[None, :] == shift_src[:, None]).astype(BF16)

    def pad_lanes(v, offset=0, width=LANES):
        return jnp.zeros((1, width), F32).at[0, offset:offset + v.shape[0]].set(v)

    small_w = jnp.zeros((depth, d, LANES), F32)
    small_w = small_w.at[:, :, 0:N_FOX_HEADS].set(w_in[:, :, offs[3]:offs[4]])
    small_w = small_w.at[:, :, SUBLANES:SUBLANES + M2_HEADS].set(w_in[:, :, offs[10]:offs[11]])
    in_weights = tuple(w.astype(BF16) for w in (
        w_in[:, :, offs[0]:offs[3]], small_w, w_in[:, :, offs[4]:offs[8]], w_in[:, :, offs[8]:offs[9]],
        w_in[:, :, offs[9]:offs[10]]))
    in_dtypes = (BF16, F32, F32, F32, F32)
    w_out_b, xa_wq_b, xa_wkv_b, xa_wo_b, ffn_w1_b, ffn_w2_b, moe_w1_b, moe_w2_b = (
        w.astype(BF16) for w in (w_out, xa_w_q, xa_w_kv, xa_w_o, ffn_w1, ffn_w2, moe_w1, moe_w2))

    h = x.reshape(t, d)
    mem2 = mem.reshape(batch * mem.shape[1], d)
    for layer in range(depth):
        qkv, small, hg, z, xbc = _in_proj(h, mix_norm_w[layer][None, :], in_weights, layer, in_dtypes)

        qa, ka, va = _fox_prep(qkv, small, pad_lanes(fox_f_bias[layer]), tri_seq, batch, seq)
        fox_nw = jnp.zeros((SUBLANES, LANES), F32).at[:fox_w // LANES].set(
            fox_norm_w[layer].reshape(fox_w // LANES, LANES))
        o_fox = _fox_attn(qa, ka, va, fox_nw, batch, seq)

        lb = hg_lb[layer]
        hg_params = jnp.zeros((SUBLANES, hg_w), F32)
        hg_params = hg_params.at[0].set(jnp.log(jnp.maximum(lb, LB_FLOOR)))
        hg_params = hg_params.at[1].set(jnp.log1p(-lb))
        hg_params = hg_params.at[2].set(1.0 - lb)
        hg_params = hg_params.at[3].set(hg_norm_w[layer])
        o_hg = _hgrn2(hg, hg_params, tri_chunk, seg_ones, batch, seq)

        conv_w = jnp.zeros((SUBLANES, conv_ch), F32).at[:M2_CONV].set(m2_conv_w[layer])
        head_rows = jnp.zeros((SUBLANES, m2_w), F32).at[0].set(jnp.repeat(m2_d[layer], HEAD_DIM))
        small_rows = jnp.concatenate(
            [pad_lanes(m2_dt_bias[layer], SUBLANES), pad_lanes(m2_a_log[layer], SUBLANES),
             jnp.zeros((SUBLANES - 2, LANES), F32)], axis=0)
        o_m2 = _ssd(z, xbc, small, conv_w, m2_conv_b[layer][None, :], head_rows, small_rows,
                    m2_norm_w[layer][None, :], tri_chunk, expand, conv_shift, batch, seq)

        kv = _norm_proj(mem2, xa_mem_norm_w[layer][None, :], xa_wkv_b, layer, BF16, mem.shape[1])
        nw = ffn_norm_w[layer][None, :]
        attn_args = (h, o_fox, o_hg, o_m2, w_out_b, xa_norm_w[layer][None, :], xa_wq_b, kv, xa_wo_b, layer,
                     batch, seq)
        if layer % 2 == 0:
            h = _xattn(*attn_args, ffn=(nw, ffn_w1_b, ffn_w2_b, layer // 2))
        else:
            h, *routing = _xattn(*attn_args, router=(nw, _router_weights(moe_router[layer // 2]), tri_seq))
            h = _moe(h, routing, nw, moe_w1_b, moe_w2_b, layer // 2, final_norm_w[None, :], layer == depth - 1)
    if depth % 2:
        h = _final_norm(h, final_norm_w[None, :])
    return h.reshape(batch, seq, d)
```
